```python
import math
import jax, jax.numpy as jnp
from jax import lax
import numpy as np

D_MODEL = 1024
BATCH = 8
SEQ = 4096
DEPTH = 2

CHUNK = 64
Q_BLOCK = 128
CONV_DIM = D_MODEL // 2
CONV_WIDTH = 3
HG_HEADS = 4
HG_DK = 128
HG_DV = (D_MODEL // 2) // HG_HEADS
HG_FDIM = HG_HEADS * HG_DK
HG_WIDTH = HG_HEADS * HG_DV
SB_HEADS = 16
SB_HEAD_DIM = D_MODEL // SB_HEADS
SB_WIDTH = SB_HEADS * SB_HEAD_DIM
D_FF = 4 * D_MODEL
N_EVEN = (DEPTH + 1) // 2
N_ODD = DEPTH // 2
AB_IN = 3 * CONV_DIM + 2 * HG_FDIM + 2 * HG_WIDTH
AB_MIX = CONV_DIM + HG_WIDTH
AB_SPLITS = [CONV_DIM, 2 * CONV_DIM, 3 * CONV_DIM,
             3 * CONV_DIM + HG_FDIM, 3 * CONV_DIM + 2 * HG_FDIM,
             3 * CONV_DIM + 2 * HG_FDIM + HG_WIDTH]
EPS = 1e-6

kernel_name = "hybrid_chunk_causal_conv_hgrn2_stickbreak"


def rms_norm(x, g):
    xf = x.astype(jnp.float32)
    y = xf * lax.rsqrt(jnp.mean(xf * xf, axis=-1, keepdims=True) + EPS)
    return (y * g.astype(jnp.float32)).astype(x.dtype)


def causal_depthwise_conv(u, w):
    width, ch = w.shape
    return lax.conv_general_dilated(
        u, w[:, None, :], window_strides=(1,), padding=[(width - 1, 0)],
        dimension_numbers=('NWC', 'WIO', 'NWC'), feature_group_count=ch)


def hgrn2_chunkwise(q, k, v, log_f):
    B, S, H, Dk = q.shape
    Dv = v.shape[-1]
    n = S // CHUNK

    def to_chunks(t):
        return t.reshape(B, n, CHUNK, H, t.shape[-1]).transpose(1, 0, 3, 2, 4)

    qc, kc, vc, gc = to_chunks(q), to_chunks(k), to_chunks(v), to_chunks(log_f)
    bc = jnp.cumsum(gc, axis=3)
    causal = jnp.tril(jnp.ones((CHUNK, CHUNK), dtype=bool))

    def step(state, inp):
        q_, k_, v_, b_ = inp
        inter = jnp.einsum('bhtk,bhkv->bhtv', q_ * jnp.exp(b_), state)
        diff = jnp.where(causal[None, None, :, :, None],
                         b_[:, :, :, None, :] - b_[:, :, None, :, :], -jnp.inf)
        decay = jnp.exp(diff)
        scores = jnp.einsum('bhtk,bhsk,bhtsk->bhts', q_, k_, decay)
        intra = jnp.einsum('bhts,bhsv->bhtv', scores, v_)
        b_last = b_[:, :, -1:, :]
        new_state = (jnp.exp(b_last[:, :, 0, :])[..., None] * state
                     + jnp.einsum('bhsk,bhsv->bhkv', k_ * jnp.exp(b_last - b_), v_))
        return new_state, inter + intra

    init = jnp.zeros((B, H, Dk, Dv), jnp.float32)
    _, out = lax.scan(step, init, (qc, kc, vc, bc))
    return out.transpose(1, 0, 3, 2, 4).reshape(B, S, H, Dv)


def stick_breaking_attention(q, k, v):
    S = q.shape[2]
    scale = SB_HEAD_DIM ** -0.5
    outs = []
    for blk in range(S // Q_BLOCK):
        q0 = blk * Q_BLOCK
        end = q0 + Q_BLOCK
        qb = q[:, :, q0:end]
        kb = k[:, :, :end]
        vb = v[:, :, :end]
        z = jnp.einsum('bhqd,bhkd->bhqk', qb, kb) * scale
        qpos = q0 + jnp.arange(Q_BLOCK)
        kpos = jnp.arange(end)
        mask = (kpos[None, :] < qpos[:, None])[None, None]
        log_beta = jax.nn.log_sigmoid(z)
        log_one_minus = jnp.where(mask, log_beta - z, 0.0)
        later = lax.cumsum(log_one_minus, axis=3, reverse=True) - log_one_minus
        w = jnp.where(mask, jnp.exp(log_beta + later), 0.0)
        outs.append(jnp.einsum('bhqk,bhkd->bhqd', w, vb))
    return jnp.concatenate(outs, axis=2)


def mixer_conv_hgrn(h, w_in, conv_w, hg_norm, lower_bound, w_out):
    B, S, _ = h.shape
    u = h @ w_in
    a_b, a_c, a_h, hq, hf, hi, hg = jnp.split(u, AB_SPLITS, axis=-1)
    y_a = a_b * causal_depthwise_conv(a_c * a_h, conv_w)
    f = lower_bound + (1.0 - lower_bound) * jax.nn.sigmoid(hf.astype(jnp.float32))
    log_f = jnp.log(f)
    k_in = 1.0 - f
    heads_k = lambda t: t.reshape(B, S, HG_HEADS, HG_DK)
    o = hgrn2_chunkwise(heads_k(hq.astype(jnp.float32)), heads_k(k_in),
                        hi.astype(jnp.float32).reshape(B, S, HG_HEADS, HG_DV), heads_k(log_f))
    o = rms_norm(o, hg_norm.reshape(HG_HEADS, HG_DV))
    y_b = (o.reshape(B, S, HG_WIDTH) * jax.nn.silu(hg.astype(jnp.float32))).astype(h.dtype)
    return jnp.concatenate([y_a, y_b], axis=-1) @ w_out


def mixer_stick_breaking(h, w_qkv, q_norm, k_norm, w_out):
    B, S, _ = h.shape
    qkv = (h @ w_qkv).reshape(B, S, 3, SB_HEADS, SB_HEAD_DIM)
    q = rms_norm(qkv[:, :, 0], q_norm)
    k = rms_norm(qkv[:, :, 1], k_norm)
    v = qkv[:, :, 2]
    to_bhsd = lambda t: t.astype(jnp.float32).transpose(0, 2, 1, 3)
    o = stick_breaking_attention(to_bhsd(q), to_bhsd(k), to_bhsd(v))
    o = o.transpose(0, 2, 1, 3).reshape(B, S, SB_WIDTH).astype(h.dtype)
    return o @ w_out


def _fwd_setup_inputs(seed: int = 0) -> dict:
    key = jax.random.key(seed)
    ks = jax.random.split(key, 20)
    nrm = lambda k, shape, s: jax.random.normal(k, shape, jnp.float32) * s
    gain = lambda k, shape: 1.0 + 0.02 * jax.random.normal(k, shape, jnp.float32)
    return {
        "x": nrm(ks[0], (BATCH, SEQ, D_MODEL), 1.0),
        "c": nrm(ks[1], (BATCH, D_MODEL), 1.0),
        "ada_w": nrm(ks[2], (DEPTH, D_MODEL, 6 * D_MODEL), 0.5 * D_MODEL ** -0.5),
        "ada_b": nrm(ks[3], (DEPTH, 6 * D_MODEL), 0.02),
        "norm_mix": gain(ks[4], (DEPTH, D_MODEL)),
        "norm_mlp": gain(ks[5], (DEPTH, D_MODEL)),
        "w_in_ab": nrm(ks[6], (N_EVEN, D_MODEL, AB_IN), D_MODEL ** -0.5),
        "conv_w": nrm(ks[7], (N_EVEN, CONV_WIDTH, CONV_DIM), CONV_WIDTH ** -0.5),
        "hg_norm": gain(ks[8], (N_EVEN, HG_WIDTH)),
        "lb_logits": nrm(ks[9], (DEPTH + 1, HG_FDIM), 0.1),
        "w_out_ab": nrm(ks[10], (N_EVEN, AB_MIX, D_MODEL), AB_MIX ** -0.5),
        "w_qkv": nrm(ks[11], (N_ODD, D_MODEL, 3 * SB_WIDTH), D_MODEL ** -0.5),
        "q_norm": gain(ks[12], (N_ODD, SB_HEAD_DIM)),
        "k_norm": gain(ks[13], (N_ODD, SB_HEAD_DIM)),
        "w_out_c": nrm(ks[14], (N_ODD, SB_WIDTH, D_MODEL), SB_WIDTH ** -0.5),
        "mlp_w1": nrm(ks[15], (DEPTH, D_MODEL, D_FF), D_MODEL ** -0.5),
        "mlp_w2": nrm(ks[16], (DEPTH, D_FF, D_MODEL), D_FF ** -0.5),
    }


def _fwd_reference(x, c, ada_w, ada_b, norm_mix, norm_mlp, w_in_ab, conv_w, hg_norm,
              lb_logits, w_out_ab, w_qkv, q_norm, k_norm, w_out_c, mlp_w1, mlp_w2):
    c_act = jax.nn.silu(c)
    lower_bounds = jnp.cumsum(jax.nn.softmax(lb_logits.astype(jnp.float32), axis=0), axis=0)
    for layer in range(DEPTH):
        mod = c_act @ ada_w[layer] + ada_b[layer]
        shift1, scale1, gate1, shift2, scale2, gate2 = jnp.split(mod[:, None, :], 6, axis=-1)
        h = rms_norm(x, norm_mix[layer]) * (1.0 + scale1) + shift1
        j = layer // 2
        if layer % 2 == 0:
            y = mixer_conv_hgrn(h, w_in_ab[j], conv_w[j], hg_norm[j],
                                lower_bounds[layer], w_out_ab[j])
        else:
            y = mixer_stick_breaking(h, w_qkv[j], q_norm[j], k_norm[j], w_out_c[j])
        x = x + gate1 * y
        h = rms_norm(x, norm_mlp[layer]) * (1.0 + scale2) + shift2
        x = x + gate2 * (jnp.square(jax.nn.relu(h @ mlp_w1[layer])) @ mlp_w2[layer])
    return x


import jax as _jax
import jax.numpy as _jnp

TWIN_FORMAT = 'train_step'
FWD_PARAMS = ['x', 'c', 'ada_w', 'ada_b', 'norm_mix', 'norm_mlp', 'w_in_ab', 'conv_w', 'hg_norm', 'lb_logits', 'w_out_ab', 'w_qkv', 'q_norm', 'k_norm', 'w_out_c', 'mlp_w1', 'mlp_w2']
TWIN_WEIGHTS = ['ada_w', 'ada_b', 'norm_mix', 'norm_mlp', 'w_in_ab', 'conv_w', 'hg_norm', 'lb_logits', 'w_out_ab', 'w_qkv', 'q_norm', 'k_norm', 'w_out_c', 'mlp_w1', 'mlp_w2']
TWIN_DIFF_INPUT = 'x'
TWIN_INPUTS = ['x', 'c', 'ada_w', 'ada_b', 'norm_mix', 'norm_mlp', 'w_in_ab', 'conv_w', 'hg_norm', 'lb_logits', 'w_out_ab', 'w_qkv', 'q_norm', 'k_norm', 'w_out_c', 'mlp_w1', 'mlp_w2', 'loss_target', 'm_ada_w', 'm_ada_b', 'm_norm_mix', 'm_norm_mlp', 'm_w_in_ab', 'm_conv_w', 'm_hg_norm', 'm_lb_logits', 'm_w_out_ab', 'm_w_qkv', 'm_q_norm', 'm_k_norm', 'm_w_out_c', 'm_mlp_w1', 'm_mlp_w2', 'v_ada_w', 'v_ada_b', 'v_norm_mix', 'v_norm_mlp', 'v_w_in_ab', 'v_conv_w', 'v_hg_norm', 'v_lb_logits', 'v_w_out_ab', 'v_w_qkv', 'v_q_norm', 'v_k_norm', 'v_w_out_c', 'v_mlp_w1', 'v_mlp_w2']
TWIN_OUTPUTS = ['loss', 'grad_x', 'grad_ada_w', 'grad_ada_b', 'grad_norm_mix', 'grad_norm_mlp', 'grad_w_in_ab', 'grad_conv_w', 'grad_hg_norm', 'grad_lb_logits', 'grad_w_out_ab', 'grad_w_qkv', 'grad_q_norm', 'grad_k_norm', 'grad_w_out_c', 'grad_mlp_w1', 'grad_mlp_w2', 'delta_ada_w', 'delta_ada_b', 'delta_norm_mix', 'delta_norm_mlp', 'delta_w_in_ab', 'delta_conv_w', 'delta_hg_norm', 'delta_lb_logits', 'delta_w_out_ab', 'delta_w_qkv', 'delta_q_norm', 'delta_k_norm', 'delta_w_out_c', 'delta_mlp_w1', 'delta_mlp_w2', 'new_m_ada_w', 'new_m_ada_b', 'new_m_norm_mix', 'new_m_norm_mlp', 'new_m_w_in_ab', 'new_m_conv_w', 'new_m_hg_norm', 'new_m_lb_logits', 'new_m_w_out_ab', 'new_m_w_qkv', 'new_m_q_norm', 'new_m_k_norm', 'new_m_w_out_c', 'new_m_mlp_w1', 'new_m_mlp_w2', 'new_v_ada_w', 'new_v_ada_b', 'new_v_norm_mix', 'new_v_norm_mlp', 'new_v_w_in_ab', 'new_v_conv_w', 'new_v_hg_norm', 'new_v_lb_logits', 'new_v_w_out_ab', 'new_v_w_qkv', 'new_v_q_norm', 'new_v_k_norm', 'new_v_w_out_c', 'new_v_mlp_w1', 'new_v_mlp_w2']
TWIN_LEAF_KINDS = {'loss': 'loss', 'grad_x': 'grad_x', 'grad_ada_w': 'grad_w', 'grad_ada_b': 'grad_w', 'grad_norm_mix': 'grad_w', 'grad_norm_mlp': 'grad_w', 'grad_w_in_ab': 'grad_w', 'grad_conv_w': 'grad_w', 'grad_hg_norm': 'grad_w', 'grad_lb_logits': 'grad_w', 'grad_w_out_ab': 'grad_w', 'grad_w_qkv': 'grad_w', 'grad_q_norm': 'grad_w', 'grad_k_norm': 'grad_w', 'grad_w_out_c': 'grad_w', 'grad_mlp_w1': 'grad_w', 'grad_mlp_w2': 'grad_w', 'delta_ada_w': 'delta_w', 'delta_ada_b': 'delta_w', 'delta_norm_mix': 'delta_w', 'delta_norm_mlp': 'delta_w', 'delta_w_in_ab': 'delta_w', 'delta_conv_w': 'delta_w', 'delta_hg_norm': 'delta_w', 'delta_lb_logits': 'delta_w', 'delta_w_out_ab': 'delta_w', 'delta_w_qkv': 'delta_w', 'delta_q_norm': 'delta_w', 'delta_k_norm': 'delta_w', 'delta_w_out_c': 'delta_w', 'delta_mlp_w1': 'delta_w', 'delta_mlp_w2': 'delta_w', 'new_m_ada_w': 'new_m', 'new_m_ada_b': 'new_m', 'new_m_norm_mix': 'new_m', 'new_m_norm_mlp': 'new_m', 'new_m_w_in_ab': 'new_m', 'new_m_conv_w': 'new_m', 'new_m_hg_norm': 'new_m', 'new_m_lb_logits': 'new_m', 'new_m_w_out_ab': 'new_m', 'new_m_w_qkv': 'new_m', 'new_m_q_norm': 'new_m', 'new_m_k_norm': 'new_m', 'new_m_w_out_c': 'new_m', 'new_m_mlp_w1': 'new_m', 'new_m_mlp_w2': 'new_m', 'new_v_ada_w': 'new_v', 'new_v_ada_b': 'new_v', 'new_v_norm_mix': 'new_v', 'new_v_norm_mlp': 'new_v', 'new_v_w_in_ab': 'new_v', 'new_v_conv_w': 'new_v', 'new_v_hg_norm': 'new_v', 'new_v_lb_logits': 'new_v', 'new_v_w_out_ab': 'new_v', 'new_v_w_qkv': 'new_v', 'new_v_q_norm': 'new_v', 'new_v_k_norm': 'new_v', 'new_v_w_out_c': 'new_v', 'new_v_mlp_w1': 'new_v', 'new_v_mlp_w2': 'new_v'}


def _forward(args):
    return _fwd_reference(*[args[k] for k in FWD_PARAMS])


def _output_shape():
    out = _jax.eval_shape(lambda: _forward(_fwd_setup_inputs(0)))
    return out.shape, out.dtype

N_MICROBATCH = 1
ADAM_LR = 0.001
ADAM_B1 = 0.9
ADAM_B2 = 0.999
ADAM_EPS = 1e-08
ADAM_WD = 0.01
ADAM_STEP = 10
PER_EXAMPLE_BATCH_AXIS = {'x': 0, 'c': 0, 'loss_target': 0}
SHARED_INPUTS = []
_WEIGHT_DTYPES = {'ada_w': _jnp.float32, 'ada_b': _jnp.float32, 'norm_mix': _jnp.float32, 'norm_mlp': _jnp.float32, 'w_in_ab': _jnp.float32, 'conv_w': _jnp.float32, 'hg_norm': _jnp.float32, 'lb_logits': _jnp.float32, 'w_out_ab': _jnp.float32, 'w_qkv': _jnp.float32, 'q_norm': _jnp.float32, 'k_norm': _jnp.float32, 'w_out_c': _jnp.float32, 'mlp_w1': _jnp.float32, 'mlp_w2': _jnp.float32}
MOMENT_SCALE = {'ada_w': 3.845427e+00, 'ada_b': 8.119758e+00, 'norm_mix': 6.091785e+00, 'norm_mlp': 1.269733e+01, 'w_in_ab': 2.124512e-01, 'conv_w': 3.015619e+00, 'hg_norm': 1.358256e+00, 'lb_logits': 2.609565e-02, 'w_out_ab': 1.773820e-01, 'w_qkv': 3.822402e-01, 'q_norm': 3.121517e+00, 'k_norm': 3.107790e+00, 'w_out_c': 6.368866e-01, 'mlp_w1': 4.562637e-01, 'mlp_w2': 1.690484e+00}


def _to_microbatches(a, axis):
    t = _jnp.moveaxis(a, axis, 0)
    t = t.reshape((N_MICROBATCH, t.shape[0] // N_MICROBATCH) + t.shape[1:])
    return _jnp.moveaxis(t, 1, axis + 1)


def setup_inputs(seed: int = 0) -> dict:
    inp = _fwd_setup_inputs(seed)
    key = _jax.random.fold_in(_jax.random.key(seed), 7919)
    shape, _ = _output_shape()
    out = dict(inp)
    out["loss_target"] = _jax.random.normal(_jax.random.fold_in(key, 0), shape, _jnp.float32)
    for i, name in enumerate(TWIN_WEIGHTS):
        w = inp[name].astype(_jnp.float32)
        if MOMENT_SCALE is None:
            s = _jnp.sqrt(_jnp.mean(_jnp.square(w)) + 1e-30)
        else:
            s = MOMENT_SCALE[name]
        km, kv = _jax.random.split(_jax.random.fold_in(key, i + 1))
        out[name] = w
        out["m_" + name] = s * _jax.random.normal(km, w.shape, _jnp.float32)
        out["v_" + name] = (s * s) * _jax.random.uniform(kv, w.shape, _jnp.float32, 0.5, 1.5)
    if N_MICROBATCH > 1:
        for name, axis in PER_EXAMPLE_BATCH_AXIS.items():
            out[name] = _to_microbatches(out[name], axis)
    return {'x': out['x'], 'c': out['c'], 'ada_w': out['ada_w'], 'ada_b': out['ada_b'], 'norm_mix': out['norm_mix'], 'norm_mlp': out['norm_mlp'], 'w_in_ab': out['w_in_ab'], 'conv_w': out['conv_w'], 'hg_norm': out['hg_norm'], 'lb_logits': out['lb_logits'], 'w_out_ab': out['w_out_ab'], 'w_qkv': out['w_qkv'], 'q_norm': out['q_norm'], 'k_norm': out['k_norm'], 'w_out_c': out['w_out_c'], 'mlp_w1': out['mlp_w1'], 'mlp_w2': out['mlp_w2'], 'loss_target': out['loss_target'], 'm_ada_w': out['m_ada_w'], 'm_ada_b': out['m_ada_b'], 'm_norm_mix': out['m_norm_mix'], 'm_norm_mlp': out['m_norm_mlp'], 'm_w_in_ab': out['m_w_in_ab'], 'm_conv_w': out['m_conv_w'], 'm_hg_norm': out['m_hg_norm'], 'm_lb_logits': out['m_lb_logits'], 'm_w_out_ab': out['m_w_out_ab'], 'm_w_qkv': out['m_w_qkv'], 'm_q_norm': out['m_q_norm'], 'm_k_norm': out['m_k_norm'], 'm_w_out_c': out['m_w_out_c'], 'm_mlp_w1': out['m_mlp_w1'], 'm_mlp_w2': out['m_mlp_w2'], 'v_ada_w': out['v_ada_w'], 'v_ada_b': out['v_ada_b'], 'v_norm_mix': out['v_norm_mix'], 'v_norm_mlp': out['v_norm_mlp'], 'v_w_in_ab': out['v_w_in_ab'], 'v_conv_w': out['v_conv_w'], 'v_hg_norm': out['v_hg_norm'], 'v_lb_logits': out['v_lb_logits'], 'v_w_out_ab': out['v_w_out_ab'], 'v_w_qkv': out['v_w_qkv'], 'v_q_norm': out['v_q_norm'], 'v_k_norm': out['v_k_norm'], 'v_w_out_c': out['v_w_out_c'], 'v_mlp_w1': out['v_mlp_w1'], 'v_mlp_w2': out['v_mlp_w2']}


def _loss(weights, diff, rest, loss_target):
    with _jax.named_scope("forward"):
        args = {**rest, TWIN_DIFF_INPUT: diff, **{k: w.astype(_WEIGHT_DTYPES[k]) for k, w in weights.items()}}
        y = _forward(args)
    with _jax.named_scope("loss_head"):
        err = _jnp.square(y.astype(_jnp.float32) - loss_target)
        return 0.5 * _jnp.sum(_jnp.mean(err, axis=-1)) if err.ndim else 0.5 * err


def _adamw(w, g, m, v):
    m = ADAM_B1 * m + (1.0 - ADAM_B1) * g
    v = ADAM_B2 * v + (1.0 - ADAM_B2) * _jnp.square(g)
    m_hat = m / (1.0 - ADAM_B1 ** ADAM_STEP)
    v_hat = v / (1.0 - ADAM_B2 ** ADAM_STEP)
    delta = -ADAM_LR * (m_hat / (_jnp.sqrt(v_hat) + ADAM_EPS) + ADAM_WD * w)
    return delta, m, v


def reference(x, c, ada_w, ada_b, norm_mix, norm_mlp, w_in_ab, conv_w, hg_norm, lb_logits, w_out_ab, w_qkv, q_norm, k_norm, w_out_c, mlp_w1, mlp_w2, loss_target, m_ada_w, m_ada_b, m_norm_mix, m_norm_mlp, m_w_in_ab, m_conv_w, m_hg_norm, m_lb_logits, m_w_out_ab, m_w_qkv, m_q_norm, m_k_norm, m_w_out_c, m_mlp_w1, m_mlp_w2, v_ada_w, v_ada_b, v_norm_mix, v_norm_mlp, v_w_in_ab, v_conv_w, v_hg_norm, v_lb_logits, v_w_out_ab, v_w_qkv, v_q_norm, v_k_norm, v_w_out_c, v_mlp_w1, v_mlp_w2):
    given = dict(x=x, c=c, ada_w=ada_w, ada_b=ada_b, norm_mix=norm_mix, norm_mlp=norm_mlp, w_in_ab=w_in_ab, conv_w=conv_w, hg_norm=hg_norm, lb_logits=lb_logits, w_out_ab=w_out_ab, w_qkv=w_qkv, q_norm=q_norm, k_norm=k_norm, w_out_c=w_out_c, mlp_w1=mlp_w1, mlp_w2=mlp_w2, loss_target=loss_target, m_ada_w=m_ada_w, m_ada_b=m_ada_b, m_norm_mix=m_norm_mix, m_norm_mlp=m_norm_mlp, m_w_in_ab=m_w_in_ab, m_conv_w=m_conv_w, m_hg_norm=m_hg_norm, m_lb_logits=m_lb_logits, m_w_out_ab=m_w_out_ab, m_w_qkv=m_w_qkv, m_q_norm=m_q_norm, m_k_norm=m_k_norm, m_w_out_c=m_w_out_c, m_mlp_w1=m_mlp_w1, m_mlp_w2=m_mlp_w2, v_ada_w=v_ada_w, v_ada_b=v_ada_b, v_norm_mix=v_norm_mix, v_norm_mlp=v_norm_mlp, v_w_in_ab=v_w_in_ab, v_conv_w=v_conv_w, v_hg_norm=v_hg_norm, v_lb_logits=v_lb_logits, v_w_out_ab=v_w_out_ab, v_w_qkv=v_w_qkv, v_q_norm=v_q_norm, v_k_norm=v_k_norm, v_w_out_c=v_w_out_c, v_mlp_w1=v_mlp_w1, v_mlp_w2=v_mlp_w2)
    weights = {n: given[n] for n in TWIN_WEIGHTS}
    shared = {n: given[n] for n in SHARED_INPUTS}
    per_example = {n: given[n] for n in ['x', 'c']}
    grad_fn = _jax.value_and_grad(_loss, argnums=(0, 1))

    def one_microbatch(ex, loss_target):
        ex = dict(ex)
        diff = ex.pop(TWIN_DIFF_INPUT)
        return grad_fn(weights, diff, {**shared, **ex}, loss_target)

    if N_MICROBATCH == 1:
        loss, (grad_w, grad_x) = one_microbatch(per_example, given["loss_target"])
    else:
        def body(carry, xs):
            loss_sum, grad_sum = carry
            l_k, (gw_k, gx_k) = one_microbatch(xs[0], xs[1])
            with _jax.named_scope("update"):
                return (loss_sum + l_k, _jax.tree.map(_jnp.add, grad_sum, gw_k)), gx_k

        init = (_jnp.zeros((), _jnp.float32), _jax.tree.map(_jnp.zeros_like, weights))
        (loss, grad_w), grad_x = _jax.lax.scan(body, init, (per_example, given["loss_target"]))
    with _jax.named_scope("update"):
        delta_w, new_m, new_v = {}, {}, {}
        for n in TWIN_WEIGHTS:
            delta_w[n], new_m[n], new_v[n] = _adamw(weights[n], grad_w[n], given["m_" + n], given["v_" + n])
    return (loss, grad_x, *[grad_w[n] for n in TWIN_WEIGHTS], *[delta_w[n] for n in TWIN_WEIGHTS],
            *[new_m[n] for n in TWIN_WEIGHTS], *[new_v[n] for n in TWIN_WEIGHTS])
```

```python
import functools

import jax
import jax.numpy as jnp
from jax import lax
from jax.experimental import pallas as pl
from jax.experimental.pallas import tpu as pltpu

F32 = jnp.float32
BF16 = jnp.bfloat16
EPS = 1e-6
N_DEV = 8
MESH_AXES = ("x", "y", "c")

D_MODEL = 1024
CONV_DIM = 512
HG_HEADS = 4
HG_DK = 128
HG_WIDTH = 512
CHUNK = 64
HG_TILE = 128
HG_SUB = 16
HG_EXP_CLAMP = 60.0
SB_HEAD_DIM = 64
SB_SCALE = SB_HEAD_DIM ** -0.5
SB_BLOCK = 128
D_FF = 4096
AB_IN = 3584

ADAM_LR = 0.001
ADAM_B1 = 0.9
ADAM_B2 = 0.999
ADAM_EPS = 1e-08
ADAM_WD = 0.01
ADAM_STEP = 10

VMEM_LIMIT = 48 * 1024 * 1024
LANES = 128


def _cparams(n_grid):
    return pltpu.CompilerParams(dimension_semantics=("arbitrary",) * n_grid, vmem_limit_bytes=VMEM_LIMIT)


def _nt(a, b):
    return lax.dot_general(a, b, (((1,), (1,)), ((), ())), preferred_element_type=F32)


def _tn(a, b):
    return lax.dot_general(a, b, (((0,), (0,)), ((), ())), preferred_element_type=F32)


def _nn(a, b):
    return jnp.dot(a, b, preferred_element_type=F32)


def _split3(x):
    hi = x.astype(BF16)
    r1 = x - hi.astype(F32)
    mid = r1.astype(BF16)
    lo = (r1 - mid.astype(F32)).astype(BF16)
    return hi, mid, lo


def _exact_left(m01, x):
    hi, mid, lo = _split3(x)
    return _nn(m01, hi) + _nn(m01, mid) + _nn(m01, lo)


def _exact_right(x, m01):
    hi, mid, lo = _split3(x)
    return _nn(hi, m01) + _nn(mid, m01) + _nn(lo, m01)


def _exact_right2(x, m01):
    hi = x.astype(BF16)
    lo = (x - hi.astype(F32)).astype(BF16)
    return _nn(hi, m01) + _nn(lo, m01)


def _sp(x):
    hi = x.astype(BF16)
    return hi, (x - hi.astype(F32)).astype(BF16)


def _dot3(fn, a, b):
    return fn(a[0], b[0]) + fn(a[0], b[1]) + fn(a[1], b[0])


def _tile(pref, n):
    t = min(pref, n)
    assert n % t == 0, (pref, n)
    return t


def _all_gather(x, *, name, in_vmem):
    m_per, n = x.shape

    def body(x_ref, out_ref, send_sems, recv_sems, local_sem):
        mx, my, mc = lax.axis_index("x"), lax.axis_index("y"), lax.axis_index("c")
        me, sibling = (mx, my, mc), (mx, my, 1 - mc)
        chips = [(1 - mx, my), (mx, 1 - my), (1 - mx, 1 - my)]

        def rows(px, py, pc):
            return out_ref.at[pl.ds((4 * px + 2 * py + pc) * m_per, m_per), :]

        def copy(k, block, to, src=None):
            return pltpu.make_async_remote_copy(
                src_ref=rows(*block) if src is None else src, dst_ref=rows(*block),
                send_sem=send_sems.at[k], recv_sem=recv_sems.at[k],
                device_id=to, device_id_type=pl.DeviceIdType.MESH)

        mine = pltpu.make_async_copy(x_ref, rows(*me), local_sem)
        mine.start()
        first = [copy(0, me, sibling, src=x_ref)]
        first += [copy(1 + j, me, (*chip, mc), src=x_ref) for j, chip in enumerate(chips)]
        for cp in first:
            cp.start()
        passed = [copy(4 + j, (*chip, mc), sibling) for j, chip in enumerate(chips)]
        for j, chip in enumerate(chips):
            copy(1 + j, (*chip, mc), me).wait_recv()
            passed[j].start()
        copy(0, sibling, me).wait_recv()
        for j, chip in enumerate(chips):
            copy(4 + j, (*chip, 1 - mc), me).wait_recv()
        for cp in first + passed:
            cp.wait_send()
        mine.wait()

    space = pltpu.VMEM if in_vmem else pl.ANY
    return pl.pallas_call(
        body, name=name,
        out_shape=jax.ShapeDtypeStruct((N_DEV * m_per, n), x.dtype),
        in_specs=[pl.BlockSpec(memory_space=space)],
        out_specs=pl.BlockSpec(memory_space=space),
        scratch_shapes=[pltpu.SemaphoreType.DMA((7,)), pltpu.SemaphoreType.DMA((7,)), pltpu.SemaphoreType.DMA],
    )(x)


def _rs_sibling_exchange(g, *, name):
    _, r, n = g.shape

    def body(g_ref, out_ref, send_sems, recv_sems):
        mx, my, mc = lax.axis_index("x"), lax.axis_index("y"), lax.axis_index("c")
        copies = []
        for q in range(4):
            copies.append(pltpu.make_async_remote_copy(
                src_ref=g_ref.at[2 * q + (1 - mc)], dst_ref=out_ref.at[q],
                send_sem=send_sems.at[q], recv_sem=recv_sems.at[q],
                device_id=(mx, my, 1 - mc), device_id_type=pl.DeviceIdType.MESH))
        for cp in copies:
            cp.start()
        for cp in copies:
            cp.wait_recv()
        for cp in copies:
            cp.wait_send()

    return pl.pallas_call(
        body, name=name,
        out_shape=jax.ShapeDtypeStruct((4, r, n), g.dtype),
        in_specs=[pl.BlockSpec(memory_space=pl.ANY)],
        out_specs=pl.BlockSpec(memory_space=pl.ANY),
        scratch_shapes=[pltpu.SemaphoreType.DMA((4,)), pltpu.SemaphoreType.DMA((4,))],
    )(g)


def _rs_chip_exchange(t, *, name):
    _, r, n = t.shape

    def body(t_ref, out_ref, send_sems, recv_sems):
        mx, my, mc = lax.axis_index("x"), lax.axis_index("y"), lax.axis_index("c")
        chips = [(1 - mx, my), (mx, 1 - my), (1 - mx, 1 - my)]
        copies = []
        for k, (px, py) in enumerate(chips):
            copies.append(pltpu.make_async_remote_copy(
                src_ref=t_ref.at[2 * px + py], dst_ref=out_ref.at[k],
                send_sem=send_sems.at[k], recv_sem=recv_sems.at[k],
                device_id=(px, py, mc), device_id_type=pl.DeviceIdType.MESH))
        for cp in copies:
            cp.start()
        for cp in copies:
            cp.wait_recv()
        for cp in copies:
            cp.wait_send()

    return pl.pallas_call(
        body, name=name,
        out_shape=jax.ShapeDtypeStruct((3, r, n), t.dtype),
        in_specs=[pl.BlockSpec(memory_space=pl.ANY)],
        out_specs=pl.BlockSpec(memory_space=pl.ANY),
        scratch_shapes=[pltpu.SemaphoreType.DMA((3,)), pltpu.SemaphoreType.DMA((3,))],
    )(t)


def _rs_pair_sum(g, p1, my_c, *, name, tr=448):
    _, r, n = g.shape
    tr = _tile(tr, r)

    def body(c_ref, g_ref, p_ref, o_ref):
        o_ref[...] = g_ref[...] + p_ref[...]

    return pl.pallas_call(
        body, name=name,
        out_shape=jax.ShapeDtypeStruct((4, r, n), F32),
        grid_spec=pltpu.PrefetchScalarGridSpec(
            num_scalar_prefetch=1, grid=(4, r // tr),
            in_specs=[pl.BlockSpec((None, tr, n), lambda q, i, c: (2 * q + c[0], i, 0)),
                      pl.BlockSpec((None, tr, n), lambda q, i, c: (q, i, 0))],
            out_specs=pl.BlockSpec((None, tr, n), lambda q, i, c: (q, i, 0))),
        compiler_params=_cparams(2),
    )(my_c, g, p1)


def _rs_final_sum(t, p3, my_q, *, name, tr=448):
    _, r, n = t.shape
    tr = _tile(tr, r)

    def body(q_ref, t_ref, a_ref, b_ref, c_ref, o_ref):
        o_ref[...] = ((t_ref[...] + a_ref[...]) + b_ref[...]) + c_ref[...]

    return pl.pallas_call(
        body, name=name,
        out_shape=jax.ShapeDtypeStruct((r, n), F32),
        grid_spec=pltpu.PrefetchScalarGridSpec(
            num_scalar_prefetch=1, grid=(r // tr,),
            in_specs=[pl.BlockSpec((None, tr, n), lambda i, q: (q[0], i, 0)),
                      pl.BlockSpec((None, tr, n), lambda i, q: (0, i, 0)),
                      pl.BlockSpec((None, tr, n), lambda i, q: (1, i, 0)),
                      pl.BlockSpec((None, tr, n), lambda i, q: (2, i, 0))],
            out_specs=pl.BlockSpec((tr, n), lambda i, q: (i, 0))),
        compiler_params=_cparams(1),
    )(my_q, t, p3, p3, p3)


def _matmul(a, b, *, name, ta=False, tb=False, epi="plain", extras=(), out_dtype=F32, tm=512, tn=512, tk=1024):
    if ta:
        kdim, m = a.shape
    else:
        m, kdim = a.shape
    if tb:
        n, kb = b.shape
    else:
        kb, n = b.shape
    assert kdim == kb, (a.shape, b.shape)
    tm, tn, tk = _tile(tm, m), _tile(tn, n), _tile(tk, kdim)
    nk = kdim // tk
    a_spec = pl.BlockSpec((tk, tm), lambda i, j, k: (k, i)) if ta else pl.BlockSpec((tm, tk), lambda i, j, k: (i, k))
    b_spec = pl.BlockSpec((tn, tk), lambda i, j, k: (j, k)) if tb else pl.BlockSpec((tk, tn), lambda i, j, k: (k, j))
    dims = (((0 if ta else 1,), (1 if tb else 0,)), ((), ()))
    mn_spec = pl.BlockSpec((tm, tn), lambda i, j, k: (i, j))
    row_spec = pl.BlockSpec((1, tn), lambda i, j, k: (0, j))
    if epi == "resgate":
        extra_specs = [mn_spec, row_spec]
        out_shape = (jax.ShapeDtypeStruct((m, n), F32), jax.ShapeDtypeStruct((m, n), F32))
        out_specs = (mn_spec, mn_spec)
    elif epi == "dact":
        extra_specs = [mn_spec]
        out_shape = jax.ShapeDtypeStruct((m, n), out_dtype)
        out_specs = mn_spec
    else:
        extra_specs = []
        out_shape = jax.ShapeDtypeStruct((m, n), out_dtype)
        out_specs = mn_spec
    n_extra = len(extra_specs)

    def body(a_ref, b_ref, *rest):
        ex = rest[:n_extra]
        outs = rest[n_extra:-1]
        acc = rest[-1]
        k = pl.program_id(2)

        @pl.when(k == 0)
        def _():
            acc[...] = jnp.zeros_like(acc)

        acc[...] += lax.dot_general(a_ref[...].astype(BF16), b_ref[...].astype(BF16), dims,
                                    preferred_element_type=F32)

        @pl.when(k == nk - 1)
        def _():
            r = acc[...]
            if epi == "plain":
                outs[0][...] = r.astype(outs[0].dtype)
            elif epi == "resgate":
                outs[0][...] = r
                outs[1][...] = ex[0][...] + ex[1][...] * r
            elif epi == "relu2":
                p = jnp.maximum(r, 0.0)
                outs[0][...] = (p * p).astype(outs[0].dtype)
            elif epi == "dact":
                outs[0][...] = (r * (2.0 * jnp.sqrt(ex[0][...].astype(F32)))).astype(outs[0].dtype)

    return pl.pallas_call(
        body, name=name, out_shape=out_shape, grid=(m // tm, n // tn, nk),
        in_specs=[a_spec, b_spec] + extra_specs, out_specs=out_specs,
        scratch_shapes=[pltpu.VMEM((tm, tn), F32)],
        compiler_params=_cparams(3),
    )(a, b, *extras)


def _norm_mod(x, g, scale, shift, *, name, tm=512):
    s, d = x.shape
    tm = _tile(tm, s)

    def body(x_ref, g_ref, sc_ref, sh_ref, h_ref):
        xv = x_ref[...]
        r = lax.rsqrt(jnp.mean(xv * xv, axis=-1, keepdims=True) + EPS)
        h_ref[...] = (((xv * r) * g_ref[...]) * (1.0 + sc_ref[...]) + sh_ref[...]).astype(BF16)

    row = pl.BlockSpec((1, d), lambda i: (0, 0))
    return pl.pallas_call(
        body, name=name, out_shape=jax.ShapeDtypeStruct((s, d), BF16), grid=(s // tm,),
        in_specs=[pl.BlockSpec((tm, d), lambda i: (i, 0)), row, row, row],
        out_specs=pl.BlockSpec((tm, d), lambda i: (i, 0)),
        compiler_params=_cparams(1),
    )(x, g, scale, shift)


def _norm_mod_bwd(x, dh, dres, g, scale, *, name, tm=512):
    s, d = x.shape
    tm = _tile(tm, s)

    def body(x_ref, dh_ref, dr_ref, g_ref, sc_ref, dx_ref, acc_ref):
        i = pl.program_id(0)

        @pl.when(i == 0)
        def _():
            acc_ref[...] = jnp.zeros_like(acc_ref)

        xv = x_ref[...]
        dhv = dh_ref[...]
        gv = g_ref[...]
        one_sc = 1.0 + sc_ref[...]
        r = lax.rsqrt(jnp.mean(xv * xv, axis=-1, keepdims=True) + EPS)
        xn = xv * r
        dxn = dhv * (gv * one_sc)
        dx_ref[...] = dr_ref[...] + r * (dxn - xn * jnp.mean(dxn * xn, axis=-1, keepdims=True))
        dhxn = dhv * xn
        acc_ref[0:1, :] += jnp.sum(dhv, axis=0, keepdims=True)
        acc_ref[1:2, :] += jnp.sum(dhxn * gv, axis=0, keepdims=True)
        acc_ref[2:3, :] += jnp.sum(dhxn * one_sc, axis=0, keepdims=True)

    row = pl.BlockSpec((1, d), lambda i: (0, 0))
    blk = pl.BlockSpec((tm, d), lambda i: (i, 0))
    return pl.pallas_call(
        body, name=name,
        out_shape=(jax.ShapeDtypeStruct((s, d), F32), jax.ShapeDtypeStruct((8, d), F32)),
        grid=(s // tm,), in_specs=[blk, blk, blk, row, row],
        out_specs=(blk, pl.BlockSpec((8, d), lambda i: (0, 0))),
        compiler_params=_cparams(1),
    )(x, dh, dres, g, scale)


def _gate_bwd(dx, z, gate, *, name, tm=512):
    s, d = dx.shape
    tm = _tile(tm, s)

    def body(dx_ref, z_ref, g_ref, dz_ref, acc_ref):
        i = pl.program_id(0)

        @pl.when(i == 0)
        def _():
            acc_ref[...] = jnp.zeros_like(acc_ref)

        dxv = dx_ref[...]
        dz_ref[...] = (dxv * g_ref[...]).astype(BF16)
        acc_ref[0:1, :] += jnp.sum(dxv * z_ref[...], axis=0, keepdims=True)

    blk = pl.BlockSpec((tm, d), lambda i: (i, 0))
    return pl.pallas_call(
        body, name=name,
        out_shape=(jax.ShapeDtypeStruct((s, d), BF16), jax.ShapeDtypeStruct((8, d), F32)),
        grid=(s // tm,), in_specs=[blk, blk, pl.BlockSpec((1, d), lambda i: (0, 0))],
        out_specs=(blk, pl.BlockSpec((8, d), lambda i: (0, 0))),
        compiler_params=_cparams(1),
    )(dx, z, gate)


def _loss_grad(xf, target, *, name, tm=512):
    s, d = xf.shape
    tm = _tile(tm, s)
    nt = s // tm

    def body(x_ref, t_ref, dx_ref, loss_ref, acc_ref):
        i = pl.program_id(0)

        @pl.when(i == 0)
        def _():
            acc_ref[...] = jnp.zeros_like(acc_ref)

        e = x_ref[...] - t_ref[...]
        dx_ref[...] = e * (1.0 / d)
        acc_ref[...] += jnp.sum(e * e, axis=0, keepdims=True)

        @pl.when(i == nt - 1)
        def _():
            loss_ref[...] = (0.5 / d) * jnp.sum(acc_ref[...], axis=1, keepdims=True)

    blk = pl.BlockSpec((tm, d), lambda i: (i, 0))
    return pl.pallas_call(
        body, name=name,
        out_shape=(jax.ShapeDtypeStruct((s, d), F32), jax.ShapeDtypeStruct((1, 1), F32)),
        grid=(nt,), in_specs=[blk, blk],
        out_specs=(blk, pl.BlockSpec((1, 1), lambda i: (0, 0))),
        scratch_shapes=[pltpu.VMEM((1, d), F32)],
        compiler_params=_cparams(1),
    )(xf, target)


def _shift_down(p, prev, k):
    tm = p.shape[0]
    row = lax.broadcasted_iota(jnp.int32, p.shape, 0)
    out = pltpu.roll(p, k, 0)
    for j in range(k):
        out = jnp.where(row == j, prev[8 - k + j:8 - k + j + 1, :], out)
    return out


def _shift_up(p, nxt, k):
    tm = p.shape[0]
    row = lax.broadcasted_iota(jnp.int32, p.shape, 0)
    out = pltpu.roll(p, tm - k, 0)
    for j in range(k):
        out = jnp.where(row == tm - k + j, nxt[j:j + 1, :], out)
    return out


def _conv_fwd(u, w, *, name, tm=512):
    s = u.shape[0]
    tm = _tile(tm, s)
    c = CONV_DIM

    def body(ab_ref, ac_ref, ah_ref, w_ref, y_ref, carry_ref):
        i = pl.program_id(0)

        @pl.when(i == 0)
        def _():
            carry_ref[...] = jnp.zeros_like(carry_ref)

        p = ac_ref[...] * ah_ref[...]
        prev = carry_ref[...]
        wv = w_ref[...]
        conv = wv[2:3, :] * p + wv[1:2, :] * _shift_down(p, prev, 1) + wv[0:1, :] * _shift_down(p, prev, 2)
        y_ref[...] = (ab_ref[...] * conv).astype(BF16)
        carry_ref[...] = p[tm - 8:tm, :]

    return pl.pallas_call(
        body, name=name, out_shape=jax.ShapeDtypeStruct((s, c), BF16), grid=(s // tm,),
        in_specs=[pl.BlockSpec((tm, c), lambda i: (i, 0)), pl.BlockSpec((tm, c), lambda i: (i, 1)),
                  pl.BlockSpec((tm, c), lambda i: (i, 2)), pl.BlockSpec((3, c), lambda i: (0, 0))],
        out_specs=pl.BlockSpec((tm, c), lambda i: (i, 0)),
        scratch_shapes=[pltpu.VMEM((8, c), F32)],
        compiler_params=_cparams(1),
    )(u, u, u, w)


def _conv_bwd(u, dy, w, *, name, tm=512):
    s = u.shape[0]
    tm = _tile(tm, s)
    nt = s // tm
    c = CONV_DIM
    hb = tm // 8

    def body(ab_ref, ac_ref, ah_ref, hc_ref, hh_ref, dy_ref, w_ref, du_ref, dw_ref, carry_ref):
        i = pl.program_id(0)

        @pl.when(i == 0)
        def _():
            carry_ref[...] = jnp.zeros_like(carry_ref)
            dw_ref[...] = jnp.zeros_like(dw_ref)

        first_tile = (nt - 1 - i) == 0
        ab, ac, ah = ab_ref[...], ac_ref[...], ah_ref[...]
        p = ac * ah
        prev = jnp.where(first_tile, 0.0, hc_ref[...] * hh_ref[...])
        wv = w_ref[...]
        p1 = _shift_down(p, prev, 1)
        p2 = _shift_down(p, prev, 2)
        conv = wv[2:3, :] * p + wv[1:2, :] * p1 + wv[0:1, :] * p2
        dyv = dy_ref[...]
        dconv = dyv * ab
        nxt = carry_ref[...]
        dp = wv[2:3, :] * dconv + wv[1:2, :] * _shift_up(dconv, nxt, 1) + wv[0:1, :] * _shift_up(dconv, nxt, 2)
        du_ref[:, 0:c] = (dyv * conv).astype(BF16)
        du_ref[:, c:2 * c] = (dp * ah).astype(BF16)
        du_ref[:, 2 * c:3 * c] = (dp * ac).astype(BF16)
        dw_ref[0:1, :] += jnp.sum(dconv * p2, axis=0, keepdims=True)
        dw_ref[1:2, :] += jnp.sum(dconv * p1, axis=0, keepdims=True)
        dw_ref[2:3, :] += jnp.sum(dconv * p, axis=0, keepdims=True)
        carry_ref[...] = dconv[0:8, :]

    rev = lambda i: nt - 1 - i
    halo = lambda i: jnp.maximum(rev(i) * hb - 1, 0)
    return pl.pallas_call(
        body, name=name,
        out_shape=(jax.ShapeDtypeStruct((s, 3 * c), BF16), jax.ShapeDtypeStruct((8, c), F32)),
        grid=(nt,),
        in_specs=[pl.BlockSpec((tm, c), lambda i: (rev(i), 0)), pl.BlockSpec((tm, c), lambda i: (rev(i), 1)),
                  pl.BlockSpec((tm, c), lambda i: (rev(i), 2)),
                  pl.BlockSpec((8, c), lambda i: (halo(i), 1)), pl.BlockSpec((8, c), lambda i: (halo(i), 2)),
                  pl.BlockSpec((tm, c), lambda i: (rev(i), 0)), pl.BlockSpec((3, c), lambda i: (0, 0))],
        out_specs=(pl.BlockSpec((tm, 3 * c), lambda i: (rev(i), 0)), pl.BlockSpec((8, c), lambda i: (0, 0))),
        scratch_shapes=[pltpu.VMEM((8, c), F32)],
        compiler_params=_cparams(1),
    )(u, u, u, u, u, dy, w)


def _lower_bound(lbl):
    m = jnp.max(lbl, axis=0, keepdims=True)
    e = jnp.exp(lbl - m)
    return e[0:1, :] / jnp.sum(e, axis=0, keepdims=True)


def _hg_masks():
    t = HG_TILE
    row = lax.broadcasted_iota(jnp.int32, (t, t), 0)
    col = lax.broadcasted_iota(jnp.int32, (t, t), 1)
    same = (row >= CHUNK) == (col >= CHUNK)
    lower = same & (col <= row)
    upper = same & (row <= col)
    return row, col, lower, upper


def _hg_gates(hf, lb):
    sig = jax.nn.sigmoid(hf)
    f = lb + (1.0 - lb) * sig
    return sig, f, jnp.log(f), 1.0 - f


def _hg_refs(b_ref, hs):
    refs = []
    for i in range(HG_TILE // HG_SUB):
        if (i * HG_SUB) % CHUNK == 0:
            refs.append(jnp.zeros((1, HG_DK), F32))
        else:
            refs.append(b_ref[i * HG_SUB - 1:i * HG_SUB, hs])
    return refs


def _hgrn_fwd(u, lbl, gn, *, name):
    s = u.shape[0]
    t = HG_TILE
    nt = s // t
    nsub = t // HG_SUB
    w = HG_WIDTH

    def body(hq_ref, hf_ref, hi_ref, hg_ref, lbl_ref, gn_ref, y_ref, o_ref, sall_ref, st_ref, b_ref):
        i = pl.program_id(0)

        @pl.when(i == 0)
        def _():
            st_ref[...] = jnp.zeros_like(st_ref)

        lb = _lower_bound(lbl_ref[...])
        _, _, g, kin = _hg_gates(hf_ref[...], lb)
        _, _, lower, _ = _hg_masks()
        b_ref[...] = _exact_left(lower.astype(BF16), g)

        for h in range(HG_HEADS):
            hs = slice(h * HG_DK, (h + 1) * HG_DK)
            bh = b_ref[:, hs]
            qh = hq_ref[:, hs]
            kh = kin[:, hs]
            vh = hi_ref[:, hs]
            vsp = _sp(vh)
            refs = _hg_refs(b_ref, hs)
            rmat = jnp.concatenate([jnp.broadcast_to(r, (HG_SUB, HG_DK)) for r in refs], axis=0)
            qt = qh * jnp.exp(bh - rmat)
            prow = []
            for j in range(nsub):
                kj = kh * jnp.exp(jnp.minimum(refs[j] - bh, HG_EXP_CLAMP))
                prow.append(_dot3(_nt, _sp(qt[j * HG_SUB:(j + 1) * HG_SUB]), _sp(kj)))
            p = jnp.where(lower, jnp.concatenate(prow, axis=0), 0.0)
            intra = _dot3(_nn, _sp(p), vsp)
            o_parts = []
            for c in range(t // CHUNK):
                rs = slice(c * CHUNK, (c + 1) * CHUNK)
                st0 = st_ref[hs, :]
                sall_ref[c * w + h * HG_DK:c * w + (h + 1) * HG_DK, :] = st0
                bl = b_ref[c * CHUNK + CHUNK - 1:c * CHUNK + CHUNK, hs]
                qf = qh[rs] * jnp.exp(bh[rs])
                o_parts.append(_dot3(_nt, _sp(qf), _sp(st0)) + intra[rs])
                khat = kh[rs] * jnp.exp(bl - bh[rs])
                st_ref[hs, :] = st0 * jnp.exp(bl) + _dot3(_tn, _sp(vh[rs]), _sp(khat))
            o = jnp.concatenate(o_parts, axis=0)
            o_ref[:, hs] = o
            r = lax.rsqrt(jnp.mean(o * o, axis=-1, keepdims=True) + EPS)
            hg = hg_ref[:, hs]
            y_ref[:, hs] = (((o * r) * gn_ref[:, hs]) * (hg * jax.nn.sigmoid(hg))).astype(BF16)

    blk = lambda j: pl.BlockSpec((t, w), lambda i, j=j: (i, j))
    srows = (t // CHUNK) * w
    return pl.pallas_call(
        body, name=name,
        out_shape=(jax.ShapeDtypeStruct((s, w), BF16), jax.ShapeDtypeStruct((s, w), F32),
                   jax.ShapeDtypeStruct((nt * srows, HG_DK), F32)),
        grid=(nt,),
        in_specs=[blk(3), blk(4), blk(5), blk(6), pl.BlockSpec((3, w), lambda i: (0, 0)),
                  pl.BlockSpec((1, w), lambda i: (0, 0))],
        out_specs=(pl.BlockSpec((t, w), lambda i: (i, 0)), pl.BlockSpec((t, w), lambda i: (i, 0)),
                   pl.BlockSpec((srows, HG_DK), lambda i: (i, 0))),
        scratch_shapes=[pltpu.VMEM((w, HG_DK), F32), pltpu.VMEM((t, w), F32)],
        compiler_params=_cparams(1),
    )(u, u, u, u, lbl, gn)


def _hgrn_bwd(u, o_all, sall, dy, lbl, gn, *, name):
    s = u.shape[0]
    t = HG_TILE
    nt = s // t
    nsub = t // HG_SUB
    w = HG_WIDTH
    nch = t // CHUNK

    def body(hq_ref, hf_ref, hi_ref, hg_ref, o_ref, sall_ref, dy_ref, lbl_ref, gn_ref,
             du_ref, acc_ref, dst_ref, b_ref):
        i = pl.program_id(0)

        @pl.when(i == 0)
        def _():
            dst_ref[...] = jnp.zeros_like(dst_ref)
            acc_ref[...] = jnp.zeros_like(acc_ref)

        lb = _lower_bound(lbl_ref[...])
        sig, f, g, kin = _hg_gates(hf_ref[...], lb)
        row, col, lower, upper = _hg_masks()
        b_ref[...] = _exact_left(lower.astype(BF16), g)
        upper_bf = upper.astype(BF16)
        rowblk = [((row >= j * HG_SUB) & (row < (j + 1) * HG_SUB)) for j in range(nsub)]
        colblk = [((col >= j * HG_SUB) & (col < (j + 1) * HG_SUB)) for j in range(nsub)]
        row1 = lax.broadcasted_iota(jnp.int32, (t, HG_DK), 0)

        for h in range(HG_HEADS):
            hs = slice(h * HG_DK, (h + 1) * HG_DK)
            bh = b_ref[:, hs]
            qh = hq_ref[:, hs]
            kh = kin[:, hs]
            vh = hi_ref[:, hs]
            vsp = _sp(vh)
            hg = hg_ref[:, hs]
            gnh = gn_ref[:, hs]
            o = o_ref[:, hs]
            dyv = dy_ref[:, hs]
            sg = jax.nn.sigmoid(hg)
            r = lax.rsqrt(jnp.mean(o * o, axis=-1, keepdims=True) + EPS)
            ohat = o * r
            du_ref[:, 3 * w + h * HG_DK:3 * w + (h + 1) * HG_DK] = (
                dyv * (ohat * gnh) * (sg * (1.0 + hg * (1.0 - sg)))).astype(BF16)
            don = dyv * (hg * sg)
            acc_ref[0:1, hs] += jnp.sum(don * ohat, axis=0, keepdims=True)
            dohat = don * gnh
            do = r * (dohat - ohat * jnp.mean(dohat * ohat, axis=-1, keepdims=True))
            dosp = _sp(do)
            refs = _hg_refs(b_ref, hs)
            rmat = jnp.concatenate([jnp.broadcast_to(rr, (HG_SUB, HG_DK)) for rr in refs], axis=0)
            eq = jnp.exp(bh - rmat)
            qt = qh * eq
            qtsp = _sp(qt)
            dp = jnp.where(lower, _dot3(_nt, dosp, vsp), 0.0)
            dpt = jnp.where(upper, _dot3(_nt, vsp, dosp), 0.0)
            pt = jnp.zeros((t, t), F32)
            dk = jnp.zeros((t, HG_DK), F32)
            dq_rows = []
            for j in range(nsub):
                ek = jnp.exp(jnp.minimum(refs[j] - bh, HG_EXP_CLAMP))
                kjsp = _sp(kh * ek)
                pt = pt + _dot3(_nt, kjsp, _sp(jnp.where(rowblk[j], qt, 0.0)))
                dq_rows.append(_dot3(_nn, _sp(dp[j * HG_SUB:(j + 1) * HG_SUB]), kjsp))
                dk = dk + ek * _dot3(_nn, _sp(jnp.where(colblk[j], dpt, 0.0)), qtsp)
            pt = jnp.where(upper, pt, 0.0)
            dv = _dot3(_nn, _sp(pt), dosp)
            dq = jnp.concatenate(dq_rows, axis=0) * eq
            dq_c, dk_c, dv_c, ex_c = [None] * nch, [None] * nch, [None] * nch, [None] * nch
            for c in reversed(range(nch)):
                rs = slice(c * CHUNK, (c + 1) * CHUNK)
                st0 = sall_ref[c * w + h * HG_DK:c * w + (h + 1) * HG_DK, :]
                dst1 = dst_ref[hs, :]
                dst1sp = _sp(dst1)
                dosp_c = _sp(do[rs])
                bl = b_ref[c * CHUNK + CHUNK - 1:c * CHUNK + CHUNK, hs]
                e = jnp.exp(bh[rs])
                el = jnp.exp(bl)
                ekl = jnp.exp(bl - bh[rs])
                dq_c[c] = _dot3(_nn, dosp_c, _sp(st0)) * e
                khat = kh[rs] * ekl
                dv_c[c] = _dot3(_nt, _sp(khat), dst1sp)
                dkhat = _dot3(_nn, _sp(vh[rs]), dst1sp)
                dk_c[c] = dkhat * ekl
                ex_c[c] = (jnp.sum(dkhat * khat, axis=0, keepdims=True)
                           + el * jnp.sum(dst1 * st0, axis=0, keepdims=True))
                dst_ref[hs, :] = _dot3(_tn, dosp_c, _sp(qh[rs] * e)) + dst1 * el
            dq = dq + jnp.concatenate(dq_c, axis=0)
            dk = dk + jnp.concatenate(dk_c, axis=0)
            dv = dv + jnp.concatenate(dv_c, axis=0)
            db = qh * dq - kh * dk
            for c in range(nch):
                db = db + jnp.where(row1 == c * CHUNK + CHUNK - 1, ex_c[c], 0.0)
            dg = _exact_left(upper_bf, db)
            fh = f[:, hs]
            sgf = sig[:, hs]
            lbh = lb[:, hs]
            df = dg / fh - dk
            du_ref[:, hs] = dq.astype(BF16)
            du_ref[:, w + h * HG_DK:w + (h + 1) * HG_DK] = (df * (1.0 - lbh) * sgf * (1.0 - sgf)).astype(BF16)
            du_ref[:, 2 * w + h * HG_DK:2 * w + (h + 1) * HG_DK] = dv.astype(BF16)
            acc_ref[1:2, hs] += jnp.sum(df * (1.0 - sgf), axis=0, keepdims=True)

    rev = lambda i: nt - 1 - i
    blk = lambda j: pl.BlockSpec((t, w), lambda i, j=j: (rev(i), j))
    srows = nch * w
    return pl.pallas_call(
        body, name=name,
        out_shape=(jax.ShapeDtypeStruct((s, 4 * w), BF16), jax.ShapeDtypeStruct((8, w), F32)),
        grid=(nt,),
        in_specs=[blk(3), blk(4), blk(5), blk(6), pl.BlockSpec((t, w), lambda i: (rev(i), 0)),
                  pl.BlockSpec((srows, HG_DK), lambda i: (rev(i), 0)),
                  pl.BlockSpec((t, w), lambda i: (rev(i), 1)),
                  pl.BlockSpec((3, w), lambda i: (0, 0)), pl.BlockSpec((1, w), lambda i: (0, 0))],
        out_specs=(pl.BlockSpec((t, 4 * w), lambda i: (rev(i), 0)), pl.BlockSpec((8, w), lambda i: (0, 0))),
        scratch_shapes=[pltpu.VMEM((w, HG_DK), F32), pltpu.VMEM((t, w), F32)],
        compiler_params=_cparams(1),
    )(u, u, u, u, o_all, sall, dy, lbl, gn)


def _pair_matrix():
    row = lax.broadcasted_iota(jnp.int32, (LANES, LANES), 0)
    col = lax.broadcasted_iota(jnp.int32, (LANES, LANES), 1)
    return ((row >= SB_HEAD_DIM) == (col >= SB_HEAD_DIM)).astype(BF16)


def _qk_norm_fwd(qkv, qn, kn, *, name, tm=256):
    s = qkv.shape[0]
    d = D_MODEL
    tm = _tile(tm, s)

    def body(q_ref, k_ref, v_ref, qn_ref, kn_ref, qo_ref, ko_ref, vo_ref):
        bd = _pair_matrix()
        for src, gain, dst in ((q_ref, qn_ref, qo_ref), (k_ref, kn_ref, ko_ref)):
            for grp in range(d // LANES):
                ls = slice(grp * LANES, (grp + 1) * LANES)
                xv = src[:, ls]
                ms = _exact_right(xv * xv, bd) * (1.0 / SB_HEAD_DIM)
                dst[:, ls] = ((xv * lax.rsqrt(ms + EPS)) * gain[:, ls]).astype(BF16)
        vo_ref[...] = v_ref[...].astype(BF16)

    blk = lambda j: pl.BlockSpec((tm, d), lambda i, j=j: (i, j))
    row = pl.BlockSpec((1, d), lambda i: (0, 0))
    out = jax.ShapeDtypeStruct((s, d), BF16)
    return pl.pallas_call(
        body, name=name, out_shape=(out, out, out), grid=(s // tm,),
        in_specs=[blk(0), blk(1), blk(2), row, row],
        out_specs=(blk(0), blk(0), blk(0)),
        compiler_params=_cparams(1),
    )(qkv, qkv, qkv, qn, kn)


def _qk_norm_bwd(qkv, dqn, dkn, dv, qn, kn, *, name, tm=256):
    s = qkv.shape[0]
    d = D_MODEL
    tm = _tile(tm, s)

    def body(q_ref, k_ref, dq_ref, dk_ref, dv_ref, qn_ref, kn_ref, o_ref, acc_ref):
        i = pl.program_id(0)

        @pl.when(i == 0)
        def _():
            acc_ref[...] = jnp.zeros_like(acc_ref)

        bd = _pair_matrix()
        for idx, (src, dsrc, gain) in enumerate(((q_ref, dq_ref, qn_ref), (k_ref, dk_ref, kn_ref))):
            for grp in range(d // LANES):
                ls = slice(grp * LANES, (grp + 1) * LANES)
                xv = src[:, ls]
                dyv = dsrc[:, ls]
                r = lax.rsqrt(_exact_right(xv * xv, bd) * (1.0 / SB_HEAD_DIM) + EPS)
                xh = xv * r
                acc_ref[idx:idx + 1, ls] += jnp.sum(dyv * xh, axis=0, keepdims=True)
                dxh = dyv * gain[:, ls]
                mean = _exact_right(dxh * xh, bd) * (1.0 / SB_HEAD_DIM)
                o_ref[:, idx * d + grp * LANES:idx * d + (grp + 1) * LANES] = (r * (dxh - xh * mean)).astype(BF16)
        o_ref[:, 2 * d:3 * d] = dv_ref[...].astype(BF16)

    blk = lambda j: pl.BlockSpec((tm, d), lambda i, j=j: (i, j))
    row = pl.BlockSpec((1, d), lambda i: (0, 0))
    return pl.pallas_call(
        body, name=name,
        out_shape=(jax.ShapeDtypeStruct((s, 3 * d), BF16), jax.ShapeDtypeStruct((8, d), F32)),
        grid=(s // tm,),
        in_specs=[blk(0), blk(1), blk(0), blk(0), blk(0), row, row],
        out_specs=(pl.BlockSpec((tm, 3 * d), lambda i: (i, 0)), pl.BlockSpec((8, d), lambda i: (0, 0))),
        compiler_params=_cparams(1),
    )(qkv, qkv, dqn, dkn, dv, qn, kn)


def _sb_scores(qh, kb, qi, j, suffix_ones, run):
    tq = SB_BLOCK
    z = _nt(qh, kb) * SB_SCALE
    qpos = qi * tq + lax.broadcasted_iota(jnp.int32, (tq, tq), 0)
    kpos = j * tq + lax.broadcasted_iota(jnp.int32, (tq, tq), 1)
    mask = kpos < qpos
    sp = jnp.log(1.0 + jnp.exp(-jnp.abs(z)))
    logb = jnp.minimum(z, 0.0) - sp
    l1m = jnp.where(mask, jnp.minimum(-z, 0.0) - sp, 0.0)
    later = _exact_right2(l1m, suffix_ones) + run
    wgt = jnp.where(mask, jnp.exp(logb + later), 0.0)
    return logb, l1m, wgt, mask


def _suffix_ones():
    row = lax.broadcasted_iota(jnp.int32, (SB_BLOCK, SB_BLOCK), 0)
    col = lax.broadcasted_iota(jnp.int32, (SB_BLOCK, SB_BLOCK), 1)
    return (row > col).astype(BF16)


def _sb_fwd(qn, kn, v, *, name):
    s, d = qn.shape
    tq = SB_BLOCK
    nq = s // tq

    def body(q_ref, k_ref, v_ref, o_ref, acc_ref):
        qi = pl.program_id(1)
        lane = lax.broadcasted_iota(jnp.int32, (tq, LANES), 1)
        first = lane < SB_HEAD_DIM
        q = q_ref[...]
        qh = [jnp.where(first, q, 0).astype(BF16), jnp.where(first, 0, q).astype(BF16)]
        ones = _suffix_ones()
        acc_ref[...] = jnp.zeros_like(acc_ref)

        def step(it, runs):
            j = qi - it
            kb = k_ref[pl.ds(pl.multiple_of(j * tq, tq), tq), :]
            vb = v_ref[pl.ds(pl.multiple_of(j * tq, tq), tq), :]
            new_runs = []
            for hh in range(2):
                _, l1m, wgt, _ = _sb_scores(qh[hh], kb, qi, j, ones, runs[hh])
                acc_ref[hh] += _exact_right2(wgt, vb)
                new_runs.append(runs[hh] + jnp.sum(l1m, axis=1, keepdims=True))
            return tuple(new_runs)

        zero = jnp.zeros((tq, 1), F32)
        lax.fori_loop(0, qi + 1, step, (zero, zero))
        o_ref[...] = jnp.where(first, acc_ref[0], acc_ref[1])

    return pl.pallas_call(
        body, name=name, out_shape=jax.ShapeDtypeStruct((s, d), F32), grid=(d // LANES, nq),
        in_specs=[pl.BlockSpec((tq, LANES), lambda p, i: (i, p)), pl.BlockSpec((s, LANES), lambda p, i: (0, p)),
                  pl.BlockSpec((s, LANES), lambda p, i: (0, p))],
        out_specs=pl.BlockSpec((tq, LANES), lambda p, i: (i, p)),
        scratch_shapes=[pltpu.VMEM((2, tq, LANES), F32)],
        compiler_params=_cparams(2),
    )(qn, kn, v)


def _sb_bwd(qn, kn, v, o, do, *, name):
    s, d = qn.shape
    tq = SB_BLOCK
    nq = s // tq

    def body(q_ref, k_ref, v_ref, o_ref, do_ref, dq_ref, dk_ref, dv_ref, acc_ref):
        qi = pl.program_id(1)

        @pl.when(qi == 0)
        def _():
            dk_ref[...] = jnp.zeros_like(dk_ref)
            dv_ref[...] = jnp.zeros_like(dv_ref)

        lane = lax.broadcasted_iota(jnp.int32, (tq, LANES), 1)
        first = lane < SB_HEAD_DIM
        sel = [first, jnp.logical_not(first)]
        q = q_ref[...]
        dov = do_ref[...]
        ov = o_ref[...]
        qh = [jnp.where(sel[hh], q, 0).astype(BF16) for hh in range(2)]
        dof = [jnp.where(sel[hh], dov, 0.0) for hh in range(2)]
        doh = [t.astype(BF16) for t in dof]
        dol = [(dof[hh] - doh[hh].astype(F32)).astype(BF16) for hh in range(2)]
        prod = dov * ov
        gtot = [jnp.sum(jnp.where(sel[hh], prod, 0.0), axis=1, keepdims=True) for hh in range(2)]
        ones = _suffix_ones()
        acc_ref[...] = jnp.zeros_like(acc_ref)

        def step(it, carry):
            runs, gruns = carry
            j = qi - it
            ks = pl.ds(pl.multiple_of(j * tq, tq), tq)
            kb = k_ref[ks, :]
            vb = v_ref[ks, :]
            new_runs, new_gruns = [], []
            dk_add = jnp.zeros((tq, LANES), F32)
            dv_add = jnp.zeros((tq, LANES), F32)
            for hh in range(2):
                logb, l1m, wgt, mask = _sb_scores(qh[hh], kb, qi, j, ones, runs[hh])
                g = (_nt(doh[hh], vb) + _nt(dol[hh], vb)) * wgt
                gsuf = _exact_right2(g, ones) + g + gruns[hh]
                beta = jnp.exp(logb)
                dz = jnp.where(mask, g * (1.0 - beta) - beta * (gtot[hh] - gsuf), 0.0) * SB_SCALE
                dzb = dz.astype(BF16)
                acc_ref[hh] += _nn(dzb, kb)
                dk_add = dk_add + jnp.where(sel[hh], _tn(dzb, qh[hh]), 0.0)
                dv_add = dv_add + jnp.where(sel[hh], _tn(wgt.astype(BF16), doh[hh]), 0.0)
                new_runs.append(runs[hh] + jnp.sum(l1m, axis=1, keepdims=True))
                new_gruns.append(gruns[hh] + jnp.sum(g, axis=1, keepdims=True))
            dk_ref[ks, :] += dk_add
            dv_ref[ks, :] += dv_add
            return tuple(new_runs), tuple(new_gruns)

        zero = jnp.zeros((tq, 1), F32)
        lax.fori_loop(0, qi + 1, step, ((zero, zero), (zero, zero)))
        dq_ref[...] = jnp.where(first, acc_ref[0], acc_ref[1])

    blk = pl.BlockSpec((tq, LANES), lambda p, i: (i, p))
    full = pl.BlockSpec((s, LANES), lambda p, i: (0, p))
    out = jax.ShapeDtypeStruct((s, d), F32)
    return pl.pallas_call(
        body, name=name, out_shape=(out, out, out), grid=(d // LANES, nq),
        in_specs=[blk, full, full, blk, blk],
        out_specs=(blk, full, full),
        scratch_shapes=[pltpu.VMEM((2, tq, LANES), F32)],
        compiler_params=_cparams(2),
    )(qn, kn, v, o, do)


def _mod_part(c_all, ada_w, ada_b_my, *, name):
    nl, d, ncol = ada_w.shape

    def body(c_ref, w_ref, b_ref, part_ref, ca_ref):
        cv = c_ref[...]
        ca = cv * jax.nn.sigmoid(cv)
        ca_ref[...] = ca
        part_ref[...] = _nn(ca.astype(BF16), w_ref[...].astype(BF16)) + b_ref[...]

    return pl.pallas_call(
        body, name=name,
        out_shape=(jax.ShapeDtypeStruct((nl, N_DEV, ncol), F32), jax.ShapeDtypeStruct((N_DEV, d), F32)),
        grid=(nl,),
        in_specs=[pl.BlockSpec((N_DEV, d), lambda l: (0, 0)), pl.BlockSpec((None, d, ncol), lambda l: (l, 0, 0)),
                  pl.BlockSpec((None, 1, ncol), lambda l: (l, 0, 0))],
        out_specs=(pl.BlockSpec((None, N_DEV, ncol), lambda l: (l, 0, 0)), pl.BlockSpec((N_DEV, d), lambda l: (0, 0))),
        compiler_params=_cparams(1),
    )(c_all, ada_w, ada_b_my)


PK_MOD, PK_NMIX, PK_NMLP, PK_HGN, PK_LB, PK_QN, PK_KN, PK_CONV, PK_ROWS = 0, 96, 112, 128, 132, 136, 144, 152, 168


def _small_grads(gath, ca_col, dmod_my, lbl4, *, name):
    def body(g_ref, ca_ref, dm_ref, lbl_ref, gw_ref, gsum_ref, glb_ref, gqk_ref):
        tot = g_ref[0]
        for dev in range(1, N_DEV):
            tot = tot + g_ref[dev]
        gsum_ref[...] = tot
        lv = lbl_ref[...]
        m = jnp.maximum(jnp.maximum(lv[0], lv[1]), lv[2])
        e = [jnp.exp(lv[k] - m) for k in range(3)]
        den = e[0] + e[1] + e[2]
        p = [ek / den for ek in e]
        dlb = tot[PK_LB:PK_LB + 4, :]
        glb_ref[0] = dlb * p[0] * (1.0 - p[0])
        glb_ref[1] = -dlb * p[0] * p[1]
        glb_ref[2] = -dlb * p[0] * p[2]
        for idx, base in enumerate((PK_QN, PK_KN)):
            rowsum = jnp.sum(tot[base:base + 8, :], axis=0, keepdims=True)
            gqk_ref[idx:idx + 1, :] = rowsum + pltpu.roll(rowsum, SB_HEAD_DIM, 1)
        for l in range(2):
            acc = ca_ref[0] * dm_ref[0, l:l + 1, :]
            for smp in range(1, N_DEV):
                acc = acc + ca_ref[smp] * dm_ref[smp, l:l + 1, :]
            gw_ref[l] = acc

    d, ncol = ca_col.shape[1], dmod_my.shape[2]
    vm = pl.BlockSpec(memory_space=pltpu.VMEM)
    return pl.pallas_call(
        body, name=name,
        out_shape=(jax.ShapeDtypeStruct((2, d, ncol), F32), jax.ShapeDtypeStruct((PK_ROWS, LANES), F32),
                   jax.ShapeDtypeStruct((3, 4, LANES), F32), jax.ShapeDtypeStruct((8, LANES), F32)),
        in_specs=[vm, vm, vm, vm], out_specs=(vm, vm, vm, vm),
        compiler_params=pltpu.CompilerParams(vmem_limit_bytes=VMEM_LIMIT),
    )(gath, ca_col, dmod_my, lbl4)


def _adamw_math(w, g, m, v):
    m = ADAM_B1 * m + (1.0 - ADAM_B1) * g
    v = ADAM_B2 * v + (1.0 - ADAM_B2) * (g * g)
    m_hat = m / (1.0 - ADAM_B1 ** ADAM_STEP)
    v_hat = v / (1.0 - ADAM_B2 ** ADAM_STEP)
    delta = -ADAM_LR * (m_hat / (jnp.sqrt(v_hat) + ADAM_EPS) + ADAM_WD * w)
    return delta, m, v


def _adamw(w, g, m, v, *, name, tr=256):
    r, n = w.shape
    tr = _tile(tr, r)

    def body(w_ref, g_ref, m_ref, v_ref, d_ref, mo_ref, vo_ref):
        dl, mn, vn = _adamw_math(w_ref[...], g_ref[...], m_ref[...], v_ref[...])
        d_ref[...] = dl
        mo_ref[...] = mn
        vo_ref[...] = vn

    blk = pl.BlockSpec((tr, n), lambda i: (i, 0))
    out = jax.ShapeDtypeStruct((r, n), F32)
    return pl.pallas_call(
        body, name=name, out_shape=(out, out, out), grid=(r // tr,),
        in_specs=[blk, blk, blk, blk], out_specs=(blk, blk, blk),
        compiler_params=_cparams(1),
    )(w, g, m, v)


def _adamw_small(items, *, name):
    n = len(items)

    def body(*refs):
        ins, outs = refs[:4 * n], refs[4 * n:]
        for k in range(n):
            dl, mn, vn = _adamw_math(*(r[...] for r in ins[4 * k:4 * k + 4]))
            outs[3 * k][...] = dl
            outs[3 * k + 1][...] = mn
            outs[3 * k + 2][...] = vn

    flat = [a for it in items for a in it]
    out_shape = tuple(jax.ShapeDtypeStruct(it[0].shape, F32) for it in items for _ in range(3))
    vm = pl.BlockSpec(memory_space=pltpu.VMEM)
    res = pl.pallas_call(
        body, name=name, out_shape=out_shape, in_specs=[vm] * (4 * n), out_specs=tuple([vm] * (3 * n)),
    )(*flat)
    return [tuple(res[3 * k:3 * k + 3]) for k in range(n)]


def _mlp_fwd(x, g, scale, shift, gate, w1, w2, tag):
    h = _norm_mod(x, g, scale, shift, name=f"{tag}_norm")
    act = _matmul(h, w1, epi="relu2", out_dtype=BF16, name=f"{tag}_w1")
    z, x_out = _matmul(act, w2, epi="resgate", extras=(x, gate), name=f"{tag}_w2")
    return x_out, (h, act, z)


def _mlp_bwd(dx_out, x, saved, g, scale, gate, w1, w2, tag):
    h, act, z = saved
    dz, gate_acc = _gate_bwd(dx_out, z, gate, name=f"{tag}_gate_bwd")
    du = _matmul(dz, w2, tb=True, epi="dact", extras=(act,), out_dtype=BF16, name=f"{tag}_dact")
    dw2 = _matmul(act, dz, ta=True, name=f"{tag}_dw2")
    dw1 = _matmul(h, du, ta=True, name=f"{tag}_dw1")
    dh = _matmul(du, w1, tb=True, name=f"{tag}_dh")
    dx, nacc = _norm_mod_bwd(x, dh, dx_out, g, scale, name=f"{tag}_norm_bwd")
    return dx, dw1, dw2, (nacc[0:1], nacc[1:2], gate_acc[0:1]), nacc[2:3]


def kernel(x, c, ada_w, ada_b, norm_mix, norm_mlp, w_in_ab, conv_w, hg_norm, lb_logits, w_out_ab, w_qkv, q_norm, k_norm, w_out_c, mlp_w1, mlp_w2, loss_target, m_ada_w, m_ada_b, m_norm_mix, m_norm_mlp, m_w_in_ab, m_conv_w, m_hg_norm, m_lb_logits, m_w_out_ab, m_w_qkv, m_q_norm, m_k_norm, m_w_out_c, m_mlp_w1, m_mlp_w2, v_ada_w, v_ada_b, v_norm_mix, v_norm_mlp, v_w_in_ab, v_conv_w, v_hg_norm, v_lb_logits, v_w_out_ab, v_w_qkv, v_q_norm, v_k_norm, v_w_out_c, v_mlp_w1, v_mlp_w2):
    d = D_MODEL
    my_x, my_y, my_c = lax.axis_index("x"), lax.axis_index("y"), lax.axis_index("c")
    me = 4 * my_x + 2 * my_y + my_c
    xs = x[0]
    tgt = loss_target[0]

    big = [w_in_ab, w_out_ab, w_qkv, w_out_c, mlp_w1, mlp_w2]
    rows = [w.size // d for w in big]
    offs = [sum(rows[:k]) for k in range(len(rows) + 1)]
    packed = jnp.concatenate([w.astype(BF16).reshape(-1, d) for w in big], axis=0)
    gathered = _all_gather(packed, name="gather_weights", in_vmem=False).reshape(N_DEV, offs[-1], d)

    def piece(k):
        return gathered[:, offs[k]:offs[k + 1]]

    def cols(k, nl, kdim, ncol):
        return piece(k).reshape(N_DEV, nl, kdim, ncol).transpose(1, 2, 0, 3).reshape(nl, kdim, N_DEV * ncol)

    def rws(k, nl, nrow, n):
        return piece(k).reshape(N_DEV, nl, nrow, n).transpose(1, 0, 2, 3).reshape(nl, N_DEV * nrow, n)

    win = cols(0, 1, d, AB_IN // N_DEV)[0]
    wout_ab = rws(1, 1, d // N_DEV, d)[0]
    wqkv = cols(2, 1, d, 3 * d // N_DEV)[0]
    wout_c = rws(3, 1, d // N_DEV, d)[0]
    w1 = cols(4, 2, d, D_FF // N_DEV)
    w2 = rws(5, 2, D_FF // N_DEV, d)

    ncv = CONV_DIM // N_DEV
    c_and_conv = jnp.concatenate([c, jnp.pad(conv_w[0], ((0, 0), (0, d - ncv))), jnp.zeros((4, d), F32)], axis=0)
    c_and_conv = _all_gather(c_and_conv, name="gather_c", in_vmem=True).reshape(N_DEV, 8, d)
    c_all = c_and_conv[:, 0]
    conv_full = c_and_conv[:, 1:4, :ncv].transpose(1, 0, 2).reshape(3, CONV_DIM)
    ncol = ada_w.shape[2]
    ada_b_my = lax.dynamic_slice(ada_b, (0, me * ncol), (2, ncol)).reshape(2, 1, ncol)
    part, c_act = _mod_part(c_all, ada_w, ada_b_my, name="mod_part")
    parts = _all_gather(part.reshape(2 * N_DEV, ncol), name="gather_mod", in_vmem=True)
    parts = parts.reshape(N_DEV, 2, N_DEV, ncol)
    mod = lax.dynamic_index_in_dim(parts, me, axis=2, keepdims=False)
    mod = mod.transpose(1, 0, 2).reshape(2, 6, 1, d)

    qn_t = jnp.tile(q_norm, (1, d // SB_HEAD_DIM))
    kn_t = jnp.tile(k_norm, (1, d // SB_HEAD_DIM))

    sh1, sc1, gt1, sh2, sc2, gt2 = [mod[0, k] for k in range(6)]
    h0 = _norm_mod(xs, norm_mix[0:1], sc1, sh1, name="l0_mix_norm")
    u = _matmul(h0, win, name="l0_in_proj")
    y_a = _conv_fwd(u, conv_full, name="l0_conv")
    y_b, o_hg, sall = _hgrn_fwd(u, lb_logits, hg_norm, name="l0_hgrn")
    y_ab = jnp.concatenate([y_a, y_b], axis=1)
    z0, x_mid0 = _matmul(y_ab, wout_ab, epi="resgate", extras=(xs, gt1), name="l0_out_proj")
    x1, mlp0 = _mlp_fwd(x_mid0, norm_mlp[0:1], sc2, sh2, gt2, w1[0], w2[0], "l0_mlp")

    sh1b, sc1b, gt1b, sh2b, sc2b, gt2b = [mod[1, k] for k in range(6)]
    h1 = _norm_mod(x1, norm_mix[1:2], sc1b, sh1b, name="l1_mix_norm")
    qkv = _matmul(h1, wqkv, name="l1_qkv_proj")
    qn_a, kn_a, v_a = _qk_norm_fwd(qkv, qn_t, kn_t, name="l1_qk_norm")
    o_sb = _sb_fwd(qn_a, kn_a, v_a, name="l1_sb")
    z1, x_mid1 = _matmul(o_sb, wout_c, epi="resgate", extras=(x1, gt1b), name="l1_out_proj")
    x2, mlp1 = _mlp_fwd(x_mid1, norm_mlp[1:2], sc2b, sh2b, gt2b, w1[1], w2[1], "l1_mlp")

    dx, loss_part = _loss_grad(x2, tgt, name="loss")
    loss = lax.psum(loss_part[0, 0], MESH_AXES)

    dx, dw1_1, dw2_1, (dsh2b, dsc2b, dgt2b), dnmlp1 = _mlp_bwd(
        dx, x_mid1, mlp1, norm_mlp[1:2], sc2b, gt2b, w1[1], w2[1], "l1_mlp")
    dyp, gacc = _gate_bwd(dx, z1, gt1b, name="l1_mix_gate_bwd")
    dwout_c = _matmul(o_sb, dyp, ta=True, name="l1_dwout")
    do_sb = _matmul(dyp, wout_c, tb=True, name="l1_do")
    dqn_a, dkn_a, dv_a = _sb_bwd(qn_a, kn_a, v_a, o_sb, do_sb, name="l1_sb_bwd")
    dqkv, qkacc = _qk_norm_bwd(qkv, dqn_a, dkn_a, dv_a, qn_t, kn_t, name="l1_qk_norm_bwd")
    dwqkv = _matmul(h1, dqkv, ta=True, name="l1_dwqkv")
    dh1 = _matmul(dqkv, wqkv, tb=True, name="l1_dh")
    dx, nacc = _norm_mod_bwd(x1, dh1, dx, norm_mix[1:2], sc1b, name="l1_mix_norm_bwd")
    dmod1 = [nacc[0:1], nacc[1:2], gacc[0:1], dsh2b, dsc2b, dgt2b]
    dnmix1 = nacc[2:3]

    dx, dw1_0, dw2_0, (dsh2, dsc2, dgt2), dnmlp0 = _mlp_bwd(
        dx, x_mid0, mlp0, norm_mlp[0:1], sc2, gt2, w1[0], w2[0], "l0_mlp")
    dyp, gacc = _gate_bwd(dx, z0, gt1, name="l0_mix_gate_bwd")
    dwout_ab = _matmul(y_ab, dyp, ta=True, name="l0_dwout")
    dy_ab = _matmul(dyp, wout_ab, tb=True, name="l0_dy")
    du_a, dconv = _conv_bwd(u, dy_ab, conv_full, name="l0_conv_bwd")
    du_b, hgacc = _hgrn_bwd(u, o_hg, sall, dy_ab, lb_logits, hg_norm, name="l0_hgrn_bwd")
    du = jnp.concatenate([du_a, du_b], axis=1)
    dwin = _matmul(h0, du, ta=True, name="l0_dwin")
    dh0 = _matmul(du, win, tb=True, tk=512, name="l0_dh")
    grad_x, nacc = _norm_mod_bwd(xs, dh0, dx, norm_mix[0:1], sc1, name="l0_mix_norm_bwd")
    dmod0 = [nacc[0:1], nacc[1:2], gacc[0:1], dsh2, dsc2, dgt2]
    dnmix0 = nacc[2:3]

    def to_cols(gw, nl, kdim, ncol_):
        return gw.reshape(nl, kdim, N_DEV, ncol_).transpose(2, 0, 1, 3).reshape(N_DEV, -1, d)

    def to_rows(gw, nl, nrow, n):
        return gw.reshape(nl, N_DEV, nrow, n).transpose(1, 0, 2, 3).reshape(N_DEV, -1, d)

    gbig = jnp.concatenate([
        to_cols(dwin[None], 1, d, AB_IN // N_DEV), to_rows(dwout_ab[None], 1, d // N_DEV, d),
        to_cols(dwqkv[None], 1, d, 3 * d // N_DEV), to_rows(dwout_c[None], 1, d // N_DEV, d),
        to_cols(jnp.stack([dw1_0, dw1_1]), 2, d, D_FF // N_DEV),
        to_rows(jnp.stack([dw2_0, dw2_1]), 2, D_FF // N_DEV, d)], axis=1)
    sib = _rs_sibling_exchange(gbig, name="rs_sibling")
    pair = _rs_pair_sum(gbig, sib, my_c.reshape(1).astype(jnp.int32), name="rs_pair_sum")
    far = _rs_chip_exchange(pair, name="rs_chips")
    gsh = _rs_final_sum(pair, far, (2 * my_x + my_y).reshape(1).astype(jnp.int32), name="rs_final_sum")
    g_big = [gsh[offs[k]:offs[k + 1]].reshape(big[k].shape) for k in range(len(big))]

    packed_small = jnp.concatenate(
        [jnp.concatenate(dmod0, axis=1).reshape(-1, LANES), jnp.concatenate(dmod1, axis=1).reshape(-1, LANES),
         dnmix0.reshape(-1, LANES), dnmix1.reshape(-1, LANES), dnmlp0.reshape(-1, LANES), dnmlp1.reshape(-1, LANES),
         hgacc[0:1].reshape(-1, LANES), hgacc[1:2].reshape(-1, LANES),
         qkacc[0:1].reshape(-1, LANES), qkacc[1:2].reshape(-1, LANES),
         dconv[0:3].reshape(-1, LANES), jnp.zeros((PK_ROWS - PK_CONV - 12, LANES), F32)], axis=0)
    gath = _all_gather(packed_small, name="gather_small_grads", in_vmem=True).reshape(N_DEV, PK_ROWS, LANES)
    dmod_all = gath[:, PK_MOD:PK_NMIX].reshape(N_DEV, 2, 6 * d)
    dmod_my = lax.dynamic_slice(dmod_all, (0, 0, me * ncol), (N_DEV, 2, ncol))
    g_ada_w, gsum, g_lb, g_qk = _small_grads(gath, c_act[:, :, None], dmod_my, lb_logits.reshape(3, 4, LANES),
                                             name="small_grads")
    g_ada_b = gsum[PK_MOD:PK_NMIX].reshape(2, 6 * d)
    g_norm_mix = gsum[PK_NMIX:PK_NMLP].reshape(2, d)
    g_norm_mlp = gsum[PK_NMLP:PK_HGN].reshape(2, d)
    g_hg_norm = gsum[PK_HGN:PK_LB].reshape(1, HG_WIDTH)
    g_lb_logits = g_lb.reshape(3, HG_WIDTH)
    g_q_norm = g_qk[0:1, :SB_HEAD_DIM]
    g_k_norm = g_qk[1:2, :SB_HEAD_DIM]
    g_conv_w = lax.dynamic_slice(gsum[PK_CONV:PK_CONV + 12].reshape(3, CONV_DIM), (0, me * ncv), (3, ncv))[None]

    def flat2(a):
        return a.reshape(-1, a.shape[-1])

    grads = dict(ada_w=g_ada_w, ada_b=g_ada_b, norm_mix=g_norm_mix, norm_mlp=g_norm_mlp, w_in_ab=g_big[0],
                 conv_w=g_conv_w, hg_norm=g_hg_norm, lb_logits=g_lb_logits, w_out_ab=g_big[1], w_qkv=g_big[2],
                 q_norm=g_q_norm, k_norm=g_k_norm, w_out_c=g_big[3], mlp_w1=g_big[4], mlp_w2=g_big[5])
    weights = dict(ada_w=(ada_w, m_ada_w, v_ada_w), ada_b=(ada_b, m_ada_b, v_ada_b),
                   norm_mix=(norm_mix, m_norm_mix, v_norm_mix), norm_mlp=(norm_mlp, m_norm_mlp, v_norm_mlp),
                   w_in_ab=(w_in_ab, m_w_in_ab, v_w_in_ab), conv_w=(conv_w, m_conv_w, v_conv_w),
                   hg_norm=(hg_norm, m_hg_norm, v_hg_norm), lb_logits=(lb_logits, m_lb_logits, v_lb_logits),
                   w_out_ab=(w_out_ab, m_w_out_ab, v_w_out_ab), w_qkv=(w_qkv, m_w_qkv, v_w_qkv),
                   q_norm=(q_norm, m_q_norm, v_q_norm), k_norm=(k_norm, m_k_norm, v_k_norm),
                   w_out_c=(w_out_c, m_w_out_c, v_w_out_c), mlp_w1=(mlp_w1, m_mlp_w1, v_mlp_w1),
                   mlp_w2=(mlp_w2, m_mlp_w2, v_mlp_w2))
    names = list(weights)
    small_names = ["ada_b", "norm_mix", "norm_mlp", "conv_w", "hg_norm", "lb_logits", "q_norm", "k_norm"]
    upd = {}
    small_items = []
    for n in small_names:
        wv, mv, vv = weights[n]
        small_items.append((flat2(wv), flat2(grads[n]), flat2(mv), flat2(vv)))
    for n, res in zip(small_names, _adamw_small(small_items, name="adamw_small")):
        upd[n] = tuple(r.reshape(weights[n][0].shape) for r in res)
    for n in names:
        if n in small_names:
            continue
        wv, mv, vv = weights[n]
        res = _adamw(flat2(wv), flat2(grads[n]), flat2(mv), flat2(vv), name=f"adamw_{n}")
        upd[n] = tuple(r.reshape(wv.shape) for r in res)

    return (loss, grad_x[None], *[grads[n].reshape(weights[n][0].shape) for n in names],
            *[upd[n][0] for n in names], *[upd[n][1] for n in names], *[upd[n][2] for n in names])
```

```python
import functools

import jax
import jax.numpy as jnp
from jax import lax
from jax.experimental import pallas as pl
from jax.experimental.pallas import tpu as pltpu

F32 = jnp.float32
BF16 = jnp.bfloat16
EPS = 1e-6
N_DEV = 8
MESH_AXES = ("x", "y", "c")

D_MODEL = 1024
CONV_DIM = 512
HG_HEADS = 4
HG_DK = 128
HG_WIDTH = 512
CHUNK = 64
HG_TILE = 128
HG_SUB = 16
HG_EXP_CLAMP = 60.0
SB_HEAD_DIM = 64
SB_SCALE = SB_HEAD_DIM ** -0.5
LOG2E = 1.4426950408889634
LN2 = 0.6931471805599453
SB_TQ = 512
SB_TK = 256
D_FF = 4096
AB_IN = 3584

ADAM_LR = 0.001
ADAM_B1 = 0.9
ADAM_B2 = 0.999
ADAM_EPS = 1e-08
ADAM_WD = 0.01
ADAM_STEP = 10

VMEM_LIMIT = 48 * 1024 * 1024
LANES = 128


def _cparams(n_grid):
    return pltpu.CompilerParams(dimension_semantics=("arbitrary",) * n_grid, vmem_limit_bytes=VMEM_LIMIT)


def _nt(a, b):
    return lax.dot_general(a, b, (((1,), (1,)), ((), ())), preferred_element_type=F32)


def _tn(a, b):
    return lax.dot_general(a, b, (((0,), (0,)), ((), ())), preferred_element_type=F32)


def _nn(a, b):
    return jnp.dot(a, b, preferred_element_type=F32)


def _split3(x):
    hi = x.astype(BF16)
    r1 = x - hi.astype(F32)
    mid = r1.astype(BF16)
    lo = (r1 - mid.astype(F32)).astype(BF16)
    return hi, mid, lo


def _exact_left(m01, x):
    hi, mid, lo = _split3(x)
    return _nn(m01, hi) + _nn(m01, mid) + _nn(m01, lo)


def _exact_right(x, m01):
    hi, mid, lo = _split3(x)
    return _nn(hi, m01) + _nn(mid, m01) + _nn(lo, m01)


def _exact_right2(x, m01):
    hi = x.astype(BF16)
    lo = (x - hi.astype(F32)).astype(BF16)
    return _nn(hi, m01) + _nn(lo, m01)


def _sp(x):
    hi = x.astype(BF16)
    return hi, (x - hi.astype(F32)).astype(BF16)


def _dot3(fn, a, b):
    return fn(a[0], b[0]) + fn(a[0], b[1]) + fn(a[1], b[0])


def _tile(pref, n):
    t = min(pref, n)
    assert n % t == 0, (pref, n)
    return t


def _all_gather(x, *, name, in_vmem):
    m_per, n = x.shape

    def body(x_ref, out_ref, send_sems, recv_sems, local_sem):
        mx, my, mc = lax.axis_index("x"), lax.axis_index("y"), lax.axis_index("c")
        me, sibling = (mx, my, mc), (mx, my, 1 - mc)
        chips = [(1 - mx, my), (mx, 1 - my), (1 - mx, 1 - my)]

        def rows(px, py, pc):
            return out_ref.at[pl.ds((4 * px + 2 * py + pc) * m_per, m_per), :]

        def copy(k, block, to, src=None):
            return pltpu.make_async_remote_copy(
                src_ref=rows(*block) if src is None else src, dst_ref=rows(*block),
                send_sem=send_sems.at[k], recv_sem=recv_sems.at[k],
                device_id=to, device_id_type=pl.DeviceIdType.MESH)

        mine = pltpu.make_async_copy(x_ref, rows(*me), local_sem)
        mine.start()
        first = [copy(0, me, sibling, src=x_ref)]
        first += [copy(1 + j, me, (*chip, mc), src=x_ref) for j, chip in enumerate(chips)]
        for cp in first:
            cp.start()
        passed = [copy(4 + j, (*chip, mc), sibling) for j, chip in enumerate(chips)]
        for j, chip in enumerate(chips):
            copy(1 + j, (*chip, mc), me).wait_recv()
            passed[j].start()
        copy(0, sibling, me).wait_recv()
        for j, chip in enumerate(chips):
            copy(4 + j, (*chip, 1 - mc), me).wait_recv()
        for cp in first + passed:
            cp.wait_send()
        mine.wait()

    space = pltpu.VMEM if in_vmem else pl.ANY
    return pl.pallas_call(
        body, name=name,
        out_shape=jax.ShapeDtypeStruct((N_DEV * m_per, n), x.dtype),
        in_specs=[pl.BlockSpec(memory_space=space)],
        out_specs=pl.BlockSpec(memory_space=space),
        scratch_shapes=[pltpu.SemaphoreType.DMA((7,)), pltpu.SemaphoreType.DMA((7,)), pltpu.SemaphoreType.DMA],
    )(x)


def _rs_sibling_exchange(g, *, name):
    _, r, n = g.shape

    def body(g_ref, out_ref, send_sems, recv_sems):
        mx, my, mc = lax.axis_index("x"), lax.axis_index("y"), lax.axis_index("c")
        copies = []
        for q in range(4):
            copies.append(pltpu.make_async_remote_copy(
                src_ref=g_ref.at[2 * q + (1 - mc)], dst_ref=out_ref.at[q],
                send_sem=send_sems.at[q], recv_sem=recv_sems.at[q],
                device_id=(mx, my, 1 - mc), device_id_type=pl.DeviceIdType.MESH))
        for cp in copies:
            cp.start()
        for cp in copies:
            cp.wait_recv()
        for cp in copies:
            cp.wait_send()

    return pl.pallas_call(
        body, name=name,
        out_shape=jax.ShapeDtypeStruct((4, r, n), g.dtype),
        in_specs=[pl.BlockSpec(memory_space=pl.ANY)],
        out_specs=pl.BlockSpec(memory_space=pl.ANY),
        scratch_shapes=[pltpu.SemaphoreType.DMA((4,)), pltpu.SemaphoreType.DMA((4,))],
    )(g)


def _rs_chip_exchange(t, *, name):
    _, r, n = t.shape

    def body(t_ref, out_ref, send_sems, recv_sems):
        mx, my, mc = lax.axis_index("x"), lax.axis_index("y"), lax.axis_index("c")
        chips = [(1 - mx, my), (mx, 1 - my), (1 - mx, 1 - my)]
        copies = []
        for k, (px, py) in enumerate(chips):
            copies.append(pltpu.make_async_remote_copy(
                src_ref=t_ref.at[2 * px + py], dst_ref=out_ref.at[k],
                send_sem=send_sems.at[k], recv_sem=recv_sems.at[k],
                device_id=(px, py, mc), device_id_type=pl.DeviceIdType.MESH))
        for cp in copies:
            cp.start()
        for cp in copies:
            cp.wait_recv()
        for cp in copies:
            cp.wait_send()

    return pl.pallas_call(
        body, name=name,
        out_shape=jax.ShapeDtypeStruct((3, r, n), t.dtype),
        in_specs=[pl.BlockSpec(memory_space=pl.ANY)],
        out_specs=pl.BlockSpec(memory_space=pl.ANY),
        scratch_shapes=[pltpu.SemaphoreType.DMA((3,)), pltpu.SemaphoreType.DMA((3,))],
    )(t)


def _rs_pair_sum(g, p1, my_c, *, name, tr=448):
    _, r, n = g.shape
    tr = _tile(tr, r)

    def body(c_ref, g_ref, p_ref, o_ref):
        o_ref[...] = g_ref[...] + p_ref[...]

    return pl.pallas_call(
        body, name=name,
        out_shape=jax.ShapeDtypeStruct((4, r, n), F32),
        grid_spec=pltpu.PrefetchScalarGridSpec(
            num_scalar_prefetch=1, grid=(4, r // tr),
            in_specs=[pl.BlockSpec((None, tr, n), lambda q, i, c: (2 * q + c[0], i, 0)),
                      pl.BlockSpec((None, tr, n), lambda q, i, c: (q, i, 0))],
            out_specs=pl.BlockSpec((None, tr, n), lambda q, i, c: (q, i, 0))),
        compiler_params=_cparams(2),
    )(my_c, g, p1)


def _rs_final_sum(t, p3, my_q, *, name, tr=448):
    _, r, n = t.shape
    tr = _tile(tr, r)

    def body(q_ref, t_ref, a_ref, b_ref, c_ref, o_ref):
        o_ref[...] = ((t_ref[...] + a_ref[...]) + b_ref[...]) + c_ref[...]

    return pl.pallas_call(
        body, name=name,
        out_shape=jax.ShapeDtypeStruct((r, n), F32),
        grid_spec=pltpu.PrefetchScalarGridSpec(
            num_scalar_prefetch=1, grid=(r // tr,),
            in_specs=[pl.BlockSpec((None, tr, n), lambda i, q: (q[0], i, 0)),
                      pl.BlockSpec((None, tr, n), lambda i, q: (0, i, 0)),
                      pl.BlockSpec((None, tr, n), lambda i, q: (1, i, 0)),
                      pl.BlockSpec((None, tr, n), lambda i, q: (2, i, 0))],
            out_specs=pl.BlockSpec((tr, n), lambda i, q: (i, 0))),
        compiler_params=_cparams(1),
    )(my_q, t, p3, p3, p3)


def _matmul(a, b, *, name, ta=False, tb=False, epi="plain", extras=(), out_dtype=F32, tm=512, tn=512, tk=1024):
    if ta:
        kdim, m = a.shape
    else:
        m, kdim = a.shape
    if tb:
        n, kb = b.shape
    else:
        kb, n = b.shape
    assert kdim == kb, (a.shape, b.shape)
    tm, tn, tk = _tile(tm, m), _tile(tn, n), _tile(tk, kdim)
    nk = kdim // tk
    a_spec = pl.BlockSpec((tk, tm), lambda i, j, k: (k, i)) if ta else pl.BlockSpec((tm, tk), lambda i, j, k: (i, k))
    b_spec = pl.BlockSpec((tn, tk), lambda i, j, k: (j, k)) if tb else pl.BlockSpec((tk, tn), lambda i, j, k: (k, j))
    dims = (((0 if ta else 1,), (1 if tb else 0,)), ((), ()))
    mn_spec = pl.BlockSpec((tm, tn), lambda i, j, k: (i, j))
    row_spec = pl.BlockSpec((1, tn), lambda i, j, k: (0, j))
    if epi == "resgate":
        extra_specs = [mn_spec, row_spec]
        out_shape = (jax.ShapeDtypeStruct((m, n), F32), jax.ShapeDtypeStruct((m, n), F32))
        out_specs = (mn_spec, mn_spec)
    elif epi == "dact":
        extra_specs = [mn_spec]
        out_shape = jax.ShapeDtypeStruct((m, n), out_dtype)
        out_specs = mn_spec
    else:
        extra_specs = []
        out_shape = jax.ShapeDtypeStruct((m, n), out_dtype)
        out_specs = mn_spec
    n_extra = len(extra_specs)

    def body(a_ref, b_ref, *rest):
        ex = rest[:n_extra]
        outs = rest[n_extra:-1]
        acc = rest[-1]
        k = pl.program_id(2)

        @pl.when(k == 0)
        def _():
            acc[...] = jnp.zeros_like(acc)

        acc[...] += lax.dot_general(a_ref[...].astype(BF16), b_ref[...].astype(BF16), dims,
                                    preferred_element_type=F32)

        @pl.when(k == nk - 1)
        def _():
            r = acc[...]
            if epi == "plain":
                outs[0][...] = r.astype(outs[0].dtype)
            elif epi == "resgate":
                outs[0][...] = r
                outs[1][...] = ex[0][...] + ex[1][...] * r
            elif epi == "relu2":
                p = jnp.maximum(r, 0.0)
                outs[0][...] = (p * p).astype(outs[0].dtype)
            elif epi == "dact":
                outs[0][...] = (r * (2.0 * jnp.sqrt(ex[0][...].astype(F32)))).astype(outs[0].dtype)

    return pl.pallas_call(
        body, name=name, out_shape=out_shape, grid=(m // tm, n // tn, nk),
        in_specs=[a_spec, b_spec] + extra_specs, out_specs=out_specs,
        scratch_shapes=[pltpu.VMEM((tm, tn), F32)],
        compiler_params=_cparams(3),
    )(a, b, *extras)


def _norm_mod(x, g, scale, shift, *, name, tm=512):
    s, d = x.shape
    tm = _tile(tm, s)

    def body(x_ref, g_ref, sc_ref, sh_ref, h_ref):
        xv = x_ref[...]
        r = lax.rsqrt(jnp.mean(xv * xv, axis=-1, keepdims=True) + EPS)
        h_ref[...] = (((xv * r) * g_ref[...]) * (1.0 + sc_ref[...]) + sh_ref[...]).astype(BF16)

    row = pl.BlockSpec((1, d), lambda i: (0, 0))
    return pl.pallas_call(
        body, name=name, out_shape=jax.ShapeDtypeStruct((s, d), BF16), grid=(s // tm,),
        in_specs=[pl.BlockSpec((tm, d), lambda i: (i, 0)), row, row, row],
        out_specs=pl.BlockSpec((tm, d), lambda i: (i, 0)),
        compiler_params=_cparams(1),
    )(x, g, scale, shift)


def _norm_mod_bwd(x, dh, dres, g, scale, *, name, tm=512):
    s, d = x.shape
    tm = _tile(tm, s)

    def body(x_ref, dh_ref, dr_ref, g_ref, sc_ref, dx_ref, acc_ref):
        i = pl.program_id(0)

        @pl.when(i == 0)
        def _():
            acc_ref[...] = jnp.zeros_like(acc_ref)

        xv = x_ref[...]
        dhv = dh_ref[...]
        gv = g_ref[...]
        one_sc = 1.0 + sc_ref[...]
        r = lax.rsqrt(jnp.mean(xv * xv, axis=-1, keepdims=True) + EPS)
        xn = xv * r
        dxn = dhv * (gv * one_sc)
        dx_ref[...] = dr_ref[...] + r * (dxn - xn * jnp.mean(dxn * xn, axis=-1, keepdims=True))
        dhxn = dhv * xn
        acc_ref[0:1, :] += jnp.sum(dhv, axis=0, keepdims=True)
        acc_ref[1:2, :] += jnp.sum(dhxn * gv, axis=0, keepdims=True)
        acc_ref[2:3, :] += jnp.sum(dhxn * one_sc, axis=0, keepdims=True)

    row = pl.BlockSpec((1, d), lambda i: (0, 0))
    blk = pl.BlockSpec((tm, d), lambda i: (i, 0))
    return pl.pallas_call(
        body, name=name,
        out_shape=(jax.ShapeDtypeStruct((s, d), F32), jax.ShapeDtypeStruct((8, d), F32)),
        grid=(s // tm,), in_specs=[blk, blk, blk, row, row],
        out_specs=(blk, pl.BlockSpec((8, d), lambda i: (0, 0))),
        compiler_params=_cparams(1),
    )(x, dh, dres, g, scale)


def _gate_bwd(dx, z, gate, *, name, tm=512):
    s, d = dx.shape
    tm = _tile(tm, s)

    def body(dx_ref, z_ref, g_ref, dz_ref, acc_ref):
        i = pl.program_id(0)

        @pl.when(i == 0)
        def _():
            acc_ref[...] = jnp.zeros_like(acc_ref)

        dxv = dx_ref[...]
        dz_ref[...] = (dxv * g_ref[...]).astype(BF16)
        acc_ref[0:1, :] += jnp.sum(dxv * z_ref[...], axis=0, keepdims=True)

    blk = pl.BlockSpec((tm, d), lambda i: (i, 0))
    return pl.pallas_call(
        body, name=name,
        out_shape=(jax.ShapeDtypeStruct((s, d), BF16), jax.ShapeDtypeStruct((8, d), F32)),
        grid=(s // tm,), in_specs=[blk, blk, pl.BlockSpec((1, d), lambda i: (0, 0))],
        out_specs=(blk, pl.BlockSpec((8, d), lambda i: (0, 0))),
        compiler_params=_cparams(1),
    )(dx, z, gate)


def _loss_grad(xf, target, *, name, tm=512):
    s, d = xf.shape
    tm = _tile(tm, s)
    nt = s // tm

    def body(x_ref, t_ref, dx_ref, loss_ref, acc_ref):
        i = pl.program_id(0)

        @pl.when(i == 0)
        def _():
            acc_ref[...] = jnp.zeros_like(acc_ref)

        e = x_ref[...] - t_ref[...]
        dx_ref[...] = e * (1.0 / d)
        acc_ref[...] += jnp.sum(e * e, axis=0, keepdims=True)

        @pl.when(i == nt - 1)
        def _():
            loss_ref[...] = (0.5 / d) * jnp.sum(acc_ref[...], axis=1, keepdims=True)

    blk = pl.BlockSpec((tm, d), lambda i: (i, 0))
    return pl.pallas_call(
        body, name=name,
        out_shape=(jax.ShapeDtypeStruct((s, d), F32), jax.ShapeDtypeStruct((1, 1), F32)),
        grid=(nt,), in_specs=[blk, blk],
        out_specs=(blk, pl.BlockSpec((1, 1), lambda i: (0, 0))),
        scratch_shapes=[pltpu.VMEM((1, d), F32)],
        compiler_params=_cparams(1),
    )(xf, target)


def _shift_down(p, prev, k):
    tm = p.shape[0]
    row = lax.broadcasted_iota(jnp.int32, p.shape, 0)
    out = pltpu.roll(p, k, 0)
    for j in range(k):
        out = jnp.where(row == j, prev[8 - k + j:8 - k + j + 1, :], out)
    return out


def _shift_up(p, nxt, k):
    tm = p.shape[0]
    row = lax.broadcasted_iota(jnp.int32, p.shape, 0)
    out = pltpu.roll(p, tm - k, 0)
    for j in range(k):
        out = jnp.where(row == tm - k + j, nxt[j:j + 1, :], out)
    return out


def _conv_fwd(u, w, *, name, tm=512):
    s = u.shape[0]
    tm = _tile(tm, s)
    c = CONV_DIM

    def body(ab_ref, ac_ref, ah_ref, w_ref, y_ref, carry_ref):
        i = pl.program_id(0)

        @pl.when(i == 0)
        def _():
            carry_ref[...] = jnp.zeros_like(carry_ref)

        p = ac_ref[...] * ah_ref[...]
        prev = carry_ref[...]
        wv = w_ref[...]
        conv = wv[2:3, :] * p + wv[1:2, :] * _shift_down(p, prev, 1) + wv[0:1, :] * _shift_down(p, prev, 2)
        y_ref[...] = (ab_ref[...] * conv).astype(BF16)
        carry_ref[...] = p[tm - 8:tm, :]

    return pl.pallas_call(
        body, name=name, out_shape=jax.ShapeDtypeStruct((s, c), BF16), grid=(s // tm,),
        in_specs=[pl.BlockSpec((tm, c), lambda i: (i, 0)), pl.BlockSpec((tm, c), lambda i: (i, 1)),
                  pl.BlockSpec((tm, c), lambda i: (i, 2)), pl.BlockSpec((3, c), lambda i: (0, 0))],
        out_specs=pl.BlockSpec((tm, c), lambda i: (i, 0)),
        scratch_shapes=[pltpu.VMEM((8, c), F32)],
        compiler_params=_cparams(1),
    )(u, u, u, w)


def _conv_bwd(u, dy, w, *, name, tm=512):
    s = u.shape[0]
    tm = _tile(tm, s)
    nt = s // tm
    c = CONV_DIM
    hb = tm // 8

    def body(ab_ref, ac_ref, ah_ref, hc_ref, hh_ref, dy_ref, w_ref, du_ref, dw_ref, carry_ref):
        i = pl.program_id(0)

        @pl.when(i == 0)
        def _():
            carry_ref[...] = jnp.zeros_like(carry_ref)
            dw_ref[...] = jnp.zeros_like(dw_ref)

        first_tile = (nt - 1 - i) == 0
        ab, ac, ah = ab_ref[...], ac_ref[...], ah_ref[...]
        p = ac * ah
        prev = jnp.where(first_tile, 0.0, hc_ref[...] * hh_ref[...])
        wv = w_ref[...]
        p1 = _shift_down(p, prev, 1)
        p2 = _shift_down(p, prev, 2)
        conv = wv[2:3, :] * p + wv[1:2, :] * p1 + wv[0:1, :] * p2
        dyv = dy_ref[...]
        dconv = dyv * ab
        nxt = carry_ref[...]
        dp = wv[2:3, :] * dconv + wv[1:2, :] * _shift_up(dconv, nxt, 1) + wv[0:1, :] * _shift_up(dconv, nxt, 2)
        du_ref[:, 0:c] = (dyv * conv).astype(BF16)
        du_ref[:, c:2 * c] = (dp * ah).astype(BF16)
        du_ref[:, 2 * c:3 * c] = (dp * ac).astype(BF16)
        dw_ref[0:1, :] += jnp.sum(dconv * p2, axis=0, keepdims=True)
        dw_ref[1:2, :] += jnp.sum(dconv * p1, axis=0, keepdims=True)
        dw_ref[2:3, :] += jnp.sum(dconv * p, axis=0, keepdims=True)
        carry_ref[...] = dconv[0:8, :]

    rev = lambda i: nt - 1 - i
    halo = lambda i: jnp.maximum(rev(i) * hb - 1, 0)
    return pl.pallas_call(
        body, name=name,
        out_shape=(jax.ShapeDtypeStruct((s, 3 * c), BF16), jax.ShapeDtypeStruct((8, c), F32)),
        grid=(nt,),
        in_specs=[pl.BlockSpec((tm, c), lambda i: (rev(i), 0)), pl.BlockSpec((tm, c), lambda i: (rev(i), 1)),
                  pl.BlockSpec((tm, c), lambda i: (rev(i), 2)),
                  pl.BlockSpec((8, c), lambda i: (halo(i), 1)), pl.BlockSpec((8, c), lambda i: (halo(i), 2)),
                  pl.BlockSpec((tm, c), lambda i: (rev(i), 0)), pl.BlockSpec((3, c), lambda i: (0, 0))],
        out_specs=(pl.BlockSpec((tm, 3 * c), lambda i: (rev(i), 0)), pl.BlockSpec((8, c), lambda i: (0, 0))),
        scratch_shapes=[pltpu.VMEM((8, c), F32)],
        compiler_params=_cparams(1),
    )(u, u, u, u, u, dy, w)


def _lower_bound(lbl):
    m = jnp.max(lbl, axis=0, keepdims=True)
    e = jnp.exp(lbl - m)
    return e[0:1, :] / jnp.sum(e, axis=0, keepdims=True)


def _hg_masks():
    t = HG_TILE
    row = lax.broadcasted_iota(jnp.int32, (t, t), 0)
    col = lax.broadcasted_iota(jnp.int32, (t, t), 1)
    same = (row >= CHUNK) == (col >= CHUNK)
    lower = same & (col <= row)
    upper = same & (row <= col)
    return row, col, lower, upper


def _hg_gates(hf, lb):
    sig = jax.nn.sigmoid(hf)
    f = lb + (1.0 - lb) * sig
    return sig, f, jnp.log(f), 1.0 - f


def _hg_refs(b_ref, hs):
    refs = []
    for i in range(HG_TILE // HG_SUB):
        if (i * HG_SUB) % CHUNK == 0:
            refs.append(jnp.zeros((1, HG_DK), F32))
        else:
            refs.append(b_ref[i * HG_SUB - 1:i * HG_SUB, hs])
    return refs


def _hgrn_fwd(u, lbl, gn, *, name):
    s = u.shape[0]
    t = HG_TILE
    nt = s // t
    nsub = t // HG_SUB
    w = HG_WIDTH

    def body(hq_ref, hf_ref, hi_ref, hg_ref, lbl_ref, gn_ref, y_ref, o_ref, sall_ref, st_ref, b_ref):
        i = pl.program_id(0)

        @pl.when(i == 0)
        def _():
            st_ref[...] = jnp.zeros_like(st_ref)

        lb = _lower_bound(lbl_ref[...])
        _, _, g, kin = _hg_gates(hf_ref[...], lb)
        _, _, lower, _ = _hg_masks()
        b_ref[...] = _exact_left(lower.astype(BF16), g)

        for h in range(HG_HEADS):
            hs = slice(h * HG_DK, (h + 1) * HG_DK)
            bh = b_ref[:, hs]
            qh = hq_ref[:, hs]
            kh = kin[:, hs]
            vh = hi_ref[:, hs]
            vsp = _sp(vh)
            refs = _hg_refs(b_ref, hs)
            rmat = jnp.concatenate([jnp.broadcast_to(r, (HG_SUB, HG_DK)) for r in refs], axis=0)
            qt = qh * jnp.exp(bh - rmat)
            prow = []
            for j in range(nsub):
                kj = kh * jnp.exp(jnp.minimum(refs[j] - bh, HG_EXP_CLAMP))
                prow.append(_dot3(_nt, _sp(qt[j * HG_SUB:(j + 1) * HG_SUB]), _sp(kj)))
            p = jnp.where(lower, jnp.concatenate(prow, axis=0), 0.0)
            intra = _dot3(_nn, _sp(p), vsp)
            o_parts = []
            for c in range(t // CHUNK):
                rs = slice(c * CHUNK, (c + 1) * CHUNK)
                st0 = st_ref[hs, :]
                sall_ref[c * w + h * HG_DK:c * w + (h + 1) * HG_DK, :] = st0
                bl = b_ref[c * CHUNK + CHUNK - 1:c * CHUNK + CHUNK, hs]
                qf = qh[rs] * jnp.exp(bh[rs])
                o_parts.append(_dot3(_nt, _sp(qf), _sp(st0)) + intra[rs])
                khat = kh[rs] * jnp.exp(bl - bh[rs])
                st_ref[hs, :] = st0 * jnp.exp(bl) + _dot3(_tn, _sp(vh[rs]), _sp(khat))
            o = jnp.concatenate(o_parts, axis=0)
            o_ref[:, hs] = o
            r = lax.rsqrt(jnp.mean(o * o, axis=-1, keepdims=True) + EPS)
            hg = hg_ref[:, hs]
            y_ref[:, hs] = (((o * r) * gn_ref[:, hs]) * (hg * jax.nn.sigmoid(hg))).astype(BF16)

    blk = lambda j: pl.BlockSpec((t, w), lambda i, j=j: (i, j))
    srows = (t // CHUNK) * w
    return pl.pallas_call(
        body, name=name,
        out_shape=(jax.ShapeDtypeStruct((s, w), BF16), jax.ShapeDtypeStruct((s, w), F32),
                   jax.ShapeDtypeStruct((nt * srows, HG_DK), F32)),
        grid=(nt,),
        in_specs=[blk(3), blk(4), blk(5), blk(6), pl.BlockSpec((3, w), lambda i: (0, 0)),
                  pl.BlockSpec((1, w), lambda i: (0, 0))],
        out_specs=(pl.BlockSpec((t, w), lambda i: (i, 0)), pl.BlockSpec((t, w), lambda i: (i, 0)),
                   pl.BlockSpec((srows, HG_DK), lambda i: (i, 0))),
        scratch_shapes=[pltpu.VMEM((w, HG_DK), F32), pltpu.VMEM((t, w), F32)],
        compiler_params=_cparams(1),
    )(u, u, u, u, lbl, gn)


def _hgrn_bwd(u, o_all, sall, dy, lbl, gn, *, name):
    s = u.shape[0]
    t = HG_TILE
    nt = s // t
    nsub = t // HG_SUB
    w = HG_WIDTH
    nch = t // CHUNK

    def body(hq_ref, hf_ref, hi_ref, hg_ref, o_ref, sall_ref, dy_ref, lbl_ref, gn_ref,
             du_ref, acc_ref, dst_ref, b_ref):
        i = pl.program_id(0)

        @pl.when(i == 0)
        def _():
            dst_ref[...] = jnp.zeros_like(dst_ref)
            acc_ref[...] = jnp.zeros_like(acc_ref)

        lb = _lower_bound(lbl_ref[...])
        sig, f, g, kin = _hg_gates(hf_ref[...], lb)
        row, col, lower, upper = _hg_masks()
        b_ref[...] = _exact_left(lower.astype(BF16), g)
        upper_bf = upper.astype(BF16)
        rowblk = [((row >= j * HG_SUB) & (row < (j + 1) * HG_SUB)) for j in range(nsub)]
        colblk = [((col >= j * HG_SUB) & (col < (j + 1) * HG_SUB)) for j in range(nsub)]
        row1 = lax.broadcasted_iota(jnp.int32, (t, HG_DK), 0)

        for h in range(HG_HEADS):
            hs = slice(h * HG_DK, (h + 1) * HG_DK)
            bh = b_ref[:, hs]
            qh = hq_ref[:, hs]
            kh = kin[:, hs]
            vh = hi_ref[:, hs]
            vsp = _sp(vh)
            hg = hg_ref[:, hs]
            gnh = gn_ref[:, hs]
            o = o_ref[:, hs]
            dyv = dy_ref[:, hs]
            sg = jax.nn.sigmoid(hg)
            r = lax.rsqrt(jnp.mean(o * o, axis=-1, keepdims=True) + EPS)
            ohat = o * r
            du_ref[:, 3 * w + h * HG_DK:3 * w + (h + 1) * HG_DK] = (
                dyv * (ohat * gnh) * (sg * (1.0 + hg * (1.0 - sg)))).astype(BF16)
            don = dyv * (hg * sg)
            acc_ref[0:1, hs] += jnp.sum(don * ohat, axis=0, keepdims=True)
            dohat = don * gnh
            do = r * (dohat - ohat * jnp.mean(dohat * ohat, axis=-1, keepdims=True))
            dosp = _sp(do)
            refs = _hg_refs(b_ref, hs)
            rmat = jnp.concatenate([jnp.broadcast_to(rr, (HG_SUB, HG_DK)) for rr in refs], axis=0)
            eq = jnp.exp(bh - rmat)
            qt = qh * eq
            qtsp = _sp(qt)
            dp = jnp.where(lower, _dot3(_nt, dosp, vsp), 0.0)
            dpt = jnp.where(upper, _dot3(_nt, vsp, dosp), 0.0)
            pt = jnp.zeros((t, t), F32)
            dk = jnp.zeros((t, HG_DK), F32)
            dq_rows = []
            for j in range(nsub):
                ek = jnp.exp(jnp.minimum(refs[j] - bh, HG_EXP_CLAMP))
                kjsp = _sp(kh * ek)
                pt = pt + _dot3(_nt, kjsp, _sp(jnp.where(rowblk[j], qt, 0.0)))
                dq_rows.append(_dot3(_nn, _sp(dp[j * HG_SUB:(j + 1) * HG_SUB]), kjsp))
                dk = dk + ek * _dot3(_nn, _sp(jnp.where(colblk[j], dpt, 0.0)), qtsp)
            pt = jnp.where(upper, pt, 0.0)
            dv = _dot3(_nn, _sp(pt), dosp)
            dq = jnp.concatenate(dq_rows, axis=0) * eq
            dq_c, dk_c, dv_c, ex_c = [None] * nch, [None] * nch, [None] * nch, [None] * nch
            for c in reversed(range(nch)):
                rs = slice(c * CHUNK, (c + 1) * CHUNK)
                st0 = sall_ref[c * w + h * HG_DK:c * w + (h + 1) * HG_DK, :]
                dst1 = dst_ref[hs, :]
                dst1sp = _sp(dst1)
                dosp_c = _sp(do[rs])
                bl = b_ref[c * CHUNK + CHUNK - 1:c * CHUNK + CHUNK, hs]
                e = jnp.exp(bh[rs])
                el = jnp.exp(bl)
                ekl = jnp.exp(bl - bh[rs])
                dq_c[c] = _dot3(_nn, dosp_c, _sp(st0)) * e
                khat = kh[rs] * ekl
                dv_c[c] = _dot3(_nt, _sp(khat), dst1sp)
                dkhat = _dot3(_nn, _sp(vh[rs]), dst1sp)
                dk_c[c] = dkhat * ekl
                ex_c[c] = (jnp.sum(dkhat * khat, axis=0, keepdims=True)
                           + el * jnp.sum(dst1 * st0, axis=0, keepdims=True))
                dst_ref[hs, :] = _dot3(_tn, dosp_c, _sp(qh[rs] * e)) + dst1 * el
            dq = dq + jnp.concatenate(dq_c, axis=0)
            dk = dk + jnp.concatenate(dk_c, axis=0)
            dv = dv + jnp.concatenate(dv_c, axis=0)
            db = qh * dq - kh * dk
            for c in range(nch):
                db = db + jnp.where(row1 == c * CHUNK + CHUNK - 1, ex_c[c], 0.0)
            dg = _exact_left(upper_bf, db)
            fh = f[:, hs]
            sgf = sig[:, hs]
            lbh = lb[:, hs]
            df = dg / fh - dk
            du_ref[:, hs] = dq.astype(BF16)
            du_ref[:, w + h * HG_DK:w + (h + 1) * HG_DK] = (df * (1.0 - lbh) * sgf * (1.0 - sgf)).astype(BF16)
            du_ref[:, 2 * w + h * HG_DK:2 * w + (h + 1) * HG_DK] = dv.astype(BF16)
            acc_ref[1:2, hs] += jnp.sum(df * (1.0 - sgf), axis=0, keepdims=True)

    rev = lambda i: nt - 1 - i
    blk = lambda j: pl.BlockSpec((t, w), lambda i, j=j: (rev(i), j))
    srows = nch * w
    return pl.pallas_call(
        body, name=name,
        out_shape=(jax.ShapeDtypeStruct((s, 4 * w), BF16), jax.ShapeDtypeStruct((8, w), F32)),
        grid=(nt,),
        in_specs=[blk(3), blk(4), blk(5), blk(6), pl.BlockSpec((t, w), lambda i: (rev(i), 0)),
                  pl.BlockSpec((srows, HG_DK), lambda i: (rev(i), 0)),
                  pl.BlockSpec((t, w), lambda i: (rev(i), 1)),
                  pl.BlockSpec((3, w), lambda i: (0, 0)), pl.BlockSpec((1, w), lambda i: (0, 0))],
        out_specs=(pl.BlockSpec((t, 4 * w), lambda i: (rev(i), 0)), pl.BlockSpec((8, w), lambda i: (0, 0))),
        scratch_shapes=[pltpu.VMEM((w, HG_DK), F32), pltpu.VMEM((t, w), F32)],
        compiler_params=_cparams(1),
    )(u, u, u, u, o_all, sall, dy, lbl, gn)


def _pair_matrix():
    row = lax.broadcasted_iota(jnp.int32, (LANES, LANES), 0)
    col = lax.broadcasted_iota(jnp.int32, (LANES, LANES), 1)
    return ((row >= SB_HEAD_DIM) == (col >= SB_HEAD_DIM)).astype(BF16)


def _qk_norm_fwd(qkv, qn, kn, *, name, tm=256):
    s = qkv.shape[0]
    d = D_MODEL
    tm = _tile(tm, s)

    def body(q_ref, k_ref, v_ref, qn_ref, kn_ref, qo_ref, ko_ref, vo_ref):
        bd = _pair_matrix()
        for src, gain, dst, fac in ((q_ref, qn_ref, qo_ref, SB_SCALE * LOG2E), (k_ref, kn_ref, ko_ref, None)):
            for grp in range(d // LANES):
                ls = slice(grp * LANES, (grp + 1) * LANES)
                xv = src[:, ls]
                ms = _exact_right(xv * xv, bd) * (1.0 / SB_HEAD_DIM)
                y = (xv * lax.rsqrt(ms + EPS)) * gain[:, ls]
                dst[:, ls] = (y if fac is None else y * fac).astype(BF16)
        vo_ref[...] = v_ref[...].astype(BF16)

    blk = lambda j: pl.BlockSpec((tm, d), lambda i, j=j: (i, j))
    row = pl.BlockSpec((1, d), lambda i: (0, 0))
    out = jax.ShapeDtypeStruct((s, d), BF16)
    return pl.pallas_call(
        body, name=name, out_shape=(out, out, out), grid=(s // tm,),
        in_specs=[blk(0), blk(1), blk(2), row, row],
        out_specs=(blk(0), blk(0), blk(0)),
        compiler_params=_cparams(1),
    )(qkv, qkv, qkv, qn, kn)


def _qk_norm_bwd(qkv, dqn, dkn, dv, qn, kn, *, name, tm=256):
    s = qkv.shape[0]
    d = D_MODEL
    tm = _tile(tm, s)

    def body(q_ref, k_ref, dq_ref, dk_ref, dv_ref, qn_ref, kn_ref, o_ref, acc_ref):
        i = pl.program_id(0)

        @pl.when(i == 0)
        def _():
            acc_ref[...] = jnp.zeros_like(acc_ref)

        bd = _pair_matrix()
        for idx, (src, dsrc, gain) in enumerate(((q_ref, dq_ref, qn_ref), (k_ref, dk_ref, kn_ref))):
            for grp in range(d // LANES):
                ls = slice(grp * LANES, (grp + 1) * LANES)
                xv = src[:, ls]
                dyv = dsrc[:, ls]
                r = lax.rsqrt(_exact_right(xv * xv, bd) * (1.0 / SB_HEAD_DIM) + EPS)
                xh = xv * r
                acc_ref[idx:idx + 1, ls] += jnp.sum(dyv * xh, axis=0, keepdims=True)
                dxh = dyv * gain[:, ls]
                mean = _exact_right(dxh * xh, bd) * (1.0 / SB_HEAD_DIM)
                o_ref[:, idx * d + grp * LANES:idx * d + (grp + 1) * LANES] = (r * (dxh - xh * mean)).astype(BF16)
        o_ref[:, 2 * d:3 * d] = dv_ref[...].astype(BF16)

    blk = lambda j: pl.BlockSpec((tm, d), lambda i, j=j: (i, j))
    row = pl.BlockSpec((1, d), lambda i: (0, 0))
    return pl.pallas_call(
        body, name=name,
        out_shape=(jax.ShapeDtypeStruct((s, 3 * d), BF16), jax.ShapeDtypeStruct((8, d), F32)),
        grid=(s // tm,),
        in_specs=[blk(0), blk(1), blk(0), blk(0), blk(0), row, row],
        out_specs=(pl.BlockSpec((tm, 3 * d), lambda i: (i, 0)), pl.BlockSpec((8, d), lambda i: (0, 0))),
        compiler_params=_cparams(1),
    )(qkv, qkv, dqn, dkn, dv, qn, kn)


def _sb_tile(qh, kb, suffix_ones, run, mask):
    z = _nt(qh, kb)
    neg_abs = lax.bitcast_convert_type(lax.bitcast_convert_type(z, jnp.uint32) | jnp.uint32(0x80000000), F32)
    l1m = -(jnp.maximum(z, 0.0) + jnp.log2(1.0 + jnp.exp2(neg_abs)))
    logb = z + l1m
    if mask is not None:
        l1m = jnp.where(mask, l1m, 0.0)
    later = _nn(l1m.astype(BF16), suffix_ones) + run
    wgt = jnp.exp2(logb + later)
    if mask is not None:
        wgt = jnp.where(mask, wgt, 0.0)
    return logb, l1m, wgt


def _suffix_ones(tk):
    row = lax.broadcasted_iota(jnp.int32, (tk, tk), 0)
    col = lax.broadcasted_iota(jnp.int32, (tk, tk), 1)
    return (row > col).astype(BF16)


def _sb_mask(qi, j, tq, tk):
    qpos = qi * tq + lax.broadcasted_iota(jnp.int32, (tq, tk), 0)
    kpos = j * tk + lax.broadcasted_iota(jnp.int32, (tq, tk), 1)
    return kpos < qpos


def _sb_fwd(qn, kn, v, *, name):
    s, d = qn.shape
    tq, tk = _tile(SB_TQ, s), _tile(SB_TK, s)
    assert tk % tq == 0 or tq % tk == 0
    nq = s // tq

    def body(q_ref, k_ref, v_ref, o_ref, acc_ref):
        qi = pl.program_id(1)
        lane = lax.broadcasted_iota(jnp.int32, (tq, LANES), 1)
        first = lane < SB_HEAD_DIM
        q = q_ref[...]
        qh = [jnp.where(first, q, 0).astype(BF16), jnp.where(first, 0, q).astype(BF16)]
        ones = _suffix_ones(tk)
        acc_ref[...] = jnp.zeros_like(acc_ref)

        def tile(j, runs, masked):
            ks = pl.ds(pl.multiple_of(j * tk, tk), tk)
            kb = k_ref[ks, :]
            vb = v_ref[ks, :]
            mask = _sb_mask(qi, j, tq, tk) if masked else None
            new_runs = []
            for hh in range(2):
                _, l1m, wgt = _sb_tile(qh[hh], kb, ones, runs[hh], mask)
                acc_ref[hh] += _nn(wgt.astype(BF16), vb)
                new_runs.append(runs[hh] + jnp.sum(l1m, axis=1, keepdims=True))
            return tuple(new_runs)

        nfull = (qi * tq) // tk
        zero = jnp.zeros((tq, 1), F32)
        runs = (zero, zero)
        for m in reversed(range(max(tq // tk, 1))):
            runs = tile(nfull + m, runs, True)
        lax.fori_loop(0, nfull, lambda it, r: tile(nfull - 1 - it, r, False), runs)
        o_ref[...] = jnp.where(first, acc_ref[0], acc_ref[1])

    return pl.pallas_call(
        body, name=name, out_shape=jax.ShapeDtypeStruct((s, d), F32), grid=(d // LANES, nq),
        in_specs=[pl.BlockSpec((tq, LANES), lambda p, i: (i, p)), pl.BlockSpec((s, LANES), lambda p, i: (0, p)),
                  pl.BlockSpec((s, LANES), lambda p, i: (0, p))],
        out_specs=pl.BlockSpec((tq, LANES), lambda p, i: (i, p)),
        scratch_shapes=[pltpu.VMEM((2, tq, LANES), F32)],
        compiler_params=_cparams(2),
    )(qn, kn, v)


def _sb_bwd(qn, kn, v, o, do, *, name):
    s, d = qn.shape
    tq, tk = _tile(SB_TQ, s), _tile(SB_TK, s)
    assert tk % tq == 0 or tq % tk == 0
    nq = s // tq

    def body(q_ref, k_ref, v_ref, o_ref, do_ref, dq_ref, dk_ref, dv_ref, acc_ref):
        qi = pl.program_id(1)

        @pl.when(qi == 0)
        def _():
            dk_ref[...] = jnp.zeros_like(dk_ref)
            dv_ref[...] = jnp.zeros_like(dv_ref)

        first = lax.broadcasted_iota(jnp.int32, (tq, LANES), 1) < SB_HEAD_DIM
        sel = [first, jnp.logical_not(first)]
        kfirst = lax.broadcasted_iota(jnp.int32, (tk, LANES), 1) < SB_HEAD_DIM
        ksel = [kfirst, jnp.logical_not(kfirst)]
        q = q_ref[...]
        dob = do_ref[...].astype(BF16)
        qh = [jnp.where(sel[hh], q, 0).astype(BF16) for hh in range(2)]
        doh = [jnp.where(sel[hh], dob, 0).astype(BF16) for hh in range(2)]
        prod = dob.astype(F32) * o_ref[...]
        gtot = [jnp.sum(jnp.where(sel[hh], prod, 0.0), axis=1, keepdims=True) for hh in range(2)]
        ones = _suffix_ones(tk)
        acc_ref[...] = jnp.zeros_like(acc_ref)

        def tile(j, carry, masked):
            runs, gruns = carry
            ks = pl.ds(pl.multiple_of(j * tk, tk), tk)
            kb = k_ref[ks, :]
            vb = v_ref[ks, :]
            mask = _sb_mask(qi, j, tq, tk) if masked else None
            new_runs, new_gruns = [], []
            dk_add = jnp.zeros((tk, LANES), F32)
            dv_add = jnp.zeros((tk, LANES), F32)
            for hh in range(2):
                logb, l1m, wgt = _sb_tile(qh[hh], kb, ones, runs[hh], mask)
                wb = wgt.astype(BF16)
                g = _nt(doh[hh], vb) * wb.astype(F32)
                gsuf = _exact_right2(g, ones) + g + gruns[hh]
                dz = g - jnp.exp2(logb) * (g + (gtot[hh] - gsuf))
                if masked:
                    dz = jnp.where(mask, dz, 0.0)
                dzb = dz.astype(BF16)
                acc_ref[hh] += _nn(dzb, kb)
                dk_add = dk_add + jnp.where(ksel[hh], _tn(dzb, qh[hh]), 0.0)
                dv_add = dv_add + jnp.where(ksel[hh], _tn(wb, doh[hh]), 0.0)
                new_runs.append(runs[hh] + jnp.sum(l1m, axis=1, keepdims=True))
                new_gruns.append(gruns[hh] + jnp.sum(g, axis=1, keepdims=True))
            dk_ref[ks, :] += dk_add * LN2
            dv_ref[ks, :] += dv_add
            return tuple(new_runs), tuple(new_gruns)

        nfull = (qi * tq) // tk
        zero = jnp.zeros((tq, 1), F32)
        carry = ((zero, zero), (zero, zero))
        for m in reversed(range(max(tq // tk, 1))):
            carry = tile(nfull + m, carry, True)
        lax.fori_loop(0, nfull, lambda it, cr: tile(nfull - 1 - it, cr, False), carry)
        dq_ref[...] = jnp.where(first, acc_ref[0], acc_ref[1]) * SB_SCALE

    blk = pl.BlockSpec((tq, LANES), lambda p, i: (i, p))
    full = pl.BlockSpec((s, LANES), lambda p, i: (0, p))
    out = jax.ShapeDtypeStruct((s, d), F32)
    return pl.pallas_call(
        body, name=name, out_shape=(out, out, out), grid=(d // LANES, nq),
        in_specs=[blk, full, full, blk, blk],
        out_specs=(blk, full, full),
        scratch_shapes=[pltpu.VMEM((2, tq, LANES), F32)],
        compiler_params=_cparams(2),
    )(qn, kn, v, o, do)


def _mod_part(c_all, ada_w, ada_b_my, *, name):
    nl, d, ncol = ada_w.shape

    def body(c_ref, w_ref, b_ref, part_ref, ca_ref):
        cv = c_ref[...]
        ca = cv * jax.nn.sigmoid(cv)
        ca_ref[...] = ca
        part_ref[...] = _nn(ca.astype(BF16), w_ref[...].astype(BF16)) + b_ref[...]

    return pl.pallas_call(
        body, name=name,
        out_shape=(jax.ShapeDtypeStruct((nl, N_DEV, ncol), F32), jax.ShapeDtypeStruct((N_DEV, d), F32)),
        grid=(nl,),
        in_specs=[pl.BlockSpec((N_DEV, d), lambda l: (0, 0)), pl.BlockSpec((None, d, ncol), lambda l: (l, 0, 0)),
                  pl.BlockSpec((None, 1, ncol), lambda l: (l, 0, 0))],
        out_specs=(pl.BlockSpec((None, N_DEV, ncol), lambda l: (l, 0, 0)), pl.BlockSpec((N_DEV, d), lambda l: (0, 0))),
        compiler_params=_cparams(1),
    )(c_all, ada_w, ada_b_my)


PK_MOD, PK_NMIX, PK_NMLP, PK_HGN, PK_LB, PK_QN, PK_KN, PK_CONV, PK_ROWS = 0, 96, 112, 128, 132, 136, 144, 152, 168


def _small_grads(gath, ca_col, dmod_my, lbl4, *, name):
    def body(g_ref, ca_ref, dm_ref, lbl_ref, gw_ref, gsum_ref, glb_ref, gqk_ref):
        tot = g_ref[0]
        for dev in range(1, N_DEV):
            tot = tot + g_ref[dev]
        gsum_ref[...] = tot
        lv = lbl_ref[...]
        m = jnp.maximum(jnp.maximum(lv[0], lv[1]), lv[2])
        e = [jnp.exp(lv[k] - m) for k in range(3)]
        den = e[0] + e[1] + e[2]
        p = [ek / den for ek in e]
        dlb = tot[PK_LB:PK_LB + 4, :]
        glb_ref[0] = dlb * p[0] * (1.0 - p[0])
        glb_ref[1] = -dlb * p[0] * p[1]
        glb_ref[2] = -dlb * p[0] * p[2]
        for idx, base in enumerate((PK_QN, PK_KN)):
            rowsum = jnp.sum(tot[base:base + 8, :], axis=0, keepdims=True)
            gqk_ref[idx:idx + 1, :] = rowsum + pltpu.roll(rowsum, SB_HEAD_DIM, 1)
        for l in range(2):
            acc = ca_ref[0] * dm_ref[0, l:l + 1, :]
            for smp in range(1, N_DEV):
                acc = acc + ca_ref[smp] * dm_ref[smp, l:l + 1, :]
            gw_ref[l] = acc

    d, ncol = ca_col.shape[1], dmod_my.shape[2]
    vm = pl.BlockSpec(memory_space=pltpu.VMEM)
    return pl.pallas_call(
        body, name=name,
        out_shape=(jax.ShapeDtypeStruct((2, d, ncol), F32), jax.ShapeDtypeStruct((PK_ROWS, LANES), F32),
                   jax.ShapeDtypeStruct((3, 4, LANES), F32), jax.ShapeDtypeStruct((8, LANES), F32)),
        in_specs=[vm, vm, vm, vm], out_specs=(vm, vm, vm, vm),
        compiler_params=pltpu.CompilerParams(vmem_limit_bytes=VMEM_LIMIT),
    )(gath, ca_col, dmod_my, lbl4)


def _adamw_math(w, g, m, v):
    m = ADAM_B1 * m + (1.0 - ADAM_B1) * g
    v = ADAM_B2 * v + (1.0 - ADAM_B2) * (g * g)
    m_hat = m / (1.0 - ADAM_B1 ** ADAM_STEP)
    v_hat = v / (1.0 - ADAM_B2 ** ADAM_STEP)
    delta = -ADAM_LR * (m_hat / (jnp.sqrt(v_hat) + ADAM_EPS) + ADAM_WD * w)
    return delta, m, v


def _adamw(w, g, m, v, *, name, tr=256):
    r, n = w.shape
    tr = _tile(tr, r)

    def body(w_ref, g_ref, m_ref, v_ref, d_ref, mo_ref, vo_ref):
        dl, mn, vn = _adamw_math(w_ref[...], g_ref[...], m_ref[...], v_ref[...])
        d_ref[...] = dl
        mo_ref[...] = mn
        vo_ref[...] = vn

    blk = pl.BlockSpec((tr, n), lambda i: (i, 0))
    out = jax.ShapeDtypeStruct((r, n), F32)
    return pl.pallas_call(
        body, name=name, out_shape=(out, out, out), grid=(r // tr,),
        in_specs=[blk, blk, blk, blk], out_specs=(blk, blk, blk),
        compiler_params=_cparams(1),
    )(w, g, m, v)


def _adamw_small(items, *, name):
    n = len(items)

    def body(*refs):
        ins, outs = refs[:4 * n], refs[4 * n:]
        for k in range(n):
            dl, mn, vn = _adamw_math(*(r[...] for r in ins[4 * k:4 * k + 4]))
            outs[3 * k][...] = dl
            outs[3 * k + 1][...] = mn
            outs[3 * k + 2][...] = vn

    flat = [a for it in items for a in it]
    out_shape = tuple(jax.ShapeDtypeStruct(it[0].shape, F32) for it in items for _ in range(3))
    vm = pl.BlockSpec(memory_space=pltpu.VMEM)
    res = pl.pallas_call(
        body, name=name, out_shape=out_shape, in_specs=[vm] * (4 * n), out_specs=tuple([vm] * (3 * n)),
    )(*flat)
    return [tuple(res[3 * k:3 * k + 3]) for k in range(n)]


def _mlp_fwd(x, g, scale, shift, gate, w1, w2, tag):
    h = _norm_mod(x, g, scale, shift, name=f"{tag}_norm")
    act = _matmul(h, w1, epi="relu2", out_dtype=BF16, name=f"{tag}_w1")
    z, x_out = _matmul(act, w2, epi="resgate", extras=(x, gate), name=f"{tag}_w2")
    return x_out, (h, act, z)


def _mlp_bwd(dx_out, x, saved, g, scale, gate, w1, w2, tag):
    h, act, z = saved
    dz, gate_acc = _gate_bwd(dx_out, z, gate, name=f"{tag}_gate_bwd")
    du = _matmul(dz, w2, tb=True, epi="dact", extras=(act,), out_dtype=BF16, name=f"{tag}_dact")
    dw2 = _matmul(act, dz, ta=True, name=f"{tag}_dw2")
    dw1 = _matmul(h, du, ta=True, name=f"{tag}_dw1")
    dh = _matmul(du, w1, tb=True, name=f"{tag}_dh")
    dx, nacc = _norm_mod_bwd(x, dh, dx_out, g, scale, name=f"{tag}_norm_bwd")
    return dx, dw1, dw2, (nacc[0:1], nacc[1:2], gate_acc[0:1]), nacc[2:3]


def kernel(x, c, ada_w, ada_b, norm_mix, norm_mlp, w_in_ab, conv_w, hg_norm, lb_logits, w_out_ab, w_qkv, q_norm, k_norm, w_out_c, mlp_w1, mlp_w2, loss_target, m_ada_w, m_ada_b, m_norm_mix, m_norm_mlp, m_w_in_ab, m_conv_w, m_hg_norm, m_lb_logits, m_w_out_ab, m_w_qkv, m_q_norm, m_k_norm, m_w_out_c, m_mlp_w1, m_mlp_w2, v_ada_w, v_ada_b, v_norm_mix, v_norm_mlp, v_w_in_ab, v_conv_w, v_hg_norm, v_lb_logits, v_w_out_ab, v_w_qkv, v_q_norm, v_k_norm, v_w_out_c, v_mlp_w1, v_mlp_w2):
    d = D_MODEL
    my_x, my_y, my_c = lax.axis_index("x"), lax.axis_index("y"), lax.axis_index("c")
    me = 4 * my_x + 2 * my_y + my_c
    xs = x[0]
    tgt = loss_target[0]

    big = [w_in_ab, w_out_ab, w_qkv, w_out_c, mlp_w1, mlp_w2]
    rows = [w.size // d for w in big]
    offs = [sum(rows[:k]) for k in range(len(rows) + 1)]
    packed = jnp.concatenate([w.astype(BF16).reshape(-1, d) for w in big], axis=0)
    gathered = _all_gather(packed, name="gather_weights", in_vmem=False).reshape(N_DEV, offs[-1], d)

    def piece(k):
        return gathered[:, offs[k]:offs[k + 1]]

    def cols(k, nl, kdim, ncol):
        return piece(k).reshape(N_DEV, nl, kdim, ncol).transpose(1, 2, 0, 3).reshape(nl, kdim, N_DEV * ncol)

    def rws(k, nl, nrow, n):
        return piece(k).reshape(N_DEV, nl, nrow, n).transpose(1, 0, 2, 3).reshape(nl, N_DEV * nrow, n)

    win = cols(0, 1, d, AB_IN // N_DEV)[0]
    wout_ab = rws(1, 1, d // N_DEV, d)[0]
    wqkv = cols(2, 1, d, 3 * d // N_DEV)[0]
    wout_c = rws(3, 1, d // N_DEV, d)[0]
    w1 = cols(4, 2, d, D_FF // N_DEV)
    w2 = rws(5, 2, D_FF // N_DEV, d)

    ncv = CONV_DIM // N_DEV
    c_and_conv = jnp.concatenate([c, jnp.pad(conv_w[0], ((0, 0), (0, d - ncv))), jnp.zeros((4, d), F32)], axis=0)
    c_and_conv = _all_gather(c_and_conv, name="gather_c", in_vmem=True).reshape(N_DEV, 8, d)
    c_all = c_and_conv[:, 0]
    conv_full = c_and_conv[:, 1:4, :ncv].transpose(1, 0, 2).reshape(3, CONV_DIM)
    ncol = ada_w.shape[2]
    ada_b_my = lax.dynamic_slice(ada_b, (0, me * ncol), (2, ncol)).reshape(2, 1, ncol)
    part, c_act = _mod_part(c_all, ada_w, ada_b_my, name="mod_part")
    parts = _all_gather(part.reshape(2 * N_DEV, ncol), name="gather_mod", in_vmem=True)
    parts = parts.reshape(N_DEV, 2, N_DEV, ncol)
    mod = lax.dynamic_index_in_dim(parts, me, axis=2, keepdims=False)
    mod = mod.transpose(1, 0, 2).reshape(2, 6, 1, d)

    qn_t = jnp.tile(q_norm, (1, d // SB_HEAD_DIM))
    kn_t = jnp.tile(k_norm, (1, d // SB_HEAD_DIM))

    sh1, sc1, gt1, sh2, sc2, gt2 = [mod[0, k] for k in range(6)]
    h0 = _norm_mod(xs, norm_mix[0:1], sc1, sh1, name="l0_mix_norm")
    u = _matmul(h0, win, name="l0_in_proj")
    y_a = _conv_fwd(u, conv_full, name="l0_conv")
    y_b, o_hg, sall = _hgrn_fwd(u, lb_logits, hg_norm, name="l0_hgrn")
    y_ab = jnp.concatenate([y_a, y_b], axis=1)
    z0, x_mid0 = _matmul(y_ab, wout_ab, epi="resgate", extras=(xs, gt1), name="l0_out_proj")
    x1, mlp0 = _mlp_fwd(x_mid0, norm_mlp[0:1], sc2, sh2, gt2, w1[0], w2[0], "l0_mlp")

    sh1b, sc1b, gt1b, sh2b, sc2b, gt2b = [mod[1, k] for k in range(6)]
    h1 = _norm_mod(x1, norm_mix[1:2], sc1b, sh1b, name="l1_mix_norm")
    qkv = _matmul(h1, wqkv, name="l1_qkv_proj")
    qn_a, kn_a, v_a = _qk_norm_fwd(qkv, qn_t, kn_t, name="l1_qk_norm")
    o_sb = _sb_fwd(qn_a, kn_a, v_a, name="l1_sb")
    z1, x_mid1 = _matmul(o_sb, wout_c, epi="resgate", extras=(x1, gt1b), name="l1_out_proj")
    x2, mlp1 = _mlp_fwd(x_mid1, norm_mlp[1:2], sc2b, sh2b, gt2b, w1[1], w2[1], "l1_mlp")

    dx, loss_part = _loss_grad(x2, tgt, name="loss")
    loss = lax.psum(loss_part[0, 0], MESH_AXES)

    dx, dw1_1, dw2_1, (dsh2b, dsc2b, dgt2b), dnmlp1 = _mlp_bwd(
        dx, x_mid1, mlp1, norm_mlp[1:2], sc2b, gt2b, w1[1], w2[1], "l1_mlp")
    dyp, gacc = _gate_bwd(dx, z1, gt1b, name="l1_mix_gate_bwd")
    dwout_c = _matmul(o_sb, dyp, ta=True, name="l1_dwout")
    do_sb = _matmul(dyp, wout_c, tb=True, name="l1_do")
    dqn_a, dkn_a, dv_a = _sb_bwd(qn_a, kn_a, v_a, o_sb, do_sb, name="l1_sb_bwd")
    dqkv, qkacc = _qk_norm_bwd(qkv, dqn_a, dkn_a, dv_a, qn_t, kn_t, name="l1_qk_norm_bwd")
    dwqkv = _matmul(h1, dqkv, ta=True, name="l1_dwqkv")
    dh1 = _matmul(dqkv, wqkv, tb=True, name="l1_dh")
    dx, nacc = _norm_mod_bwd(x1, dh1, dx, norm_mix[1:2], sc1b, name="l1_mix_norm_bwd")
    dmod1 = [nacc[0:1], nacc[1:2], gacc[0:1], dsh2b, dsc2b, dgt2b]
    dnmix1 = nacc[2:3]

    dx, dw1_0, dw2_0, (dsh2, dsc2, dgt2), dnmlp0 = _mlp_bwd(
        dx, x_mid0, mlp0, norm_mlp[0:1], sc2, gt2, w1[0], w2[0], "l0_mlp")
    dyp, gacc = _gate_bwd(dx, z0, gt1, name="l0_mix_gate_bwd")
    dwout_ab = _matmul(y_ab, dyp, ta=True, name="l0_dwout")
    dy_ab = _matmul(dyp, wout_ab, tb=True, name="l0_dy")
    du_a, dconv = _conv_bwd(u, dy_ab, conv_full, name="l0_conv_bwd")
    du_b, hgacc = _hgrn_bwd(u, o_hg, sall, dy_ab, lb_logits, hg_norm, name="l0_hgrn_bwd")
    du = jnp.concatenate([du_a, du_b], axis=1)
    dwin = _matmul(h0, du, ta=True, name="l0_dwin")
    dh0 = _matmul(du, win, tb=True, tk=512, name="l0_dh")
    grad_x, nacc = _norm_mod_bwd(xs, dh0, dx, norm_mix[0:1], sc1, name="l0_mix_norm_bwd")
    dmod0 = [nacc[0:1], nacc[1:2], gacc[0:1], dsh2, dsc2, dgt2]
    dnmix0 = nacc[2:3]

    def to_cols(gw, nl, kdim, ncol_):
        return gw.reshape(nl, kdim, N_DEV, ncol_).transpose(2, 0, 1, 3).reshape(N_DEV, -1, d)

    def to_rows(gw, nl, nrow, n):
        return gw.reshape(nl, N_DEV, nrow, n).transpose(1, 0, 2, 3).reshape(N_DEV, -1, d)

    gbig = jnp.concatenate([
        to_cols(dwin[None], 1, d, AB_IN // N_DEV), to_rows(dwout_ab[None], 1, d // N_DEV, d),
        to_cols(dwqkv[None], 1, d, 3 * d // N_DEV), to_rows(dwout_c[None], 1, d // N_DEV, d),
        to_cols(jnp.stack([dw1_0, dw1_1]), 2, d, D_FF // N_DEV),
        to_rows(jnp.stack([dw2_0, dw2_1]), 2, D_FF // N_DEV, d)], axis=1)
    sib = _rs_sibling_exchange(gbig, name="rs_sibling")
    pair = _rs_pair_sum(gbig, sib, my_c.reshape(1).astype(jnp.int32), name="rs_pair_sum")
    far = _rs_chip_exchange(pair, name="rs_chips")
    gsh = _rs_final_sum(pair, far, (2 * my_x + my_y).reshape(1).astype(jnp.int32), name="rs_final_sum")
    g_big = [gsh[offs[k]:offs[k + 1]].reshape(big[k].shape) for k in range(len(big))]

    packed_small = jnp.concatenate(
        [jnp.concatenate(dmod0, axis=1).reshape(-1, LANES), jnp.concatenate(dmod1, axis=1).reshape(-1, LANES),
         dnmix0.reshape(-1, LANES), dnmix1.reshape(-1, LANES), dnmlp0.reshape(-1, LANES), dnmlp1.reshape(-1, LANES),
         hgacc[0:1].reshape(-1, LANES), hgacc[1:2].reshape(-1, LANES),
         qkacc[0:1].reshape(-1, LANES), qkacc[1:2].reshape(-1, LANES),
         dconv[0:3].reshape(-1, LANES), jnp.zeros((PK_ROWS - PK_CONV - 12, LANES), F32)], axis=0)
    gath = _all_gather(packed_small, name="gather_small_grads", in_vmem=True).reshape(N_DEV, PK_ROWS, LANES)
    dmod_all = gath[:, PK_MOD:PK_NMIX].reshape(N_DEV, 2, 6 * d)
    dmod_my = lax.dynamic_slice(dmod_all, (0, 0, me * ncol), (N_DEV, 2, ncol))
    g_ada_w, gsum, g_lb, g_qk = _small_grads(gath, c_act[:, :, None], dmod_my, lb_logits.reshape(3, 4, LANES),
                                             name="small_grads")
    g_ada_b = gsum[PK_MOD:PK_NMIX].reshape(2, 6 * d)
    g_norm_mix = gsum[PK_NMIX:PK_NMLP].reshape(2, d)
    g_norm_mlp = gsum[PK_NMLP:PK_HGN].reshape(2, d)
    g_hg_norm = gsum[PK_HGN:PK_LB].reshape(1, HG_WIDTH)
    g_lb_logits = g_lb.reshape(3, HG_WIDTH)
    g_q_norm = g_qk[0:1, :SB_HEAD_DIM]
    g_k_norm = g_qk[1:2, :SB_HEAD_DIM]
    g_conv_w = lax.dynamic_slice(gsum[PK_CONV:PK_CONV + 12].reshape(3, CONV_DIM), (0, me * ncv), (3, ncv))[None]

    def flat2(a):
        return a.reshape(-1, a.shape[-1])

    grads = dict(ada_w=g_ada_w, ada_b=g_ada_b, norm_mix=g_norm_mix, norm_mlp=g_norm_mlp, w_in_ab=g_big[0],
                 conv_w=g_conv_w, hg_norm=g_hg_norm, lb_logits=g_lb_logits, w_out_ab=g_big[1], w_qkv=g_big[2],
                 q_norm=g_q_norm, k_norm=g_k_norm, w_out_c=g_big[3], mlp_w1=g_big[4], mlp_w2=g_big[5])
    weights = dict(ada_w=(ada_w, m_ada_w, v_ada_w), ada_b=(ada_b, m_ada_b, v_ada_b),
                   norm_mix=(norm_mix, m_norm_mix, v_norm_mix), norm_mlp=(norm_mlp, m_norm_mlp, v_norm_mlp),
                   w_in_ab=(w_in_ab, m_w_in_ab, v_w_in_ab), conv_w=(conv_w, m_conv_w, v_conv_w),
                   hg_norm=(hg_norm, m_hg_norm, v_hg_norm), lb_logits=(lb_logits, m_lb_logits, v_lb_logits),
                   w_out_ab=(w_out_ab, m_w_out_ab, v_w_out_ab), w_qkv=(w_qkv, m_w_qkv, v_w_qkv),
                   q_norm=(q_norm, m_q_norm, v_q_norm), k_norm=(k_norm, m_k_norm, v_k_norm),
                   w_out_c=(w_out_c, m_w_out_c, v_w_out_c), mlp_w1=(mlp_w1, m_mlp_w1, v_mlp_w1),
                   mlp_w2=(mlp_w2, m_mlp_w2, v_mlp_w2))
    names = list(weights)
    small_names = ["ada_b", "norm_mix", "norm_mlp", "conv_w", "hg_norm", "lb_logits", "q_norm", "k_norm"]
    upd = {}
    small_items = []
    for n in small_names:
        wv, mv, vv = weights[n]
        small_items.append((flat2(wv), flat2(grads[n]), flat2(mv), flat2(vv)))
    for n, res in zip(small_names, _adamw_small(small_items, name="adamw_small")):
        upd[n] = tuple(r.reshape(weights[n][0].shape) for r in res)
    for n in names:
        if n in small_names:
            continue
        wv, mv, vv = weights[n]
        res = _adamw(flat2(wv), flat2(grads[n]), flat2(mv), flat2(vv), name=f"adamw_{n}")
        upd[n] = tuple(r.reshape(wv.shape) for r in res)

    return (loss, grad_x[None], *[grads[n].reshape(weights[n][0].shape) for n in names],
            *[upd[n][0] for n in names], *[upd[n][1] for n in names], *[upd[n][2] for n in names])
```

```python
import functools

import jax
import jax.numpy as jnp
from jax import lax
from jax.experimental import pallas as pl
from jax.experimental.pallas import tpu as pltpu

F32 = jnp.float32
BF16 = jnp.bfloat16
EPS = 1e-6
N_DEV = 8
MESH_AXES = ("x", "y", "c")

D_MODEL = 1024
CONV_DIM = 512
HG_HEADS = 4
HG_DK = 128
HG_WIDTH = 512
CHUNK = 64
HG_TILE = 128
HG_SUB = 16
HG_EXP_CLAMP = 60.0
SB_HEAD_DIM = 64
SB_SCALE = SB_HEAD_DIM ** -0.5
LOG2E = 1.4426950408889634
LN2 = 0.6931471805599453
SB_TQ = 512
SB_TK = 256
D_FF = 4096
AB_IN = 3584

ADAM_LR = 0.001
ADAM_B1 = 0.9
ADAM_B2 = 0.999
ADAM_EPS = 1e-08
ADAM_WD = 0.01
ADAM_STEP = 10

VMEM_LIMIT = 48 * 1024 * 1024
LANES = 128


def _cparams(n_grid):
    return pltpu.CompilerParams(dimension_semantics=("arbitrary",) * n_grid, vmem_limit_bytes=VMEM_LIMIT)


def _nt(a, b):
    return lax.dot_general(a, b, (((1,), (1,)), ((), ())), preferred_element_type=F32)


def _tn(a, b):
    return lax.dot_general(a, b, (((0,), (0,)), ((), ())), preferred_element_type=F32)


def _nn(a, b):
    return jnp.dot(a, b, preferred_element_type=F32)


def _split3(x):
    hi = x.astype(BF16)
    r1 = x - hi.astype(F32)
    mid = r1.astype(BF16)
    lo = (r1 - mid.astype(F32)).astype(BF16)
    return hi, mid, lo


def _exact_left(m01, x):
    hi, mid, lo = _split3(x)
    return _nn(m01, hi) + _nn(m01, mid) + _nn(m01, lo)


def _exact_right(x, m01):
    hi, mid, lo = _split3(x)
    return _nn(hi, m01) + _nn(mid, m01) + _nn(lo, m01)


def _exact_right2(x, m01):
    hi = x.astype(BF16)
    lo = (x - hi.astype(F32)).astype(BF16)
    return _nn(hi, m01) + _nn(lo, m01)


def _sp(x):
    hi = x.astype(BF16)
    return hi, (x - hi.astype(F32)).astype(BF16)


def _dot3(fn, a, b):
    return fn(a[0], b[0]) + fn(a[0], b[1]) + fn(a[1], b[0])


def _tile(pref, n):
    t = min(pref, n)
    assert n % t == 0, (pref, n)
    return t


def _tile_lanes(pref, n):
    if n <= pref:
        return n
    for t in range(pref - pref % LANES, 0, -LANES):
        if n % t == 0:
            return t
    raise ValueError((pref, n))


def _all_gather(x, *, name, in_vmem):
    m_per, n = x.shape

    def body(x_ref, out_ref, send_sems, recv_sems, local_sem):
        mx, my, mc = lax.axis_index("x"), lax.axis_index("y"), lax.axis_index("c")
        me, sibling = (mx, my, mc), (mx, my, 1 - mc)
        chips = [(1 - mx, my), (mx, 1 - my), (1 - mx, 1 - my)]

        def rows(px, py, pc):
            return out_ref.at[pl.ds((4 * px + 2 * py + pc) * m_per, m_per), :]

        def copy(k, block, to, src=None):
            return pltpu.make_async_remote_copy(
                src_ref=rows(*block) if src is None else src, dst_ref=rows(*block),
                send_sem=send_sems.at[k], recv_sem=recv_sems.at[k],
                device_id=to, device_id_type=pl.DeviceIdType.MESH)

        mine = pltpu.make_async_copy(x_ref, rows(*me), local_sem)
        mine.start()
        first = [copy(0, me, sibling, src=x_ref)]
        first += [copy(1 + j, me, (*chip, mc), src=x_ref) for j, chip in enumerate(chips)]
        for cp in first:
            cp.start()
        passed = [copy(4 + j, (*chip, mc), sibling) for j, chip in enumerate(chips)]
        for j, chip in enumerate(chips):
            copy(1 + j, (*chip, mc), me).wait_recv()
            passed[j].start()
        copy(0, sibling, me).wait_recv()
        for j, chip in enumerate(chips):
            copy(4 + j, (*chip, 1 - mc), me).wait_recv()
        for cp in first + passed:
            cp.wait_send()
        mine.wait()

    space = pltpu.VMEM if in_vmem else pl.ANY
    return pl.pallas_call(
        body, name=name,
        out_shape=jax.ShapeDtypeStruct((N_DEV * m_per, n), x.dtype),
        in_specs=[pl.BlockSpec(memory_space=space)],
        out_specs=pl.BlockSpec(memory_space=space),
        scratch_shapes=[pltpu.SemaphoreType.DMA((7,)), pltpu.SemaphoreType.DMA((7,)), pltpu.SemaphoreType.DMA],
    )(x)


def _rs_sibling_exchange(g, *, name):
    _, r, n = g.shape

    def body(g_ref, out_ref, send_sems, recv_sems):
        mx, my, mc = lax.axis_index("x"), lax.axis_index("y"), lax.axis_index("c")
        copies = []
        for q in range(4):
            copies.append(pltpu.make_async_remote_copy(
                src_ref=g_ref.at[2 * q + (1 - mc)], dst_ref=out_ref.at[q],
                send_sem=send_sems.at[q], recv_sem=recv_sems.at[q],
                device_id=(mx, my, 1 - mc), device_id_type=pl.DeviceIdType.MESH))
        for cp in copies:
            cp.start()
        for cp in copies:
            cp.wait_recv()
        for cp in copies:
            cp.wait_send()

    return pl.pallas_call(
        body, name=name,
        out_shape=jax.ShapeDtypeStruct((4, r, n), g.dtype),
        in_specs=[pl.BlockSpec(memory_space=pl.ANY)],
        out_specs=pl.BlockSpec(memory_space=pl.ANY),
        scratch_shapes=[pltpu.SemaphoreType.DMA((4,)), pltpu.SemaphoreType.DMA((4,))],
    )(g)


def _rs_chip_exchange(t, *, name):
    _, r, n = t.shape

    def body(t_ref, out_ref, send_sems, recv_sems):
        mx, my, mc = lax.axis_index("x"), lax.axis_index("y"), lax.axis_index("c")
        chips = [(1 - mx, my), (mx, 1 - my), (1 - mx, 1 - my)]
        copies = []
        for k, (px, py) in enumerate(chips):
            copies.append(pltpu.make_async_remote_copy(
                src_ref=t_ref.at[2 * px + py], dst_ref=out_ref.at[k],
                send_sem=send_sems.at[k], recv_sem=recv_sems.at[k],
                device_id=(px, py, mc), device_id_type=pl.DeviceIdType.MESH))
        for cp in copies:
            cp.start()
        for cp in copies:
            cp.wait_recv()
        for cp in copies:
            cp.wait_send()

    return pl.pallas_call(
        body, name=name,
        out_shape=jax.ShapeDtypeStruct((3, r, n), t.dtype),
        in_specs=[pl.BlockSpec(memory_space=pl.ANY)],
        out_specs=pl.BlockSpec(memory_space=pl.ANY),
        scratch_shapes=[pltpu.SemaphoreType.DMA((3,)), pltpu.SemaphoreType.DMA((3,))],
    )(t)


def _rs_pair_sum(g, p1, my_c, *, name, tr=448):
    _, r, n = g.shape
    tr = _tile(tr, r)

    def body(c_ref, g_ref, p_ref, o_ref):
        o_ref[...] = (g_ref[...] + p_ref[...]).astype(BF16)

    return pl.pallas_call(
        body, name=name,
        out_shape=jax.ShapeDtypeStruct((4, r, n), BF16),
        grid_spec=pltpu.PrefetchScalarGridSpec(
            num_scalar_prefetch=1, grid=(4, r // tr),
            in_specs=[pl.BlockSpec((None, tr, n), lambda q, i, c: (2 * q + c[0], i, 0)),
                      pl.BlockSpec((None, tr, n), lambda q, i, c: (q, i, 0))],
            out_specs=pl.BlockSpec((None, tr, n), lambda q, i, c: (q, i, 0))),
        compiler_params=_cparams(2),
    )(my_c, g, p1)


def _rs_final_sum(g, p1, p3, my_ids, *, name, tr=448):
    _, r, n = g.shape
    tr = _tile(tr, r)

    def body(id_ref, g_ref, s_ref, a_ref, b_ref, c_ref, o_ref):
        own = g_ref[...] + s_ref[...]
        o_ref[...] = ((own + a_ref[...].astype(F32)) + b_ref[...].astype(F32)) + c_ref[...].astype(F32)

    return pl.pallas_call(
        body, name=name,
        out_shape=jax.ShapeDtypeStruct((r, n), F32),
        grid_spec=pltpu.PrefetchScalarGridSpec(
            num_scalar_prefetch=1, grid=(r // tr,),
            in_specs=[pl.BlockSpec((None, tr, n), lambda i, ids: (ids[0], i, 0)),
                      pl.BlockSpec((None, tr, n), lambda i, ids: (ids[1], i, 0)),
                      pl.BlockSpec((None, tr, n), lambda i, ids: (0, i, 0)),
                      pl.BlockSpec((None, tr, n), lambda i, ids: (1, i, 0)),
                      pl.BlockSpec((None, tr, n), lambda i, ids: (2, i, 0))],
            out_specs=pl.BlockSpec((tr, n), lambda i, ids: (i, 0))),
        compiler_params=_cparams(1),
    )(my_ids, g, p1, p3, p3, p3)


def _matmul(a, b, *, name, ta=False, tb=False, epi="plain", extras=(), out_dtype=F32, tm=1024, tn=1024, tk=1024):
    if ta:
        kdim, m = a.shape
    else:
        m, kdim = a.shape
    if tb:
        n, kb = b.shape
    else:
        kb, n = b.shape
    assert kdim == kb, (a.shape, b.shape)
    if epi == "resgate":
        tn = min(tn, 512)
    tm, tn, tk = _tile_lanes(tm, m), _tile_lanes(tn, n), _tile_lanes(tk, kdim)
    nk = kdim // tk
    a_spec = pl.BlockSpec((tk, tm), lambda i, j, k: (k, i)) if ta else pl.BlockSpec((tm, tk), lambda i, j, k: (i, k))
    b_spec = pl.BlockSpec((tn, tk), lambda i, j, k: (j, k)) if tb else pl.BlockSpec((tk, tn), lambda i, j, k: (k, j))
    dims = (((0 if ta else 1,), (1 if tb else 0,)), ((), ()))
    mn_spec = pl.BlockSpec((tm, tn), lambda i, j, k: (i, j))
    row_spec = pl.BlockSpec((1, tn), lambda i, j, k: (0, j))
    if epi == "resgate":
        extra_specs = [mn_spec, row_spec]
        out_shape = (jax.ShapeDtypeStruct((m, n), F32), jax.ShapeDtypeStruct((m, n), F32))
        out_specs = (mn_spec, mn_spec)
    elif epi == "dact":
        extra_specs = [mn_spec]
        out_shape = jax.ShapeDtypeStruct((m, n), out_dtype)
        out_specs = mn_spec
    else:
        extra_specs = []
        out_shape = jax.ShapeDtypeStruct((m, n), out_dtype)
        out_specs = mn_spec
    n_extra = len(extra_specs)

    def body(a_ref, b_ref, *rest):
        ex = rest[:n_extra]
        outs = rest[n_extra:n_extra + n_out]
        k = pl.program_id(2)

        def prod():
            return lax.dot_general(a_ref[...].astype(BF16), b_ref[...].astype(BF16), dims,
                                   preferred_element_type=F32)

        def finish(r):
            if epi == "plain":
                outs[0][...] = r.astype(outs[0].dtype)
            elif epi == "resgate":
                outs[0][...] = r
                outs[1][...] = ex[0][...] + ex[1][...] * r
            elif epi == "relu2":
                p = jnp.maximum(r, 0.0)
                outs[0][...] = (p * p).astype(outs[0].dtype)
            elif epi == "dact":
                outs[0][...] = (r * (2.0 * jnp.sqrt(ex[0][...].astype(F32)))).astype(outs[0].dtype)

        if nk == 1:
            finish(prod())
        else:
            acc = rest[-1]

            @pl.when(k == 0)
            def _():
                acc[...] = prod()

            if nk > 2:
                @pl.when(jnp.logical_and(k > 0, k < nk - 1))
                def _():
                    acc[...] += prod()

            @pl.when(k == nk - 1)
            def _():
                finish(acc[...] + prod())

    n_out = 2 if epi == "resgate" else 1
    return pl.pallas_call(
        body, name=name, out_shape=out_shape, grid=(m // tm, n // tn, nk),
        in_specs=[a_spec, b_spec] + extra_specs, out_specs=out_specs,
        scratch_shapes=[pltpu.VMEM((tm, tn), F32)] if nk > 1 else [],
        compiler_params=_cparams(3),
    )(a, b, *extras)


def _norm_mod(x, g, scale, shift, *, name, tm=512):
    s, d = x.shape
    tm = _tile(tm, s)

    def body(x_ref, g_ref, sc_ref, sh_ref, h_ref):
        xv = x_ref[...]
        r = lax.rsqrt(jnp.mean(xv * xv, axis=-1, keepdims=True) + EPS)
        h_ref[...] = (((xv * r) * g_ref[...]) * (1.0 + sc_ref[...]) + sh_ref[...]).astype(BF16)

    row = pl.BlockSpec((1, d), lambda i: (0, 0))
    return pl.pallas_call(
        body, name=name, out_shape=jax.ShapeDtypeStruct((s, d), BF16), grid=(s // tm,),
        in_specs=[pl.BlockSpec((tm, d), lambda i: (i, 0)), row, row, row],
        out_specs=pl.BlockSpec((tm, d), lambda i: (i, 0)),
        compiler_params=_cparams(1),
    )(x, g, scale, shift)


def _norm_mod_bwd(x, dh, dres, g, scale, *, name, tm=512):
    s, d = x.shape
    tm = _tile(tm, s)

    def body(x_ref, dh_ref, dr_ref, g_ref, sc_ref, dx_ref, acc_ref):
        i = pl.program_id(0)

        @pl.when(i == 0)
        def _():
            acc_ref[...] = jnp.zeros_like(acc_ref)

        xv = x_ref[...]
        dhv = dh_ref[...]
        gv = g_ref[...]
        one_sc = 1.0 + sc_ref[...]
        r = lax.rsqrt(jnp.mean(xv * xv, axis=-1, keepdims=True) + EPS)
        xn = xv * r
        dxn = dhv * (gv * one_sc)
        dx_ref[...] = dr_ref[...] + r * (dxn - xn * jnp.mean(dxn * xn, axis=-1, keepdims=True))
        dhxn = dhv * xn
        acc_ref[0:1, :] += jnp.sum(dhv, axis=0, keepdims=True)
        acc_ref[1:2, :] += jnp.sum(dhxn * gv, axis=0, keepdims=True)
        acc_ref[2:3, :] += jnp.sum(dhxn * one_sc, axis=0, keepdims=True)

    row = pl.BlockSpec((1, d), lambda i: (0, 0))
    blk = pl.BlockSpec((tm, d), lambda i: (i, 0))
    return pl.pallas_call(
        body, name=name,
        out_shape=(jax.ShapeDtypeStruct((s, d), F32), jax.ShapeDtypeStruct((8, d), F32)),
        grid=(s // tm,), in_specs=[blk, blk, blk, row, row],
        out_specs=(blk, pl.BlockSpec((8, d), lambda i: (0, 0))),
        compiler_params=_cparams(1),
    )(x, dh, dres, g, scale)


def _gate_bwd(dx, z, gate, *, name, tm=512):
    s, d = dx.shape
    tm = _tile(tm, s)

    def body(dx_ref, z_ref, g_ref, dz_ref, acc_ref):
        i = pl.program_id(0)

        @pl.when(i == 0)
        def _():
            acc_ref[...] = jnp.zeros_like(acc_ref)

        dxv = dx_ref[...]
        dz_ref[...] = (dxv * g_ref[...]).astype(BF16)
        acc_ref[0:1, :] += jnp.sum(dxv * z_ref[...], axis=0, keepdims=True)

    blk = pl.BlockSpec((tm, d), lambda i: (i, 0))
    return pl.pallas_call(
        body, name=name,
        out_shape=(jax.ShapeDtypeStruct((s, d), BF16), jax.ShapeDtypeStruct((8, d), F32)),
        grid=(s // tm,), in_specs=[blk, blk, pl.BlockSpec((1, d), lambda i: (0, 0))],
        out_specs=(blk, pl.BlockSpec((8, d), lambda i: (0, 0))),
        compiler_params=_cparams(1),
    )(dx, z, gate)


def _loss_grad(xf, target, *, name, tm=512):
    s, d = xf.shape
    tm = _tile(tm, s)
    nt = s // tm

    def body(x_ref, t_ref, dx_ref, loss_ref, acc_ref):
        i = pl.program_id(0)

        @pl.when(i == 0)
        def _():
            acc_ref[...] = jnp.zeros_like(acc_ref)

        e = x_ref[...] - t_ref[...]
        dx_ref[...] = e * (1.0 / d)
        acc_ref[...] += jnp.sum(e * e, axis=0, keepdims=True)

        @pl.when(i == nt - 1)
        def _():
            loss_ref[...] = (0.5 / d) * jnp.sum(acc_ref[...], axis=1, keepdims=True)

    blk = pl.BlockSpec((tm, d), lambda i: (i, 0))
    return pl.pallas_call(
        body, name=name,
        out_shape=(jax.ShapeDtypeStruct((s, d), F32), jax.ShapeDtypeStruct((1, 1), F32)),
        grid=(nt,), in_specs=[blk, blk],
        out_specs=(blk, pl.BlockSpec((1, 1), lambda i: (0, 0))),
        scratch_shapes=[pltpu.VMEM((1, d), F32)],
        compiler_params=_cparams(1),
    )(xf, target)


def _shift_down(p, prev, k):
    tm = p.shape[0]
    row = lax.broadcasted_iota(jnp.int32, p.shape, 0)
    out = pltpu.roll(p, k, 0)
    for j in range(k):
        out = jnp.where(row == j, prev[8 - k + j:8 - k + j + 1, :], out)
    return out


def _shift_up(p, nxt, k):
    tm = p.shape[0]
    row = lax.broadcasted_iota(jnp.int32, p.shape, 0)
    out = pltpu.roll(p, tm - k, 0)
    for j in range(k):
        out = jnp.where(row == tm - k + j, nxt[j:j + 1, :], out)
    return out


def _conv_fwd(u, w, *, name, tm=512):
    s = u.shape[0]
    tm = _tile(tm, s)
    c = CONV_DIM

    def body(ab_ref, ac_ref, ah_ref, w_ref, y_ref, carry_ref):
        i = pl.program_id(0)

        @pl.when(i == 0)
        def _():
            carry_ref[...] = jnp.zeros_like(carry_ref)

        p = ac_ref[...] * ah_ref[...]
        prev = carry_ref[...]
        wv = w_ref[...]
        conv = wv[2:3, :] * p + wv[1:2, :] * _shift_down(p, prev, 1) + wv[0:1, :] * _shift_down(p, prev, 2)
        y_ref[...] = (ab_ref[...] * conv).astype(BF16)
        carry_ref[...] = p[tm - 8:tm, :]

    return pl.pallas_call(
        body, name=name, out_shape=jax.ShapeDtypeStruct((s, c), BF16), grid=(s // tm,),
        in_specs=[pl.BlockSpec((tm, c), lambda i: (i, 0)), pl.BlockSpec((tm, c), lambda i: (i, 1)),
                  pl.BlockSpec((tm, c), lambda i: (i, 2)), pl.BlockSpec((3, c), lambda i: (0, 0))],
        out_specs=pl.BlockSpec((tm, c), lambda i: (i, 0)),
        scratch_shapes=[pltpu.VMEM((8, c), F32)],
        compiler_params=_cparams(1),
    )(u, u, u, w)


def _conv_bwd(u, dy, w, *, name, tm=512):
    s = u.shape[0]
    tm = _tile(tm, s)
    nt = s // tm
    c = CONV_DIM
    hb = tm // 8

    def body(ab_ref, ac_ref, ah_ref, hc_ref, hh_ref, dy_ref, w_ref, du_ref, dw_ref, carry_ref):
        i = pl.program_id(0)

        @pl.when(i == 0)
        def _():
            carry_ref[...] = jnp.zeros_like(carry_ref)
            dw_ref[...] = jnp.zeros_like(dw_ref)

        first_tile = (nt - 1 - i) == 0
        ab, ac, ah = ab_ref[...], ac_ref[...], ah_ref[...]
        p = ac * ah
        prev = jnp.where(first_tile, 0.0, hc_ref[...] * hh_ref[...])
        wv = w_ref[...]
        p1 = _shift_down(p, prev, 1)
        p2 = _shift_down(p, prev, 2)
        conv = wv[2:3, :] * p + wv[1:2, :] * p1 + wv[0:1, :] * p2
        dyv = dy_ref[...]
        dconv = dyv * ab
        nxt = carry_ref[...]
        dp = wv[2:3, :] * dconv + wv[1:2, :] * _shift_up(dconv, nxt, 1) + wv[0:1, :] * _shift_up(dconv, nxt, 2)
        du_ref[:, 0:c] = (dyv * conv).astype(BF16)
        du_ref[:, c:2 * c] = (dp * ah).astype(BF16)
        du_ref[:, 2 * c:3 * c] = (dp * ac).astype(BF16)
        dw_ref[0:1, :] += jnp.sum(dconv * p2, axis=0, keepdims=True)
        dw_ref[1:2, :] += jnp.sum(dconv * p1, axis=0, keepdims=True)
        dw_ref[2:3, :] += jnp.sum(dconv * p, axis=0, keepdims=True)
        carry_ref[...] = dconv[0:8, :]

    rev = lambda i: nt - 1 - i
    halo = lambda i: jnp.maximum(rev(i) * hb - 1, 0)
    return pl.pallas_call(
        body, name=name,
        out_shape=(jax.ShapeDtypeStruct((s, 3 * c), BF16), jax.ShapeDtypeStruct((8, c), F32)),
        grid=(nt,),
        in_specs=[pl.BlockSpec((tm, c), lambda i: (rev(i), 0)), pl.BlockSpec((tm, c), lambda i: (rev(i), 1)),
                  pl.BlockSpec((tm, c), lambda i: (rev(i), 2)),
                  pl.BlockSpec((8, c), lambda i: (halo(i), 1)), pl.BlockSpec((8, c), lambda i: (halo(i), 2)),
                  pl.BlockSpec((tm, c), lambda i: (rev(i), 0)), pl.BlockSpec((3, c), lambda i: (0, 0))],
        out_specs=(pl.BlockSpec((tm, 3 * c), lambda i: (rev(i), 0)), pl.BlockSpec((8, c), lambda i: (0, 0))),
        scratch_shapes=[pltpu.VMEM((8, c), F32)],
        compiler_params=_cparams(1),
    )(u, u, u, u, u, dy, w)


def _lower_bound(lbl):
    m = jnp.max(lbl, axis=0, keepdims=True)
    e = jnp.exp(lbl - m)
    return e[0:1, :] / jnp.sum(e, axis=0, keepdims=True)


def _hg_masks():
    t = HG_TILE
    row = lax.broadcasted_iota(jnp.int32, (t, t), 0)
    col = lax.broadcasted_iota(jnp.int32, (t, t), 1)
    same = (row >= CHUNK) == (col >= CHUNK)
    lower = same & (col <= row)
    upper = same & (row <= col)
    return row, col, lower, upper


def _hg_gates(hf, lb):
    sig = jax.nn.sigmoid(hf)
    f = lb + (1.0 - lb) * sig
    return sig, f, jnp.log(f), 1.0 - f


def _hg_refs(b_ref, hs):
    refs = []
    for i in range(HG_TILE // HG_SUB):
        if (i * HG_SUB) % CHUNK == 0:
            refs.append(jnp.zeros((1, HG_DK), F32))
        else:
            refs.append(b_ref[i * HG_SUB - 1:i * HG_SUB, hs])
    return refs


def _hgrn_fwd(u, lbl, gn, *, name):
    s = u.shape[0]
    t = HG_TILE
    nt = s // t
    nsub = t // HG_SUB
    w = HG_WIDTH

    def body(hq_ref, hf_ref, hi_ref, hg_ref, lbl_ref, gn_ref, y_ref, o_ref, sall_ref, st_ref, b_ref):
        i = pl.program_id(0)

        @pl.when(i == 0)
        def _():
            st_ref[...] = jnp.zeros_like(st_ref)

        lb = _lower_bound(lbl_ref[...])
        _, _, g, kin = _hg_gates(hf_ref[...], lb)
        _, _, lower, _ = _hg_masks()
        b_ref[...] = _exact_left(lower.astype(BF16), g)

        for h in range(HG_HEADS):
            hs = slice(h * HG_DK, (h + 1) * HG_DK)
            bh = b_ref[:, hs]
            qh = hq_ref[:, hs]
            kh = kin[:, hs]
            vh = hi_ref[:, hs]
            vsp = _sp(vh)
            refs = _hg_refs(b_ref, hs)
            rmat = jnp.concatenate([jnp.broadcast_to(r, (HG_SUB, HG_DK)) for r in refs], axis=0)
            qt = qh * jnp.exp(bh - rmat)
            prow = []
            for j in range(nsub):
                kj = kh * jnp.exp(jnp.minimum(refs[j] - bh, HG_EXP_CLAMP))
                prow.append(_dot3(_nt, _sp(qt[j * HG_SUB:(j + 1) * HG_SUB]), _sp(kj)))
            p = jnp.where(lower, jnp.concatenate(prow, axis=0), 0.0)
            intra = _dot3(_nn, _sp(p), vsp)
            o_parts = []
            for c in range(t // CHUNK):
                rs = slice(c * CHUNK, (c + 1) * CHUNK)
                st0 = st_ref[hs, :]
                sall_ref[c * w + h * HG_DK:c * w + (h + 1) * HG_DK, :] = st0
                bl = b_ref[c * CHUNK + CHUNK - 1:c * CHUNK + CHUNK, hs]
                qf = qh[rs] * jnp.exp(bh[rs])
                o_parts.append(_dot3(_nt, _sp(qf), _sp(st0)) + intra[rs])
                khat = kh[rs] * jnp.exp(bl - bh[rs])
                st_ref[hs, :] = st0 * jnp.exp(bl) + _dot3(_tn, _sp(vh[rs]), _sp(khat))
            o = jnp.concatenate(o_parts, axis=0)
            o_ref[:, hs] = o
            r = lax.rsqrt(jnp.mean(o * o, axis=-1, keepdims=True) + EPS)
            hg = hg_ref[:, hs]
            y_ref[:, hs] = (((o * r) * gn_ref[:, hs]) * (hg * jax.nn.sigmoid(hg))).astype(BF16)

    blk = lambda j: pl.BlockSpec((t, w), lambda i, j=j: (i, j))
    srows = (t // CHUNK) * w
    return pl.pallas_call(
        body, name=name,
        out_shape=(jax.ShapeDtypeStruct((s, w), BF16), jax.ShapeDtypeStruct((s, w), F32),
                   jax.ShapeDtypeStruct((nt * srows, HG_DK), F32)),
        grid=(nt,),
        in_specs=[blk(3), blk(4), blk(5), blk(6), pl.BlockSpec((3, w), lambda i: (0, 0)),
                  pl.BlockSpec((1, w), lambda i: (0, 0))],
        out_specs=(pl.BlockSpec((t, w), lambda i: (i, 0)), pl.BlockSpec((t, w), lambda i: (i, 0)),
                   pl.BlockSpec((srows, HG_DK), lambda i: (i, 0))),
        scratch_shapes=[pltpu.VMEM((w, HG_DK), F32), pltpu.VMEM((t, w), F32)],
        compiler_params=_cparams(1),
    )(u, u, u, u, lbl, gn)


def _hgrn_bwd(u, o_all, sall, dy, lbl, gn, *, name):
    s = u.shape[0]
    t = HG_TILE
    nt = s // t
    nsub = t // HG_SUB
    w = HG_WIDTH
    nch = t // CHUNK

    def body(hq_ref, hf_ref, hi_ref, hg_ref, o_ref, sall_ref, dy_ref, lbl_ref, gn_ref,
             du_ref, acc_ref, dst_ref, b_ref):
        i = pl.program_id(0)

        @pl.when(i == 0)
        def _():
            dst_ref[...] = jnp.zeros_like(dst_ref)
            acc_ref[...] = jnp.zeros_like(acc_ref)

        lb = _lower_bound(lbl_ref[...])
        sig, f, g, kin = _hg_gates(hf_ref[...], lb)
        row, col, lower, upper = _hg_masks()
        b_ref[...] = _exact_left(lower.astype(BF16), g)
        upper_bf = upper.astype(BF16)
        rowblk = [((row >= j * HG_SUB) & (row < (j + 1) * HG_SUB)) for j in range(nsub)]
        colblk = [((col >= j * HG_SUB) & (col < (j + 1) * HG_SUB)) for j in range(nsub)]
        row1 = lax.broadcasted_iota(jnp.int32, (t, HG_DK), 0)

        for h in range(HG_HEADS):
            hs = slice(h * HG_DK, (h + 1) * HG_DK)
            bh = b_ref[:, hs]
            qh = hq_ref[:, hs]
            kh = kin[:, hs]
            vh = hi_ref[:, hs]
            vsp = _sp(vh)
            hg = hg_ref[:, hs]
            gnh = gn_ref[:, hs]
            o = o_ref[:, hs]
            dyv = dy_ref[:, hs]
            sg = jax.nn.sigmoid(hg)
            r = lax.rsqrt(jnp.mean(o * o, axis=-1, keepdims=True) + EPS)
            ohat = o * r
            du_ref[:, 3 * w + h * HG_DK:3 * w + (h + 1) * HG_DK] = (
                dyv * (ohat * gnh) * (sg * (1.0 + hg * (1.0 - sg)))).astype(BF16)
            don = dyv * (hg * sg)
            acc_ref[0:1, hs] += jnp.sum(don * ohat, axis=0, keepdims=True)
            dohat = don * gnh
            do = r * (dohat - ohat * jnp.mean(dohat * ohat, axis=-1, keepdims=True))
            dosp = _sp(do)
            refs = _hg_refs(b_ref, hs)
            rmat = jnp.concatenate([jnp.broadcast_to(rr, (HG_SUB, HG_DK)) for rr in refs], axis=0)
            eq = jnp.exp(bh - rmat)
            qt = qh * eq
            qtsp = _sp(qt)
            dp = jnp.where(lower, _dot3(_nt, dosp, vsp), 0.0)
            dpt = jnp.where(upper, _dot3(_nt, vsp, dosp), 0.0)
            pt = jnp.zeros((t, t), F32)
            dk = jnp.zeros((t, HG_DK), F32)
            dq_rows = []
            for j in range(nsub):
                ek = jnp.exp(jnp.minimum(refs[j] - bh, HG_EXP_CLAMP))
                kjsp = _sp(kh * ek)
                pt = pt + _dot3(_nt, kjsp, _sp(jnp.where(rowblk[j], qt, 0.0)))
                dq_rows.append(_dot3(_nn, _sp(dp[j * HG_SUB:(j + 1) * HG_SUB]), kjsp))
                dk = dk + ek * _dot3(_nn, _sp(jnp.where(colblk[j], dpt, 0.0)), qtsp)
            pt = jnp.where(upper, pt, 0.0)
            dv = _dot3(_nn, _sp(pt), dosp)
            dq = jnp.concatenate(dq_rows, axis=0) * eq
            dq_c, dk_c, dv_c, ex_c = [None] * nch, [None] * nch, [None] * nch, [None] * nch
            for c in reversed(range(nch)):
                rs = slice(c * CHUNK, (c + 1) * CHUNK)
                st0 = sall_ref[c * w + h * HG_DK:c * w + (h + 1) * HG_DK, :]
                dst1 = dst_ref[hs, :]
                dst1sp = _sp(dst1)
                dosp_c = _sp(do[rs])
                bl = b_ref[c * CHUNK + CHUNK - 1:c * CHUNK + CHUNK, hs]
                e = jnp.exp(bh[rs])
                el = jnp.exp(bl)
                ekl = jnp.exp(bl - bh[rs])
                dq_c[c] = _dot3(_nn, dosp_c, _sp(st0)) * e
                khat = kh[rs] * ekl
                dv_c[c] = _dot3(_nt, _sp(khat), dst1sp)
                dkhat = _dot3(_nn, _sp(vh[rs]), dst1sp)
                dk_c[c] = dkhat * ekl
                ex_c[c] = (jnp.sum(dkhat * khat, axis=0, keepdims=True)
                           + el * jnp.sum(dst1 * st0, axis=0, keepdims=True))
                dst_ref[hs, :] = _dot3(_tn, dosp_c, _sp(qh[rs] * e)) + dst1 * el
            dq = dq + jnp.concatenate(dq_c, axis=0)
            dk = dk + jnp.concatenate(dk_c, axis=0)
            dv = dv + jnp.concatenate(dv_c, axis=0)
            db = qh * dq - kh * dk
            for c in range(nch):
                db = db + jnp.where(row1 == c * CHUNK + CHUNK - 1, ex_c[c], 0.0)
            dg = _exact_left(upper_bf, db)
            fh = f[:, hs]
            sgf = sig[:, hs]
            lbh = lb[:, hs]
            df = dg / fh - dk
            du_ref[:, hs] = dq.astype(BF16)
            du_ref[:, w + h * HG_DK:w + (h + 1) * HG_DK] = (df * (1.0 - lbh) * sgf * (1.0 - sgf)).astype(BF16)
            du_ref[:, 2 * w + h * HG_DK:2 * w + (h + 1) * HG_DK] = dv.astype(BF16)
            acc_ref[1:2, hs] += jnp.sum(df * (1.0 - sgf), axis=0, keepdims=True)

    rev = lambda i: nt - 1 - i
    blk = lambda j: pl.BlockSpec((t, w), lambda i, j=j: (rev(i), j))
    srows = nch * w
    return pl.pallas_call(
        body, name=name,
        out_shape=(jax.ShapeDtypeStruct((s, 4 * w), BF16), jax.ShapeDtypeStruct((8, w), F32)),
        grid=(nt,),
        in_specs=[blk(3), blk(4), blk(5), blk(6), pl.BlockSpec((t, w), lambda i: (rev(i), 0)),
                  pl.BlockSpec((srows, HG_DK), lambda i: (rev(i), 0)),
                  pl.BlockSpec((t, w), lambda i: (rev(i), 1)),
                  pl.BlockSpec((3, w), lambda i: (0, 0)), pl.BlockSpec((1, w), lambda i: (0, 0))],
        out_specs=(pl.BlockSpec((t, 4 * w), lambda i: (rev(i), 0)), pl.BlockSpec((8, w), lambda i: (0, 0))),
        scratch_shapes=[pltpu.VMEM((w, HG_DK), F32), pltpu.VMEM((t, w), F32)],
        compiler_params=_cparams(1),
    )(u, u, u, u, o_all, sall, dy, lbl, gn)


def _pair_matrix():
    row = lax.broadcasted_iota(jnp.int32, (LANES, LANES), 0)
    col = lax.broadcasted_iota(jnp.int32, (LANES, LANES), 1)
    return ((row >= SB_HEAD_DIM) == (col >= SB_HEAD_DIM)).astype(BF16)


def _qk_norm_fwd(qkv, qn, kn, *, name, tm=256):
    s = qkv.shape[0]
    d = D_MODEL
    tm = _tile(tm, s)

    def body(q_ref, k_ref, v_ref, qn_ref, kn_ref, qo_ref, ko_ref, vo_ref):
        bd = _pair_matrix()
        for src, gain, dst, fac in ((q_ref, qn_ref, qo_ref, SB_SCALE * LOG2E), (k_ref, kn_ref, ko_ref, None)):
            for grp in range(d // LANES):
                ls = slice(grp * LANES, (grp + 1) * LANES)
                xv = src[:, ls]
                ms = _exact_right(xv * xv, bd) * (1.0 / SB_HEAD_DIM)
                y = (xv * lax.rsqrt(ms + EPS)) * gain[:, ls]
                dst[:, ls] = (y if fac is None else y * fac).astype(BF16)
        vo_ref[...] = v_ref[...].astype(BF16)

    blk = lambda j: pl.BlockSpec((tm, d), lambda i, j=j: (i, j))
    row = pl.BlockSpec((1, d), lambda i: (0, 0))
    out = jax.ShapeDtypeStruct((s, d), BF16)
    return pl.pallas_call(
        body, name=name, out_shape=(out, out, out), grid=(s // tm,),
        in_specs=[blk(0), blk(1), blk(2), row, row],
        out_specs=(blk(0), blk(0), blk(0)),
        compiler_params=_cparams(1),
    )(qkv, qkv, qkv, qn, kn)


def _qk_norm_bwd(qkv, dqn, dkn, dv, qn, kn, *, name, tm=256):
    s = qkv.shape[0]
    d = D_MODEL
    tm = _tile(tm, s)

    def body(q_ref, k_ref, dq_ref, dk_ref, dv_ref, qn_ref, kn_ref, o_ref, acc_ref):
        i = pl.program_id(0)

        @pl.when(i == 0)
        def _():
            acc_ref[...] = jnp.zeros_like(acc_ref)

        bd = _pair_matrix()
        for idx, (src, dsrc, gain) in enumerate(((q_ref, dq_ref, qn_ref), (k_ref, dk_ref, kn_ref))):
            for grp in range(d // LANES):
                ls = slice(grp * LANES, (grp + 1) * LANES)
                xv = src[:, ls]
                dyv = dsrc[:, ls]
                r = lax.rsqrt(_exact_right(xv * xv, bd) * (1.0 / SB_HEAD_DIM) + EPS)
                xh = xv * r
                acc_ref[idx:idx + 1, ls] += jnp.sum(dyv * xh, axis=0, keepdims=True)
                dxh = dyv * gain[:, ls]
                mean = _exact_right(dxh * xh, bd) * (1.0 / SB_HEAD_DIM)
                o_ref[:, idx * d + grp * LANES:idx * d + (grp + 1) * LANES] = (r * (dxh - xh * mean)).astype(BF16)
        o_ref[:, 2 * d:3 * d] = dv_ref[...].astype(BF16)

    blk = lambda j: pl.BlockSpec((tm, d), lambda i, j=j: (i, j))
    row = pl.BlockSpec((1, d), lambda i: (0, 0))
    return pl.pallas_call(
        body, name=name,
        out_shape=(jax.ShapeDtypeStruct((s, 3 * d), BF16), jax.ShapeDtypeStruct((8, d), F32)),
        grid=(s // tm,),
        in_specs=[blk(0), blk(1), blk(0), blk(0), blk(0), row, row],
        out_specs=(pl.BlockSpec((tm, 3 * d), lambda i: (i, 0)), pl.BlockSpec((8, d), lambda i: (0, 0))),
        compiler_params=_cparams(1),
    )(qkv, qkv, dqn, dkn, dv, qn, kn)


def _sb_tile(qh, kb, suffix_ones, run, mask):
    z = _nt(qh, kb)
    neg_abs = lax.bitcast_convert_type(lax.bitcast_convert_type(z, jnp.uint32) | jnp.uint32(0x80000000), F32)
    l1m = -(jnp.maximum(z, 0.0) + jnp.log2(1.0 + jnp.exp2(neg_abs)))
    logb = z + l1m
    if mask is not None:
        l1m = jnp.where(mask, l1m, 0.0)
    later = _nn(l1m.astype(BF16), suffix_ones) + run
    wgt = jnp.exp2(logb + later)
    if mask is not None:
        wgt = jnp.where(mask, wgt, 0.0)
    return logb, l1m, wgt


def _suffix_ones(tk):
    row = lax.broadcasted_iota(jnp.int32, (tk, tk), 0)
    col = lax.broadcasted_iota(jnp.int32, (tk, tk), 1)
    return (row > col).astype(BF16)


def _sb_mask(qi, j, tq, tk):
    qpos = qi * tq + lax.broadcasted_iota(jnp.int32, (tq, tk), 0)
    kpos = j * tk + lax.broadcasted_iota(jnp.int32, (tq, tk), 1)
    return kpos < qpos


def _sb_fwd(qn, kn, v, *, name):
    s, d = qn.shape
    tq, tk = _tile(SB_TQ, s), _tile(SB_TK, s)
    assert tk % tq == 0 or tq % tk == 0
    nq = s // tq

    def body(q_ref, k_ref, v_ref, o_ref, acc_ref):
        qi = pl.program_id(1)
        lane = lax.broadcasted_iota(jnp.int32, (tq, LANES), 1)
        first = lane < SB_HEAD_DIM
        q = q_ref[...]
        qh = [jnp.where(first, q, 0).astype(BF16), jnp.where(first, 0, q).astype(BF16)]
        ones = _suffix_ones(tk)
        acc_ref[...] = jnp.zeros_like(acc_ref)

        def tile(j, runs, masked):
            ks = pl.ds(pl.multiple_of(j * tk, tk), tk)
            kb = k_ref[ks, :]
            vb = v_ref[ks, :]
            mask = _sb_mask(qi, j, tq, tk) if masked else None
            new_runs = []
            for hh in range(2):
                _, l1m, wgt = _sb_tile(qh[hh], kb, ones, runs[hh], mask)
                acc_ref[hh] += _nn(wgt.astype(BF16), vb)
                new_runs.append(runs[hh] + jnp.sum(l1m, axis=1, keepdims=True))
            return tuple(new_runs)

        nfull = (qi * tq) // tk
        zero = jnp.zeros((tq, 1), F32)
        runs = (zero, zero)
        for m in reversed(range(max(tq // tk, 1))):
            runs = tile(nfull + m, runs, True)
        lax.fori_loop(0, nfull, lambda it, r: tile(nfull - 1 - it, r, False), runs)
        o_ref[...] = jnp.where(first, acc_ref[0], acc_ref[1])

    return pl.pallas_call(
        body, name=name, out_shape=jax.ShapeDtypeStruct((s, d), F32), grid=(d // LANES, nq),
        in_specs=[pl.BlockSpec((tq, LANES), lambda p, i: (i, p)), pl.BlockSpec((s, LANES), lambda p, i: (0, p)),
                  pl.BlockSpec((s, LANES), lambda p, i: (0, p))],
        out_specs=pl.BlockSpec((tq, LANES), lambda p, i: (i, p)),
        scratch_shapes=[pltpu.VMEM((2, tq, LANES), F32)],
        compiler_params=_cparams(2),
    )(qn, kn, v)


def _sb_bwd(qn, kn, v, o, do, *, name):
    s, d = qn.shape
    tq, tk = _tile(SB_TQ, s), _tile(SB_TK, s)
    assert tk % tq == 0 or tq % tk == 0
    nq = s // tq

    def body(q_ref, k_ref, v_ref, o_ref, do_ref, dq_ref, dk_ref, dv_ref, acc_ref):
        qi = pl.program_id(1)

        @pl.when(qi == 0)
        def _():
            dk_ref[...] = jnp.zeros_like(dk_ref)
            dv_ref[...] = jnp.zeros_like(dv_ref)

        first = lax.broadcasted_iota(jnp.int32, (tq, LANES), 1) < SB_HEAD_DIM
        sel = [first, jnp.logical_not(first)]
        kfirst = lax.broadcasted_iota(jnp.int32, (tk, LANES), 1) < SB_HEAD_DIM
        ksel = [kfirst, jnp.logical_not(kfirst)]
        q = q_ref[...]
        dob = do_ref[...].astype(BF16)
        qh = [jnp.where(sel[hh], q, 0).astype(BF16) for hh in range(2)]
        doh = [jnp.where(sel[hh], dob, 0).astype(BF16) for hh in range(2)]
        prod = dob.astype(F32) * o_ref[...]
        gtot = [jnp.sum(jnp.where(sel[hh], prod, 0.0), axis=1, keepdims=True) for hh in range(2)]
        ones = _suffix_ones(tk)
        acc_ref[...] = jnp.zeros_like(acc_ref)

        def tile(j, carry, masked):
            runs, gruns = carry
            ks = pl.ds(pl.multiple_of(j * tk, tk), tk)
            kb = k_ref[ks, :]
            vb = v_ref[ks, :]
            mask = _sb_mask(qi, j, tq, tk) if masked else None
            new_runs, new_gruns = [], []
            dk_add = jnp.zeros((tk, LANES), F32)
            dv_add = jnp.zeros((tk, LANES), F32)
            for hh in range(2):
                logb, l1m, wgt = _sb_tile(qh[hh], kb, ones, runs[hh], mask)
                wb = wgt.astype(BF16)
                g = _nt(doh[hh], vb) * wb.astype(F32)
                gsuf = _exact_right2(g, ones) + g + gruns[hh]
                dz = g - jnp.exp2(logb) * (g + (gtot[hh] - gsuf))
                if masked:
                    dz = jnp.where(mask, dz, 0.0)
                dzb = dz.astype(BF16)
                acc_ref[hh] += _nn(dzb, kb)
                dk_add = dk_add + jnp.where(ksel[hh], _tn(dzb, qh[hh]), 0.0)
                dv_add = dv_add + jnp.where(ksel[hh], _tn(wb, doh[hh]), 0.0)
                new_runs.append(runs[hh] + jnp.sum(l1m, axis=1, keepdims=True))
                new_gruns.append(gruns[hh] + jnp.sum(g, axis=1, keepdims=True))
            dk_ref[ks, :] += dk_add * LN2
            dv_ref[ks, :] += dv_add
            return tuple(new_runs), tuple(new_gruns)

        nfull = (qi * tq) // tk
        zero = jnp.zeros((tq, 1), F32)
        carry = ((zero, zero), (zero, zero))
        for m in reversed(range(max(tq // tk, 1))):
            carry = tile(nfull + m, carry, True)
        lax.fori_loop(0, nfull, lambda it, cr: tile(nfull - 1 - it, cr, False), carry)
        dq_ref[...] = jnp.where(first, acc_ref[0], acc_ref[1]) * SB_SCALE

    blk = pl.BlockSpec((tq, LANES), lambda p, i: (i, p))
    full = pl.BlockSpec((s, LANES), lambda p, i: (0, p))
    out = jax.ShapeDtypeStruct((s, d), F32)
    return pl.pallas_call(
        body, name=name, out_shape=(out, out, out), grid=(d // LANES, nq),
        in_specs=[blk, full, full, blk, blk],
        out_specs=(blk, full, full),
        scratch_shapes=[pltpu.VMEM((2, tq, LANES), F32)],
        compiler_params=_cparams(2),
    )(qn, kn, v, o, do)


def _mod_part(c_all, ada_w, ada_b_my, *, name):
    nl, d, ncol = ada_w.shape

    def body(c_ref, w_ref, b_ref, part_ref, ca_ref):
        cv = c_ref[...]
        ca = cv * jax.nn.sigmoid(cv)
        ca_ref[...] = ca
        part_ref[...] = _nn(ca.astype(BF16), w_ref[...].astype(BF16)) + b_ref[...]

    return pl.pallas_call(
        body, name=name,
        out_shape=(jax.ShapeDtypeStruct((nl, N_DEV, ncol), F32), jax.ShapeDtypeStruct((N_DEV, d), F32)),
        grid=(nl,),
        in_specs=[pl.BlockSpec((N_DEV, d), lambda l: (0, 0)), pl.BlockSpec((None, d, ncol), lambda l: (l, 0, 0)),
                  pl.BlockSpec((None, 1, ncol), lambda l: (l, 0, 0))],
        out_specs=(pl.BlockSpec((None, N_DEV, ncol), lambda l: (l, 0, 0)), pl.BlockSpec((N_DEV, d), lambda l: (0, 0))),
        compiler_params=_cparams(1),
    )(c_all, ada_w, ada_b_my)


PK_MOD, PK_NMIX, PK_NMLP, PK_HGN, PK_LB, PK_QN, PK_KN, PK_CONV, PK_ROWS = 0, 96, 112, 128, 132, 136, 144, 152, 168


def _small_grads(gath, ca_col, dmod_my, lbl4, *, name):
    def body(g_ref, ca_ref, dm_ref, lbl_ref, gw_ref, gsum_ref, glb_ref, gqk_ref):
        tot = g_ref[0]
        for dev in range(1, N_DEV):
            tot = tot + g_ref[dev]
        gsum_ref[...] = tot
        lv = lbl_ref[...]
        m = jnp.maximum(jnp.maximum(lv[0], lv[1]), lv[2])
        e = [jnp.exp(lv[k] - m) for k in range(3)]
        den = e[0] + e[1] + e[2]
        p = [ek / den for ek in e]
        dlb = tot[PK_LB:PK_LB + 4, :]
        glb_ref[0] = dlb * p[0] * (1.0 - p[0])
        glb_ref[1] = -dlb * p[0] * p[1]
        glb_ref[2] = -dlb * p[0] * p[2]
        for idx, base in enumerate((PK_QN, PK_KN)):
            rowsum = jnp.sum(tot[base:base + 8, :], axis=0, keepdims=True)
            gqk_ref[idx:idx + 1, :] = rowsum + pltpu.roll(rowsum, SB_HEAD_DIM, 1)
        for l in range(2):
            acc = ca_ref[0] * dm_ref[0, l:l + 1, :]
            for smp in range(1, N_DEV):
                acc = acc + ca_ref[smp] * dm_ref[smp, l:l + 1, :]
            gw_ref[l] = acc

    d, ncol = ca_col.shape[1], dmod_my.shape[2]
    vm = pl.BlockSpec(memory_space=pltpu.VMEM)
    return pl.pallas_call(
        body, name=name,
        out_shape=(jax.ShapeDtypeStruct((2, d, ncol), F32), jax.ShapeDtypeStruct((PK_ROWS, LANES), F32),
                   jax.ShapeDtypeStruct((3, 4, LANES), F32), jax.ShapeDtypeStruct((8, LANES), F32)),
        in_specs=[vm, vm, vm, vm], out_specs=(vm, vm, vm, vm),
        compiler_params=pltpu.CompilerParams(vmem_limit_bytes=VMEM_LIMIT),
    )(gath, ca_col, dmod_my, lbl4)


def _adamw_math(w, g, m, v):
    m = ADAM_B1 * m + (1.0 - ADAM_B1) * g
    v = ADAM_B2 * v + (1.0 - ADAM_B2) * (g * g)
    m_hat = m / (1.0 - ADAM_B1 ** ADAM_STEP)
    v_hat = v / (1.0 - ADAM_B2 ** ADAM_STEP)
    delta = -ADAM_LR * (m_hat / (jnp.sqrt(v_hat) + ADAM_EPS) + ADAM_WD * w)
    return delta, m, v


def _adamw(w, g, m, v, *, name, tr=256):
    r, n = w.shape
    tr = _tile(tr, r)

    def body(w_ref, g_ref, m_ref, v_ref, d_ref, mo_ref, vo_ref):
        dl, mn, vn = _adamw_math(w_ref[...], g_ref[...], m_ref[...], v_ref[...])
        d_ref[...] = dl
        mo_ref[...] = mn
        vo_ref[...] = vn

    blk = pl.BlockSpec((tr, n), lambda i: (i, 0))
    out = jax.ShapeDtypeStruct((r, n), F32)
    return pl.pallas_call(
        body, name=name, out_shape=(out, out, out), grid=(r // tr,),
        in_specs=[blk, blk, blk, blk], out_specs=(blk, blk, blk),
        compiler_params=_cparams(1),
    )(w, g, m, v)


def _adamw_small(items, *, name):
    n = len(items)

    def body(*refs):
        ins, outs = refs[:4 * n], refs[4 * n:]
        for k in range(n):
            dl, mn, vn = _adamw_math(*(r[...] for r in ins[4 * k:4 * k + 4]))
            outs[3 * k][...] = dl
            outs[3 * k + 1][...] = mn
            outs[3 * k + 2][...] = vn

    flat = [a for it in items for a in it]
    out_shape = tuple(jax.ShapeDtypeStruct(it[0].shape, F32) for it in items for _ in range(3))
    vm = pl.BlockSpec(memory_space=pltpu.VMEM)
    res = pl.pallas_call(
        body, name=name, out_shape=out_shape, in_specs=[vm] * (4 * n), out_specs=tuple([vm] * (3 * n)),
    )(*flat)
    return [tuple(res[3 * k:3 * k + 3]) for k in range(n)]


def _mlp_fwd(x, g, scale, shift, gate, w1, w2, tag):
    h = _norm_mod(x, g, scale, shift, name=f"{tag}_norm")
    act = _matmul(h, w1, epi="relu2", out_dtype=BF16, name=f"{tag}_w1")
    z, x_out = _matmul(act, w2, epi="resgate", extras=(x, gate), name=f"{tag}_w2")
    return x_out, (h, act, z)


def _mlp_bwd(dx_out, x, saved, g, scale, gate, w1, w2, tag):
    h, act, z = saved
    dz, gate_acc = _gate_bwd(dx_out, z, gate, name=f"{tag}_gate_bwd")
    du = _matmul(dz, w2, tb=True, epi="dact", extras=(act,), out_dtype=BF16, name=f"{tag}_dact")
    dw2 = _matmul(act, dz, ta=True, name=f"{tag}_dw2")
    dw1 = _matmul(h, du, ta=True, name=f"{tag}_dw1")
    dh = _matmul(du, w1, tb=True, name=f"{tag}_dh")
    dx, nacc = _norm_mod_bwd(x, dh, dx_out, g, scale, name=f"{tag}_norm_bwd")
    return dx, dw1, dw2, (nacc[0:1], nacc[1:2], gate_acc[0:1]), nacc[2:3]


def kernel(x, c, ada_w, ada_b, norm_mix, norm_mlp, w_in_ab, conv_w, hg_norm, lb_logits, w_out_ab, w_qkv, q_norm, k_norm, w_out_c, mlp_w1, mlp_w2, loss_target, m_ada_w, m_ada_b, m_norm_mix, m_norm_mlp, m_w_in_ab, m_conv_w, m_hg_norm, m_lb_logits, m_w_out_ab, m_w_qkv, m_q_norm, m_k_norm, m_w_out_c, m_mlp_w1, m_mlp_w2, v_ada_w, v_ada_b, v_norm_mix, v_norm_mlp, v_w_in_ab, v_conv_w, v_hg_norm, v_lb_logits, v_w_out_ab, v_w_qkv, v_q_norm, v_k_norm, v_w_out_c, v_mlp_w1, v_mlp_w2):
    d = D_MODEL
    my_x, my_y, my_c = lax.axis_index("x"), lax.axis_index("y"), lax.axis_index("c")
    me = 4 * my_x + 2 * my_y + my_c
    xs = x[0]
    tgt = loss_target[0]

    big = [w_in_ab, w_out_ab, w_qkv, w_out_c, mlp_w1, mlp_w2]
    rows = [w.size // d for w in big]
    offs = [sum(rows[:k]) for k in range(len(rows) + 1)]
    packed = jnp.concatenate([w.astype(BF16).reshape(-1, d) for w in big], axis=0)
    gathered = _all_gather(packed, name="gather_weights", in_vmem=False).reshape(N_DEV, offs[-1], d)

    def piece(k):
        return gathered[:, offs[k]:offs[k + 1]]

    def cols(k, nl, kdim, ncol):
        return piece(k).reshape(N_DEV, nl, kdim, ncol).transpose(1, 2, 0, 3).reshape(nl, kdim, N_DEV * ncol)

    def rws(k, nl, nrow, n):
        return piece(k).reshape(N_DEV, nl, nrow, n).transpose(1, 0, 2, 3).reshape(nl, N_DEV * nrow, n)

    win = cols(0, 1, d, AB_IN // N_DEV)[0]
    wout_ab = rws(1, 1, d // N_DEV, d)[0]
    wqkv = cols(2, 1, d, 3 * d // N_DEV)[0]
    wout_c = rws(3, 1, d // N_DEV, d)[0]
    w1 = cols(4, 2, d, D_FF // N_DEV)
    w2 = rws(5, 2, D_FF // N_DEV, d)

    ncv = CONV_DIM // N_DEV
    c_and_conv = jnp.concatenate([c, jnp.pad(conv_w[0], ((0, 0), (0, d - ncv))), jnp.zeros((4, d), F32)], axis=0)
    c_and_conv = _all_gather(c_and_conv, name="gather_c", in_vmem=True).reshape(N_DEV, 8, d)
    c_all = c_and_conv[:, 0]
    conv_full = c_and_conv[:, 1:4, :ncv].transpose(1, 0, 2).reshape(3, CONV_DIM)
    ncol = ada_w.shape[2]
    ada_b_my = lax.dynamic_slice(ada_b, (0, me * ncol), (2, ncol)).reshape(2, 1, ncol)
    part, c_act = _mod_part(c_all, ada_w, ada_b_my, name="mod_part")
    parts = _all_gather(part.reshape(2 * N_DEV, ncol), name="gather_mod", in_vmem=True)
    parts = parts.reshape(N_DEV, 2, N_DEV, ncol)
    mod = lax.dynamic_index_in_dim(parts, me, axis=2, keepdims=False)
    mod = mod.transpose(1, 0, 2).reshape(2, 6, 1, d)

    qn_t = jnp.tile(q_norm, (1, d // SB_HEAD_DIM))
    kn_t = jnp.tile(k_norm, (1, d // SB_HEAD_DIM))

    sh1, sc1, gt1, sh2, sc2, gt2 = [mod[0, k] for k in range(6)]
    h0 = _norm_mod(xs, norm_mix[0:1], sc1, sh1, name="l0_mix_norm")
    u = _matmul(h0, win, name="l0_in_proj")
    y_a = _conv_fwd(u, conv_full, name="l0_conv")
    y_b, o_hg, sall = _hgrn_fwd(u, lb_logits, hg_norm, name="l0_hgrn")
    y_ab = jnp.concatenate([y_a, y_b], axis=1)
    z0, x_mid0 = _matmul(y_ab, wout_ab, epi="resgate", extras=(xs, gt1), name="l0_out_proj")
    x1, mlp0 = _mlp_fwd(x_mid0, norm_mlp[0:1], sc2, sh2, gt2, w1[0], w2[0], "l0_mlp")

    sh1b, sc1b, gt1b, sh2b, sc2b, gt2b = [mod[1, k] for k in range(6)]
    h1 = _norm_mod(x1, norm_mix[1:2], sc1b, sh1b, name="l1_mix_norm")
    qkv = _matmul(h1, wqkv, name="l1_qkv_proj")
    qn_a, kn_a, v_a = _qk_norm_fwd(qkv, qn_t, kn_t, name="l1_qk_norm")
    o_sb = _sb_fwd(qn_a, kn_a, v_a, name="l1_sb")
    z1, x_mid1 = _matmul(o_sb, wout_c, epi="resgate", extras=(x1, gt1b), name="l1_out_proj")
    x2, mlp1 = _mlp_fwd(x_mid1, norm_mlp[1:2], sc2b, sh2b, gt2b, w1[1], w2[1], "l1_mlp")

    dx, loss_part = _loss_grad(x2, tgt, name="loss")
    loss = lax.psum(loss_part[0, 0], MESH_AXES)

    dx, dw1_1, dw2_1, (dsh2b, dsc2b, dgt2b), dnmlp1 = _mlp_bwd(
        dx, x_mid1, mlp1, norm_mlp[1:2], sc2b, gt2b, w1[1], w2[1], "l1_mlp")
    dyp, gacc = _gate_bwd(dx, z1, gt1b, name="l1_mix_gate_bwd")
    dwout_c = _matmul(o_sb, dyp, ta=True, name="l1_dwout")
    do_sb = _matmul(dyp, wout_c, tb=True, name="l1_do")
    dqn_a, dkn_a, dv_a = _sb_bwd(qn_a, kn_a, v_a, o_sb, do_sb, name="l1_sb_bwd")
    dqkv, qkacc = _qk_norm_bwd(qkv, dqn_a, dkn_a, dv_a, qn_t, kn_t, name="l1_qk_norm_bwd")
    dwqkv = _matmul(h1, dqkv, ta=True, name="l1_dwqkv")
    dh1 = _matmul(dqkv, wqkv, tb=True, name="l1_dh")
    dx, nacc = _norm_mod_bwd(x1, dh1, dx, norm_mix[1:2], sc1b, name="l1_mix_norm_bwd")
    dmod1 = [nacc[0:1], nacc[1:2], gacc[0:1], dsh2b, dsc2b, dgt2b]
    dnmix1 = nacc[2:3]

    dx, dw1_0, dw2_0, (dsh2, dsc2, dgt2), dnmlp0 = _mlp_bwd(
        dx, x_mid0, mlp0, norm_mlp[0:1], sc2, gt2, w1[0], w2[0], "l0_mlp")
    dyp, gacc = _gate_bwd(dx, z0, gt1, name="l0_mix_gate_bwd")
    dwout_ab = _matmul(y_ab, dyp, ta=True, name="l0_dwout")
    dy_ab = _matmul(dyp, wout_ab, tb=True, name="l0_dy")
    du_a, dconv = _conv_bwd(u, dy_ab, conv_full, name="l0_conv_bwd")
    du_b, hgacc = _hgrn_bwd(u, o_hg, sall, dy_ab, lb_logits, hg_norm, name="l0_hgrn_bwd")
    du = jnp.concatenate([du_a, du_b], axis=1)
    dwin = _matmul(h0, du, ta=True, name="l0_dwin")
    dh0 = _matmul(du, win, tb=True, name="l0_dh")
    grad_x, nacc = _norm_mod_bwd(xs, dh0, dx, norm_mix[0:1], sc1, name="l0_mix_norm_bwd")
    dmod0 = [nacc[0:1], nacc[1:2], gacc[0:1], dsh2, dsc2, dgt2]
    dnmix0 = nacc[2:3]

    def to_cols(gw, nl, kdim, ncol_):
        return gw.reshape(nl, kdim, N_DEV, ncol_).transpose(2, 0, 1, 3).reshape(N_DEV, -1, d)

    def to_rows(gw, nl, nrow, n):
        return gw.reshape(nl, N_DEV, nrow, n).transpose(1, 0, 2, 3).reshape(N_DEV, -1, d)

    gbig = jnp.concatenate([
        to_cols(dwin[None], 1, d, AB_IN // N_DEV), to_rows(dwout_ab[None], 1, d // N_DEV, d),
        to_cols(dwqkv[None], 1, d, 3 * d // N_DEV), to_rows(dwout_c[None], 1, d // N_DEV, d),
        to_cols(jnp.stack([dw1_0, dw1_1]), 2, d, D_FF // N_DEV),
        to_rows(jnp.stack([dw2_0, dw2_1]), 2, D_FF // N_DEV, d)], axis=1)
    sib = _rs_sibling_exchange(gbig, name="rs_sibling")
    pair = _rs_pair_sum(gbig, sib, my_c.reshape(1).astype(jnp.int32), name="rs_pair_sum")
    far = _rs_chip_exchange(pair, name="rs_chips")
    my_ids = jnp.stack([me, 2 * my_x + my_y]).astype(jnp.int32)
    gsh = _rs_final_sum(gbig, sib, far, my_ids, name="rs_final_sum")
    g_big = [gsh[offs[k]:offs[k + 1]].reshape(big[k].shape) for k in range(len(big))]

    packed_small = jnp.concatenate(
        [jnp.concatenate(dmod0, axis=1).reshape(-1, LANES), jnp.concatenate(dmod1, axis=1).reshape(-1, LANES),
         dnmix0.reshape(-1, LANES), dnmix1.reshape(-1, LANES), dnmlp0.reshape(-1, LANES), dnmlp1.reshape(-1, LANES),
         hgacc[0:1].reshape(-1, LANES), hgacc[1:2].reshape(-1, LANES),
         qkacc[0:1].reshape(-1, LANES), qkacc[1:2].reshape(-1, LANES),
         dconv[0:3].reshape(-1, LANES), jnp.zeros((PK_ROWS - PK_CONV - 12, LANES), F32)], axis=0)
    gath = _all_gather(packed_small, name="gather_small_grads", in_vmem=True).reshape(N_DEV, PK_ROWS, LANES)
    dmod_all = gath[:, PK_MOD:PK_NMIX].reshape(N_DEV, 2, 6 * d)
    dmod_my = lax.dynamic_slice(dmod_all, (0, 0, me * ncol), (N_DEV, 2, ncol))
    g_ada_w, gsum, g_lb, g_qk = _small_grads(gath, c_act[:, :, None], dmod_my, lb_logits.reshape(3, 4, LANES),
                                             name="small_grads")
    g_ada_b = gsum[PK_MOD:PK_NMIX].reshape(2, 6 * d)
    g_norm_mix = gsum[PK_NMIX:PK_NMLP].reshape(2, d)
    g_norm_mlp = gsum[PK_NMLP:PK_HGN].reshape(2, d)
    g_hg_norm = gsum[PK_HGN:PK_LB].reshape(1, HG_WIDTH)
    g_lb_logits = g_lb.reshape(3, HG_WIDTH)
    g_q_norm = g_qk[0:1, :SB_HEAD_DIM]
    g_k_norm = g_qk[1:2, :SB_HEAD_DIM]
    g_conv_w = lax.dynamic_slice(gsum[PK_CONV:PK_CONV + 12].reshape(3, CONV_DIM), (0, me * ncv), (3, ncv))[None]

    def flat2(a):
        return a.reshape(-1, a.shape[-1])

    grads = dict(ada_w=g_ada_w, ada_b=g_ada_b, norm_mix=g_norm_mix, norm_mlp=g_norm_mlp, w_in_ab=g_big[0],
                 conv_w=g_conv_w, hg_norm=g_hg_norm, lb_logits=g_lb_logits, w_out_ab=g_big[1], w_qkv=g_big[2],
                 q_norm=g_q_norm, k_norm=g_k_norm, w_out_c=g_big[3], mlp_w1=g_big[4], mlp_w2=g_big[5])
    weights = dict(ada_w=(ada_w, m_ada_w, v_ada_w), ada_b=(ada_b, m_ada_b, v_ada_b),
                   norm_mix=(norm_mix, m_norm_mix, v_norm_mix), norm_mlp=(norm_mlp, m_norm_mlp, v_norm_mlp),
                   w_in_ab=(w_in_ab, m_w_in_ab, v_w_in_ab), conv_w=(conv_w, m_conv_w, v_conv_w),
                   hg_norm=(hg_norm, m_hg_norm, v_hg_norm), lb_logits=(lb_logits, m_lb_logits, v_lb_logits),
                   w_out_ab=(w_out_ab, m_w_out_ab, v_w_out_ab), w_qkv=(w_qkv, m_w_qkv, v_w_qkv),
                   q_norm=(q_norm, m_q_norm, v_q_norm), k_norm=(k_norm, m_k_norm, v_k_norm),
                   w_out_c=(w_out_c, m_w_out_c, v_w_out_c), mlp_w1=(mlp_w1, m_mlp_w1, v_mlp_w1),
                   mlp_w2=(mlp_w2, m_mlp_w2, v_mlp_w2))
    names = list(weights)
    small_names = ["ada_b", "norm_mix", "norm_mlp", "conv_w", "hg_norm", "lb_logits", "q_norm", "k_norm"]
    upd = {}
    small_items = []
    for n in small_names:
        wv, mv, vv = weights[n]
        small_items.append((flat2(wv), flat2(grads[n]), flat2(mv), flat2(vv)))
    for n, res in zip(small_names, _adamw_small(small_items, name="adamw_small")):
        upd[n] = tuple(r.reshape(weights[n][0].shape) for r in res)
    for n in names:
        if n in small_names:
            continue
        wv, mv, vv = weights[n]
        res = _adamw(flat2(wv), flat2(grads[n]), flat2(mv), flat2(vv), name=f"adamw_{n}")
        upd[n] = tuple(r.reshape(wv.shape) for r in res)

    return (loss, grad_x[None], *[grads[n].reshape(weights[n][0].shape) for n in names],
            *[upd[n][0] for n in names], *[upd[n][1] for n in names], *[upd[n][2] for n in names])
```

```python
import functools

import jax
import jax.numpy as jnp
from jax import lax
from jax.experimental import pallas as pl
from jax.experimental.pallas import tpu as pltpu

F32 = jnp.float32
BF16 = jnp.bfloat16
EPS = 1e-6
N_DEV = 8
MESH_AXES = ("x", "y", "c")

D_MODEL = 1024
CONV_DIM = 512
HG_HEADS = 4
HG_DK = 128
HG_WIDTH = 512
CHUNK = 64
HG_TILE = 128
HG_SUB = 16
HG_EXP_CLAMP = 60.0
SB_HEAD_DIM = 64
SB_SCALE = SB_HEAD_DIM ** -0.5
LOG2E = 1.4426950408889634
LN2 = 0.6931471805599453
SB_TQ = 512
SB_TK = 256
SB_DEAD = 150.0
D_FF = 4096
AB_IN = 3584

ADAM_LR = 0.001
ADAM_B1 = 0.9
ADAM_B2 = 0.999
ADAM_EPS = 1e-08
ADAM_WD = 0.01
ADAM_STEP = 10

VMEM_LIMIT = 48 * 1024 * 1024
LANES = 128


def _cparams(n_grid):
    return pltpu.CompilerParams(dimension_semantics=("arbitrary",) * n_grid, vmem_limit_bytes=VMEM_LIMIT)


def _nt(a, b):
    return lax.dot_general(a, b, (((1,), (1,)), ((), ())), preferred_element_type=F32)


def _tn(a, b):
    return lax.dot_general(a, b, (((0,), (0,)), ((), ())), preferred_element_type=F32)


def _nn(a, b):
    return jnp.dot(a, b, preferred_element_type=F32)


def _split3(x):
    hi = x.astype(BF16)
    r1 = x - hi.astype(F32)
    mid = r1.astype(BF16)
    lo = (r1 - mid.astype(F32)).astype(BF16)
    return hi, mid, lo


def _exact_left(m01, x):
    hi, mid, lo = _split3(x)
    return _nn(m01, hi) + _nn(m01, mid) + _nn(m01, lo)


def _exact_right(x, m01):
    hi, mid, lo = _split3(x)
    return _nn(hi, m01) + _nn(mid, m01) + _nn(lo, m01)


def _exact_right2(x, m01):
    hi = x.astype(BF16)
    lo = (x - hi.astype(F32)).astype(BF16)
    return _nn(hi, m01) + _nn(lo, m01)


def _sp(x):
    hi = x.astype(BF16)
    return hi, (x - hi.astype(F32)).astype(BF16)


def _dot3(fn, a, b):
    return fn(a[0], b[0]) + fn(a[0], b[1]) + fn(a[1], b[0])


def _tile(pref, n):
    t = min(pref, n)
    assert n % t == 0, (pref, n)
    return t


def _tile_lanes(pref, n):
    if n <= pref:
        return n
    for t in range(pref - pref % LANES, 0, -LANES):
        if n % t == 0:
            return t
    raise ValueError((pref, n))


def _all_gather(x, *, name, in_vmem):
    m_per, n = x.shape

    def body(x_ref, out_ref, send_sems, recv_sems, local_sem):
        mx, my, mc = lax.axis_index("x"), lax.axis_index("y"), lax.axis_index("c")
        me, sibling = (mx, my, mc), (mx, my, 1 - mc)
        chips = [(1 - mx, my), (mx, 1 - my), (1 - mx, 1 - my)]

        def rows(px, py, pc):
            return out_ref.at[pl.ds((4 * px + 2 * py + pc) * m_per, m_per), :]

        def copy(k, block, to, src=None):
            return pltpu.make_async_remote_copy(
                src_ref=rows(*block) if src is None else src, dst_ref=rows(*block),
                send_sem=send_sems.at[k], recv_sem=recv_sems.at[k],
                device_id=to, device_id_type=pl.DeviceIdType.MESH)

        mine = pltpu.make_async_copy(x_ref, rows(*me), local_sem)
        mine.start()
        first = [copy(0, me, sibling, src=x_ref)]
        first += [copy(1 + j, me, (*chip, mc), src=x_ref) for j, chip in enumerate(chips)]
        for cp in first:
            cp.start()
        passed = [copy(4 + j, (*chip, mc), sibling) for j, chip in enumerate(chips)]
        for j, chip in enumerate(chips):
            copy(1 + j, (*chip, mc), me).wait_recv()
            passed[j].start()
        copy(0, sibling, me).wait_recv()
        for j, chip in enumerate(chips):
            copy(4 + j, (*chip, 1 - mc), me).wait_recv()
        for cp in first + passed:
            cp.wait_send()
        mine.wait()

    space = pltpu.VMEM if in_vmem else pl.ANY
    return pl.pallas_call(
        body, name=name,
        out_shape=jax.ShapeDtypeStruct((N_DEV * m_per, n), x.dtype),
        in_specs=[pl.BlockSpec(memory_space=space)],
        out_specs=pl.BlockSpec(memory_space=space),
        scratch_shapes=[pltpu.SemaphoreType.DMA((7,)), pltpu.SemaphoreType.DMA((7,)), pltpu.SemaphoreType.DMA],
    )(x)


def _rs_sibling_exchange(g, *, name):
    _, r, n = g.shape

    def body(g_ref, out_ref, send_sems, recv_sems):
        mx, my, mc = lax.axis_index("x"), lax.axis_index("y"), lax.axis_index("c")
        copies = []
        for q in range(4):
            copies.append(pltpu.make_async_remote_copy(
                src_ref=g_ref.at[2 * q + (1 - mc)], dst_ref=out_ref.at[q],
                send_sem=send_sems.at[q], recv_sem=recv_sems.at[q],
                device_id=(mx, my, 1 - mc), device_id_type=pl.DeviceIdType.MESH))
        for cp in copies:
            cp.start()
        for cp in copies:
            cp.wait_recv()
        for cp in copies:
            cp.wait_send()

    return pl.pallas_call(
        body, name=name,
        out_shape=jax.ShapeDtypeStruct((4, r, n), g.dtype),
        in_specs=[pl.BlockSpec(memory_space=pl.ANY)],
        out_specs=pl.BlockSpec(memory_space=pl.ANY),
        scratch_shapes=[pltpu.SemaphoreType.DMA((4,)), pltpu.SemaphoreType.DMA((4,))],
    )(g)


def _rs_chip_exchange(t, *, name):
    _, r, n = t.shape

    def body(t_ref, out_ref, send_sems, recv_sems):
        mx, my, mc = lax.axis_index("x"), lax.axis_index("y"), lax.axis_index("c")
        chips = [(1 - mx, my), (mx, 1 - my), (1 - mx, 1 - my)]
        copies = []
        for k, (px, py) in enumerate(chips):
            copies.append(pltpu.make_async_remote_copy(
                src_ref=t_ref.at[2 * px + py], dst_ref=out_ref.at[k],
                send_sem=send_sems.at[k], recv_sem=recv_sems.at[k],
                device_id=(px, py, mc), device_id_type=pl.DeviceIdType.MESH))
        for cp in copies:
            cp.start()
        for cp in copies:
            cp.wait_recv()
        for cp in copies:
            cp.wait_send()

    return pl.pallas_call(
        body, name=name,
        out_shape=jax.ShapeDtypeStruct((3, r, n), t.dtype),
        in_specs=[pl.BlockSpec(memory_space=pl.ANY)],
        out_specs=pl.BlockSpec(memory_space=pl.ANY),
        scratch_shapes=[pltpu.SemaphoreType.DMA((3,)), pltpu.SemaphoreType.DMA((3,))],
    )(t)


def _rs_pair_sum(g, p1, my_c, *, name, tr=448):
    _, r, n = g.shape
    tr = _tile(tr, r)

    def body(c_ref, g_ref, p_ref, o_ref):
        o_ref[...] = (g_ref[...] + p_ref[...]).astype(BF16)

    return pl.pallas_call(
        body, name=name,
        out_shape=jax.ShapeDtypeStruct((4, r, n), BF16),
        grid_spec=pltpu.PrefetchScalarGridSpec(
            num_scalar_prefetch=1, grid=(4, r // tr),
            in_specs=[pl.BlockSpec((None, tr, n), lambda q, i, c: (2 * q + c[0], i, 0)),
                      pl.BlockSpec((None, tr, n), lambda q, i, c: (q, i, 0))],
            out_specs=pl.BlockSpec((None, tr, n), lambda q, i, c: (q, i, 0))),
        compiler_params=_cparams(2),
    )(my_c, g, p1)


def _rs_final_sum(g, p1, p3, my_ids, *, name, tr=448):
    _, r, n = g.shape
    tr = _tile(tr, r)

    def body(id_ref, g_ref, s_ref, a_ref, b_ref, c_ref, o_ref):
        own = g_ref[...] + s_ref[...]
        o_ref[...] = ((own + a_ref[...].astype(F32)) + b_ref[...].astype(F32)) + c_ref[...].astype(F32)

    return pl.pallas_call(
        body, name=name,
        out_shape=jax.ShapeDtypeStruct((r, n), F32),
        grid_spec=pltpu.PrefetchScalarGridSpec(
            num_scalar_prefetch=1, grid=(r // tr,),
            in_specs=[pl.BlockSpec((None, tr, n), lambda i, ids: (ids[0], i, 0)),
                      pl.BlockSpec((None, tr, n), lambda i, ids: (ids[1], i, 0)),
                      pl.BlockSpec((None, tr, n), lambda i, ids: (0, i, 0)),
                      pl.BlockSpec((None, tr, n), lambda i, ids: (1, i, 0)),
                      pl.BlockSpec((None, tr, n), lambda i, ids: (2, i, 0))],
            out_specs=pl.BlockSpec((tr, n), lambda i, ids: (i, 0))),
        compiler_params=_cparams(1),
    )(my_ids, g, p1, p3, p3, p3)


def _matmul(a, b, *, name, ta=False, tb=False, epi="plain", extras=(), out_dtype=F32, tm=1024, tn=1024, tk=1024):
    if ta:
        kdim, m = a.shape
    else:
        m, kdim = a.shape
    if tb:
        n, kb = b.shape
    else:
        kb, n = b.shape
    assert kdim == kb, (a.shape, b.shape)
    if epi == "resgate":
        tn = min(tn, 512)
    tm, tn, tk = _tile_lanes(tm, m), _tile_lanes(tn, n), _tile_lanes(tk, kdim)
    nk = kdim // tk
    a_spec = pl.BlockSpec((tk, tm), lambda i, j, k: (k, i)) if ta else pl.BlockSpec((tm, tk), lambda i, j, k: (i, k))
    b_spec = pl.BlockSpec((tn, tk), lambda i, j, k: (j, k)) if tb else pl.BlockSpec((tk, tn), lambda i, j, k: (k, j))
    dims = (((0 if ta else 1,), (1 if tb else 0,)), ((), ()))
    mn_spec = pl.BlockSpec((tm, tn), lambda i, j, k: (i, j))
    row_spec = pl.BlockSpec((1, tn), lambda i, j, k: (0, j))
    if epi == "resgate":
        extra_specs = [mn_spec, row_spec]
        out_shape = (jax.ShapeDtypeStruct((m, n), F32), jax.ShapeDtypeStruct((m, n), F32))
        out_specs = (mn_spec, mn_spec)
    elif epi == "dact":
        extra_specs = [mn_spec]
        out_shape = jax.ShapeDtypeStruct((m, n), out_dtype)
        out_specs = mn_spec
    else:
        extra_specs = []
        out_shape = jax.ShapeDtypeStruct((m, n), out_dtype)
        out_specs = mn_spec
    n_extra = len(extra_specs)

    def body(a_ref, b_ref, *rest):
        ex = rest[:n_extra]
        outs = rest[n_extra:n_extra + n_out]
        k = pl.program_id(2)

        def prod():
            return lax.dot_general(a_ref[...].astype(BF16), b_ref[...].astype(BF16), dims,
                                   preferred_element_type=F32)

        def finish(r):
            if epi == "plain":
                outs[0][...] = r.astype(outs[0].dtype)
            elif epi == "resgate":
                outs[0][...] = r
                outs[1][...] = ex[0][...] + ex[1][...] * r
            elif epi == "relu2":
                p = jnp.maximum(r, 0.0)
                outs[0][...] = (p * p).astype(outs[0].dtype)
            elif epi == "dact":
                outs[0][...] = (r * (2.0 * jnp.sqrt(ex[0][...].astype(F32)))).astype(outs[0].dtype)

        if nk == 1:
            finish(prod())
        else:
            acc = rest[-1]

            @pl.when(k == 0)
            def _():
                acc[...] = prod()

            if nk > 2:
                @pl.when(jnp.logical_and(k > 0, k < nk - 1))
                def _():
                    acc[...] += prod()

            @pl.when(k == nk - 1)
            def _():
                finish(acc[...] + prod())

    n_out = 2 if epi == "resgate" else 1
    return pl.pallas_call(
        body, name=name, out_shape=out_shape, grid=(m // tm, n // tn, nk),
        in_specs=[a_spec, b_spec] + extra_specs, out_specs=out_specs,
        scratch_shapes=[pltpu.VMEM((tm, tn), F32)] if nk > 1 else [],
        compiler_params=_cparams(3),
    )(a, b, *extras)


def _norm_mod(x, g, scale, shift, *, name, tm=512):
    s, d = x.shape
    tm = _tile(tm, s)

    def body(x_ref, g_ref, sc_ref, sh_ref, h_ref):
        xv = x_ref[...]
        r = lax.rsqrt(jnp.mean(xv * xv, axis=-1, keepdims=True) + EPS)
        h_ref[...] = (((xv * r) * g_ref[...]) * (1.0 + sc_ref[...]) + sh_ref[...]).astype(BF16)

    row = pl.BlockSpec((1, d), lambda i: (0, 0))
    return pl.pallas_call(
        body, name=name, out_shape=jax.ShapeDtypeStruct((s, d), BF16), grid=(s // tm,),
        in_specs=[pl.BlockSpec((tm, d), lambda i: (i, 0)), row, row, row],
        out_specs=pl.BlockSpec((tm, d), lambda i: (i, 0)),
        compiler_params=_cparams(1),
    )(x, g, scale, shift)


def _norm_mod_bwd(x, dh, dres, g, scale, *, name, tm=512):
    s, d = x.shape
    tm = _tile(tm, s)

    def body(x_ref, dh_ref, dr_ref, g_ref, sc_ref, dx_ref, acc_ref):
        i = pl.program_id(0)

        @pl.when(i == 0)
        def _():
            acc_ref[...] = jnp.zeros_like(acc_ref)

        xv = x_ref[...]
        dhv = dh_ref[...]
        gv = g_ref[...]
        one_sc = 1.0 + sc_ref[...]
        r = lax.rsqrt(jnp.mean(xv * xv, axis=-1, keepdims=True) + EPS)
        xn = xv * r
        dxn = dhv * (gv * one_sc)
        dx_ref[...] = dr_ref[...] + r * (dxn - xn * jnp.mean(dxn * xn, axis=-1, keepdims=True))
        dhxn = dhv * xn
        acc_ref[0:1, :] += jnp.sum(dhv, axis=0, keepdims=True)
        acc_ref[1:2, :] += jnp.sum(dhxn * gv, axis=0, keepdims=True)
        acc_ref[2:3, :] += jnp.sum(dhxn * one_sc, axis=0, keepdims=True)

    row = pl.BlockSpec((1, d), lambda i: (0, 0))
    blk = pl.BlockSpec((tm, d), lambda i: (i, 0))
    return pl.pallas_call(
        body, name=name,
        out_shape=(jax.ShapeDtypeStruct((s, d), F32), jax.ShapeDtypeStruct((8, d), F32)),
        grid=(s // tm,), in_specs=[blk, blk, blk, row, row],
        out_specs=(blk, pl.BlockSpec((8, d), lambda i: (0, 0))),
        compiler_params=_cparams(1),
    )(x, dh, dres, g, scale)


def _gate_bwd(dx, z, gate, *, name, tm=512):
    s, d = dx.shape
    tm = _tile(tm, s)

    def body(dx_ref, z_ref, g_ref, dz_ref, acc_ref):
        i = pl.program_id(0)

        @pl.when(i == 0)
        def _():
            acc_ref[...] = jnp.zeros_like(acc_ref)

        dxv = dx_ref[...]
        dz_ref[...] = (dxv * g_ref[...]).astype(BF16)
        acc_ref[0:1, :] += jnp.sum(dxv * z_ref[...], axis=0, keepdims=True)

    blk = pl.BlockSpec((tm, d), lambda i: (i, 0))
    return pl.pallas_call(
        body, name=name,
        out_shape=(jax.ShapeDtypeStruct((s, d), BF16), jax.ShapeDtypeStruct((8, d), F32)),
        grid=(s // tm,), in_specs=[blk, blk, pl.BlockSpec((1, d), lambda i: (0, 0))],
        out_specs=(blk, pl.BlockSpec((8, d), lambda i: (0, 0))),
        compiler_params=_cparams(1),
    )(dx, z, gate)


def _loss_grad(xf, target, *, name, tm=512):
    s, d = xf.shape
    tm = _tile(tm, s)
    nt = s // tm

    def body(x_ref, t_ref, dx_ref, loss_ref, acc_ref):
        i = pl.program_id(0)

        @pl.when(i == 0)
        def _():
            acc_ref[...] = jnp.zeros_like(acc_ref)

        e = x_ref[...] - t_ref[...]
        dx_ref[...] = e * (1.0 / d)
        acc_ref[...] += jnp.sum(e * e, axis=0, keepdims=True)

        @pl.when(i == nt - 1)
        def _():
            loss_ref[...] = (0.5 / d) * jnp.sum(acc_ref[...], axis=1, keepdims=True)

    blk = pl.BlockSpec((tm, d), lambda i: (i, 0))
    return pl.pallas_call(
        body, name=name,
        out_shape=(jax.ShapeDtypeStruct((s, d), F32), jax.ShapeDtypeStruct((1, 1), F32)),
        grid=(nt,), in_specs=[blk, blk],
        out_specs=(blk, pl.BlockSpec((1, 1), lambda i: (0, 0))),
        scratch_shapes=[pltpu.VMEM((1, d), F32)],
        compiler_params=_cparams(1),
    )(xf, target)


def _shift_down(p, prev, k):
    tm = p.shape[0]
    row = lax.broadcasted_iota(jnp.int32, p.shape, 0)
    out = pltpu.roll(p, k, 0)
    for j in range(k):
        out = jnp.where(row == j, prev[8 - k + j:8 - k + j + 1, :], out)
    return out


def _shift_up(p, nxt, k):
    tm = p.shape[0]
    row = lax.broadcasted_iota(jnp.int32, p.shape, 0)
    out = pltpu.roll(p, tm - k, 0)
    for j in range(k):
        out = jnp.where(row == tm - k + j, nxt[j:j + 1, :], out)
    return out


def _conv_fwd(u, w, *, name, tm=512):
    s = u.shape[0]
    tm = _tile(tm, s)
    c = CONV_DIM

    def body(ab_ref, ac_ref, ah_ref, w_ref, y_ref, carry_ref):
        i = pl.program_id(0)

        @pl.when(i == 0)
        def _():
            carry_ref[...] = jnp.zeros_like(carry_ref)

        p = ac_ref[...] * ah_ref[...]
        prev = carry_ref[...]
        wv = w_ref[...]
        conv = wv[2:3, :] * p + wv[1:2, :] * _shift_down(p, prev, 1) + wv[0:1, :] * _shift_down(p, prev, 2)
        y_ref[...] = (ab_ref[...] * conv).astype(BF16)
        carry_ref[...] = p[tm - 8:tm, :]

    return pl.pallas_call(
        body, name=name, out_shape=jax.ShapeDtypeStruct((s, c), BF16), grid=(s // tm,),
        in_specs=[pl.BlockSpec((tm, c), lambda i: (i, 0)), pl.BlockSpec((tm, c), lambda i: (i, 1)),
                  pl.BlockSpec((tm, c), lambda i: (i, 2)), pl.BlockSpec((3, c), lambda i: (0, 0))],
        out_specs=pl.BlockSpec((tm, c), lambda i: (i, 0)),
        scratch_shapes=[pltpu.VMEM((8, c), F32)],
        compiler_params=_cparams(1),
    )(u, u, u, w)


def _conv_bwd(u, dy, w, *, name, tm=512):
    s = u.shape[0]
    tm = _tile(tm, s)
    nt = s // tm
    c = CONV_DIM
    hb = tm // 8

    def body(ab_ref, ac_ref, ah_ref, hc_ref, hh_ref, dy_ref, w_ref, du_ref, dw_ref, carry_ref):
        i = pl.program_id(0)

        @pl.when(i == 0)
        def _():
            carry_ref[...] = jnp.zeros_like(carry_ref)
            dw_ref[...] = jnp.zeros_like(dw_ref)

        first_tile = (nt - 1 - i) == 0
        ab, ac, ah = ab_ref[...], ac_ref[...], ah_ref[...]
        p = ac * ah
        prev = jnp.where(first_tile, 0.0, hc_ref[...] * hh_ref[...])
        wv = w_ref[...]
        p1 = _shift_down(p, prev, 1)
        p2 = _shift_down(p, prev, 2)
        conv = wv[2:3, :] * p + wv[1:2, :] * p1 + wv[0:1, :] * p2
        dyv = dy_ref[...]
        dconv = dyv * ab
        nxt = carry_ref[...]
        dp = wv[2:3, :] * dconv + wv[1:2, :] * _shift_up(dconv, nxt, 1) + wv[0:1, :] * _shift_up(dconv, nxt, 2)
        du_ref[:, 0:c] = (dyv * conv).astype(BF16)
        du_ref[:, c:2 * c] = (dp * ah).astype(BF16)
        du_ref[:, 2 * c:3 * c] = (dp * ac).astype(BF16)
        dw_ref[0:1, :] += jnp.sum(dconv * p2, axis=0, keepdims=True)
        dw_ref[1:2, :] += jnp.sum(dconv * p1, axis=0, keepdims=True)
        dw_ref[2:3, :] += jnp.sum(dconv * p, axis=0, keepdims=True)
        carry_ref[...] = dconv[0:8, :]

    rev = lambda i: nt - 1 - i
    halo = lambda i: jnp.maximum(rev(i) * hb - 1, 0)
    return pl.pallas_call(
        body, name=name,
        out_shape=(jax.ShapeDtypeStruct((s, 3 * c), BF16), jax.ShapeDtypeStruct((8, c), F32)),
        grid=(nt,),
        in_specs=[pl.BlockSpec((tm, c), lambda i: (rev(i), 0)), pl.BlockSpec((tm, c), lambda i: (rev(i), 1)),
                  pl.BlockSpec((tm, c), lambda i: (rev(i), 2)),
                  pl.BlockSpec((8, c), lambda i: (halo(i), 1)), pl.BlockSpec((8, c), lambda i: (halo(i), 2)),
                  pl.BlockSpec((tm, c), lambda i: (rev(i), 0)), pl.BlockSpec((3, c), lambda i: (0, 0))],
        out_specs=(pl.BlockSpec((tm, 3 * c), lambda i: (rev(i), 0)), pl.BlockSpec((8, c), lambda i: (0, 0))),
        scratch_shapes=[pltpu.VMEM((8, c), F32)],
        compiler_params=_cparams(1),
    )(u, u, u, u, u, dy, w)


def _lower_bound(lbl):
    m = jnp.max(lbl, axis=0, keepdims=True)
    e = jnp.exp(lbl - m)
    return e[0:1, :] / jnp.sum(e, axis=0, keepdims=True)


def _hg_masks():
    t = HG_TILE
    row = lax.broadcasted_iota(jnp.int32, (t, t), 0)
    col = lax.broadcasted_iota(jnp.int32, (t, t), 1)
    same = (row >= CHUNK) == (col >= CHUNK)
    lower = same & (col <= row)
    upper = same & (row <= col)
    return row, col, lower, upper


def _hg_gates(hf, lb):
    sig = jax.nn.sigmoid(hf)
    f = lb + (1.0 - lb) * sig
    return sig, f, jnp.log(f), 1.0 - f


def _hg_refs(b_ref, hs):
    refs = []
    for i in range(HG_TILE // HG_SUB):
        if (i * HG_SUB) % CHUNK == 0:
            refs.append(jnp.zeros((1, HG_DK), F32))
        else:
            refs.append(b_ref[i * HG_SUB - 1:i * HG_SUB, hs])
    return refs


def _hgrn_fwd(u, lbl, gn, *, name):
    s = u.shape[0]
    t = HG_TILE
    nt = s // t
    nsub = t // HG_SUB
    w = HG_WIDTH

    def body(hq_ref, hf_ref, hi_ref, hg_ref, lbl_ref, gn_ref, y_ref, o_ref, sall_ref, st_ref, b_ref):
        i = pl.program_id(0)

        @pl.when(i == 0)
        def _():
            st_ref[...] = jnp.zeros_like(st_ref)

        lb = _lower_bound(lbl_ref[...])
        _, _, g, kin = _hg_gates(hf_ref[...], lb)
        _, _, lower, _ = _hg_masks()
        b_ref[...] = _exact_left(lower.astype(BF16), g)

        for h in range(HG_HEADS):
            hs = slice(h * HG_DK, (h + 1) * HG_DK)
            bh = b_ref[:, hs]
            qh = hq_ref[:, hs]
            kh = kin[:, hs]
            vh = hi_ref[:, hs]
            vsp = _sp(vh)
            refs = _hg_refs(b_ref, hs)
            rmat = jnp.concatenate([jnp.broadcast_to(r, (HG_SUB, HG_DK)) for r in refs], axis=0)
            qt = qh * jnp.exp(bh - rmat)
            prow = []
            for j in range(nsub):
                kj = kh * jnp.exp(jnp.minimum(refs[j] - bh, HG_EXP_CLAMP))
                prow.append(_dot3(_nt, _sp(qt[j * HG_SUB:(j + 1) * HG_SUB]), _sp(kj)))
            p = jnp.where(lower, jnp.concatenate(prow, axis=0), 0.0)
            intra = _dot3(_nn, _sp(p), vsp)
            o_parts = []
            for c in range(t // CHUNK):
                rs = slice(c * CHUNK, (c + 1) * CHUNK)
                st0 = st_ref[hs, :]
                sall_ref[c * w + h * HG_DK:c * w + (h + 1) * HG_DK, :] = st0
                bl = b_ref[c * CHUNK + CHUNK - 1:c * CHUNK + CHUNK, hs]
                qf = qh[rs] * jnp.exp(bh[rs])
                o_parts.append(_dot3(_nt, _sp(qf), _sp(st0)) + intra[rs])
                khat = kh[rs] * jnp.exp(bl - bh[rs])
                st_ref[hs, :] = st0 * jnp.exp(bl) + _dot3(_tn, _sp(vh[rs]), _sp(khat))
            o = jnp.concatenate(o_parts, axis=0)
            o_ref[:, hs] = o
            r = lax.rsqrt(jnp.mean(o * o, axis=-1, keepdims=True) + EPS)
            hg = hg_ref[:, hs]
            y_ref[:, hs] = (((o * r) * gn_ref[:, hs]) * (hg * jax.nn.sigmoid(hg))).astype(BF16)

    blk = lambda j: pl.BlockSpec((t, w), lambda i, j=j: (i, j))
    srows = (t // CHUNK) * w
    return pl.pallas_call(
        body, name=name,
        out_shape=(jax.ShapeDtypeStruct((s, w), BF16), jax.ShapeDtypeStruct((s, w), F32),
                   jax.ShapeDtypeStruct((nt * srows, HG_DK), F32)),
        grid=(nt,),
        in_specs=[blk(3), blk(4), blk(5), blk(6), pl.BlockSpec((3, w), lambda i: (0, 0)),
                  pl.BlockSpec((1, w), lambda i: (0, 0))],
        out_specs=(pl.BlockSpec((t, w), lambda i: (i, 0)), pl.BlockSpec((t, w), lambda i: (i, 0)),
                   pl.BlockSpec((srows, HG_DK), lambda i: (i, 0))),
        scratch_shapes=[pltpu.VMEM((w, HG_DK), F32), pltpu.VMEM((t, w), F32)],
        compiler_params=_cparams(1),
    )(u, u, u, u, lbl, gn)


def _hgrn_bwd(u, o_all, sall, dy, lbl, gn, *, name):
    s = u.shape[0]
    t = HG_TILE
    nt = s // t
    nsub = t // HG_SUB
    w = HG_WIDTH
    nch = t // CHUNK

    def body(hq_ref, hf_ref, hi_ref, hg_ref, o_ref, sall_ref, dy_ref, lbl_ref, gn_ref,
             du_ref, acc_ref, dst_ref, b_ref):
        i = pl.program_id(0)

        @pl.when(i == 0)
        def _():
            dst_ref[...] = jnp.zeros_like(dst_ref)
            acc_ref[...] = jnp.zeros_like(acc_ref)

        lb = _lower_bound(lbl_ref[...])
        sig, f, g, kin = _hg_gates(hf_ref[...], lb)
        row, col, lower, upper = _hg_masks()
        b_ref[...] = _exact_left(lower.astype(BF16), g)
        upper_bf = upper.astype(BF16)
        rowblk = [((row >= j * HG_SUB) & (row < (j + 1) * HG_SUB)) for j in range(nsub)]
        colblk = [((col >= j * HG_SUB) & (col < (j + 1) * HG_SUB)) for j in range(nsub)]
        row1 = lax.broadcasted_iota(jnp.int32, (t, HG_DK), 0)

        for h in range(HG_HEADS):
            hs = slice(h * HG_DK, (h + 1) * HG_DK)
            bh = b_ref[:, hs]
            qh = hq_ref[:, hs]
            kh = kin[:, hs]
            vh = hi_ref[:, hs]
            vsp = _sp(vh)
            hg = hg_ref[:, hs]
            gnh = gn_ref[:, hs]
            o = o_ref[:, hs]
            dyv = dy_ref[:, hs]
            sg = jax.nn.sigmoid(hg)
            r = lax.rsqrt(jnp.mean(o * o, axis=-1, keepdims=True) + EPS)
            ohat = o * r
            du_ref[:, 3 * w + h * HG_DK:3 * w + (h + 1) * HG_DK] = (
                dyv * (ohat * gnh) * (sg * (1.0 + hg * (1.0 - sg)))).astype(BF16)
            don = dyv * (hg * sg)
            acc_ref[0:1, hs] += jnp.sum(don * ohat, axis=0, keepdims=True)
            dohat = don * gnh
            do = r * (dohat - ohat * jnp.mean(dohat * ohat, axis=-1, keepdims=True))
            dosp = _sp(do)
            refs = _hg_refs(b_ref, hs)
            rmat = jnp.concatenate([jnp.broadcast_to(rr, (HG_SUB, HG_DK)) for rr in refs], axis=0)
            eq = jnp.exp(bh - rmat)
            qt = qh * eq
            qtsp = _sp(qt)
            dp = jnp.where(lower, _dot3(_nt, dosp, vsp), 0.0)
            dpt = jnp.where(upper, _dot3(_nt, vsp, dosp), 0.0)
            pt = jnp.zeros((t, t), F32)
            dk = jnp.zeros((t, HG_DK), F32)
            dq_rows = []
            for j in range(nsub):
                ek = jnp.exp(jnp.minimum(refs[j] - bh, HG_EXP_CLAMP))
                kjsp = _sp(kh * ek)
                pt = pt + _dot3(_nt, kjsp, _sp(jnp.where(rowblk[j], qt, 0.0)))
                dq_rows.append(_dot3(_nn, _sp(dp[j * HG_SUB:(j + 1) * HG_SUB]), kjsp))
                dk = dk + ek * _dot3(_nn, _sp(jnp.where(colblk[j], dpt, 0.0)), qtsp)
            pt = jnp.where(upper, pt, 0.0)
            dv = _dot3(_nn, _sp(pt), dosp)
            dq = jnp.concatenate(dq_rows, axis=0) * eq
            dq_c, dk_c, dv_c, ex_c = [None] * nch, [None] * nch, [None] * nch, [None] * nch
            for c in reversed(range(nch)):
                rs = slice(c * CHUNK, (c + 1) * CHUNK)
                st0 = sall_ref[c * w + h * HG_DK:c * w + (h + 1) * HG_DK, :]
                dst1 = dst_ref[hs, :]
                dst1sp = _sp(dst1)
                dosp_c = _sp(do[rs])
                bl = b_ref[c * CHUNK + CHUNK - 1:c * CHUNK + CHUNK, hs]
                e = jnp.exp(bh[rs])
                el = jnp.exp(bl)
                ekl = jnp.exp(bl - bh[rs])
                dq_c[c] = _dot3(_nn, dosp_c, _sp(st0)) * e
                khat = kh[rs] * ekl
                dv_c[c] = _dot3(_nt, _sp(khat), dst1sp)
                dkhat = _dot3(_nn, _sp(vh[rs]), dst1sp)
                dk_c[c] = dkhat * ekl
                ex_c[c] = (jnp.sum(dkhat * khat, axis=0, keepdims=True)
                           + el * jnp.sum(dst1 * st0, axis=0, keepdims=True))
                dst_ref[hs, :] = _dot3(_tn, dosp_c, _sp(qh[rs] * e)) + dst1 * el
            dq = dq + jnp.concatenate(dq_c, axis=0)
            dk = dk + jnp.concatenate(dk_c, axis=0)
            dv = dv + jnp.concatenate(dv_c, axis=0)
            db = qh * dq - kh * dk
            for c in range(nch):
                db = db + jnp.where(row1 == c * CHUNK + CHUNK - 1, ex_c[c], 0.0)
            dg = _exact_left(upper_bf, db)
            fh = f[:, hs]
            sgf = sig[:, hs]
            lbh = lb[:, hs]
            df = dg / fh - dk
            du_ref[:, hs] = dq.astype(BF16)
            du_ref[:, w + h * HG_DK:w + (h + 1) * HG_DK] = (df * (1.0 - lbh) * sgf * (1.0 - sgf)).astype(BF16)
            du_ref[:, 2 * w + h * HG_DK:2 * w + (h + 1) * HG_DK] = dv.astype(BF16)
            acc_ref[1:2, hs] += jnp.sum(df * (1.0 - sgf), axis=0, keepdims=True)

    rev = lambda i: nt - 1 - i
    blk = lambda j: pl.BlockSpec((t, w), lambda i, j=j: (rev(i), j))
    srows = nch * w
    return pl.pallas_call(
        body, name=name,
        out_shape=(jax.ShapeDtypeStruct((s, 4 * w), BF16), jax.ShapeDtypeStruct((8, w), F32)),
        grid=(nt,),
        in_specs=[blk(3), blk(4), blk(5), blk(6), pl.BlockSpec((t, w), lambda i: (rev(i), 0)),
                  pl.BlockSpec((srows, HG_DK), lambda i: (rev(i), 0)),
                  pl.BlockSpec((t, w), lambda i: (rev(i), 1)),
                  pl.BlockSpec((3, w), lambda i: (0, 0)), pl.BlockSpec((1, w), lambda i: (0, 0))],
        out_specs=(pl.BlockSpec((t, 4 * w), lambda i: (rev(i), 0)), pl.BlockSpec((8, w), lambda i: (0, 0))),
        scratch_shapes=[pltpu.VMEM((w, HG_DK), F32), pltpu.VMEM((t, w), F32)],
        compiler_params=_cparams(1),
    )(u, u, u, u, o_all, sall, dy, lbl, gn)


def _pair_matrix():
    row = lax.broadcasted_iota(jnp.int32, (LANES, LANES), 0)
    col = lax.broadcasted_iota(jnp.int32, (LANES, LANES), 1)
    return ((row >= SB_HEAD_DIM) == (col >= SB_HEAD_DIM)).astype(BF16)


def _qk_norm_fwd(qkv, qn, kn, *, name, tm=256):
    s = qkv.shape[0]
    d = D_MODEL
    tm = _tile(tm, s)

    def body(q_ref, k_ref, v_ref, qn_ref, kn_ref, qo_ref, ko_ref, vo_ref):
        bd = _pair_matrix()
        for src, gain, dst, fac in ((q_ref, qn_ref, qo_ref, SB_SCALE * LOG2E), (k_ref, kn_ref, ko_ref, None)):
            for grp in range(d // LANES):
                ls = slice(grp * LANES, (grp + 1) * LANES)
                xv = src[:, ls]
                ms = _exact_right(xv * xv, bd) * (1.0 / SB_HEAD_DIM)
                y = (xv * lax.rsqrt(ms + EPS)) * gain[:, ls]
                dst[:, ls] = (y if fac is None else y * fac).astype(BF16)
        vo_ref[...] = v_ref[...].astype(BF16)

    blk = lambda j: pl.BlockSpec((tm, d), lambda i, j=j: (i, j))
    row = pl.BlockSpec((1, d), lambda i: (0, 0))
    out = jax.ShapeDtypeStruct((s, d), BF16)
    return pl.pallas_call(
        body, name=name, out_shape=(out, out, out), grid=(s // tm,),
        in_specs=[blk(0), blk(1), blk(2), row, row],
        out_specs=(blk(0), blk(0), blk(0)),
        compiler_params=_cparams(1),
    )(qkv, qkv, qkv, qn, kn)


def _qk_norm_bwd(qkv, dqn, dkn, dv, qn, kn, *, name, tm=256):
    s = qkv.shape[0]
    d = D_MODEL
    tm = _tile(tm, s)

    def body(q_ref, k_ref, dq_ref, dk_ref, dv_ref, qn_ref, kn_ref, o_ref, acc_ref):
        i = pl.program_id(0)

        @pl.when(i == 0)
        def _():
            acc_ref[...] = jnp.zeros_like(acc_ref)

        bd = _pair_matrix()
        for idx, (src, dsrc, gain) in enumerate(((q_ref, dq_ref, qn_ref), (k_ref, dk_ref, kn_ref))):
            for grp in range(d // LANES):
                ls = slice(grp * LANES, (grp + 1) * LANES)
                xv = src[:, ls]
                dyv = dsrc[:, ls]
                r = lax.rsqrt(_exact_right(xv * xv, bd) * (1.0 / SB_HEAD_DIM) + EPS)
                xh = xv * r
                acc_ref[idx:idx + 1, ls] += jnp.sum(dyv * xh, axis=0, keepdims=True)
                dxh = dyv * gain[:, ls]
                mean = _exact_right(dxh * xh, bd) * (1.0 / SB_HEAD_DIM)
                o_ref[:, idx * d + grp * LANES:idx * d + (grp + 1) * LANES] = (r * (dxh - xh * mean)).astype(BF16)
        o_ref[:, 2 * d:3 * d] = dv_ref[...].astype(BF16)

    blk = lambda j: pl.BlockSpec((tm, d), lambda i, j=j: (i, j))
    row = pl.BlockSpec((1, d), lambda i: (0, 0))
    return pl.pallas_call(
        body, name=name,
        out_shape=(jax.ShapeDtypeStruct((s, 3 * d), BF16), jax.ShapeDtypeStruct((8, d), F32)),
        grid=(s // tm,),
        in_specs=[blk(0), blk(1), blk(0), blk(0), blk(0), row, row],
        out_specs=(pl.BlockSpec((tm, 3 * d), lambda i: (i, 0)), pl.BlockSpec((8, d), lambda i: (0, 0))),
        compiler_params=_cparams(1),
    )(qkv, qkv, dqn, dkn, dv, qn, kn)


def _sb_tile(qh, kb, suffix_ones, run, mask):
    z = _nt(qh, kb)
    neg_abs = lax.bitcast_convert_type(lax.bitcast_convert_type(z, jnp.uint32) | jnp.uint32(0x80000000), F32)
    l1m = -(jnp.maximum(z, 0.0) + jnp.log2(1.0 + jnp.exp2(neg_abs)))
    logb = z + l1m
    if mask is not None:
        l1m = jnp.where(mask, l1m, 0.0)
    later = _nn(l1m.astype(BF16), suffix_ones) + run
    wgt = jnp.exp2(logb + later)
    if mask is not None:
        wgt = jnp.where(mask, wgt, 0.0)
    return logb, l1m, wgt


def _suffix_ones(tk):
    row = lax.broadcasted_iota(jnp.int32, (tk, tk), 0)
    col = lax.broadcasted_iota(jnp.int32, (tk, tk), 1)
    return (row > col).astype(BF16)


def _sb_alive(runs):
    return jnp.max(jnp.maximum(runs[0], runs[1])) > -SB_DEAD


def _sb_mask(qi, j, tq, tk):
    qpos = qi * tq + lax.broadcasted_iota(jnp.int32, (tq, tk), 0)
    kpos = j * tk + lax.broadcasted_iota(jnp.int32, (tq, tk), 1)
    return kpos < qpos


def _sb_fwd(qn, kn, v, *, name):
    s, d = qn.shape
    tq, tk = _tile(SB_TQ, s), _tile(SB_TK, s)
    assert tk % tq == 0 or tq % tk == 0
    nq = s // tq

    def body(q_ref, k_ref, v_ref, o_ref, acc_ref):
        qi = pl.program_id(1)
        lane = lax.broadcasted_iota(jnp.int32, (tq, LANES), 1)
        first = lane < SB_HEAD_DIM
        q = q_ref[...]
        qh = [jnp.where(first, q, 0).astype(BF16), jnp.where(first, 0, q).astype(BF16)]
        ones = _suffix_ones(tk)
        acc_ref[...] = jnp.zeros_like(acc_ref)

        def tile(j, runs, masked):
            ks = pl.ds(pl.multiple_of(j * tk, tk), tk)
            kb = k_ref[ks, :]
            vb = v_ref[ks, :]
            mask = _sb_mask(qi, j, tq, tk) if masked else None
            new_runs = []
            for hh in range(2):
                _, l1m, wgt = _sb_tile(qh[hh], kb, ones, runs[hh], mask)
                acc_ref[hh] += _nn(wgt.astype(BF16), vb)
                new_runs.append(runs[hh] + jnp.sum(l1m, axis=1, keepdims=True))
            return tuple(new_runs)

        nfull = (qi * tq) // tk
        zero = jnp.zeros((tq, 1), F32)
        runs = (zero, zero)
        for m in reversed(range(max(tq // tk, 1))):
            runs = tile(nfull + m, runs, True)

        def step(c):
            it, _, r = c
            r = tile(nfull - 1 - it, r, False)
            return it + 1, _sb_alive(r), r

        lax.while_loop(lambda c: jnp.logical_and(c[0] < nfull, c[1]), step, (0, _sb_alive(runs), runs))
        o_ref[...] = jnp.where(first, acc_ref[0], acc_ref[1])

    return pl.pallas_call(
        body, name=name, out_shape=jax.ShapeDtypeStruct((s, d), F32), grid=(d // LANES, nq),
        in_specs=[pl.BlockSpec((tq, LANES), lambda p, i: (i, p)), pl.BlockSpec((s, LANES), lambda p, i: (0, p)),
                  pl.BlockSpec((s, LANES), lambda p, i: (0, p))],
        out_specs=pl.BlockSpec((tq, LANES), lambda p, i: (i, p)),
        scratch_shapes=[pltpu.VMEM((2, tq, LANES), F32)],
        compiler_params=_cparams(2),
    )(qn, kn, v)


def _sb_bwd(qn, kn, v, o, do, *, name):
    s, d = qn.shape
    tq, tk = _tile(SB_TQ, s), _tile(SB_TK, s)
    assert tk % tq == 0 or tq % tk == 0
    nq = s // tq

    def body(q_ref, k_ref, v_ref, o_ref, do_ref, dq_ref, dk_ref, dv_ref, acc_ref):
        qi = pl.program_id(1)

        @pl.when(qi == 0)
        def _():
            dk_ref[...] = jnp.zeros_like(dk_ref)
            dv_ref[...] = jnp.zeros_like(dv_ref)

        first = lax.broadcasted_iota(jnp.int32, (tq, LANES), 1) < SB_HEAD_DIM
        sel = [first, jnp.logical_not(first)]
        kfirst = lax.broadcasted_iota(jnp.int32, (tk, LANES), 1) < SB_HEAD_DIM
        ksel = [kfirst, jnp.logical_not(kfirst)]
        q = q_ref[...]
        dob = do_ref[...].astype(BF16)
        qh = [jnp.where(sel[hh], q, 0).astype(BF16) for hh in range(2)]
        doh = [jnp.where(sel[hh], dob, 0).astype(BF16) for hh in range(2)]
        prod = dob.astype(F32) * o_ref[...]
        gtot = [jnp.sum(jnp.where(sel[hh], prod, 0.0), axis=1, keepdims=True) for hh in range(2)]
        ones = _suffix_ones(tk)
        acc_ref[...] = jnp.zeros_like(acc_ref)

        def tile(j, carry, masked):
            runs, gruns = carry
            ks = pl.ds(pl.multiple_of(j * tk, tk), tk)
            kb = k_ref[ks, :]
            vb = v_ref[ks, :]
            mask = _sb_mask(qi, j, tq, tk) if masked else None
            new_runs, new_gruns = [], []
            dk_add = jnp.zeros((tk, LANES), F32)
            dv_add = jnp.zeros((tk, LANES), F32)
            for hh in range(2):
                logb, l1m, wgt = _sb_tile(qh[hh], kb, ones, runs[hh], mask)
                wb = wgt.astype(BF16)
                g = _nt(doh[hh], vb) * wb.astype(F32)
                gsuf = _exact_right2(g, ones) + g + gruns[hh]
                dz = g - jnp.exp2(logb) * (g + (gtot[hh] - gsuf))
                if masked:
                    dz = jnp.where(mask, dz, 0.0)
                dzb = dz.astype(BF16)
                acc_ref[hh] += _nn(dzb, kb)
                dk_add = dk_add + jnp.where(ksel[hh], _tn(dzb, qh[hh]), 0.0)
                dv_add = dv_add + jnp.where(ksel[hh], _tn(wb, doh[hh]), 0.0)
                new_runs.append(runs[hh] + jnp.sum(l1m, axis=1, keepdims=True))
                new_gruns.append(gruns[hh] + jnp.sum(g, axis=1, keepdims=True))
            dk_ref[ks, :] += dk_add * LN2
            dv_ref[ks, :] += dv_add
            return tuple(new_runs), tuple(new_gruns)

        nfull = (qi * tq) // tk
        zero = jnp.zeros((tq, 1), F32)
        carry = ((zero, zero), (zero, zero))
        for m in reversed(range(max(tq // tk, 1))):
            carry = tile(nfull + m, carry, True)

        def step(c):
            it, _, cr = c
            cr = tile(nfull - 1 - it, cr, False)
            return it + 1, _sb_alive(cr[0]), cr

        lax.while_loop(lambda c: jnp.logical_and(c[0] < nfull, c[1]), step, (0, _sb_alive(carry[0]), carry))
        dq_ref[...] = jnp.where(first, acc_ref[0], acc_ref[1]) * SB_SCALE

    blk = pl.BlockSpec((tq, LANES), lambda p, i: (i, p))
    full = pl.BlockSpec((s, LANES), lambda p, i: (0, p))
    out = jax.ShapeDtypeStruct((s, d), F32)
    return pl.pallas_call(
        body, name=name, out_shape=(out, out, out), grid=(d // LANES, nq),
        in_specs=[blk, full, full, blk, blk],
        out_specs=(blk, full, full),
        scratch_shapes=[pltpu.VMEM((2, tq, LANES), F32)],
        compiler_params=_cparams(2),
    )(qn, kn, v, o, do)


def _mod_part(c_all, ada_w, ada_b_my, *, name):
    nl, d, ncol = ada_w.shape

    def body(c_ref, w_ref, b_ref, part_ref, ca_ref):
        cv = c_ref[...]
        ca = cv * jax.nn.sigmoid(cv)
        ca_ref[...] = ca
        part_ref[...] = _nn(ca.astype(BF16), w_ref[...].astype(BF16)) + b_ref[...]

    return pl.pallas_call(
        body, name=name,
        out_shape=(jax.ShapeDtypeStruct((nl, N_DEV, ncol), F32), jax.ShapeDtypeStruct((N_DEV, d), F32)),
        grid=(nl,),
        in_specs=[pl.BlockSpec((N_DEV, d), lambda l: (0, 0)), pl.BlockSpec((None, d, ncol), lambda l: (l, 0, 0)),
                  pl.BlockSpec((None, 1, ncol), lambda l: (l, 0, 0))],
        out_specs=(pl.BlockSpec((None, N_DEV, ncol), lambda l: (l, 0, 0)), pl.BlockSpec((N_DEV, d), lambda l: (0, 0))),
        compiler_params=_cparams(1),
    )(c_all, ada_w, ada_b_my)


PK_MOD, PK_NMIX, PK_NMLP, PK_HGN, PK_LB, PK_QN, PK_KN, PK_CONV, PK_ROWS = 0, 96, 112, 128, 132, 136, 144, 152, 168


def _small_grads(gath, ca_col, dmod_my, lbl4, *, name):
    def body(g_ref, ca_ref, dm_ref, lbl_ref, gw_ref, gsum_ref, glb_ref, gqk_ref):
        tot = g_ref[0]
        for dev in range(1, N_DEV):
            tot = tot + g_ref[dev]
        gsum_ref[...] = tot
        lv = lbl_ref[...]
        m = jnp.maximum(jnp.maximum(lv[0], lv[1]), lv[2])
        e = [jnp.exp(lv[k] - m) for k in range(3)]
        den = e[0] + e[1] + e[2]
        p = [ek / den for ek in e]
        dlb = tot[PK_LB:PK_LB + 4, :]
        glb_ref[0] = dlb * p[0] * (1.0 - p[0])
        glb_ref[1] = -dlb * p[0] * p[1]
        glb_ref[2] = -dlb * p[0] * p[2]
        for idx, base in enumerate((PK_QN, PK_KN)):
            rowsum = jnp.sum(tot[base:base + 8, :], axis=0, keepdims=True)
            gqk_ref[idx:idx + 1, :] = rowsum + pltpu.roll(rowsum, SB_HEAD_DIM, 1)
        for l in range(2):
            acc = ca_ref[0] * dm_ref[0, l:l + 1, :]
            for smp in range(1, N_DEV):
                acc = acc + ca_ref[smp] * dm_ref[smp, l:l + 1, :]
            gw_ref[l] = acc

    d, ncol = ca_col.shape[1], dmod_my.shape[2]
    vm = pl.BlockSpec(memory_space=pltpu.VMEM)
    return pl.pallas_call(
        body, name=name,
        out_shape=(jax.ShapeDtypeStruct((2, d, ncol), F32), jax.ShapeDtypeStruct((PK_ROWS, LANES), F32),
                   jax.ShapeDtypeStruct((3, 4, LANES), F32), jax.ShapeDtypeStruct((8, LANES), F32)),
        in_specs=[vm, vm, vm, vm], out_specs=(vm, vm, vm, vm),
        compiler_params=pltpu.CompilerParams(vmem_limit_bytes=VMEM_LIMIT),
    )(gath, ca_col, dmod_my, lbl4)


def _adamw_math(w, g, m, v):
    m = ADAM_B1 * m + (1.0 - ADAM_B1) * g
    v = ADAM_B2 * v + (1.0 - ADAM_B2) * (g * g)
    m_hat = m / (1.0 - ADAM_B1 ** ADAM_STEP)
    v_hat = v / (1.0 - ADAM_B2 ** ADAM_STEP)
    delta = -ADAM_LR * (m_hat / (jnp.sqrt(v_hat) + ADAM_EPS) + ADAM_WD * w)
    return delta, m, v


def _adamw(w, g, m, v, *, name, tr=256):
    r, n = w.shape
    tr = _tile(tr, r)

    def body(w_ref, g_ref, m_ref, v_ref, d_ref, mo_ref, vo_ref):
        dl, mn, vn = _adamw_math(w_ref[...], g_ref[...], m_ref[...], v_ref[...])
        d_ref[...] = dl
        mo_ref[...] = mn
        vo_ref[...] = vn

    blk = pl.BlockSpec((tr, n), lambda i: (i, 0))
    out = jax.ShapeDtypeStruct((r, n), F32)
    return pl.pallas_call(
        body, name=name, out_shape=(out, out, out), grid=(r // tr,),
        in_specs=[blk, blk, blk, blk], out_specs=(blk, blk, blk),
        compiler_params=_cparams(1),
    )(w, g, m, v)


def _adamw_small(items, *, name):
    n = len(items)

    def body(*refs):
        ins, outs = refs[:4 * n], refs[4 * n:]
        for k in range(n):
            dl, mn, vn = _adamw_math(*(r[...] for r in ins[4 * k:4 * k + 4]))
            outs[3 * k][...] = dl
            outs[3 * k + 1][...] = mn
            outs[3 * k + 2][...] = vn

    flat = [a for it in items for a in it]
    out_shape = tuple(jax.ShapeDtypeStruct(it[0].shape, F32) for it in items for _ in range(3))
    vm = pl.BlockSpec(memory_space=pltpu.VMEM)
    res = pl.pallas_call(
        body, name=name, out_shape=out_shape, in_specs=[vm] * (4 * n), out_specs=tuple([vm] * (3 * n)),
    )(*flat)
    return [tuple(res[3 * k:3 * k + 3]) for k in range(n)]


def _mlp_fwd(x, g, scale, shift, gate, w1, w2, tag):
    h = _norm_mod(x, g, scale, shift, name=f"{tag}_norm")
    act = _matmul(h, w1, epi="relu2", out_dtype=BF16, name=f"{tag}_w1")
    z, x_out = _matmul(act, w2, epi="resgate", extras=(x, gate), name=f"{tag}_w2")
    return x_out, (h, act, z)


def _mlp_bwd(dx_out, x, saved, g, scale, gate, w1, w2, tag):
    h, act, z = saved
    dz, gate_acc = _gate_bwd(dx_out, z, gate, name=f"{tag}_gate_bwd")
    du = _matmul(dz, w2, tb=True, epi="dact", extras=(act,), out_dtype=BF16, name=f"{tag}_dact")
    dw2 = _matmul(act, dz, ta=True, name=f"{tag}_dw2")
    dw1 = _matmul(h, du, ta=True, name=f"{tag}_dw1")
    dh = _matmul(du, w1, tb=True, name=f"{tag}_dh")
    dx, nacc = _norm_mod_bwd(x, dh, dx_out, g, scale, name=f"{tag}_norm_bwd")
    return dx, dw1, dw2, (nacc[0:1], nacc[1:2], gate_acc[0:1]), nacc[2:3]


def kernel(x, c, ada_w, ada_b, norm_mix, norm_mlp, w_in_ab, conv_w, hg_norm, lb_logits, w_out_ab, w_qkv, q_norm, k_norm, w_out_c, mlp_w1, mlp_w2, loss_target, m_ada_w, m_ada_b, m_norm_mix, m_norm_mlp, m_w_in_ab, m_conv_w, m_hg_norm, m_lb_logits, m_w_out_ab, m_w_qkv, m_q_norm, m_k_norm, m_w_out_c, m_mlp_w1, m_mlp_w2, v_ada_w, v_ada_b, v_norm_mix, v_norm_mlp, v_w_in_ab, v_conv_w, v_hg_norm, v_lb_logits, v_w_out_ab, v_w_qkv, v_q_norm, v_k_norm, v_w_out_c, v_mlp_w1, v_mlp_w2):
    d = D_MODEL
    my_x, my_y, my_c = lax.axis_index("x"), lax.axis_index("y"), lax.axis_index("c")
    me = 4 * my_x + 2 * my_y + my_c
    xs = x[0]
    tgt = loss_target[0]

    big = [w_in_ab, w_out_ab, w_qkv, w_out_c, mlp_w1, mlp_w2]
    rows = [w.size // d for w in big]
    offs = [sum(rows[:k]) for k in range(len(rows) + 1)]
    packed = jnp.concatenate([w.astype(BF16).reshape(-1, d) for w in big], axis=0)
    gathered = _all_gather(packed, name="gather_weights", in_vmem=False).reshape(N_DEV, offs[-1], d)

    def piece(k):
        return gathered[:, offs[k]:offs[k + 1]]

    def cols(k, nl, kdim, ncol):
        return piece(k).reshape(N_DEV, nl, kdim, ncol).transpose(1, 2, 0, 3).reshape(nl, kdim, N_DEV * ncol)

    def rws(k, nl, nrow, n):
        return piece(k).reshape(N_DEV, nl, nrow, n).transpose(1, 0, 2, 3).reshape(nl, N_DEV * nrow, n)

    win = cols(0, 1, d, AB_IN // N_DEV)[0]
    wout_ab = rws(1, 1, d // N_DEV, d)[0]
    wqkv = cols(2, 1, d, 3 * d // N_DEV)[0]
    wout_c = rws(3, 1, d // N_DEV, d)[0]
    w1 = cols(4, 2, d, D_FF // N_DEV)
    w2 = rws(5, 2, D_FF // N_DEV, d)

    ncv = CONV_DIM // N_DEV
    c_and_conv = jnp.concatenate([c, jnp.pad(conv_w[0], ((0, 0), (0, d - ncv))), jnp.zeros((4, d), F32)], axis=0)
    c_and_conv = _all_gather(c_and_conv, name="gather_c", in_vmem=True).reshape(N_DEV, 8, d)
    c_all = c_and_conv[:, 0]
    conv_full = c_and_conv[:, 1:4, :ncv].transpose(1, 0, 2).reshape(3, CONV_DIM)
    ncol = ada_w.shape[2]
    ada_b_my = lax.dynamic_slice(ada_b, (0, me * ncol), (2, ncol)).reshape(2, 1, ncol)
    part, c_act = _mod_part(c_all, ada_w, ada_b_my, name="mod_part")
    parts = _all_gather(part.reshape(2 * N_DEV, ncol), name="gather_mod", in_vmem=True)
    parts = parts.reshape(N_DEV, 2, N_DEV, ncol)
    mod = lax.dynamic_index_in_dim(parts, me, axis=2, keepdims=False)
    mod = mod.transpose(1, 0, 2).reshape(2, 6, 1, d)

    qn_t = jnp.tile(q_norm, (1, d // SB_HEAD_DIM))
    kn_t = jnp.tile(k_norm, (1, d // SB_HEAD_DIM))

    sh1, sc1, gt1, sh2, sc2, gt2 = [mod[0, k] for k in range(6)]
    h0 = _norm_mod(xs, norm_mix[0:1], sc1, sh1, name="l0_mix_norm")
    u = _matmul(h0, win, name="l0_in_proj")
    y_a = _conv_fwd(u, conv_full, name="l0_conv")
    y_b, o_hg, sall = _hgrn_fwd(u, lb_logits, hg_norm, name="l0_hgrn")
    y_ab = jnp.concatenate([y_a, y_b], axis=1)
    z0, x_mid0 = _matmul(y_ab, wout_ab, epi="resgate", extras=(xs, gt1), name="l0_out_proj")
    x1, mlp0 = _mlp_fwd(x_mid0, norm_mlp[0:1], sc2, sh2, gt2, w1[0], w2[0], "l0_mlp")

    sh1b, sc1b, gt1b, sh2b, sc2b, gt2b = [mod[1, k] for k in range(6)]
    h1 = _norm_mod(x1, norm_mix[1:2], sc1b, sh1b, name="l1_mix_norm")
    qkv = _matmul(h1, wqkv, name="l1_qkv_proj")
    qn_a, kn_a, v_a = _qk_norm_fwd(qkv, qn_t, kn_t, name="l1_qk_norm")
    o_sb = _sb_fwd(qn_a, kn_a, v_a, name="l1_sb")
    z1, x_mid1 = _matmul(o_sb, wout_c, epi="resgate", extras=(x1, gt1b), name="l1_out_proj")
    x2, mlp1 = _mlp_fwd(x_mid1, norm_mlp[1:2], sc2b, sh2b, gt2b, w1[1], w2[1], "l1_mlp")

    dx, loss_part = _loss_grad(x2, tgt, name="loss")
    loss = lax.psum(loss_part[0, 0], MESH_AXES)

    dx, dw1_1, dw2_1, (dsh2b, dsc2b, dgt2b), dnmlp1 = _mlp_bwd(
        dx, x_mid1, mlp1, norm_mlp[1:2], sc2b, gt2b, w1[1], w2[1], "l1_mlp")
    dyp, gacc = _gate_bwd(dx, z1, gt1b, name="l1_mix_gate_bwd")
    dwout_c = _matmul(o_sb, dyp, ta=True, name="l1_dwout")
    do_sb = _matmul(dyp, wout_c, tb=True, name="l1_do")
    dqn_a, dkn_a, dv_a = _sb_bwd(qn_a, kn_a, v_a, o_sb, do_sb, name="l1_sb_bwd")
    dqkv, qkacc = _qk_norm_bwd(qkv, dqn_a, dkn_a, dv_a, qn_t, kn_t, name="l1_qk_norm_bwd")
    dwqkv = _matmul(h1, dqkv, ta=True, name="l1_dwqkv")
    dh1 = _matmul(dqkv, wqkv, tb=True, name="l1_dh")
    dx, nacc = _norm_mod_bwd(x1, dh1, dx, norm_mix[1:2], sc1b, name="l1_mix_norm_bwd")
    dmod1 = [nacc[0:1], nacc[1:2], gacc[0:1], dsh2b, dsc2b, dgt2b]
    dnmix1 = nacc[2:3]

    dx, dw1_0, dw2_0, (dsh2, dsc2, dgt2), dnmlp0 = _mlp_bwd(
        dx, x_mid0, mlp0, norm_mlp[0:1], sc2, gt2, w1[0], w2[0], "l0_mlp")
    dyp, gacc = _gate_bwd(dx, z0, gt1, name="l0_mix_gate_bwd")
    dwout_ab = _matmul(y_ab, dyp, ta=True, name="l0_dwout")
    dy_ab = _matmul(dyp, wout_ab, tb=True, name="l0_dy")
    du_a, dconv = _conv_bwd(u, dy_ab, conv_full, name="l0_conv_bwd")
    du_b, hgacc = _hgrn_bwd(u, o_hg, sall, dy_ab, lb_logits, hg_norm, name="l0_hgrn_bwd")
    du = jnp.concatenate([du_a, du_b], axis=1)
    dwin = _matmul(h0, du, ta=True, name="l0_dwin")
    dh0 = _matmul(du, win, tb=True, name="l0_dh")
    grad_x, nacc = _norm_mod_bwd(xs, dh0, dx, norm_mix[0:1], sc1, name="l0_mix_norm_bwd")
    dmod0 = [nacc[0:1], nacc[1:2], gacc[0:1], dsh2, dsc2, dgt2]
    dnmix0 = nacc[2:3]

    def to_cols(gw, nl, kdim, ncol_):
        return gw.reshape(nl, kdim, N_DEV, ncol_).transpose(2, 0, 1, 3).reshape(N_DEV, -1, d)

    def to_rows(gw, nl, nrow, n):
        return gw.reshape(nl, N_DEV, nrow, n).transpose(1, 0, 2, 3).reshape(N_DEV, -1, d)

    gbig = jnp.concatenate([
        to_cols(dwin[None], 1, d, AB_IN // N_DEV), to_rows(dwout_ab[None], 1, d // N_DEV, d),
        to_cols(dwqkv[None], 1, d, 3 * d // N_DEV), to_rows(dwout_c[None], 1, d // N_DEV, d),
        to_cols(jnp.stack([dw1_0, dw1_1]), 2, d, D_FF // N_DEV),
        to_rows(jnp.stack([dw2_0, dw2_1]), 2, D_FF // N_DEV, d)], axis=1)
    sib = _rs_sibling_exchange(gbig, name="rs_sibling")
    pair = _rs_pair_sum(gbig, sib, my_c.reshape(1).astype(jnp.int32), name="rs_pair_sum")
    far = _rs_chip_exchange(pair, name="rs_chips")
    my_ids = jnp.stack([me, 2 * my_x + my_y]).astype(jnp.int32)
    gsh = _rs_final_sum(gbig, sib, far, my_ids, name="rs_final_sum")
    g_big = [gsh[offs[k]:offs[k + 1]].reshape(big[k].shape) for k in range(len(big))]

    packed_small = jnp.concatenate(
        [jnp.concatenate(dmod0, axis=1).reshape(-1, LANES), jnp.concatenate(dmod1, axis=1).reshape(-1, LANES),
         dnmix0.reshape(-1, LANES), dnmix1.reshape(-1, LANES), dnmlp0.reshape(-1, LANES), dnmlp1.reshape(-1, LANES),
         hgacc[0:1].reshape(-1, LANES), hgacc[1:2].reshape(-1, LANES),
         qkacc[0:1].reshape(-1, LANES), qkacc[1:2].reshape(-1, LANES),
         dconv[0:3].reshape(-1, LANES), jnp.zeros((PK_ROWS - PK_CONV - 12, LANES), F32)], axis=0)
    gath = _all_gather(packed_small, name="gather_small_grads", in_vmem=True).reshape(N_DEV, PK_ROWS, LANES)
    dmod_all = gath[:, PK_MOD:PK_NMIX].reshape(N_DEV, 2, 6 * d)
    dmod_my = lax.dynamic_slice(dmod_all, (0, 0, me * ncol), (N_DEV, 2, ncol))
    g_ada_w, gsum, g_lb, g_qk = _small_grads(gath, c_act[:, :, None], dmod_my, lb_logits.reshape(3, 4, LANES),
                                             name="small_grads")
    g_ada_b = gsum[PK_MOD:PK_NMIX].reshape(2, 6 * d)
    g_norm_mix = gsum[PK_NMIX:PK_NMLP].reshape(2, d)
    g_norm_mlp = gsum[PK_NMLP:PK_HGN].reshape(2, d)
    g_hg_norm = gsum[PK_HGN:PK_LB].reshape(1, HG_WIDTH)
    g_lb_logits = g_lb.reshape(3, HG_WIDTH)
    g_q_norm = g_qk[0:1, :SB_HEAD_DIM]
    g_k_norm = g_qk[1:2, :SB_HEAD_DIM]
    g_conv_w = lax.dynamic_slice(gsum[PK_CONV:PK_CONV + 12].reshape(3, CONV_DIM), (0, me * ncv), (3, ncv))[None]

    def flat2(a):
        return a.reshape(-1, a.shape[-1])

    grads = dict(ada_w=g_ada_w, ada_b=g_ada_b, norm_mix=g_norm_mix, norm_mlp=g_norm_mlp, w_in_ab=g_big[0],
                 conv_w=g_conv_w, hg_norm=g_hg_norm, lb_logits=g_lb_logits, w_out_ab=g_big[1], w_qkv=g_big[2],
                 q_norm=g_q_norm, k_norm=g_k_norm, w_out_c=g_big[3], mlp_w1=g_big[4], mlp_w2=g_big[5])
    weights = dict(ada_w=(ada_w, m_ada_w, v_ada_w), ada_b=(ada_b, m_ada_b, v_ada_b),
                   norm_mix=(norm_mix, m_norm_mix, v_norm_mix), norm_mlp=(norm_mlp, m_norm_mlp, v_norm_mlp),
                   w_in_ab=(w_in_ab, m_w_in_ab, v_w_in_ab), conv_w=(conv_w, m_conv_w, v_conv_w),
                   hg_norm=(hg_norm, m_hg_norm, v_hg_norm), lb_logits=(lb_logits, m_lb_logits, v_lb_logits),
                   w_out_ab=(w_out_ab, m_w_out_ab, v_w_out_ab), w_qkv=(w_qkv, m_w_qkv, v_w_qkv),
                   q_norm=(q_norm, m_q_norm, v_q_norm), k_norm=(k_norm, m_k_norm, v_k_norm),
                   w_out_c=(w_out_c, m_w_out_c, v_w_out_c), mlp_w1=(mlp_w1, m_mlp_w1, v_mlp_w1),
                   mlp_w2=(mlp_w2, m_mlp_w2, v_mlp_w2))
    names = list(weights)
    small_names = ["ada_b", "norm_mix", "norm_mlp", "conv_w", "hg_norm", "lb_logits", "q_norm", "k_norm"]
    upd = {}
    small_items = []
    for n in small_names:
        wv, mv, vv = weights[n]
        small_items.append((flat2(wv), flat2(grads[n]), flat2(mv), flat2(vv)))
    for n, res in zip(small_names, _adamw_small(small_items, name="adamw_small")):
        upd[n] = tuple(r.reshape(weights[n][0].shape) for r in res)
    for n in names:
        if n in small_names:
            continue
        wv, mv, vv = weights[n]
        res = _adamw(flat2(wv), flat2(grads[n]), flat2(mv), flat2(vv), name=f"adamw_{n}")
        upd[n] = tuple(r.reshape(wv.shape) for r in res)

    return (loss, grad_x[None], *[grads[n].reshape(weights[n][0].shape) for n in names],
            *[upd[n][0] for n in names], *[upd[n][1] for n in names], *[upd[n][2] for n in names])
```

```python
import functools

import jax
import jax.numpy as jnp
from jax import lax
from jax.experimental import pallas as pl
from jax.experimental.pallas import tpu as pltpu

F32 = jnp.float32
BF16 = jnp.bfloat16
EPS = 1e-6
N_DEV = 8
MESH_AXES = ("x", "y", "c")

D_MODEL = 1024
CONV_DIM = 512
HG_HEADS = 4
HG_DK = 128
HG_WIDTH = 512
CHUNK = 64
HG_TILE = 128
HG_SUB = 16
HG_EXP_CLAMP = 60.0
SB_HEAD_DIM = 64
SB_SCALE = SB_HEAD_DIM ** -0.5
LOG2E = 1.4426950408889634
LN2 = 0.6931471805599453
SB_TQ = 512
SB_TK = 256
SB_DEAD = 150.0
D_FF = 4096
AB_IN = 3584

ADAM_LR = 0.001
ADAM_B1 = 0.9
ADAM_B2 = 0.999
ADAM_EPS = 1e-08
ADAM_WD = 0.01
ADAM_STEP = 10

VMEM_LIMIT = 48 * 1024 * 1024
LANES = 128


def _cparams(n_grid):
    return pltpu.CompilerParams(dimension_semantics=("arbitrary",) * n_grid, vmem_limit_bytes=VMEM_LIMIT)


def _nt(a, b):
    return lax.dot_general(a, b, (((1,), (1,)), ((), ())), preferred_element_type=F32)


def _tn(a, b):
    return lax.dot_general(a, b, (((0,), (0,)), ((), ())), preferred_element_type=F32)


def _nn(a, b):
    return jnp.dot(a, b, preferred_element_type=F32)


def _split3(x):
    hi = x.astype(BF16)
    r1 = x - hi.astype(F32)
    mid = r1.astype(BF16)
    lo = (r1 - mid.astype(F32)).astype(BF16)
    return hi, mid, lo


def _exact_left(m01, x):
    hi, mid, lo = _split3(x)
    return _nn(m01, hi) + _nn(m01, mid) + _nn(m01, lo)


def _exact_right(x, m01):
    hi, mid, lo = _split3(x)
    return _nn(hi, m01) + _nn(mid, m01) + _nn(lo, m01)


def _exact_right2(x, m01):
    hi = x.astype(BF16)
    lo = (x - hi.astype(F32)).astype(BF16)
    return _nn(hi, m01) + _nn(lo, m01)


def _sp(x):
    hi = x.astype(BF16)
    return hi, (x - hi.astype(F32)).astype(BF16)


def _dot3(fn, a, b):
    return fn(a[0], b[0]) + fn(a[0], b[1]) + fn(a[1], b[0])


def _tile(pref, n):
    t = min(pref, n)
    assert n % t == 0, (pref, n)
    return t


def _tile_rows(pref, n):
    for t in range(min(pref, n) - min(pref, n) % 16, 0, -16):
        if n % t == 0:
            return t
    raise ValueError((pref, n))


def _tile_lanes(pref, n):
    if n <= pref:
        return n
    for t in range(pref - pref % LANES, 0, -LANES):
        if n % t == 0:
            return t
    raise ValueError((pref, n))


def _all_gather(x, *, name, in_vmem):
    m_per, n = x.shape

    def body(x_ref, out_ref, send_sems, recv_sems, local_sem):
        mx, my, mc = lax.axis_index("x"), lax.axis_index("y"), lax.axis_index("c")
        me, sibling = (mx, my, mc), (mx, my, 1 - mc)
        chips = [(1 - mx, my), (mx, 1 - my), (1 - mx, 1 - my)]

        def rows(px, py, pc):
            return out_ref.at[pl.ds((4 * px + 2 * py + pc) * m_per, m_per), :]

        def copy(k, block, to, src=None):
            return pltpu.make_async_remote_copy(
                src_ref=rows(*block) if src is None else src, dst_ref=rows(*block),
                send_sem=send_sems.at[k], recv_sem=recv_sems.at[k],
                device_id=to, device_id_type=pl.DeviceIdType.MESH)

        mine = pltpu.make_async_copy(x_ref, rows(*me), local_sem)
        mine.start()
        first = [copy(0, me, sibling, src=x_ref)]
        first += [copy(1 + j, me, (*chip, mc), src=x_ref) for j, chip in enumerate(chips)]
        for cp in first:
            cp.start()
        passed = [copy(4 + j, (*chip, mc), sibling) for j, chip in enumerate(chips)]
        for j, chip in enumerate(chips):
            copy(1 + j, (*chip, mc), me).wait_recv()
            passed[j].start()
        copy(0, sibling, me).wait_recv()
        for j, chip in enumerate(chips):
            copy(4 + j, (*chip, 1 - mc), me).wait_recv()
        for cp in first + passed:
            cp.wait_send()
        mine.wait()

    space = pltpu.VMEM if in_vmem else pl.ANY
    return pl.pallas_call(
        body, name=name,
        out_shape=jax.ShapeDtypeStruct((N_DEV * m_per, n), x.dtype),
        in_specs=[pl.BlockSpec(memory_space=space)],
        out_specs=pl.BlockSpec(memory_space=space),
        scratch_shapes=[pltpu.SemaphoreType.DMA((7,)), pltpu.SemaphoreType.DMA((7,)), pltpu.SemaphoreType.DMA],
    )(x)


def _all_gather_many(xs, *, name):
    n = len(xs)

    def body(*refs):
        x_refs, out_refs = refs[:n], refs[n:2 * n]
        send_sems, recv_sems, local_sems = refs[2 * n:]
        mx, my, mc = lax.axis_index("x"), lax.axis_index("y"), lax.axis_index("c")
        me, sibling = (mx, my, mc), (mx, my, 1 - mc)
        chips = [(1 - mx, my), (mx, 1 - my), (1 - mx, 1 - my)]

        def slot(t, px, py, pc):
            return out_refs[t].at[4 * px + 2 * py + pc]

        def copy(t, k, block, to, src=None):
            return pltpu.make_async_remote_copy(
                src_ref=slot(t, *block) if src is None else src, dst_ref=slot(t, *block),
                send_sem=send_sems.at[7 * t + k], recv_sem=recv_sems.at[7 * t + k],
                device_id=to, device_id_type=pl.DeviceIdType.MESH)

        mine = [pltpu.make_async_copy(x_refs[t], slot(t, *me), local_sems.at[t]) for t in range(n)]
        for cp in mine:
            cp.start()
        first = []
        for t in range(n):
            first.append(copy(t, 0, me, sibling, src=x_refs[t]))
            first += [copy(t, 1 + j, me, (*chip, mc), src=x_refs[t]) for j, chip in enumerate(chips)]
        for cp in first:
            cp.start()
        passed = []
        for j, chip in enumerate(chips):
            for t in range(n):
                copy(t, 1 + j, (*chip, mc), me).wait_recv()
                passed.append(copy(t, 4 + j, (*chip, mc), sibling))
                passed[-1].start()
        for t in range(n):
            copy(t, 0, sibling, me).wait_recv()
            for j, chip in enumerate(chips):
                copy(t, 4 + j, (*chip, 1 - mc), me).wait_recv()
        for cp in first + passed:
            cp.wait_send()
        for cp in mine:
            cp.wait()

    hbm = pl.BlockSpec(memory_space=pl.ANY)
    return pl.pallas_call(
        body, name=name,
        out_shape=tuple(jax.ShapeDtypeStruct((N_DEV,) + x.shape, x.dtype) for x in xs),
        in_specs=[hbm] * n, out_specs=tuple([hbm] * n),
        scratch_shapes=[pltpu.SemaphoreType.DMA((7 * n,)), pltpu.SemaphoreType.DMA((7 * n,)),
                        pltpu.SemaphoreType.DMA((n,))],
    )(*xs)


def _rs_sibling_exchange(gs, *, name):
    n = len(gs)

    def body(*refs):
        g_refs, out_refs, send_sems, recv_sems = refs[:n], refs[n:2 * n], refs[2 * n], refs[2 * n + 1]
        mx, my, mc = lax.axis_index("x"), lax.axis_index("y"), lax.axis_index("c")
        copies = []
        for t in range(n):
            for q in range(4):
                copies.append(pltpu.make_async_remote_copy(
                    src_ref=g_refs[t].at[2 * q + (1 - mc)], dst_ref=out_refs[t].at[q],
                    send_sem=send_sems.at[4 * t + q], recv_sem=recv_sems.at[4 * t + q],
                    device_id=(mx, my, 1 - mc), device_id_type=pl.DeviceIdType.MESH))
        for cp in copies:
            cp.start()
        for cp in copies:
            cp.wait_recv()
        for cp in copies:
            cp.wait_send()

    hbm = pl.BlockSpec(memory_space=pl.ANY)
    return pl.pallas_call(
        body, name=name,
        out_shape=tuple(jax.ShapeDtypeStruct((4,) + g.shape[1:], g.dtype) for g in gs),
        in_specs=[hbm] * n, out_specs=tuple([hbm] * n),
        scratch_shapes=[pltpu.SemaphoreType.DMA((4 * n,)), pltpu.SemaphoreType.DMA((4 * n,))],
    )(*gs)


def _rs_chip_exchange(ts, *, name):
    n = len(ts)

    def body(*refs):
        t_refs, out_refs, send_sems, recv_sems = refs[:n], refs[n:2 * n], refs[2 * n], refs[2 * n + 1]
        mx, my, mc = lax.axis_index("x"), lax.axis_index("y"), lax.axis_index("c")
        chips = [(1 - mx, my), (mx, 1 - my), (1 - mx, 1 - my)]
        copies = []
        for t in range(n):
            for k, (px, py) in enumerate(chips):
                copies.append(pltpu.make_async_remote_copy(
                    src_ref=t_refs[t].at[k], dst_ref=out_refs[t].at[k],
                    send_sem=send_sems.at[3 * t + k], recv_sem=recv_sems.at[3 * t + k],
                    device_id=(px, py, mc), device_id_type=pl.DeviceIdType.MESH))
        for cp in copies:
            cp.start()
        for cp in copies:
            cp.wait_recv()
        for cp in copies:
            cp.wait_send()

    hbm = pl.BlockSpec(memory_space=pl.ANY)
    return pl.pallas_call(
        body, name=name,
        out_shape=tuple(jax.ShapeDtypeStruct(t.shape, t.dtype) for t in ts),
        in_specs=[hbm] * n, out_specs=tuple([hbm] * n),
        scratch_shapes=[pltpu.SemaphoreType.DMA((3 * n,)), pltpu.SemaphoreType.DMA((3 * n,))],
    )(*ts)


def _rs_pair_sum(gs, p1s, blk_ids, *, name, tr=256):
    n = len(gs)
    _, r, ncol = gs[0].shape
    tr = _tile_rows(tr, r)

    def body(id_ref, *refs):
        for t in range(n):
            refs[2 * n + t][...] = (refs[t][...] + refs[n + t][...]).astype(BF16)

    blk = lambda off: pl.BlockSpec((None, tr, ncol), lambda k, i, ids: (ids[off + k], i, 0))
    out = pl.BlockSpec((None, tr, ncol), lambda k, i, ids: (k, i, 0))
    return pl.pallas_call(
        body, name=name,
        out_shape=tuple(jax.ShapeDtypeStruct((3, r, ncol), BF16) for _ in gs),
        grid_spec=pltpu.PrefetchScalarGridSpec(
            num_scalar_prefetch=1, grid=(3, r // tr),
            in_specs=[blk(0)] * n + [blk(3)] * n, out_specs=tuple([out] * n)),
        compiler_params=_cparams(2),
    )(blk_ids, *gs, *p1s)


def _rs_final_sum(gs, p1s, p3s, my_ids, *, name, tr=256):
    n = len(gs)
    _, r, ncol = gs[0].shape
    tr = _tile_rows(tr, r)

    def body(id_ref, *refs):
        for t in range(n):
            g_ref, s_ref = refs[t], refs[n + t]
            a_ref, b_ref, c_ref = refs[2 * n + 3 * t:2 * n + 3 * t + 3]
            own = g_ref[...] + s_ref[...]
            refs[5 * n + t][...] = (((own + a_ref[...].astype(F32)) + b_ref[...].astype(F32))
                                    + c_ref[...].astype(F32))

    sel = lambda which: pl.BlockSpec((None, tr, ncol), lambda i, ids: (ids[which], i, 0))
    fix = lambda k: pl.BlockSpec((None, tr, ncol), lambda i, ids: (k, i, 0))
    p3_specs, p3_args = [], []
    for p3 in p3s:
        p3_specs += [fix(0), fix(1), fix(2)]
        p3_args += [p3, p3, p3]
    return pl.pallas_call(
        body, name=name,
        out_shape=tuple(jax.ShapeDtypeStruct((r, ncol), F32) for _ in gs),
        grid_spec=pltpu.PrefetchScalarGridSpec(
            num_scalar_prefetch=1, grid=(r // tr,),
            in_specs=[sel(0)] * n + [sel(1)] * n + p3_specs,
            out_specs=tuple([pl.BlockSpec((tr, ncol), lambda i, ids: (i, 0))] * n)),
        compiler_params=_cparams(1),
    )(my_ids, *gs, *p1s, *p3_args)


def _matmul(a, b, *, name, ta=False, tb=False, epi="plain", extras=(), out_dtype=F32, tm=1024, tn=1024, tk=1024,
            b_kind=None, layer=0):
    if ta:
        kdim, m = a.shape
    else:
        m, kdim = a.shape
    if epi == "resgate":
        tn = min(tn, 512)
    pair = 1
    if b_kind is None:
        if tb:
            n, kb = b.shape
        else:
            kb, n = b.shape
        tn, tk = _tile_lanes(tn, n), _tile_lanes(tk, kb)
        b_spec = (pl.BlockSpec((tn, tk), lambda i, j, k: (j, k)) if tb
                  else pl.BlockSpec((tk, tn), lambda i, j, k: (k, j)))
    elif b_kind == "colblk":
        assert not ta
        _, _, kw, nsh = b.shape
        if tb:
            kb, n, pair = N_DEV * nsh, kw, 2
            tn, tk = _tile_lanes(tn, n), pair * nsh
            b_spec = pl.BlockSpec((pair, None, tn, nsh), lambda i, j, k: (k, layer, j, 0))
        else:
            kb, n = kw, N_DEV * nsh
            tn, tk = nsh, _tile_lanes(tk, kb)
            b_spec = pl.BlockSpec((None, None, tk, nsh), lambda i, j, k: (j, layer, k, 0))
    elif b_kind == "rowblk":
        assert not ta
        _, _, r, ncol = b.shape
        pair = 2
        if tb:
            kb, n = ncol, N_DEV * r
            tn, tk = pair * r, _tile_lanes(tk, kb)
            b_spec = pl.BlockSpec((pair, None, r, tk), lambda i, j, k: (j, layer, 0, k))
        else:
            kb, n = N_DEV * r, ncol
            tn, tk = _tile_lanes(tn, n), pair * r
            b_spec = pl.BlockSpec((pair, None, r, tn), lambda i, j, k: (k, layer, 0, j))
    else:
        raise ValueError(b_kind)
    assert kdim == kb, (a.shape, b.shape)
    tm = _tile_lanes(tm, m)
    nk = kdim // tk
    a_spec = pl.BlockSpec((tk, tm), lambda i, j, k: (k, i)) if ta else pl.BlockSpec((tm, tk), lambda i, j, k: (i, k))
    dims = (((0 if ta else 1,), (1 if tb else 0,)), ((), ()))
    mn_spec = pl.BlockSpec((tm, tn), lambda i, j, k: (i, j))
    row_spec = pl.BlockSpec((1, tn), lambda i, j, k: (0, j))
    if epi == "resgate":
        extra_specs = [mn_spec, row_spec]
        out_shape = (jax.ShapeDtypeStruct((m, n), F32), jax.ShapeDtypeStruct((m, n), F32))
        out_specs = (mn_spec, mn_spec)
    elif epi == "dact":
        extra_specs = [mn_spec]
        out_shape = jax.ShapeDtypeStruct((m, n), out_dtype)
        out_specs = mn_spec
    else:
        extra_specs = []
        out_shape = jax.ShapeDtypeStruct((m, n), out_dtype)
        out_specs = mn_spec
    n_extra = len(extra_specs)

    def body(a_ref, b_ref, *rest):
        ex = rest[:n_extra]
        outs = rest[n_extra:n_extra + n_out]
        k = pl.program_id(2)

        def prod():
            av = a_ref[...].astype(BF16)
            if b_kind == "rowblk":
                bv = b_ref[...].astype(BF16)
                return lax.dot_general(av, bv.reshape(bv.shape[0] * bv.shape[1], bv.shape[2]), dims,
                                       preferred_element_type=F32)
            if b_kind == "colblk" and tb:
                nsh = b_ref.shape[-1]
                return sum(lax.dot_general(av[:, p * nsh:(p + 1) * nsh], b_ref[p].astype(BF16), dims,
                                           preferred_element_type=F32) for p in range(pair))
            return lax.dot_general(av, b_ref[...].astype(BF16), dims, preferred_element_type=F32)

        def finish(r):
            if epi == "plain":
                outs[0][...] = r.astype(outs[0].dtype)
            elif epi == "resgate":
                outs[0][...] = r
                outs[1][...] = ex[0][...] + ex[1][...] * r
            elif epi == "relu2":
                p = jnp.maximum(r, 0.0)
                outs[0][...] = (p * p).astype(outs[0].dtype)
            elif epi == "dact":
                outs[0][...] = (r * (2.0 * jnp.sqrt(ex[0][...].astype(F32)))).astype(outs[0].dtype)

        if nk == 1:
            finish(prod())
        else:
            acc = rest[-1]

            @pl.when(k == 0)
            def _():
                acc[...] = prod()

            if nk > 2:
                @pl.when(jnp.logical_and(k > 0, k < nk - 1))
                def _():
                    acc[...] += prod()

            @pl.when(k == nk - 1)
            def _():
                finish(acc[...] + prod())

    n_out = 2 if epi == "resgate" else 1
    return pl.pallas_call(
        body, name=name, out_shape=out_shape, grid=(m // tm, n // tn, nk),
        in_specs=[a_spec, b_spec] + extra_specs, out_specs=out_specs,
        scratch_shapes=[pltpu.VMEM((tm, tn), F32)] if nk > 1 else [],
        compiler_params=_cparams(3),
    )(a, b, *extras)


def _norm_mod(x, g, scale, shift, *, name, tm=512):
    s, d = x.shape
    tm = _tile(tm, s)

    def body(x_ref, g_ref, sc_ref, sh_ref, h_ref):
        xv = x_ref[...]
        r = lax.rsqrt(jnp.mean(xv * xv, axis=-1, keepdims=True) + EPS)
        h_ref[...] = (((xv * r) * g_ref[...]) * (1.0 + sc_ref[...]) + sh_ref[...]).astype(BF16)

    row = pl.BlockSpec((1, d), lambda i: (0, 0))
    return pl.pallas_call(
        body, name=name, out_shape=jax.ShapeDtypeStruct((s, d), BF16), grid=(s // tm,),
        in_specs=[pl.BlockSpec((tm, d), lambda i: (i, 0)), row, row, row],
        out_specs=pl.BlockSpec((tm, d), lambda i: (i, 0)),
        compiler_params=_cparams(1),
    )(x, g, scale, shift)


def _norm_mod_bwd(x, dh, dres, g, scale, *, name, tm=512):
    s, d = x.shape
    tm = _tile(tm, s)

    def body(x_ref, dh_ref, dr_ref, g_ref, sc_ref, dx_ref, acc_ref):
        i = pl.program_id(0)

        @pl.when(i == 0)
        def _():
            acc_ref[...] = jnp.zeros_like(acc_ref)

        xv = x_ref[...]
        dhv = dh_ref[...]
        gv = g_ref[...]
        one_sc = 1.0 + sc_ref[...]
        r = lax.rsqrt(jnp.mean(xv * xv, axis=-1, keepdims=True) + EPS)
        xn = xv * r
        dxn = dhv * (gv * one_sc)
        dx_ref[...] = dr_ref[...] + r * (dxn - xn * jnp.mean(dxn * xn, axis=-1, keepdims=True))
        dhxn = dhv * xn
        acc_ref[0:1, :] += jnp.sum(dhv, axis=0, keepdims=True)
        acc_ref[1:2, :] += jnp.sum(dhxn * gv, axis=0, keepdims=True)
        acc_ref[2:3, :] += jnp.sum(dhxn * one_sc, axis=0, keepdims=True)

    row = pl.BlockSpec((1, d), lambda i: (0, 0))
    blk = pl.BlockSpec((tm, d), lambda i: (i, 0))
    return pl.pallas_call(
        body, name=name,
        out_shape=(jax.ShapeDtypeStruct((s, d), F32), jax.ShapeDtypeStruct((8, d), F32)),
        grid=(s // tm,), in_specs=[blk, blk, blk, row, row],
        out_specs=(blk, pl.BlockSpec((8, d), lambda i: (0, 0))),
        compiler_params=_cparams(1),
    )(x, dh, dres, g, scale)


def _gate_bwd(dx, z, gate, *, name, tm=512):
    s, d = dx.shape
    tm = _tile(tm, s)

    def body(dx_ref, z_ref, g_ref, dz_ref, acc_ref):
        i = pl.program_id(0)

        @pl.when(i == 0)
        def _():
            acc_ref[...] = jnp.zeros_like(acc_ref)

        dxv = dx_ref[...]
        dz_ref[...] = (dxv * g_ref[...]).astype(BF16)
        acc_ref[0:1, :] += jnp.sum(dxv * z_ref[...], axis=0, keepdims=True)

    blk = pl.BlockSpec((tm, d), lambda i: (i, 0))
    return pl.pallas_call(
        body, name=name,
        out_shape=(jax.ShapeDtypeStruct((s, d), BF16), jax.ShapeDtypeStruct((8, d), F32)),
        grid=(s // tm,), in_specs=[blk, blk, pl.BlockSpec((1, d), lambda i: (0, 0))],
        out_specs=(blk, pl.BlockSpec((8, d), lambda i: (0, 0))),
        compiler_params=_cparams(1),
    )(dx, z, gate)


def _loss_grad(xf, target, *, name, tm=512):
    s, d = xf.shape
    tm = _tile(tm, s)
    nt = s // tm

    def body(x_ref, t_ref, dx_ref, loss_ref, acc_ref):
        i = pl.program_id(0)

        @pl.when(i == 0)
        def _():
            acc_ref[...] = jnp.zeros_like(acc_ref)

        e = x_ref[...] - t_ref[...]
        dx_ref[...] = e * (1.0 / d)
        acc_ref[...] += jnp.sum(e * e, axis=0, keepdims=True)

        @pl.when(i == nt - 1)
        def _():
            loss_ref[...] = (0.5 / d) * jnp.sum(acc_ref[...], axis=1, keepdims=True)

    blk = pl.BlockSpec((tm, d), lambda i: (i, 0))
    return pl.pallas_call(
        body, name=name,
        out_shape=(jax.ShapeDtypeStruct((s, d), F32), jax.ShapeDtypeStruct((1, 1), F32)),
        grid=(nt,), in_specs=[blk, blk],
        out_specs=(blk, pl.BlockSpec((1, 1), lambda i: (0, 0))),
        scratch_shapes=[pltpu.VMEM((1, d), F32)],
        compiler_params=_cparams(1),
    )(xf, target)


def _shift_down(p, prev, k):
    tm = p.shape[0]
    row = lax.broadcasted_iota(jnp.int32, p.shape, 0)
    out = pltpu.roll(p, k, 0)
    for j in range(k):
        out = jnp.where(row == j, prev[8 - k + j:8 - k + j + 1, :], out)
    return out


def _shift_up(p, nxt, k):
    tm = p.shape[0]
    row = lax.broadcasted_iota(jnp.int32, p.shape, 0)
    out = pltpu.roll(p, tm - k, 0)
    for j in range(k):
        out = jnp.where(row == tm - k + j, nxt[j:j + 1, :], out)
    return out


def _conv_fwd(u, w, *, name, tm=512):
    s = u.shape[0]
    tm = _tile(tm, s)
    c = CONV_DIM

    def body(ab_ref, ac_ref, ah_ref, w_ref, y_ref, carry_ref):
        i = pl.program_id(0)

        @pl.when(i == 0)
        def _():
            carry_ref[...] = jnp.zeros_like(carry_ref)

        p = ac_ref[...] * ah_ref[...]
        prev = carry_ref[...]
        wv = w_ref[...]
        conv = wv[2:3, :] * p + wv[1:2, :] * _shift_down(p, prev, 1) + wv[0:1, :] * _shift_down(p, prev, 2)
        y_ref[...] = (ab_ref[...] * conv).astype(BF16)
        carry_ref[...] = p[tm - 8:tm, :]

    return pl.pallas_call(
        body, name=name, out_shape=jax.ShapeDtypeStruct((s, c), BF16), grid=(s // tm,),
        in_specs=[pl.BlockSpec((tm, c), lambda i: (i, 0)), pl.BlockSpec((tm, c), lambda i: (i, 1)),
                  pl.BlockSpec((tm, c), lambda i: (i, 2)), pl.BlockSpec((3, c), lambda i: (0, 0))],
        out_specs=pl.BlockSpec((tm, c), lambda i: (i, 0)),
        scratch_shapes=[pltpu.VMEM((8, c), F32)],
        compiler_params=_cparams(1),
    )(u, u, u, w)


def _conv_bwd(u, dy, w, *, name, tm=512):
    s = u.shape[0]
    tm = _tile(tm, s)
    nt = s // tm
    c = CONV_DIM
    hb = tm // 8

    def body(ab_ref, ac_ref, ah_ref, hc_ref, hh_ref, dy_ref, w_ref, du_ref, dw_ref, carry_ref):
        i = pl.program_id(0)

        @pl.when(i == 0)
        def _():
            carry_ref[...] = jnp.zeros_like(carry_ref)
            dw_ref[...] = jnp.zeros_like(dw_ref)

        first_tile = (nt - 1 - i) == 0
        ab, ac, ah = ab_ref[...], ac_ref[...], ah_ref[...]
        p = ac * ah
        prev = jnp.where(first_tile, 0.0, hc_ref[...] * hh_ref[...])
        wv = w_ref[...]
        p1 = _shift_down(p, prev, 1)
        p2 = _shift_down(p, prev, 2)
        conv = wv[2:3, :] * p + wv[1:2, :] * p1 + wv[0:1, :] * p2
        dyv = dy_ref[...]
        dconv = dyv * ab
        nxt = carry_ref[...]
        dp = wv[2:3, :] * dconv + wv[1:2, :] * _shift_up(dconv, nxt, 1) + wv[0:1, :] * _shift_up(dconv, nxt, 2)
        du_ref[:, 0:c] = (dyv * conv).astype(BF16)
        du_ref[:, c:2 * c] = (dp * ah).astype(BF16)
        du_ref[:, 2 * c:3 * c] = (dp * ac).astype(BF16)
        dw_ref[0:1, :] += jnp.sum(dconv * p2, axis=0, keepdims=True)
        dw_ref[1:2, :] += jnp.sum(dconv * p1, axis=0, keepdims=True)
        dw_ref[2:3, :] += jnp.sum(dconv * p, axis=0, keepdims=True)
        carry_ref[...] = dconv[0:8, :]

    rev = lambda i: nt - 1 - i
    halo = lambda i: jnp.maximum(rev(i) * hb - 1, 0)
    return pl.pallas_call(
        body, name=name,
        out_shape=(jax.ShapeDtypeStruct((s, 3 * c), BF16), jax.ShapeDtypeStruct((8, c), F32)),
        grid=(nt,),
        in_specs=[pl.BlockSpec((tm, c), lambda i: (rev(i), 0)), pl.BlockSpec((tm, c), lambda i: (rev(i), 1)),
                  pl.BlockSpec((tm, c), lambda i: (rev(i), 2)),
                  pl.BlockSpec((8, c), lambda i: (halo(i), 1)), pl.BlockSpec((8, c), lambda i: (halo(i), 2)),
                  pl.BlockSpec((tm, c), lambda i: (rev(i), 0)), pl.BlockSpec((3, c), lambda i: (0, 0))],
        out_specs=(pl.BlockSpec((tm, 3 * c), lambda i: (rev(i), 0)), pl.BlockSpec((8, c), lambda i: (0, 0))),
        scratch_shapes=[pltpu.VMEM((8, c), F32)],
        compiler_params=_cparams(1),
    )(u, u, u, u, u, dy, w)


def _lower_bound(lbl):
    m = jnp.max(lbl, axis=0, keepdims=True)
    e = jnp.exp(lbl - m)
    return e[0:1, :] / jnp.sum(e, axis=0, keepdims=True)


def _hg_masks():
    t = HG_TILE
    row = lax.broadcasted_iota(jnp.int32, (t, t), 0)
    col = lax.broadcasted_iota(jnp.int32, (t, t), 1)
    same = (row >= CHUNK) == (col >= CHUNK)
    lower = same & (col <= row)
    upper = same & (row <= col)
    return row, col, lower, upper


def _hg_gates(hf, lb):
    sig = jax.nn.sigmoid(hf)
    f = lb + (1.0 - lb) * sig
    return sig, f, jnp.log(f), 1.0 - f


def _hg_refs(b_ref, hs):
    refs = []
    for i in range(HG_TILE // HG_SUB):
        if (i * HG_SUB) % CHUNK == 0:
            refs.append(jnp.zeros((1, HG_DK), F32))
        else:
            refs.append(b_ref[i * HG_SUB - 1:i * HG_SUB, hs])
    return refs


def _hgrn_fwd(u, lbl, gn, *, name):
    s = u.shape[0]
    t = HG_TILE
    nt = s // t
    nsub = t // HG_SUB
    w = HG_WIDTH

    def body(hq_ref, hf_ref, hi_ref, hg_ref, lbl_ref, gn_ref, y_ref, o_ref, sall_ref, st_ref, b_ref):
        i = pl.program_id(0)

        @pl.when(i == 0)
        def _():
            st_ref[...] = jnp.zeros_like(st_ref)

        lb = _lower_bound(lbl_ref[...])
        _, _, g, kin = _hg_gates(hf_ref[...], lb)
        _, _, lower, _ = _hg_masks()
        b_ref[...] = _exact_left(lower.astype(BF16), g)

        for h in range(HG_HEADS):
            hs = slice(h * HG_DK, (h + 1) * HG_DK)
            bh = b_ref[:, hs]
            qh = hq_ref[:, hs]
            kh = kin[:, hs]
            vh = hi_ref[:, hs]
            vsp = _sp(vh)
            refs = _hg_refs(b_ref, hs)
            rmat = jnp.concatenate([jnp.broadcast_to(r, (HG_SUB, HG_DK)) for r in refs], axis=0)
            qt = qh * jnp.exp(bh - rmat)
            prow = []
            for j in range(nsub):
                kj = kh * jnp.exp(jnp.minimum(refs[j] - bh, HG_EXP_CLAMP))
                prow.append(_dot3(_nt, _sp(qt[j * HG_SUB:(j + 1) * HG_SUB]), _sp(kj)))
            p = jnp.where(lower, jnp.concatenate(prow, axis=0), 0.0)
            intra = _dot3(_nn, _sp(p), vsp)
            o_parts = []
            for c in range(t // CHUNK):
                rs = slice(c * CHUNK, (c + 1) * CHUNK)
                st0 = st_ref[hs, :]
                sall_ref[c * w + h * HG_DK:c * w + (h + 1) * HG_DK, :] = st0
                bl = b_ref[c * CHUNK + CHUNK - 1:c * CHUNK + CHUNK, hs]
                qf = qh[rs] * jnp.exp(bh[rs])
                o_parts.append(_dot3(_nt, _sp(qf), _sp(st0)) + intra[rs])
                khat = kh[rs] * jnp.exp(bl - bh[rs])
                st_ref[hs, :] = st0 * jnp.exp(bl) + _dot3(_tn, _sp(vh[rs]), _sp(khat))
            o = jnp.concatenate(o_parts, axis=0)
            o_ref[:, hs] = o
            r = lax.rsqrt(jnp.mean(o * o, axis=-1, keepdims=True) + EPS)
            hg = hg_ref[:, hs]
            y_ref[:, hs] = (((o * r) * gn_ref[:, hs]) * (hg * jax.nn.sigmoid(hg))).astype(BF16)

    blk = lambda j: pl.BlockSpec((t, w), lambda i, j=j: (i, j))
    srows = (t // CHUNK) * w
    return pl.pallas_call(
        body, name=name,
        out_shape=(jax.ShapeDtypeStruct((s, w), BF16), jax.ShapeDtypeStruct((s, w), F32),
                   jax.ShapeDtypeStruct((nt * srows, HG_DK), F32)),
        grid=(nt,),
        in_specs=[blk(3), blk(4), blk(5), blk(6), pl.BlockSpec((3, w), lambda i: (0, 0)),
                  pl.BlockSpec((1, w), lambda i: (0, 0))],
        out_specs=(pl.BlockSpec((t, w), lambda i: (i, 0)), pl.BlockSpec((t, w), lambda i: (i, 0)),
                   pl.BlockSpec((srows, HG_DK), lambda i: (i, 0))),
        scratch_shapes=[pltpu.VMEM((w, HG_DK), F32), pltpu.VMEM((t, w), F32)],
        compiler_params=_cparams(1),
    )(u, u, u, u, lbl, gn)


def _hgrn_bwd(u, o_all, sall, dy, lbl, gn, *, name):
    s = u.shape[0]
    t = HG_TILE
    nt = s // t
    nsub = t // HG_SUB
    w = HG_WIDTH
    nch = t // CHUNK

    def body(hq_ref, hf_ref, hi_ref, hg_ref, o_ref, sall_ref, dy_ref, lbl_ref, gn_ref,
             du_ref, acc_ref, dst_ref, b_ref):
        i = pl.program_id(0)

        @pl.when(i == 0)
        def _():
            dst_ref[...] = jnp.zeros_like(dst_ref)
            acc_ref[...] = jnp.zeros_like(acc_ref)

        lb = _lower_bound(lbl_ref[...])
        sig, f, g, kin = _hg_gates(hf_ref[...], lb)
        row, col, lower, upper = _hg_masks()
        b_ref[...] = _exact_left(lower.astype(BF16), g)
        upper_bf = upper.astype(BF16)
        rowblk = [((row >= j * HG_SUB) & (row < (j + 1) * HG_SUB)) for j in range(nsub)]
        colblk = [((col >= j * HG_SUB) & (col < (j + 1) * HG_SUB)) for j in range(nsub)]
        row1 = lax.broadcasted_iota(jnp.int32, (t, HG_DK), 0)

        for h in range(HG_HEADS):
            hs = slice(h * HG_DK, (h + 1) * HG_DK)
            bh = b_ref[:, hs]
            qh = hq_ref[:, hs]
            kh = kin[:, hs]
            vh = hi_ref[:, hs]
            vsp = _sp(vh)
            hg = hg_ref[:, hs]
            gnh = gn_ref[:, hs]
            o = o_ref[:, hs]
            dyv = dy_ref[:, hs]
            sg = jax.nn.sigmoid(hg)
            r = lax.rsqrt(jnp.mean(o * o, axis=-1, keepdims=True) + EPS)
            ohat = o * r
            du_ref[:, 3 * w + h * HG_DK:3 * w + (h + 1) * HG_DK] = (
                dyv * (ohat * gnh) * (sg * (1.0 + hg * (1.0 - sg)))).astype(BF16)
            don = dyv * (hg * sg)
            acc_ref[0:1, hs] += jnp.sum(don * ohat, axis=0, keepdims=True)
            dohat = don * gnh
            do = r * (dohat - ohat * jnp.mean(dohat * ohat, axis=-1, keepdims=True))
            dosp = _sp(do)
            refs = _hg_refs(b_ref, hs)
            rmat = jnp.concatenate([jnp.broadcast_to(rr, (HG_SUB, HG_DK)) for rr in refs], axis=0)
            eq = jnp.exp(bh - rmat)
            qt = qh * eq
            qtsp = _sp(qt)
            dp = jnp.where(lower, _dot3(_nt, dosp, vsp), 0.0)
            dpt = jnp.where(upper, _dot3(_nt, vsp, dosp), 0.0)
            pt = jnp.zeros((t, t), F32)
            dk = jnp.zeros((t, HG_DK), F32)
            dq_rows = []
            for j in range(nsub):
                ek = jnp.exp(jnp.minimum(refs[j] - bh, HG_EXP_CLAMP))
                kjsp = _sp(kh * ek)
                pt = pt + _dot3(_nt, kjsp, _sp(jnp.where(rowblk[j], qt, 0.0)))
                dq_rows.append(_dot3(_nn, _sp(dp[j * HG_SUB:(j + 1) * HG_SUB]), kjsp))
                dk = dk + ek * _dot3(_nn, _sp(jnp.where(colblk[j], dpt, 0.0)), qtsp)
            pt = jnp.where(upper, pt, 0.0)
            dv = _dot3(_nn, _sp(pt), dosp)
            dq = jnp.concatenate(dq_rows, axis=0) * eq
            dq_c, dk_c, dv_c, ex_c = [None] * nch, [None] * nch, [None] * nch, [None] * nch
            for c in reversed(range(nch)):
                rs = slice(c * CHUNK, (c + 1) * CHUNK)
                st0 = sall_ref[c * w + h * HG_DK:c * w + (h + 1) * HG_DK, :]
                dst1 = dst_ref[hs, :]
                dst1sp = _sp(dst1)
                dosp_c = _sp(do[rs])
                bl = b_ref[c * CHUNK + CHUNK - 1:c * CHUNK + CHUNK, hs]
                e = jnp.exp(bh[rs])
                el = jnp.exp(bl)
                ekl = jnp.exp(bl - bh[rs])
                dq_c[c] = _dot3(_nn, dosp_c, _sp(st0)) * e
                khat = kh[rs] * ekl
                dv_c[c] = _dot3(_nt, _sp(khat), dst1sp)
                dkhat = _dot3(_nn, _sp(vh[rs]), dst1sp)
                dk_c[c] = dkhat * ekl
                ex_c[c] = (jnp.sum(dkhat * khat, axis=0, keepdims=True)
                           + el * jnp.sum(dst1 * st0, axis=0, keepdims=True))
                dst_ref[hs, :] = _dot3(_tn, dosp_c, _sp(qh[rs] * e)) + dst1 * el
            dq = dq + jnp.concatenate(dq_c, axis=0)
            dk = dk + jnp.concatenate(dk_c, axis=0)
            dv = dv + jnp.concatenate(dv_c, axis=0)
            db = qh * dq - kh * dk
            for c in range(nch):
                db = db + jnp.where(row1 == c * CHUNK + CHUNK - 1, ex_c[c], 0.0)
            dg = _exact_left(upper_bf, db)
            fh = f[:, hs]
            sgf = sig[:, hs]
            lbh = lb[:, hs]
            df = dg / fh - dk
            du_ref[:, hs] = dq.astype(BF16)
            du_ref[:, w + h * HG_DK:w + (h + 1) * HG_DK] = (df * (1.0 - lbh) * sgf * (1.0 - sgf)).astype(BF16)
            du_ref[:, 2 * w + h * HG_DK:2 * w + (h + 1) * HG_DK] = dv.astype(BF16)
            acc_ref[1:2, hs] += jnp.sum(df * (1.0 - sgf), axis=0, keepdims=True)

    rev = lambda i: nt - 1 - i
    blk = lambda j: pl.BlockSpec((t, w), lambda i, j=j: (rev(i), j))
    srows = nch * w
    return pl.pallas_call(
        body, name=name,
        out_shape=(jax.ShapeDtypeStruct((s, 4 * w), BF16), jax.ShapeDtypeStruct((8, w), F32)),
        grid=(nt,),
        in_specs=[blk(3), blk(4), blk(5), blk(6), pl.BlockSpec((t, w), lambda i: (rev(i), 0)),
                  pl.BlockSpec((srows, HG_DK), lambda i: (rev(i), 0)),
                  pl.BlockSpec((t, w), lambda i: (rev(i), 1)),
                  pl.BlockSpec((3, w), lambda i: (0, 0)), pl.BlockSpec((1, w), lambda i: (0, 0))],
        out_specs=(pl.BlockSpec((t, 4 * w), lambda i: (rev(i), 0)), pl.BlockSpec((8, w), lambda i: (0, 0))),
        scratch_shapes=[pltpu.VMEM((w, HG_DK), F32), pltpu.VMEM((t, w), F32)],
        compiler_params=_cparams(1),
    )(u, u, u, u, o_all, sall, dy, lbl, gn)


def _pair_matrix():
    row = lax.broadcasted_iota(jnp.int32, (LANES, LANES), 0)
    col = lax.broadcasted_iota(jnp.int32, (LANES, LANES), 1)
    return ((row >= SB_HEAD_DIM) == (col >= SB_HEAD_DIM)).astype(BF16)


def _qk_norm_fwd(qkv, qn, kn, *, name, tm=256):
    s = qkv.shape[0]
    d = D_MODEL
    tm = _tile(tm, s)

    def body(q_ref, k_ref, v_ref, qn_ref, kn_ref, qo_ref, ko_ref, vo_ref):
        bd = _pair_matrix()
        for src, gain, dst, fac in ((q_ref, qn_ref, qo_ref, SB_SCALE * LOG2E), (k_ref, kn_ref, ko_ref, None)):
            for grp in range(d // LANES):
                ls = slice(grp * LANES, (grp + 1) * LANES)
                xv = src[:, ls]
                ms = _exact_right(xv * xv, bd) * (1.0 / SB_HEAD_DIM)
                y = (xv * lax.rsqrt(ms + EPS)) * gain[:, ls]
                dst[:, ls] = (y if fac is None else y * fac).astype(BF16)
        vo_ref[...] = v_ref[...].astype(BF16)

    blk = lambda j: pl.BlockSpec((tm, d), lambda i, j=j: (i, j))
    row = pl.BlockSpec((1, d), lambda i: (0, 0))
    out = jax.ShapeDtypeStruct((s, d), BF16)
    return pl.pallas_call(
        body, name=name, out_shape=(out, out, out), grid=(s // tm,),
        in_specs=[blk(0), blk(1), blk(2), row, row],
        out_specs=(blk(0), blk(0), blk(0)),
        compiler_params=_cparams(1),
    )(qkv, qkv, qkv, qn, kn)


def _qk_norm_bwd(qkv, dqn, dkn, dv, qn, kn, *, name, tm=256):
    s = qkv.shape[0]
    d = D_MODEL
    tm = _tile(tm, s)

    def body(q_ref, k_ref, dq_ref, dk_ref, dv_ref, qn_ref, kn_ref, o_ref, acc_ref):
        i = pl.program_id(0)

        @pl.when(i == 0)
        def _():
            acc_ref[...] = jnp.zeros_like(acc_ref)

        bd = _pair_matrix()
        for idx, (src, dsrc, gain) in enumerate(((q_ref, dq_ref, qn_ref), (k_ref, dk_ref, kn_ref))):
            for grp in range(d // LANES):
                ls = slice(grp * LANES, (grp + 1) * LANES)
                xv = src[:, ls]
                dyv = dsrc[:, ls]
                r = lax.rsqrt(_exact_right(xv * xv, bd) * (1.0 / SB_HEAD_DIM) + EPS)
                xh = xv * r
                acc_ref[idx:idx + 1, ls] += jnp.sum(dyv * xh, axis=0, keepdims=True)
                dxh = dyv * gain[:, ls]
                mean = _exact_right(dxh * xh, bd) * (1.0 / SB_HEAD_DIM)
                o_ref[:, idx * d + grp * LANES:idx * d + (grp + 1) * LANES] = (r * (dxh - xh * mean)).astype(BF16)
        o_ref[:, 2 * d:3 * d] = dv_ref[...].astype(BF16)

    blk = lambda j: pl.BlockSpec((tm, d), lambda i, j=j: (i, j))
    row = pl.BlockSpec((1, d), lambda i: (0, 0))
    return pl.pallas_call(
        body, name=name,
        out_shape=(jax.ShapeDtypeStruct((s, 3 * d), BF16), jax.ShapeDtypeStruct((8, d), F32)),
        grid=(s // tm,),
        in_specs=[blk(0), blk(1), blk(0), blk(0), blk(0), row, row],
        out_specs=(pl.BlockSpec((tm, 3 * d), lambda i: (i, 0)), pl.BlockSpec((8, d), lambda i: (0, 0))),
        compiler_params=_cparams(1),
    )(qkv, qkv, dqn, dkn, dv, qn, kn)


def _sb_tile(qh, kb, suffix_ones, run, mask):
    z = _nt(qh, kb)
    neg_abs = lax.bitcast_convert_type(lax.bitcast_convert_type(z, jnp.uint32) | jnp.uint32(0x80000000), F32)
    l1m = -(jnp.maximum(z, 0.0) + jnp.log2(1.0 + jnp.exp2(neg_abs)))
    logb = z + l1m
    if mask is not None:
        l1m = jnp.where(mask, l1m, 0.0)
    later = _nn(l1m.astype(BF16), suffix_ones) + run
    wgt = jnp.exp2(logb + later)
    if mask is not None:
        wgt = jnp.where(mask, wgt, 0.0)
    return logb, l1m, wgt


def _suffix_ones(tk):
    row = lax.broadcasted_iota(jnp.int32, (tk, tk), 0)
    col = lax.broadcasted_iota(jnp.int32, (tk, tk), 1)
    return (row > col).astype(BF16)


def _sb_alive(runs):
    return jnp.max(jnp.maximum(runs[0], runs[1])) > -SB_DEAD


def _sb_mask(qi, j, tq, tk):
    qpos = qi * tq + lax.broadcasted_iota(jnp.int32, (tq, tk), 0)
    kpos = j * tk + lax.broadcasted_iota(jnp.int32, (tq, tk), 1)
    return kpos < qpos


def _sb_fwd(qn, kn, v, *, name):
    s, d = qn.shape
    tq, tk = _tile(SB_TQ, s), _tile(SB_TK, s)
    assert tk % tq == 0 or tq % tk == 0
    nq = s // tq

    def body(q_ref, k_ref, v_ref, o_ref, acc_ref):
        qi = pl.program_id(1)
        lane = lax.broadcasted_iota(jnp.int32, (tq, LANES), 1)
        first = lane < SB_HEAD_DIM
        q = q_ref[...]
        qh = [jnp.where(first, q, 0).astype(BF16), jnp.where(first, 0, q).astype(BF16)]
        ones = _suffix_ones(tk)
        acc_ref[...] = jnp.zeros_like(acc_ref)

        def tile(j, runs, masked):
            ks = pl.ds(pl.multiple_of(j * tk, tk), tk)
            kb = k_ref[ks, :]
            vb = v_ref[ks, :]
            mask = _sb_mask(qi, j, tq, tk) if masked else None
            new_runs = []
            for hh in range(2):
                _, l1m, wgt = _sb_tile(qh[hh], kb, ones, runs[hh], mask)
                acc_ref[hh] += _nn(wgt.astype(BF16), vb)
                new_runs.append(runs[hh] + jnp.sum(l1m, axis=1, keepdims=True))
            return tuple(new_runs)

        nfull = (qi * tq) // tk
        zero = jnp.zeros((tq, 1), F32)
        runs = (zero, zero)
        for m in reversed(range(max(tq // tk, 1))):
            runs = tile(nfull + m, runs, True)

        def step(c):
            it, _, r = c
            r = tile(nfull - 1 - it, r, False)
            return it + 1, _sb_alive(r), r

        lax.while_loop(lambda c: jnp.logical_and(c[0] < nfull, c[1]), step, (0, _sb_alive(runs), runs))
        o_ref[...] = jnp.where(first, acc_ref[0], acc_ref[1])

    return pl.pallas_call(
        body, name=name, out_shape=jax.ShapeDtypeStruct((s, d), F32), grid=(d // LANES, nq),
        in_specs=[pl.BlockSpec((tq, LANES), lambda p, i: (i, p)), pl.BlockSpec((s, LANES), lambda p, i: (0, p)),
                  pl.BlockSpec((s, LANES), lambda p, i: (0, p))],
        out_specs=pl.BlockSpec((tq, LANES), lambda p, i: (i, p)),
        scratch_shapes=[pltpu.VMEM((2, tq, LANES), F32)],
        compiler_params=_cparams(2),
    )(qn, kn, v)


def _sb_bwd(qn, kn, v, o, do, *, name):
    s, d = qn.shape
    tq, tk = _tile(SB_TQ, s), _tile(SB_TK, s)
    assert tk % tq == 0 or tq % tk == 0
    nq = s // tq

    def body(q_ref, k_ref, v_ref, o_ref, do_ref, dq_ref, dk_ref, dv_ref, acc_ref):
        qi = pl.program_id(1)

        @pl.when(qi == 0)
        def _():
            dk_ref[...] = jnp.zeros_like(dk_ref)
            dv_ref[...] = jnp.zeros_like(dv_ref)

        first = lax.broadcasted_iota(jnp.int32, (tq, LANES), 1) < SB_HEAD_DIM
        sel = [first, jnp.logical_not(first)]
        kfirst = lax.broadcasted_iota(jnp.int32, (tk, LANES), 1) < SB_HEAD_DIM
        ksel = [kfirst, jnp.logical_not(kfirst)]
        q = q_ref[...]
        dob = do_ref[...].astype(BF16)
        qh = [jnp.where(sel[hh], q, 0).astype(BF16) for hh in range(2)]
        doh = [jnp.where(sel[hh], dob, 0).astype(BF16) for hh in range(2)]
        prod = dob.astype(F32) * o_ref[...]
        gtot = [jnp.sum(jnp.where(sel[hh], prod, 0.0), axis=1, keepdims=True) for hh in range(2)]
        ones = _suffix_ones(tk)
        acc_ref[...] = jnp.zeros_like(acc_ref)

        def tile(j, carry, masked):
            runs, gruns = carry
            ks = pl.ds(pl.multiple_of(j * tk, tk), tk)
            kb = k_ref[ks, :]
            vb = v_ref[ks, :]
            mask = _sb_mask(qi, j, tq, tk) if masked else None
            new_runs, new_gruns = [], []
            dk_add = jnp.zeros((tk, LANES), F32)
            dv_add = jnp.zeros((tk, LANES), F32)
            for hh in range(2):
                logb, l1m, wgt = _sb_tile(qh[hh], kb, ones, runs[hh], mask)
                wb = wgt.astype(BF16)
                g = _nt(doh[hh], vb) * wb.astype(F32)
                gsuf = _exact_right2(g, ones) + g + gruns[hh]
                dz = g - jnp.exp2(logb) * (g + (gtot[hh] - gsuf))
                if masked:
                    dz = jnp.where(mask, dz, 0.0)
                dzb = dz.astype(BF16)
                acc_ref[hh] += _nn(dzb, kb)
                dk_add = dk_add + jnp.where(ksel[hh], _tn(dzb, qh[hh]), 0.0)
                dv_add = dv_add + jnp.where(ksel[hh], _tn(wb, doh[hh]), 0.0)
                new_runs.append(runs[hh] + jnp.sum(l1m, axis=1, keepdims=True))
                new_gruns.append(gruns[hh] + jnp.sum(g, axis=1, keepdims=True))
            dk_ref[ks, :] += dk_add * LN2
            dv_ref[ks, :] += dv_add
            return tuple(new_runs), tuple(new_gruns)

        nfull = (qi * tq) // tk
        zero = jnp.zeros((tq, 1), F32)
        carry = ((zero, zero), (zero, zero))
        for m in reversed(range(max(tq // tk, 1))):
            carry = tile(nfull + m, carry, True)

        def step(c):
            it, _, cr = c
            cr = tile(nfull - 1 - it, cr, False)
            return it + 1, _sb_alive(cr[0]), cr

        lax.while_loop(lambda c: jnp.logical_and(c[0] < nfull, c[1]), step, (0, _sb_alive(carry[0]), carry))
        dq_ref[...] = jnp.where(first, acc_ref[0], acc_ref[1]) * SB_SCALE

    blk = pl.BlockSpec((tq, LANES), lambda p, i: (i, p))
    full = pl.BlockSpec((s, LANES), lambda p, i: (0, p))
    out = jax.ShapeDtypeStruct((s, d), F32)
    return pl.pallas_call(
        body, name=name, out_shape=(out, out, out), grid=(d // LANES, nq),
        in_specs=[blk, full, full, blk, blk],
        out_specs=(blk, full, full),
        scratch_shapes=[pltpu.VMEM((2, tq, LANES), F32)],
        compiler_params=_cparams(2),
    )(qn, kn, v, o, do)


def _mod_part(c_all, ada_w, ada_b_my, *, name):
    nl, d, ncol = ada_w.shape

    def body(c_ref, w_ref, b_ref, part_ref, ca_ref):
        cv = c_ref[...]
        ca = cv * jax.nn.sigmoid(cv)
        ca_ref[...] = ca
        part_ref[...] = _nn(ca.astype(BF16), w_ref[...].astype(BF16)) + b_ref[...]

    return pl.pallas_call(
        body, name=name,
        out_shape=(jax.ShapeDtypeStruct((nl, N_DEV, ncol), F32), jax.ShapeDtypeStruct((N_DEV, d), F32)),
        grid=(nl,),
        in_specs=[pl.BlockSpec((N_DEV, d), lambda l: (0, 0)), pl.BlockSpec((None, d, ncol), lambda l: (l, 0, 0)),
                  pl.BlockSpec((None, 1, ncol), lambda l: (l, 0, 0))],
        out_specs=(pl.BlockSpec((None, N_DEV, ncol), lambda l: (l, 0, 0)), pl.BlockSpec((N_DEV, d), lambda l: (0, 0))),
        compiler_params=_cparams(1),
    )(c_all, ada_w, ada_b_my)


PK_MOD, PK_NMIX, PK_NMLP, PK_HGN, PK_LB, PK_QN, PK_KN, PK_CONV, PK_ROWS = 0, 96, 112, 128, 132, 136, 144, 152, 168


def _small_grads(gath, ca_col, dmod_my, lbl4, *, name):
    def body(g_ref, ca_ref, dm_ref, lbl_ref, gw_ref, gsum_ref, glb_ref, gqk_ref):
        tot = g_ref[0]
        for dev in range(1, N_DEV):
            tot = tot + g_ref[dev]
        gsum_ref[...] = tot
        lv = lbl_ref[...]
        m = jnp.maximum(jnp.maximum(lv[0], lv[1]), lv[2])
        e = [jnp.exp(lv[k] - m) for k in range(3)]
        den = e[0] + e[1] + e[2]
        p = [ek / den for ek in e]
        dlb = tot[PK_LB:PK_LB + 4, :]
        glb_ref[0] = dlb * p[0] * (1.0 - p[0])
        glb_ref[1] = -dlb * p[0] * p[1]
        glb_ref[2] = -dlb * p[0] * p[2]
        for idx, base in enumerate((PK_QN, PK_KN)):
            rowsum = jnp.sum(tot[base:base + 8, :], axis=0, keepdims=True)
            gqk_ref[idx:idx + 1, :] = rowsum + pltpu.roll(rowsum, SB_HEAD_DIM, 1)
        for l in range(2):
            acc = ca_ref[0] * dm_ref[0, l:l + 1, :]
            for smp in range(1, N_DEV):
                acc = acc + ca_ref[smp] * dm_ref[smp, l:l + 1, :]
            gw_ref[l] = acc

    d, ncol = ca_col.shape[1], dmod_my.shape[2]
    vm = pl.BlockSpec(memory_space=pltpu.VMEM)
    return pl.pallas_call(
        body, name=name,
        out_shape=(jax.ShapeDtypeStruct((2, d, ncol), F32), jax.ShapeDtypeStruct((PK_ROWS, LANES), F32),
                   jax.ShapeDtypeStruct((3, 4, LANES), F32), jax.ShapeDtypeStruct((8, LANES), F32)),
        in_specs=[vm, vm, vm, vm], out_specs=(vm, vm, vm, vm),
        compiler_params=pltpu.CompilerParams(vmem_limit_bytes=VMEM_LIMIT),
    )(gath, ca_col, dmod_my, lbl4)


def _adamw_math(w, g, m, v):
    m = ADAM_B1 * m + (1.0 - ADAM_B1) * g
    v = ADAM_B2 * v + (1.0 - ADAM_B2) * (g * g)
    m_hat = m / (1.0 - ADAM_B1 ** ADAM_STEP)
    v_hat = v / (1.0 - ADAM_B2 ** ADAM_STEP)
    delta = -ADAM_LR * (m_hat / (jnp.sqrt(v_hat) + ADAM_EPS) + ADAM_WD * w)
    return delta, m, v


def _adamw(w, g, m, v, *, name, tr=256):
    r, n = w.shape
    tr = _tile(tr, r)

    def body(w_ref, g_ref, m_ref, v_ref, d_ref, mo_ref, vo_ref):
        dl, mn, vn = _adamw_math(w_ref[...], g_ref[...], m_ref[...], v_ref[...])
        d_ref[...] = dl
        mo_ref[...] = mn
        vo_ref[...] = vn

    blk = pl.BlockSpec((tr, n), lambda i: (i, 0))
    out = jax.ShapeDtypeStruct((r, n), F32)
    return pl.pallas_call(
        body, name=name, out_shape=(out, out, out), grid=(r // tr,),
        in_specs=[blk, blk, blk, blk], out_specs=(blk, blk, blk),
        compiler_params=_cparams(1),
    )(w, g, m, v)


def _adamw_small(items, *, name):
    n = len(items)

    def body(*refs):
        ins, outs = refs[:4 * n], refs[4 * n:]
        for k in range(n):
            dl, mn, vn = _adamw_math(*(r[...] for r in ins[4 * k:4 * k + 4]))
            outs[3 * k][...] = dl
            outs[3 * k + 1][...] = mn
            outs[3 * k + 2][...] = vn

    flat = [a for it in items for a in it]
    out_shape = tuple(jax.ShapeDtypeStruct(it[0].shape, F32) for it in items for _ in range(3))
    vm = pl.BlockSpec(memory_space=pltpu.VMEM)
    res = pl.pallas_call(
        body, name=name, out_shape=out_shape, in_specs=[vm] * (4 * n), out_specs=tuple([vm] * (3 * n)),
    )(*flat)
    return [tuple(res[3 * k:3 * k + 3]) for k in range(n)]


def _mlp_fwd(x, g, scale, shift, gate, w1g, w2g, layer, tag):
    h = _norm_mod(x, g, scale, shift, name=f"{tag}_norm")
    act = _matmul(h, w1g, b_kind="colblk", layer=layer, epi="relu2", out_dtype=BF16, name=f"{tag}_w1")
    z, x_out = _matmul(act, w2g, b_kind="rowblk", layer=layer, epi="resgate", extras=(x, gate), name=f"{tag}_w2")
    return x_out, (h, act, z)


def _mlp_bwd(dx_out, x, saved, g, scale, gate, w1g, w2g, layer, tag):
    h, act, z = saved
    dz, gate_acc = _gate_bwd(dx_out, z, gate, name=f"{tag}_gate_bwd")
    du = _matmul(dz, w2g, tb=True, b_kind="rowblk", layer=layer, epi="dact", extras=(act,), out_dtype=BF16,
                 name=f"{tag}_dact")
    dw2 = _matmul(act, dz, ta=True, name=f"{tag}_dw2")
    dw1_t = _matmul(du, h, ta=True, name=f"{tag}_dw1")
    dh = _matmul(du, w1g, tb=True, b_kind="colblk", layer=layer, name=f"{tag}_dh")
    dx, nacc = _norm_mod_bwd(x, dh, dx_out, g, scale, name=f"{tag}_norm_bwd")
    return dx, dw1_t, dw2, (nacc[0:1], nacc[1:2], gate_acc[0:1]), nacc[2:3]


def kernel(x, c, ada_w, ada_b, norm_mix, norm_mlp, w_in_ab, conv_w, hg_norm, lb_logits, w_out_ab, w_qkv, q_norm, k_norm, w_out_c, mlp_w1, mlp_w2, loss_target, m_ada_w, m_ada_b, m_norm_mix, m_norm_mlp, m_w_in_ab, m_conv_w, m_hg_norm, m_lb_logits, m_w_out_ab, m_w_qkv, m_q_norm, m_k_norm, m_w_out_c, m_mlp_w1, m_mlp_w2, v_ada_w, v_ada_b, v_norm_mix, v_norm_mlp, v_w_in_ab, v_conv_w, v_hg_norm, v_lb_logits, v_w_out_ab, v_w_qkv, v_q_norm, v_k_norm, v_w_out_c, v_mlp_w1, v_mlp_w2):
    d = D_MODEL
    my_x, my_y, my_c = lax.axis_index("x"), lax.axis_index("y"), lax.axis_index("c")
    me = 4 * my_x + 2 * my_y + my_c
    xs = x[0]
    tgt = loss_target[0]

    big = [w_in_ab, w_out_ab, w_qkv, w_out_c, mlp_w1, mlp_w2]
    wing, woutg_ab, wqkvg, woutg_c, w1g, w2g = _all_gather_many([w.astype(BF16) for w in big], name="gather_weights")
    win = wing[:, 0].transpose(1, 0, 2).reshape(d, AB_IN)
    wout_ab = woutg_ab.reshape(d, d)
    wout_c = woutg_c.reshape(d, d)

    ncv = CONV_DIM // N_DEV
    c_and_conv = jnp.concatenate([c, jnp.pad(conv_w[0], ((0, 0), (0, d - ncv))), jnp.zeros((4, d), F32)], axis=0)
    c_and_conv = _all_gather(c_and_conv, name="gather_c", in_vmem=True).reshape(N_DEV, 8, d)
    c_all = c_and_conv[:, 0]
    conv_full = c_and_conv[:, 1:4, :ncv].transpose(1, 0, 2).reshape(3, CONV_DIM)
    ncol = ada_w.shape[2]
    ada_b_my = lax.dynamic_slice(ada_b, (0, me * ncol), (2, ncol)).reshape(2, 1, ncol)
    part, c_act = _mod_part(c_all, ada_w, ada_b_my, name="mod_part")
    parts = _all_gather(part.reshape(2 * N_DEV, ncol), name="gather_mod", in_vmem=True)
    parts = parts.reshape(N_DEV, 2, N_DEV, ncol)
    mod = lax.dynamic_index_in_dim(parts, me, axis=2, keepdims=False)
    mod = mod.transpose(1, 0, 2).reshape(2, 6, 1, d)

    qn_t = jnp.tile(q_norm, (1, d // SB_HEAD_DIM))
    kn_t = jnp.tile(k_norm, (1, d // SB_HEAD_DIM))

    sh1, sc1, gt1, sh2, sc2, gt2 = [mod[0, k] for k in range(6)]
    h0 = _norm_mod(xs, norm_mix[0:1], sc1, sh1, name="l0_mix_norm")
    u = _matmul(h0, win, name="l0_in_proj")
    y_a = _conv_fwd(u, conv_full, name="l0_conv")
    y_b, o_hg, sall = _hgrn_fwd(u, lb_logits, hg_norm, name="l0_hgrn")
    y_ab = jnp.concatenate([y_a, y_b], axis=1)
    z0, x_mid0 = _matmul(y_ab, wout_ab, epi="resgate", extras=(xs, gt1), name="l0_out_proj")
    x1, mlp0 = _mlp_fwd(x_mid0, norm_mlp[0:1], sc2, sh2, gt2, w1g, w2g, 0, "l0_mlp")

    sh1b, sc1b, gt1b, sh2b, sc2b, gt2b = [mod[1, k] for k in range(6)]
    h1 = _norm_mod(x1, norm_mix[1:2], sc1b, sh1b, name="l1_mix_norm")
    qkv = _matmul(h1, wqkvg, b_kind="colblk", name="l1_qkv_proj")
    qn_a, kn_a, v_a = _qk_norm_fwd(qkv, qn_t, kn_t, name="l1_qk_norm")
    o_sb = _sb_fwd(qn_a, kn_a, v_a, name="l1_sb")
    z1, x_mid1 = _matmul(o_sb, wout_c, epi="resgate", extras=(x1, gt1b), name="l1_out_proj")
    x2, mlp1 = _mlp_fwd(x_mid1, norm_mlp[1:2], sc2b, sh2b, gt2b, w1g, w2g, 1, "l1_mlp")

    dx, loss_part = _loss_grad(x2, tgt, name="loss")
    loss = lax.psum(loss_part[0, 0], MESH_AXES)

    dx, dw1t_1, dw2_1, (dsh2b, dsc2b, dgt2b), dnmlp1 = _mlp_bwd(
        dx, x_mid1, mlp1, norm_mlp[1:2], sc2b, gt2b, w1g, w2g, 1, "l1_mlp")
    dyp, gacc = _gate_bwd(dx, z1, gt1b, name="l1_mix_gate_bwd")
    dwout_c = _matmul(o_sb, dyp, ta=True, name="l1_dwout")
    do_sb = _matmul(dyp, wout_c, tb=True, name="l1_do")
    dqn_a, dkn_a, dv_a = _sb_bwd(qn_a, kn_a, v_a, o_sb, do_sb, name="l1_sb_bwd")
    dqkv, qkacc = _qk_norm_bwd(qkv, dqn_a, dkn_a, dv_a, qn_t, kn_t, name="l1_qk_norm_bwd")
    dwqkv_t = _matmul(dqkv, h1, ta=True, name="l1_dwqkv")
    dh1 = _matmul(dqkv, wqkvg, tb=True, b_kind="colblk", name="l1_dh")
    dx, nacc = _norm_mod_bwd(x1, dh1, dx, norm_mix[1:2], sc1b, name="l1_mix_norm_bwd")
    dmod1 = [nacc[0:1], nacc[1:2], gacc[0:1], dsh2b, dsc2b, dgt2b]
    dnmix1 = nacc[2:3]

    dx, dw1t_0, dw2_0, (dsh2, dsc2, dgt2), dnmlp0 = _mlp_bwd(
        dx, x_mid0, mlp0, norm_mlp[0:1], sc2, gt2, w1g, w2g, 0, "l0_mlp")
    dyp, gacc = _gate_bwd(dx, z0, gt1, name="l0_mix_gate_bwd")
    dwout_ab = _matmul(y_ab, dyp, ta=True, name="l0_dwout")
    dy_ab = _matmul(dyp, wout_ab, tb=True, name="l0_dy")
    du_a, dconv = _conv_bwd(u, dy_ab, conv_full, name="l0_conv_bwd")
    du_b, hgacc = _hgrn_bwd(u, o_hg, sall, dy_ab, lb_logits, hg_norm, name="l0_hgrn_bwd")
    du = jnp.concatenate([du_a, du_b], axis=1)
    dwin_t = _matmul(du, h0, ta=True, name="l0_dwin")
    dh0 = _matmul(du, win, tb=True, name="l0_dh")
    grad_x, nacc = _norm_mod_bwd(xs, dh0, dx, norm_mix[0:1], sc1, name="l0_mix_norm_bwd")
    dmod0 = [nacc[0:1], nacc[1:2], gacc[0:1], dsh2, dsc2, dgt2]
    dnmix0 = nacc[2:3]

    gfull = [dwin_t, dwout_ab, dwqkv_t, dwout_c, dw1t_0, dw1t_1, dw2_0, dw2_1]
    gfull = [g.reshape(N_DEV, g.shape[0] // N_DEV, d) for g in gfull]
    my_q = 2 * my_x + my_y
    far_q = [my_q ^ 2, my_q ^ 1, my_q ^ 3]
    blk_ids = jnp.stack([2 * q + my_c for q in far_q] + far_q).astype(jnp.int32)
    my_ids = jnp.stack([me, my_q]).astype(jnp.int32)
    groups = {}
    for t, g in enumerate(gfull):
        groups.setdefault(g.shape[1], []).append(t)
    sib = _rs_sibling_exchange(gfull, name="rs_sibling")
    pair = [None] * len(gfull)
    for r, ts in groups.items():
        res = _rs_pair_sum([gfull[t] for t in ts], [sib[t] for t in ts], blk_ids, name=f"rs_pair_sum_{r}")
        for t, v in zip(ts, res):
            pair[t] = v
    far = _rs_chip_exchange(pair, name="rs_chips")
    gsh = [None] * len(gfull)
    for r, ts in groups.items():
        res = _rs_final_sum([gfull[t] for t in ts], [sib[t] for t in ts], [far[t] for t in ts], my_ids,
                            name=f"rs_final_sum_{r}")
        for t, v in zip(ts, res):
            gsh[t] = v
    g_big = [gsh[0].T[None], gsh[1][None], gsh[2].T[None], gsh[3][None],
             jnp.stack([gsh[4].T, gsh[5].T]), jnp.stack([gsh[6], gsh[7]])]

    packed_small = jnp.concatenate(
        [jnp.concatenate(dmod0, axis=1).reshape(-1, LANES), jnp.concatenate(dmod1, axis=1).reshape(-1, LANES),
         dnmix0.reshape(-1, LANES), dnmix1.reshape(-1, LANES), dnmlp0.reshape(-1, LANES), dnmlp1.reshape(-1, LANES),
         hgacc[0:1].reshape(-1, LANES), hgacc[1:2].reshape(-1, LANES),
         qkacc[0:1].reshape(-1, LANES), qkacc[1:2].reshape(-1, LANES),
         dconv[0:3].reshape(-1, LANES), jnp.zeros((PK_ROWS - PK_CONV - 12, LANES), F32)], axis=0)
    gath = _all_gather(packed_small, name="gather_small_grads", in_vmem=True).reshape(N_DEV, PK_ROWS, LANES)
    dmod_all = gath[:, PK_MOD:PK_NMIX].reshape(N_DEV, 2, 6 * d)
    dmod_my = lax.dynamic_slice(dmod_all, (0, 0, me * ncol), (N_DEV, 2, ncol))
    g_ada_w, gsum, g_lb, g_qk = _small_grads(gath, c_act[:, :, None], dmod_my, lb_logits.reshape(3, 4, LANES),
                                             name="small_grads")
    g_ada_b = gsum[PK_MOD:PK_NMIX].reshape(2, 6 * d)
    g_norm_mix = gsum[PK_NMIX:PK_NMLP].reshape(2, d)
    g_norm_mlp = gsum[PK_NMLP:PK_HGN].reshape(2, d)
    g_hg_norm = gsum[PK_HGN:PK_LB].reshape(1, HG_WIDTH)
    g_lb_logits = g_lb.reshape(3, HG_WIDTH)
    g_q_norm = g_qk[0:1, :SB_HEAD_DIM]
    g_k_norm = g_qk[1:2, :SB_HEAD_DIM]
    g_conv_w = lax.dynamic_slice(gsum[PK_CONV:PK_CONV + 12].reshape(3, CONV_DIM), (0, me * ncv), (3, ncv))[None]

    def flat2(a):
        return a.reshape(-1, a.shape[-1])

    grads = dict(ada_w=g_ada_w, ada_b=g_ada_b, norm_mix=g_norm_mix, norm_mlp=g_norm_mlp, w_in_ab=g_big[0],
                 conv_w=g_conv_w, hg_norm=g_hg_norm, lb_logits=g_lb_logits, w_out_ab=g_big[1], w_qkv=g_big[2],
                 q_norm=g_q_norm, k_norm=g_k_norm, w_out_c=g_big[3], mlp_w1=g_big[4], mlp_w2=g_big[5])
    weights = dict(ada_w=(ada_w, m_ada_w, v_ada_w), ada_b=(ada_b, m_ada_b, v_ada_b),
                   norm_mix=(norm_mix, m_norm_mix, v_norm_mix), norm_mlp=(norm_mlp, m_norm_mlp, v_norm_mlp),
                   w_in_ab=(w_in_ab, m_w_in_ab, v_w_in_ab), conv_w=(conv_w, m_conv_w, v_conv_w),
                   hg_norm=(hg_norm, m_hg_norm, v_hg_norm), lb_logits=(lb_logits, m_lb_logits, v_lb_logits),
                   w_out_ab=(w_out_ab, m_w_out_ab, v_w_out_ab), w_qkv=(w_qkv, m_w_qkv, v_w_qkv),
                   q_norm=(q_norm, m_q_norm, v_q_norm), k_norm=(k_norm, m_k_norm, v_k_norm),
                   w_out_c=(w_out_c, m_w_out_c, v_w_out_c), mlp_w1=(mlp_w1, m_mlp_w1, v_mlp_w1),
                   mlp_w2=(mlp_w2, m_mlp_w2, v_mlp_w2))
    names = list(weights)
    small_names = ["ada_b", "norm_mix", "norm_mlp", "conv_w", "hg_norm", "lb_logits", "q_norm", "k_norm"]
    upd = {}
    small_items = []
    for n in small_names:
        wv, mv, vv = weights[n]
        small_items.append((flat2(wv), flat2(grads[n]), flat2(mv), flat2(vv)))
    for n, res in zip(small_names, _adamw_small(small_items, name="adamw_small")):
        upd[n] = tuple(r.reshape(weights[n][0].shape) for r in res)
    for n in names:
        if n in small_names:
            continue
        wv, mv, vv = weights[n]
        res = _adamw(flat2(wv), flat2(grads[n]), flat2(mv), flat2(vv), name=f"adamw_{n}")
        upd[n] = tuple(r.reshape(wv.shape) for r in res)

    return (loss, grad_x[None], *[grads[n].reshape(weights[n][0].shape) for n in names],
            *[upd[n][0] for n in names], *[upd[n][1] for n in names], *[upd[n][2] for n in names])
```

```python
import functools

import jax
import jax.numpy as jnp
from jax import lax
from jax.experimental import pallas as pl
from jax.experimental.pallas import tpu as pltpu

F32 = jnp.float32
BF16 = jnp.bfloat16
EPS = 1e-6
N_DEV = 8
MESH_AXES = ("x", "y", "c")

D_MODEL = 1024
CONV_DIM = 512
HG_HEADS = 4
HG_DK = 128
HG_WIDTH = 512
CHUNK = 64
HG_TILE = 128
HG_SUB = 16
HG_EXP_CLAMP = 60.0
SB_HEAD_DIM = 64
SB_SCALE = SB_HEAD_DIM ** -0.5
LOG2E = 1.4426950408889634
LN2 = 0.6931471805599453
SB_TQ = 512
SB_TK = 256
SB_DEAD = 150.0
D_FF = 4096
AB_IN = 3584

ADAM_LR = 0.001
ADAM_B1 = 0.9
ADAM_B2 = 0.999
ADAM_EPS = 1e-08
ADAM_WD = 0.01
ADAM_STEP = 10

VMEM_LIMIT = 48 * 1024 * 1024
LANES = 128


def _cparams(n_grid):
    return pltpu.CompilerParams(dimension_semantics=("arbitrary",) * n_grid, vmem_limit_bytes=VMEM_LIMIT)


def _nt(a, b):
    return lax.dot_general(a, b, (((1,), (1,)), ((), ())), preferred_element_type=F32)


def _tn(a, b):
    return lax.dot_general(a, b, (((0,), (0,)), ((), ())), preferred_element_type=F32)


def _nn(a, b):
    return jnp.dot(a, b, preferred_element_type=F32)


def _split3(x):
    hi = x.astype(BF16)
    r1 = x - hi.astype(F32)
    mid = r1.astype(BF16)
    lo = (r1 - mid.astype(F32)).astype(BF16)
    return hi, mid, lo


def _exact_left(m01, x):
    hi, mid, lo = _split3(x)
    return _nn(m01, hi) + _nn(m01, mid) + _nn(m01, lo)


def _exact_right(x, m01):
    hi, mid, lo = _split3(x)
    return _nn(hi, m01) + _nn(mid, m01) + _nn(lo, m01)


def _exact_right2(x, m01):
    hi = x.astype(BF16)
    lo = (x - hi.astype(F32)).astype(BF16)
    return _nn(hi, m01) + _nn(lo, m01)


def _sp(x):
    hi = x.astype(BF16)
    return hi, (x - hi.astype(F32)).astype(BF16)


def _dot3(fn, a, b):
    return fn(a[0], b[0]) + fn(a[0], b[1]) + fn(a[1], b[0])


def _tile(pref, n):
    t = min(pref, n)
    assert n % t == 0, (pref, n)
    return t


def _tile_rows(pref, n):
    for t in range(min(pref, n) - min(pref, n) % 16, 0, -16):
        if n % t == 0:
            return t
    raise ValueError((pref, n))


def _tile_lanes(pref, n):
    if n <= pref:
        return n
    for t in range(pref - pref % LANES, 0, -LANES):
        if n % t == 0:
            return t
    raise ValueError((pref, n))


def _all_gather(x, *, name, in_vmem):
    m_per, n = x.shape

    def body(x_ref, out_ref, send_sems, recv_sems, local_sem):
        mx, my, mc = lax.axis_index("x"), lax.axis_index("y"), lax.axis_index("c")
        me, sibling = (mx, my, mc), (mx, my, 1 - mc)
        chips = [(1 - mx, my), (mx, 1 - my), (1 - mx, 1 - my)]

        def rows(px, py, pc):
            return out_ref.at[pl.ds((4 * px + 2 * py + pc) * m_per, m_per), :]

        def copy(k, block, to, src=None):
            return pltpu.make_async_remote_copy(
                src_ref=rows(*block) if src is None else src, dst_ref=rows(*block),
                send_sem=send_sems.at[k], recv_sem=recv_sems.at[k],
                device_id=to, device_id_type=pl.DeviceIdType.MESH)

        mine = pltpu.make_async_copy(x_ref, rows(*me), local_sem)
        mine.start()
        first = [copy(0, me, sibling, src=x_ref)]
        first += [copy(1 + j, me, (*chip, mc), src=x_ref) for j, chip in enumerate(chips)]
        for cp in first:
            cp.start()
        passed = [copy(4 + j, (*chip, mc), sibling) for j, chip in enumerate(chips)]
        for j, chip in enumerate(chips):
            copy(1 + j, (*chip, mc), me).wait_recv()
            passed[j].start()
        copy(0, sibling, me).wait_recv()
        for j, chip in enumerate(chips):
            copy(4 + j, (*chip, 1 - mc), me).wait_recv()
        for cp in first + passed:
            cp.wait_send()
        mine.wait()

    space = pltpu.VMEM if in_vmem else pl.ANY
    return pl.pallas_call(
        body, name=name,
        out_shape=jax.ShapeDtypeStruct((N_DEV * m_per, n), x.dtype),
        in_specs=[pl.BlockSpec(memory_space=space)],
        out_specs=pl.BlockSpec(memory_space=space),
        scratch_shapes=[pltpu.SemaphoreType.DMA((7,)), pltpu.SemaphoreType.DMA((7,)), pltpu.SemaphoreType.DMA],
    )(x)


class _Side:
    def __init__(self, inputs, out_shape, scratch, start, finish):
        self.inputs, self.out_shape, self.scratch = list(inputs), tuple(out_shape), list(scratch)
        self.start, self.finish = start, finish


def _run_side(side, *, name):
    n_in, n_out = len(side.inputs), len(side.out_shape)

    def body(*refs):
        parts = (refs[:n_in], refs[n_in:n_in + n_out], refs[n_in + n_out:])
        side.start(*parts)
        side.finish(*parts)

    hbm = pl.BlockSpec(memory_space=pl.ANY)
    return pl.pallas_call(body, name=name, out_shape=side.out_shape, in_specs=[hbm] * n_in,
                          out_specs=tuple([hbm] * n_out), scratch_shapes=side.scratch)(*side.inputs)


def _carry(body, side, *, name, grid, in_specs, out_specs, out_shape, scratch_shapes, args):
    in_specs, out_specs, out_shape = list(in_specs), tuple(out_specs), tuple(out_shape)
    scratch_shapes = list(scratch_shapes)
    if side is None:
        res = pl.pallas_call(body, name=name, grid=grid, in_specs=in_specs, out_specs=out_specs,
                             out_shape=out_shape, scratch_shapes=scratch_shapes,
                             compiler_params=_cparams(len(grid)))(*args)
        return tuple(res), ()
    n_in, n_out, n_scr = len(in_specs), len(out_specs), len(scratch_shapes)
    s_in, s_out = len(side.inputs), len(side.out_shape)

    def wrapped(*refs):
        ins, rest = refs[:n_in], refs[n_in:]
        s_ins, rest = rest[:s_in], rest[s_in:]
        outs, rest = rest[:n_out], rest[n_out:]
        s_outs, rest = rest[:s_out], rest[s_out:]
        scr, s_scr = rest[:n_scr], rest[n_scr:]
        ids = [pl.program_id(ax) for ax in range(len(grid))]
        first = functools.reduce(jnp.logical_and, [i == 0 for i in ids])
        last = functools.reduce(jnp.logical_and, [i == g - 1 for i, g in zip(ids, grid)])

        @pl.when(first)
        def _():
            side.start(s_ins, s_outs, s_scr)

        body(*ins, *outs, *scr)

        @pl.when(last)
        def _():
            side.finish(s_ins, s_outs, s_scr)

    hbm = pl.BlockSpec(memory_space=pl.ANY)
    res = pl.pallas_call(
        wrapped, name=name, grid=grid, in_specs=in_specs + [hbm] * s_in,
        out_specs=out_specs + tuple([hbm] * s_out), out_shape=out_shape + side.out_shape,
        scratch_shapes=scratch_shapes + side.scratch, compiler_params=_cparams(len(grid)),
    )(*args, *side.inputs)
    return tuple(res[:n_out]), tuple(res[n_out:])


def _gather_side(xs):
    n = len(xs)

    def tools(x_refs, out_refs, sems):
        send_sems, recv_sems, local_sems = sems
        mx, my, mc = lax.axis_index("x"), lax.axis_index("y"), lax.axis_index("c")
        me, sibling = (mx, my, mc), (mx, my, 1 - mc)
        chips = [(1 - mx, my), (mx, 1 - my), (1 - mx, 1 - my)]

        def slot(t, px, py, pc):
            return out_refs[t].at[4 * px + 2 * py + pc]

        def copy(t, k, block, to, src=None):
            return pltpu.make_async_remote_copy(
                src_ref=slot(t, *block) if src is None else src, dst_ref=slot(t, *block),
                send_sem=send_sems.at[7 * t + k], recv_sem=recv_sems.at[7 * t + k],
                device_id=to, device_id_type=pl.DeviceIdType.MESH)

        mine = [pltpu.make_async_copy(x_refs[t], slot(t, *me), local_sems.at[t]) for t in range(n)]
        first = []
        for t in range(n):
            first.append(copy(t, 0, me, sibling, src=x_refs[t]))
            first += [copy(t, 1 + j, me, (*chip, mc), src=x_refs[t]) for j, chip in enumerate(chips)]
        return me, sibling, chips, mc, copy, mine, first

    def start(x_refs, out_refs, sems):
        *_, mine, first = tools(x_refs, out_refs, sems)
        for cp in mine + first:
            cp.start()

    def finish(x_refs, out_refs, sems):
        me, sibling, chips, mc, copy, mine, first = tools(x_refs, out_refs, sems)
        passed = []
        for j, chip in enumerate(chips):
            for t in range(n):
                copy(t, 1 + j, (*chip, mc), me).wait_recv()
                passed.append(copy(t, 4 + j, (*chip, mc), sibling))
                passed[-1].start()
        for t in range(n):
            copy(t, 0, sibling, me).wait_recv()
            for j, chip in enumerate(chips):
                copy(t, 4 + j, (*chip, 1 - mc), me).wait_recv()
        for cp in first + passed:
            cp.wait_send()
        for cp in mine:
            cp.wait()

    return _Side(xs, [jax.ShapeDtypeStruct((N_DEV,) + x.shape, x.dtype) for x in xs],
                 [pltpu.SemaphoreType.DMA((7 * n,)), pltpu.SemaphoreType.DMA((7 * n,)),
                  pltpu.SemaphoreType.DMA((n,))], start, finish)


def _sibling_exchange_side(gs):
    n = len(gs)

    def copies(g_refs, out_refs, sems):
        send_sems, recv_sems = sems
        mx, my, mc = lax.axis_index("x"), lax.axis_index("y"), lax.axis_index("c")
        return [pltpu.make_async_remote_copy(
            src_ref=g_refs[t].at[2 * q + (1 - mc)], dst_ref=out_refs[t].at[q],
            send_sem=send_sems.at[4 * t + q], recv_sem=recv_sems.at[4 * t + q],
            device_id=(mx, my, 1 - mc), device_id_type=pl.DeviceIdType.MESH)
            for t in range(n) for q in range(4)]

    def start(g_refs, out_refs, sems):
        for cp in copies(g_refs, out_refs, sems):
            cp.start()

    def finish(g_refs, out_refs, sems):
        cps = copies(g_refs, out_refs, sems)
        for cp in cps:
            cp.wait_recv()
        for cp in cps:
            cp.wait_send()

    return _Side(gs, [jax.ShapeDtypeStruct((4,) + g.shape[1:], g.dtype) for g in gs],
                 [pltpu.SemaphoreType.DMA((4 * n,)), pltpu.SemaphoreType.DMA((4 * n,))], start, finish)


def _chip_exchange_side(ts):
    n = len(ts)

    def copies(t_refs, out_refs, sems):
        send_sems, recv_sems = sems
        mx, my, mc = lax.axis_index("x"), lax.axis_index("y"), lax.axis_index("c")
        chips = [(1 - mx, my), (mx, 1 - my), (1 - mx, 1 - my)]
        return [pltpu.make_async_remote_copy(
            src_ref=t_refs[t].at[k], dst_ref=out_refs[t].at[k],
            send_sem=send_sems.at[3 * t + k], recv_sem=recv_sems.at[3 * t + k],
            device_id=(px, py, mc), device_id_type=pl.DeviceIdType.MESH)
            for t in range(n) for k, (px, py) in enumerate(chips)]

    def start(t_refs, out_refs, sems):
        for cp in copies(t_refs, out_refs, sems):
            cp.start()

    def finish(t_refs, out_refs, sems):
        cps = copies(t_refs, out_refs, sems)
        for cp in cps:
            cp.wait_recv()
        for cp in cps:
            cp.wait_send()

    return _Side(ts, [jax.ShapeDtypeStruct(t.shape, t.dtype) for t in ts],
                 [pltpu.SemaphoreType.DMA((3 * n,)), pltpu.SemaphoreType.DMA((3 * n,))], start, finish)


def _rs_pair_sum(gs, p1s, blk_ids, *, name, tr=256):
    n = len(gs)
    _, r, ncol = gs[0].shape
    tr = _tile_rows(tr, r)

    def body(id_ref, *refs):
        for t in range(n):
            refs[2 * n + t][...] = (refs[t][...] + refs[n + t][...]).astype(BF16)

    blk = lambda off: pl.BlockSpec((None, tr, ncol), lambda k, i, ids: (ids[off + k], i, 0))
    out = pl.BlockSpec((None, tr, ncol), lambda k, i, ids: (k, i, 0))
    return pl.pallas_call(
        body, name=name,
        out_shape=tuple(jax.ShapeDtypeStruct((3, r, ncol), BF16) for _ in gs),
        grid_spec=pltpu.PrefetchScalarGridSpec(
            num_scalar_prefetch=1, grid=(3, r // tr),
            in_specs=[blk(0)] * n + [blk(3)] * n, out_specs=tuple([out] * n)),
        compiler_params=_cparams(2),
    )(blk_ids, *gs, *p1s)


def _rs_final_sum(gs, p1s, p3s, my_ids, *, name, tr=256):
    n = len(gs)
    _, r, ncol = gs[0].shape
    tr = _tile_rows(tr, r)

    def body(id_ref, *refs):
        for t in range(n):
            g_ref, s_ref = refs[t], refs[n + t]
            a_ref, b_ref, c_ref = refs[2 * n + 3 * t:2 * n + 3 * t + 3]
            own = g_ref[...] + s_ref[...]
            refs[5 * n + t][...] = (((own + a_ref[...].astype(F32)) + b_ref[...].astype(F32))
                                    + c_ref[...].astype(F32))

    sel = lambda which: pl.BlockSpec((None, tr, ncol), lambda i, ids: (ids[which], i, 0))
    fix = lambda k: pl.BlockSpec((None, tr, ncol), lambda i, ids: (k, i, 0))
    p3_specs, p3_args = [], []
    for p3 in p3s:
        p3_specs += [fix(0), fix(1), fix(2)]
        p3_args += [p3, p3, p3]
    return pl.pallas_call(
        body, name=name,
        out_shape=tuple(jax.ShapeDtypeStruct((r, ncol), F32) for _ in gs),
        grid_spec=pltpu.PrefetchScalarGridSpec(
            num_scalar_prefetch=1, grid=(r // tr,),
            in_specs=[sel(0)] * n + [sel(1)] * n + p3_specs,
            out_specs=tuple([pl.BlockSpec((tr, ncol), lambda i, ids: (i, 0))] * n)),
        compiler_params=_cparams(1),
    )(my_ids, *gs, *p1s, *p3_args)


def _matmul(a, b, *, name, ta=False, tb=False, epi="plain", extras=(), out_dtype=F32, tm=1024, tn=1024, tk=1024,
            b_kind=None, layer=0, side=None):
    if ta:
        kdim, m = a.shape
    else:
        m, kdim = a.shape
    if epi == "resgate":
        tn = min(tn, 512)
    pair = 1
    if b_kind is None:
        if tb:
            n, kb = b.shape
        else:
            kb, n = b.shape
        tn, tk = _tile_lanes(tn, n), _tile_lanes(tk, kb)
        b_spec = (pl.BlockSpec((tn, tk), lambda i, j, k: (j, k)) if tb
                  else pl.BlockSpec((tk, tn), lambda i, j, k: (k, j)))
    elif b_kind == "colblk":
        assert not ta
        _, _, kw, nsh = b.shape
        if tb:
            kb, n, pair = N_DEV * nsh, kw, 2
            tn, tk = _tile_lanes(tn, n), pair * nsh
            b_spec = pl.BlockSpec((pair, None, tn, nsh), lambda i, j, k: (k, layer, j, 0))
        else:
            kb, n = kw, N_DEV * nsh
            tn, tk = nsh, _tile_lanes(tk, kb)
            b_spec = pl.BlockSpec((None, None, tk, nsh), lambda i, j, k: (j, layer, k, 0))
    elif b_kind == "rowblk":
        assert not ta
        _, _, r, ncol = b.shape
        pair = 2
        if tb:
            kb, n = ncol, N_DEV * r
            tn, tk = pair * r, _tile_lanes(tk, kb)
            b_spec = pl.BlockSpec((pair, None, r, tk), lambda i, j, k: (j, layer, 0, k))
        else:
            kb, n = N_DEV * r, ncol
            tn, tk = _tile_lanes(tn, n), pair * r
            b_spec = pl.BlockSpec((pair, None, r, tn), lambda i, j, k: (k, layer, 0, j))
    else:
        raise ValueError(b_kind)
    assert kdim == kb, (a.shape, b.shape)
    tm = _tile_lanes(tm, m)
    nk = kdim // tk
    a_spec = pl.BlockSpec((tk, tm), lambda i, j, k: (k, i)) if ta else pl.BlockSpec((tm, tk), lambda i, j, k: (i, k))
    dims = (((0 if ta else 1,), (1 if tb else 0,)), ((), ()))
    mn_spec = pl.BlockSpec((tm, tn), lambda i, j, k: (i, j))
    row_spec = pl.BlockSpec((1, tn), lambda i, j, k: (0, j))
    if epi == "resgate":
        extra_specs = [mn_spec, row_spec]
        out_shape = (jax.ShapeDtypeStruct((m, n), F32), jax.ShapeDtypeStruct((m, n), F32))
        out_specs = (mn_spec, mn_spec)
    elif epi == "dact":
        extra_specs = [mn_spec]
        out_shape = jax.ShapeDtypeStruct((m, n), out_dtype)
        out_specs = mn_spec
    else:
        extra_specs = []
        out_shape = jax.ShapeDtypeStruct((m, n), out_dtype)
        out_specs = mn_spec
    n_extra = len(extra_specs)

    def body(a_ref, b_ref, *rest):
        ex = rest[:n_extra]
        outs = rest[n_extra:n_extra + n_out]
        k = pl.program_id(2)

        def prod():
            av = a_ref[...].astype(BF16)
            if b_kind == "rowblk":
                bv = b_ref[...].astype(BF16)
                return lax.dot_general(av, bv.reshape(bv.shape[0] * bv.shape[1], bv.shape[2]), dims,
                                       preferred_element_type=F32)
            if b_kind == "colblk" and tb:
                nsh = b_ref.shape[-1]
                return sum(lax.dot_general(av[:, p * nsh:(p + 1) * nsh], b_ref[p].astype(BF16), dims,
                                           preferred_element_type=F32) for p in range(pair))
            return lax.dot_general(av, b_ref[...].astype(BF16), dims, preferred_element_type=F32)

        def finish(r):
            if epi == "plain":
                outs[0][...] = r.astype(outs[0].dtype)
            elif epi == "resgate":
                outs[0][...] = r
                outs[1][...] = ex[0][...] + ex[1][...] * r
            elif epi == "relu2":
                p = jnp.maximum(r, 0.0)
                outs[0][...] = (p * p).astype(outs[0].dtype)
            elif epi == "dact":
                outs[0][...] = (r * (2.0 * jnp.sqrt(ex[0][...].astype(F32)))).astype(outs[0].dtype)

        if nk == 1:
            finish(prod())
        else:
            acc = rest[-1]

            @pl.when(k == 0)
            def _():
                acc[...] = prod()

            if nk > 2:
                @pl.when(jnp.logical_and(k > 0, k < nk - 1))
                def _():
                    acc[...] += prod()

            @pl.when(k == nk - 1)
            def _():
                finish(acc[...] + prod())

    n_out = 2 if epi == "resgate" else 1
    if n_out == 1:
        out_shape, out_specs = (out_shape,), (out_specs,)
    res, side_res = _carry(
        body, side, name=name, grid=(m // tm, n // tn, nk), in_specs=[a_spec, b_spec] + extra_specs,
        out_specs=out_specs, out_shape=out_shape,
        scratch_shapes=[pltpu.VMEM((tm, tn), F32)] if nk > 1 else [], args=(a, b, *extras))
    res = res if n_out == 2 else res[0]
    return res if side is None else (res, side_res)


def _norm_mod(x, g, scale, shift, *, name, tm=512):
    s, d = x.shape
    tm = _tile(tm, s)

    def body(x_ref, g_ref, sc_ref, sh_ref, h_ref):
        xv = x_ref[...]
        r = lax.rsqrt(jnp.mean(xv * xv, axis=-1, keepdims=True) + EPS)
        h_ref[...] = (((xv * r) * g_ref[...]) * (1.0 + sc_ref[...]) + sh_ref[...]).astype(BF16)

    row = pl.BlockSpec((1, d), lambda i: (0, 0))
    return pl.pallas_call(
        body, name=name, out_shape=jax.ShapeDtypeStruct((s, d), BF16), grid=(s // tm,),
        in_specs=[pl.BlockSpec((tm, d), lambda i: (i, 0)), row, row, row],
        out_specs=pl.BlockSpec((tm, d), lambda i: (i, 0)),
        compiler_params=_cparams(1),
    )(x, g, scale, shift)


def _norm_mod_bwd(x, dh, dres, g, scale, *, name, tm=512):
    s, d = x.shape
    tm = _tile(tm, s)

    def body(x_ref, dh_ref, dr_ref, g_ref, sc_ref, dx_ref, acc_ref):
        i = pl.program_id(0)

        @pl.when(i == 0)
        def _():
            acc_ref[...] = jnp.zeros_like(acc_ref)

        xv = x_ref[...]
        dhv = dh_ref[...]
        gv = g_ref[...]
        one_sc = 1.0 + sc_ref[...]
        r = lax.rsqrt(jnp.mean(xv * xv, axis=-1, keepdims=True) + EPS)
        xn = xv * r
        dxn = dhv * (gv * one_sc)
        dx_ref[...] = dr_ref[...] + r * (dxn - xn * jnp.mean(dxn * xn, axis=-1, keepdims=True))
        dhxn = dhv * xn
        acc_ref[0:1, :] += jnp.sum(dhv, axis=0, keepdims=True)
        acc_ref[1:2, :] += jnp.sum(dhxn * gv, axis=0, keepdims=True)
        acc_ref[2:3, :] += jnp.sum(dhxn * one_sc, axis=0, keepdims=True)

    row = pl.BlockSpec((1, d), lambda i: (0, 0))
    blk = pl.BlockSpec((tm, d), lambda i: (i, 0))
    return pl.pallas_call(
        body, name=name,
        out_shape=(jax.ShapeDtypeStruct((s, d), F32), jax.ShapeDtypeStruct((8, d), F32)),
        grid=(s // tm,), in_specs=[blk, blk, blk, row, row],
        out_specs=(blk, pl.BlockSpec((8, d), lambda i: (0, 0))),
        compiler_params=_cparams(1),
    )(x, dh, dres, g, scale)


def _gate_bwd(dx, z, gate, *, name, tm=512):
    s, d = dx.shape
    tm = _tile(tm, s)

    def body(dx_ref, z_ref, g_ref, dz_ref, acc_ref):
        i = pl.program_id(0)

        @pl.when(i == 0)
        def _():
            acc_ref[...] = jnp.zeros_like(acc_ref)

        dxv = dx_ref[...]
        dz_ref[...] = (dxv * g_ref[...]).astype(BF16)
        acc_ref[0:1, :] += jnp.sum(dxv * z_ref[...], axis=0, keepdims=True)

    blk = pl.BlockSpec((tm, d), lambda i: (i, 0))
    return pl.pallas_call(
        body, name=name,
        out_shape=(jax.ShapeDtypeStruct((s, d), BF16), jax.ShapeDtypeStruct((8, d), F32)),
        grid=(s // tm,), in_specs=[blk, blk, pl.BlockSpec((1, d), lambda i: (0, 0))],
        out_specs=(blk, pl.BlockSpec((8, d), lambda i: (0, 0))),
        compiler_params=_cparams(1),
    )(dx, z, gate)


def _loss_grad(xf, target, *, name, tm=512):
    s, d = xf.shape
    tm = _tile(tm, s)
    nt = s // tm

    def body(x_ref, t_ref, dx_ref, loss_ref, acc_ref):
        i = pl.program_id(0)

        @pl.when(i == 0)
        def _():
            acc_ref[...] = jnp.zeros_like(acc_ref)

        e = x_ref[...] - t_ref[...]
        dx_ref[...] = e * (1.0 / d)
        acc_ref[...] += jnp.sum(e * e, axis=0, keepdims=True)

        @pl.when(i == nt - 1)
        def _():
            loss_ref[...] = (0.5 / d) * jnp.sum(acc_ref[...], axis=1, keepdims=True)

    blk = pl.BlockSpec((tm, d), lambda i: (i, 0))
    return pl.pallas_call(
        body, name=name,
        out_shape=(jax.ShapeDtypeStruct((s, d), F32), jax.ShapeDtypeStruct((1, 1), F32)),
        grid=(nt,), in_specs=[blk, blk],
        out_specs=(blk, pl.BlockSpec((1, 1), lambda i: (0, 0))),
        scratch_shapes=[pltpu.VMEM((1, d), F32)],
        compiler_params=_cparams(1),
    )(xf, target)


def _shift_down(p, prev, k):
    tm = p.shape[0]
    row = lax.broadcasted_iota(jnp.int32, p.shape, 0)
    out = pltpu.roll(p, k, 0)
    for j in range(k):
        out = jnp.where(row == j, prev[8 - k + j:8 - k + j + 1, :], out)
    return out


def _shift_up(p, nxt, k):
    tm = p.shape[0]
    row = lax.broadcasted_iota(jnp.int32, p.shape, 0)
    out = pltpu.roll(p, tm - k, 0)
    for j in range(k):
        out = jnp.where(row == tm - k + j, nxt[j:j + 1, :], out)
    return out


def _conv_fwd(u, w, *, name, tm=512):
    s = u.shape[0]
    tm = _tile(tm, s)
    c = CONV_DIM

    def body(ab_ref, ac_ref, ah_ref, w_ref, y_ref, carry_ref):
        i = pl.program_id(0)

        @pl.when(i == 0)
        def _():
            carry_ref[...] = jnp.zeros_like(carry_ref)

        p = ac_ref[...] * ah_ref[...]
        prev = carry_ref[...]
        wv = w_ref[...]
        conv = wv[2:3, :] * p + wv[1:2, :] * _shift_down(p, prev, 1) + wv[0:1, :] * _shift_down(p, prev, 2)
        y_ref[...] = (ab_ref[...] * conv).astype(BF16)
        carry_ref[...] = p[tm - 8:tm, :]

    return pl.pallas_call(
        body, name=name, out_shape=jax.ShapeDtypeStruct((s, c), BF16), grid=(s // tm,),
        in_specs=[pl.BlockSpec((tm, c), lambda i: (i, 0)), pl.BlockSpec((tm, c), lambda i: (i, 1)),
                  pl.BlockSpec((tm, c), lambda i: (i, 2)), pl.BlockSpec((3, c), lambda i: (0, 0))],
        out_specs=pl.BlockSpec((tm, c), lambda i: (i, 0)),
        scratch_shapes=[pltpu.VMEM((8, c), F32)],
        compiler_params=_cparams(1),
    )(u, u, u, w)


def _conv_bwd(u, dy, w, *, name, tm=512):
    s = u.shape[0]
    tm = _tile(tm, s)
    nt = s // tm
    c = CONV_DIM
    hb = tm // 8

    def body(ab_ref, ac_ref, ah_ref, hc_ref, hh_ref, dy_ref, w_ref, du_ref, dw_ref, carry_ref):
        i = pl.program_id(0)

        @pl.when(i == 0)
        def _():
            carry_ref[...] = jnp.zeros_like(carry_ref)
            dw_ref[...] = jnp.zeros_like(dw_ref)

        first_tile = (nt - 1 - i) == 0
        ab, ac, ah = ab_ref[...], ac_ref[...], ah_ref[...]
        p = ac * ah
        prev = jnp.where(first_tile, 0.0, hc_ref[...] * hh_ref[...])
        wv = w_ref[...]
        p1 = _shift_down(p, prev, 1)
        p2 = _shift_down(p, prev, 2)
        conv = wv[2:3, :] * p + wv[1:2, :] * p1 + wv[0:1, :] * p2
        dyv = dy_ref[...]
        dconv = dyv * ab
        nxt = carry_ref[...]
        dp = wv[2:3, :] * dconv + wv[1:2, :] * _shift_up(dconv, nxt, 1) + wv[0:1, :] * _shift_up(dconv, nxt, 2)
        du_ref[:, 0:c] = (dyv * conv).astype(BF16)
        du_ref[:, c:2 * c] = (dp * ah).astype(BF16)
        du_ref[:, 2 * c:3 * c] = (dp * ac).astype(BF16)
        dw_ref[0:1, :] += jnp.sum(dconv * p2, axis=0, keepdims=True)
        dw_ref[1:2, :] += jnp.sum(dconv * p1, axis=0, keepdims=True)
        dw_ref[2:3, :] += jnp.sum(dconv * p, axis=0, keepdims=True)
        carry_ref[...] = dconv[0:8, :]

    rev = lambda i: nt - 1 - i
    halo = lambda i: jnp.maximum(rev(i) * hb - 1, 0)
    return pl.pallas_call(
        body, name=name,
        out_shape=(jax.ShapeDtypeStruct((s, 3 * c), BF16), jax.ShapeDtypeStruct((8, c), F32)),
        grid=(nt,),
        in_specs=[pl.BlockSpec((tm, c), lambda i: (rev(i), 0)), pl.BlockSpec((tm, c), lambda i: (rev(i), 1)),
                  pl.BlockSpec((tm, c), lambda i: (rev(i), 2)),
                  pl.BlockSpec((8, c), lambda i: (halo(i), 1)), pl.BlockSpec((8, c), lambda i: (halo(i), 2)),
                  pl.BlockSpec((tm, c), lambda i: (rev(i), 0)), pl.BlockSpec((3, c), lambda i: (0, 0))],
        out_specs=(pl.BlockSpec((tm, 3 * c), lambda i: (rev(i), 0)), pl.BlockSpec((8, c), lambda i: (0, 0))),
        scratch_shapes=[pltpu.VMEM((8, c), F32)],
        compiler_params=_cparams(1),
    )(u, u, u, u, u, dy, w)


def _lower_bound(lbl):
    m = jnp.max(lbl, axis=0, keepdims=True)
    e = jnp.exp(lbl - m)
    return e[0:1, :] / jnp.sum(e, axis=0, keepdims=True)


def _hg_masks():
    t = HG_TILE
    row = lax.broadcasted_iota(jnp.int32, (t, t), 0)
    col = lax.broadcasted_iota(jnp.int32, (t, t), 1)
    same = (row >= CHUNK) == (col >= CHUNK)
    lower = same & (col <= row)
    upper = same & (row <= col)
    return row, col, lower, upper


def _hg_gates(hf, lb):
    sig = jax.nn.sigmoid(hf)
    f = lb + (1.0 - lb) * sig
    return sig, f, jnp.log(f), 1.0 - f


def _hg_refs(b_ref, hs):
    refs = []
    for i in range(HG_TILE // HG_SUB):
        if (i * HG_SUB) % CHUNK == 0:
            refs.append(jnp.zeros((1, HG_DK), F32))
        else:
            refs.append(b_ref[i * HG_SUB - 1:i * HG_SUB, hs])
    return refs


def _hgrn_fwd(u, lbl, gn, *, name, side=None):
    s = u.shape[0]
    t = HG_TILE
    nt = s // t
    nsub = t // HG_SUB
    w = HG_WIDTH

    def body(hq_ref, hf_ref, hi_ref, hg_ref, lbl_ref, gn_ref, y_ref, o_ref, sall_ref, st_ref, b_ref):
        i = pl.program_id(0)

        @pl.when(i == 0)
        def _():
            st_ref[...] = jnp.zeros_like(st_ref)

        lb = _lower_bound(lbl_ref[...])
        _, _, g, kin = _hg_gates(hf_ref[...], lb)
        _, _, lower, _ = _hg_masks()
        b_ref[...] = _exact_left(lower.astype(BF16), g)

        for h in range(HG_HEADS):
            hs = slice(h * HG_DK, (h + 1) * HG_DK)
            bh = b_ref[:, hs]
            qh = hq_ref[:, hs]
            kh = kin[:, hs]
            vh = hi_ref[:, hs]
            vsp = _sp(vh)
            refs = _hg_refs(b_ref, hs)
            rmat = jnp.concatenate([jnp.broadcast_to(r, (HG_SUB, HG_DK)) for r in refs], axis=0)
            qt = qh * jnp.exp(bh - rmat)
            prow = []
            for j in range(nsub):
                kj = kh * jnp.exp(jnp.minimum(refs[j] - bh, HG_EXP_CLAMP))
                prow.append(_dot3(_nt, _sp(qt[j * HG_SUB:(j + 1) * HG_SUB]), _sp(kj)))
            p = jnp.where(lower, jnp.concatenate(prow, axis=0), 0.0)
            intra = _dot3(_nn, _sp(p), vsp)
            o_parts = []
            for c in range(t // CHUNK):
                rs = slice(c * CHUNK, (c + 1) * CHUNK)
                st0 = st_ref[hs, :]
                sall_ref[c * w + h * HG_DK:c * w + (h + 1) * HG_DK, :] = st0
                bl = b_ref[c * CHUNK + CHUNK - 1:c * CHUNK + CHUNK, hs]
                qf = qh[rs] * jnp.exp(bh[rs])
                o_parts.append(_dot3(_nt, _sp(qf), _sp(st0)) + intra[rs])
                khat = kh[rs] * jnp.exp(bl - bh[rs])
                st_ref[hs, :] = st0 * jnp.exp(bl) + _dot3(_tn, _sp(vh[rs]), _sp(khat))
            o = jnp.concatenate(o_parts, axis=0)
            o_ref[:, hs] = o
            r = lax.rsqrt(jnp.mean(o * o, axis=-1, keepdims=True) + EPS)
            hg = hg_ref[:, hs]
            y_ref[:, hs] = (((o * r) * gn_ref[:, hs]) * (hg * jax.nn.sigmoid(hg))).astype(BF16)

    blk = lambda j: pl.BlockSpec((t, w), lambda i, j=j: (i, j))
    srows = (t // CHUNK) * w
    res, side_res = _carry(
        body, side, name=name,
        out_shape=(jax.ShapeDtypeStruct((s, w), BF16), jax.ShapeDtypeStruct((s, w), F32),
                   jax.ShapeDtypeStruct((nt * srows, HG_DK), F32)),
        grid=(nt,),
        in_specs=[blk(3), blk(4), blk(5), blk(6), pl.BlockSpec((3, w), lambda i: (0, 0)),
                  pl.BlockSpec((1, w), lambda i: (0, 0))],
        out_specs=(pl.BlockSpec((t, w), lambda i: (i, 0)), pl.BlockSpec((t, w), lambda i: (i, 0)),
                   pl.BlockSpec((srows, HG_DK), lambda i: (i, 0))),
        scratch_shapes=[pltpu.VMEM((w, HG_DK), F32), pltpu.VMEM((t, w), F32)],
        args=(u, u, u, u, lbl, gn))
    return res if side is None else (res, side_res)


def _hgrn_bwd(u, o_all, sall, dy, lbl, gn, *, name, side=None):
    s = u.shape[0]
    t = HG_TILE
    nt = s // t
    nsub = t // HG_SUB
    w = HG_WIDTH
    nch = t // CHUNK

    def body(hq_ref, hf_ref, hi_ref, hg_ref, o_ref, sall_ref, dy_ref, lbl_ref, gn_ref,
             du_ref, acc_ref, dst_ref, b_ref):
        i = pl.program_id(0)

        @pl.when(i == 0)
        def _():
            dst_ref[...] = jnp.zeros_like(dst_ref)
            acc_ref[...] = jnp.zeros_like(acc_ref)

        lb = _lower_bound(lbl_ref[...])
        sig, f, g, kin = _hg_gates(hf_ref[...], lb)
        row, col, lower, upper = _hg_masks()
        b_ref[...] = _exact_left(lower.astype(BF16), g)
        upper_bf = upper.astype(BF16)
        rowblk = [((row >= j * HG_SUB) & (row < (j + 1) * HG_SUB)) for j in range(nsub)]
        colblk = [((col >= j * HG_SUB) & (col < (j + 1) * HG_SUB)) for j in range(nsub)]
        row1 = lax.broadcasted_iota(jnp.int32, (t, HG_DK), 0)

        for h in range(HG_HEADS):
            hs = slice(h * HG_DK, (h + 1) * HG_DK)
            bh = b_ref[:, hs]
            qh = hq_ref[:, hs]
            kh = kin[:, hs]
            vh = hi_ref[:, hs]
            vsp = _sp(vh)
            hg = hg_ref[:, hs]
            gnh = gn_ref[:, hs]
            o = o_ref[:, hs]
            dyv = dy_ref[:, hs]
            sg = jax.nn.sigmoid(hg)
            r = lax.rsqrt(jnp.mean(o * o, axis=-1, keepdims=True) + EPS)
            ohat = o * r
            du_ref[:, 3 * w + h * HG_DK:3 * w + (h + 1) * HG_DK] = (
                dyv * (ohat * gnh) * (sg * (1.0 + hg * (1.0 - sg)))).astype(BF16)
            don = dyv * (hg * sg)
            acc_ref[0:1, hs] += jnp.sum(don * ohat, axis=0, keepdims=True)
            dohat = don * gnh
            do = r * (dohat - ohat * jnp.mean(dohat * ohat, axis=-1, keepdims=True))
            dosp = _sp(do)
            refs = _hg_refs(b_ref, hs)
            rmat = jnp.concatenate([jnp.broadcast_to(rr, (HG_SUB, HG_DK)) for rr in refs], axis=0)
            eq = jnp.exp(bh - rmat)
            qt = qh * eq
            qtsp = _sp(qt)
            dp = jnp.where(lower, _dot3(_nt, dosp, vsp), 0.0)
            dpt = jnp.where(upper, _dot3(_nt, vsp, dosp), 0.0)
            pt = jnp.zeros((t, t), F32)
            dk = jnp.zeros((t, HG_DK), F32)
            dq_rows = []
            for j in range(nsub):
                ek = jnp.exp(jnp.minimum(refs[j] - bh, HG_EXP_CLAMP))
                kjsp = _sp(kh * ek)
                pt = pt + _dot3(_nt, kjsp, _sp(jnp.where(rowblk[j], qt, 0.0)))
                dq_rows.append(_dot3(_nn, _sp(dp[j * HG_SUB:(j + 1) * HG_SUB]), kjsp))
                dk = dk + ek * _dot3(_nn, _sp(jnp.where(colblk[j], dpt, 0.0)), qtsp)
            pt = jnp.where(upper, pt, 0.0)
            dv = _dot3(_nn, _sp(pt), dosp)
            dq = jnp.concatenate(dq_rows, axis=0) * eq
            dq_c, dk_c, dv_c, ex_c = [None] * nch, [None] * nch, [None] * nch, [None] * nch
            for c in reversed(range(nch)):
                rs = slice(c * CHUNK, (c + 1) * CHUNK)
                st0 = sall_ref[c * w + h * HG_DK:c * w + (h + 1) * HG_DK, :]
                dst1 = dst_ref[hs, :]
                dst1sp = _sp(dst1)
                dosp_c = _sp(do[rs])
                bl = b_ref[c * CHUNK + CHUNK - 1:c * CHUNK + CHUNK, hs]
                e = jnp.exp(bh[rs])
                el = jnp.exp(bl)
                ekl = jnp.exp(bl - bh[rs])
                dq_c[c] = _dot3(_nn, dosp_c, _sp(st0)) * e
                khat = kh[rs] * ekl
                dv_c[c] = _dot3(_nt, _sp(khat), dst1sp)
                dkhat = _dot3(_nn, _sp(vh[rs]), dst1sp)
                dk_c[c] = dkhat * ekl
                ex_c[c] = (jnp.sum(dkhat * khat, axis=0, keepdims=True)
                           + el * jnp.sum(dst1 * st0, axis=0, keepdims=True))
                dst_ref[hs, :] = _dot3(_tn, dosp_c, _sp(qh[rs] * e)) + dst1 * el
            dq = dq + jnp.concatenate(dq_c, axis=0)
            dk = dk + jnp.concatenate(dk_c, axis=0)
            dv = dv + jnp.concatenate(dv_c, axis=0)
            db = qh * dq - kh * dk
            for c in range(nch):
                db = db + jnp.where(row1 == c * CHUNK + CHUNK - 1, ex_c[c], 0.0)
            dg = _exact_left(upper_bf, db)
            fh = f[:, hs]
            sgf = sig[:, hs]
            lbh = lb[:, hs]
            df = dg / fh - dk
            du_ref[:, hs] = dq.astype(BF16)
            du_ref[:, w + h * HG_DK:w + (h + 1) * HG_DK] = (df * (1.0 - lbh) * sgf * (1.0 - sgf)).astype(BF16)
            du_ref[:, 2 * w + h * HG_DK:2 * w + (h + 1) * HG_DK] = dv.astype(BF16)
            acc_ref[1:2, hs] += jnp.sum(df * (1.0 - sgf), axis=0, keepdims=True)

    rev = lambda i: nt - 1 - i
    blk = lambda j: pl.BlockSpec((t, w), lambda i, j=j: (rev(i), j))
    srows = nch * w
    res, side_res = _carry(
        body, side, name=name,
        out_shape=(jax.ShapeDtypeStruct((s, 4 * w), BF16), jax.ShapeDtypeStruct((8, w), F32)),
        grid=(nt,),
        in_specs=[blk(3), blk(4), blk(5), blk(6), pl.BlockSpec((t, w), lambda i: (rev(i), 0)),
                  pl.BlockSpec((srows, HG_DK), lambda i: (rev(i), 0)),
                  pl.BlockSpec((t, w), lambda i: (rev(i), 1)),
                  pl.BlockSpec((3, w), lambda i: (0, 0)), pl.BlockSpec((1, w), lambda i: (0, 0))],
        out_specs=(pl.BlockSpec((t, 4 * w), lambda i: (rev(i), 0)), pl.BlockSpec((8, w), lambda i: (0, 0))),
        scratch_shapes=[pltpu.VMEM((w, HG_DK), F32), pltpu.VMEM((t, w), F32)],
        args=(u, u, u, u, o_all, sall, dy, lbl, gn))
    return res if side is None else (res, side_res)


def _pair_matrix():
    row = lax.broadcasted_iota(jnp.int32, (LANES, LANES), 0)
    col = lax.broadcasted_iota(jnp.int32, (LANES, LANES), 1)
    return ((row >= SB_HEAD_DIM) == (col >= SB_HEAD_DIM)).astype(BF16)


def _qk_norm_fwd(qkv, qn, kn, *, name, tm=256):
    s = qkv.shape[0]
    d = D_MODEL
    tm = _tile(tm, s)

    def body(q_ref, k_ref, v_ref, qn_ref, kn_ref, qo_ref, ko_ref, vo_ref):
        bd = _pair_matrix()
        for src, gain, dst, fac in ((q_ref, qn_ref, qo_ref, SB_SCALE * LOG2E), (k_ref, kn_ref, ko_ref, None)):
            for grp in range(d // LANES):
                ls = slice(grp * LANES, (grp + 1) * LANES)
                xv = src[:, ls]
                ms = _exact_right(xv * xv, bd) * (1.0 / SB_HEAD_DIM)
                y = (xv * lax.rsqrt(ms + EPS)) * gain[:, ls]
                dst[:, ls] = (y if fac is None else y * fac).astype(BF16)
        vo_ref[...] = v_ref[...].astype(BF16)

    blk = lambda j: pl.BlockSpec((tm, d), lambda i, j=j: (i, j))
    row = pl.BlockSpec((1, d), lambda i: (0, 0))
    out = jax.ShapeDtypeStruct((s, d), BF16)
    return pl.pallas_call(
        body, name=name, out_shape=(out, out, out), grid=(s // tm,),
        in_specs=[blk(0), blk(1), blk(2), row, row],
        out_specs=(blk(0), blk(0), blk(0)),
        compiler_params=_cparams(1),
    )(qkv, qkv, qkv, qn, kn)


def _qk_norm_bwd(qkv, dqn, dkn, dv, qn, kn, *, name, tm=256):
    s = qkv.shape[0]
    d = D_MODEL
    tm = _tile(tm, s)

    def body(q_ref, k_ref, dq_ref, dk_ref, dv_ref, qn_ref, kn_ref, o_ref, acc_ref):
        i = pl.program_id(0)

        @pl.when(i == 0)
        def _():
            acc_ref[...] = jnp.zeros_like(acc_ref)

        bd = _pair_matrix()
        for idx, (src, dsrc, gain) in enumerate(((q_ref, dq_ref, qn_ref), (k_ref, dk_ref, kn_ref))):
            for grp in range(d // LANES):
                ls = slice(grp * LANES, (grp + 1) * LANES)
                xv = src[:, ls]
                dyv = dsrc[:, ls]
                r = lax.rsqrt(_exact_right(xv * xv, bd) * (1.0 / SB_HEAD_DIM) + EPS)
                xh = xv * r
                acc_ref[idx:idx + 1, ls] += jnp.sum(dyv * xh, axis=0, keepdims=True)
                dxh = dyv * gain[:, ls]
                mean = _exact_right(dxh * xh, bd) * (1.0 / SB_HEAD_DIM)
                o_ref[:, idx * d + grp * LANES:idx * d + (grp + 1) * LANES] = (r * (dxh - xh * mean)).astype(BF16)
        o_ref[:, 2 * d:3 * d] = dv_ref[...].astype(BF16)

    blk = lambda j: pl.BlockSpec((tm, d), lambda i, j=j: (i, j))
    row = pl.BlockSpec((1, d), lambda i: (0, 0))
    return pl.pallas_call(
        body, name=name,
        out_shape=(jax.ShapeDtypeStruct((s, 3 * d), BF16), jax.ShapeDtypeStruct((8, d), F32)),
        grid=(s // tm,),
        in_specs=[blk(0), blk(1), blk(0), blk(0), blk(0), row, row],
        out_specs=(pl.BlockSpec((tm, 3 * d), lambda i: (i, 0)), pl.BlockSpec((8, d), lambda i: (0, 0))),
        compiler_params=_cparams(1),
    )(qkv, qkv, dqn, dkn, dv, qn, kn)


def _sb_tile(qh, kb, suffix_ones, run, mask):
    z = _nt(qh, kb)
    neg_abs = lax.bitcast_convert_type(lax.bitcast_convert_type(z, jnp.uint32) | jnp.uint32(0x80000000), F32)
    l1m = -(jnp.maximum(z, 0.0) + jnp.log2(1.0 + jnp.exp2(neg_abs)))
    logb = z + l1m
    if mask is not None:
        l1m = jnp.where(mask, l1m, 0.0)
    later = _nn(l1m.astype(BF16), suffix_ones) + run
    wgt = jnp.exp2(logb + later)
    if mask is not None:
        wgt = jnp.where(mask, wgt, 0.0)
    return logb, l1m, wgt


def _suffix_ones(tk):
    row = lax.broadcasted_iota(jnp.int32, (tk, tk), 0)
    col = lax.broadcasted_iota(jnp.int32, (tk, tk), 1)
    return (row > col).astype(BF16)


def _sb_alive(runs):
    return jnp.max(jnp.maximum(runs[0], runs[1])) > -SB_DEAD


def _sb_mask(qi, j, tq, tk):
    qpos = qi * tq + lax.broadcasted_iota(jnp.int32, (tq, tk), 0)
    kpos = j * tk + lax.broadcasted_iota(jnp.int32, (tq, tk), 1)
    return kpos < qpos


def _sb_fwd(qn, kn, v, *, name, side=None):
    s, d = qn.shape
    tq, tk = _tile(SB_TQ, s), _tile(SB_TK, s)
    assert tk % tq == 0 or tq % tk == 0
    nq = s // tq

    def body(q_ref, k_ref, v_ref, o_ref, acc_ref):
        qi = pl.program_id(1)
        lane = lax.broadcasted_iota(jnp.int32, (tq, LANES), 1)
        first = lane < SB_HEAD_DIM
        q = q_ref[...]
        qh = [jnp.where(first, q, 0).astype(BF16), jnp.where(first, 0, q).astype(BF16)]
        ones = _suffix_ones(tk)
        acc_ref[...] = jnp.zeros_like(acc_ref)

        def tile(j, runs, masked):
            ks = pl.ds(pl.multiple_of(j * tk, tk), tk)
            kb = k_ref[ks, :]
            vb = v_ref[ks, :]
            mask = _sb_mask(qi, j, tq, tk) if masked else None
            new_runs = []
            for hh in range(2):
                _, l1m, wgt = _sb_tile(qh[hh], kb, ones, runs[hh], mask)
                acc_ref[hh] += _nn(wgt.astype(BF16), vb)
                new_runs.append(runs[hh] + jnp.sum(l1m, axis=1, keepdims=True))
            return tuple(new_runs)

        nfull = (qi * tq) // tk
        zero = jnp.zeros((tq, 1), F32)
        runs = (zero, zero)
        for m in reversed(range(max(tq // tk, 1))):
            runs = tile(nfull + m, runs, True)

        def step(c):
            it, _, r = c
            r = tile(nfull - 1 - it, r, False)
            return it + 1, _sb_alive(r), r

        lax.while_loop(lambda c: jnp.logical_and(c[0] < nfull, c[1]), step, (0, _sb_alive(runs), runs))
        o_ref[...] = jnp.where(first, acc_ref[0], acc_ref[1])

    res, side_res = _carry(
        body, side, name=name, out_shape=(jax.ShapeDtypeStruct((s, d), F32),), grid=(d // LANES, nq),
        in_specs=[pl.BlockSpec((tq, LANES), lambda p, i: (i, p)), pl.BlockSpec((s, LANES), lambda p, i: (0, p)),
                  pl.BlockSpec((s, LANES), lambda p, i: (0, p))],
        out_specs=(pl.BlockSpec((tq, LANES), lambda p, i: (i, p)),),
        scratch_shapes=[pltpu.VMEM((2, tq, LANES), F32)], args=(qn, kn, v))
    return res[0] if side is None else (res[0], side_res)


def _sb_bwd(qn, kn, v, o, do, *, name, side=None):
    s, d = qn.shape
    tq, tk = _tile(SB_TQ, s), _tile(SB_TK, s)
    assert tk % tq == 0 or tq % tk == 0
    nq = s // tq

    def body(q_ref, k_ref, v_ref, o_ref, do_ref, dq_ref, dk_ref, dv_ref, acc_ref):
        qi = pl.program_id(1)

        @pl.when(qi == 0)
        def _():
            dk_ref[...] = jnp.zeros_like(dk_ref)
            dv_ref[...] = jnp.zeros_like(dv_ref)

        first = lax.broadcasted_iota(jnp.int32, (tq, LANES), 1) < SB_HEAD_DIM
        sel = [first, jnp.logical_not(first)]
        kfirst = lax.broadcasted_iota(jnp.int32, (tk, LANES), 1) < SB_HEAD_DIM
        ksel = [kfirst, jnp.logical_not(kfirst)]
        q = q_ref[...]
        dob = do_ref[...].astype(BF16)
        qh = [jnp.where(sel[hh], q, 0).astype(BF16) for hh in range(2)]
        doh = [jnp.where(sel[hh], dob, 0).astype(BF16) for hh in range(2)]
        prod = dob.astype(F32) * o_ref[...]
        gtot = [jnp.sum(jnp.where(sel[hh], prod, 0.0), axis=1, keepdims=True) for hh in range(2)]
        ones = _suffix_ones(tk)
        acc_ref[...] = jnp.zeros_like(acc_ref)

        def tile(j, carry, masked):
            runs, gruns = carry
            ks = pl.ds(pl.multiple_of(j * tk, tk), tk)
            kb = k_ref[ks, :]
            vb = v_ref[ks, :]
            mask = _sb_mask(qi, j, tq, tk) if masked else None
            new_runs, new_gruns = [], []
            dk_add = jnp.zeros((tk, LANES), F32)
            dv_add = jnp.zeros((tk, LANES), F32)
            for hh in range(2):
                logb, l1m, wgt = _sb_tile(qh[hh], kb, ones, runs[hh], mask)
                wb = wgt.astype(BF16)
                g = _nt(doh[hh], vb) * wb.astype(F32)
                gsuf = _exact_right2(g, ones) + g + gruns[hh]
                dz = g - jnp.exp2(logb) * (g + (gtot[hh] - gsuf))
                if masked:
                    dz = jnp.where(mask, dz, 0.0)
                dzb = dz.astype(BF16)
                acc_ref[hh] += _nn(dzb, kb)
                dk_add = dk_add + jnp.where(ksel[hh], _tn(dzb, qh[hh]), 0.0)
                dv_add = dv_add + jnp.where(ksel[hh], _tn(wb, doh[hh]), 0.0)
                new_runs.append(runs[hh] + jnp.sum(l1m, axis=1, keepdims=True))
                new_gruns.append(gruns[hh] + jnp.sum(g, axis=1, keepdims=True))
            dk_ref[ks, :] += dk_add * LN2
            dv_ref[ks, :] += dv_add
            return tuple(new_runs), tuple(new_gruns)

        nfull = (qi * tq) // tk
        zero = jnp.zeros((tq, 1), F32)
        carry = ((zero, zero), (zero, zero))
        for m in reversed(range(max(tq // tk, 1))):
            carry = tile(nfull + m, carry, True)

        def step(c):
            it, _, cr = c
            cr = tile(nfull - 1 - it, cr, False)
            return it + 1, _sb_alive(cr[0]), cr

        lax.while_loop(lambda c: jnp.logical_and(c[0] < nfull, c[1]), step, (0, _sb_alive(carry[0]), carry))
        dq_ref[...] = jnp.where(first, acc_ref[0], acc_ref[1]) * SB_SCALE

    blk = pl.BlockSpec((tq, LANES), lambda p, i: (i, p))
    full = pl.BlockSpec((s, LANES), lambda p, i: (0, p))
    out = jax.ShapeDtypeStruct((s, d), F32)
    res, side_res = _carry(
        body, side, name=name, out_shape=(out, out, out), grid=(d // LANES, nq),
        in_specs=[blk, full, full, blk, blk], out_specs=(blk, full, full),
        scratch_shapes=[pltpu.VMEM((2, tq, LANES), F32)], args=(qn, kn, v, o, do))
    return res if side is None else (res, side_res)


def _mod_part(c_all, ada_w, ada_b_my, *, name):
    nl, d, ncol = ada_w.shape

    def body(c_ref, w_ref, b_ref, part_ref, ca_ref):
        cv = c_ref[...]
        ca = cv * jax.nn.sigmoid(cv)
        ca_ref[...] = ca
        part_ref[...] = _nn(ca.astype(BF16), w_ref[...].astype(BF16)) + b_ref[...]

    return pl.pallas_call(
        body, name=name,
        out_shape=(jax.ShapeDtypeStruct((nl, N_DEV, ncol), F32), jax.ShapeDtypeStruct((N_DEV, d), F32)),
        grid=(nl,),
        in_specs=[pl.BlockSpec((N_DEV, d), lambda l: (0, 0)), pl.BlockSpec((None, d, ncol), lambda l: (l, 0, 0)),
                  pl.BlockSpec((None, 1, ncol), lambda l: (l, 0, 0))],
        out_specs=(pl.BlockSpec((None, N_DEV, ncol), lambda l: (l, 0, 0)), pl.BlockSpec((N_DEV, d), lambda l: (0, 0))),
        compiler_params=_cparams(1),
    )(c_all, ada_w, ada_b_my)


PK_MOD, PK_NMIX, PK_NMLP, PK_HGN, PK_LB, PK_QN, PK_KN, PK_CONV, PK_ROWS = 0, 96, 112, 128, 132, 136, 144, 152, 168


def _small_grads(gath, ca_col, dmod_my, lbl4, *, name):
    def body(g_ref, ca_ref, dm_ref, lbl_ref, gw_ref, gsum_ref, glb_ref, gqk_ref):
        tot = g_ref[0]
        for dev in range(1, N_DEV):
            tot = tot + g_ref[dev]
        gsum_ref[...] = tot
        lv = lbl_ref[...]
        m = jnp.maximum(jnp.maximum(lv[0], lv[1]), lv[2])
        e = [jnp.exp(lv[k] - m) for k in range(3)]
        den = e[0] + e[1] + e[2]
        p = [ek / den for ek in e]
        dlb = tot[PK_LB:PK_LB + 4, :]
        glb_ref[0] = dlb * p[0] * (1.0 - p[0])
        glb_ref[1] = -dlb * p[0] * p[1]
        glb_ref[2] = -dlb * p[0] * p[2]
        for idx, base in enumerate((PK_QN, PK_KN)):
            rowsum = jnp.sum(tot[base:base + 8, :], axis=0, keepdims=True)
            gqk_ref[idx:idx + 1, :] = rowsum + pltpu.roll(rowsum, SB_HEAD_DIM, 1)
        for l in range(2):
            acc = ca_ref[0] * dm_ref[0, l:l + 1, :]
            for smp in range(1, N_DEV):
                acc = acc + ca_ref[smp] * dm_ref[smp, l:l + 1, :]
            gw_ref[l] = acc

    d, ncol = ca_col.shape[1], dmod_my.shape[2]
    vm = pl.BlockSpec(memory_space=pltpu.VMEM)
    return pl.pallas_call(
        body, name=name,
        out_shape=(jax.ShapeDtypeStruct((2, d, ncol), F32), jax.ShapeDtypeStruct((PK_ROWS, LANES), F32),
                   jax.ShapeDtypeStruct((3, 4, LANES), F32), jax.ShapeDtypeStruct((8, LANES), F32)),
        in_specs=[vm, vm, vm, vm], out_specs=(vm, vm, vm, vm),
        compiler_params=pltpu.CompilerParams(vmem_limit_bytes=VMEM_LIMIT),
    )(gath, ca_col, dmod_my, lbl4)


def _adamw_math(w, g, m, v):
    m = ADAM_B1 * m + (1.0 - ADAM_B1) * g
    v = ADAM_B2 * v + (1.0 - ADAM_B2) * (g * g)
    m_hat = m / (1.0 - ADAM_B1 ** ADAM_STEP)
    v_hat = v / (1.0 - ADAM_B2 ** ADAM_STEP)
    delta = -ADAM_LR * (m_hat / (jnp.sqrt(v_hat) + ADAM_EPS) + ADAM_WD * w)
    return delta, m, v


def _adamw(w, g, m, v, *, name, tr=256):
    r, n = w.shape
    tr = _tile(tr, r)

    def body(w_ref, g_ref, m_ref, v_ref, d_ref, mo_ref, vo_ref):
        dl, mn, vn = _adamw_math(w_ref[...], g_ref[...], m_ref[...], v_ref[...])
        d_ref[...] = dl
        mo_ref[...] = mn
        vo_ref[...] = vn

    blk = pl.BlockSpec((tr, n), lambda i: (i, 0))
    out = jax.ShapeDtypeStruct((r, n), F32)
    return pl.pallas_call(
        body, name=name, out_shape=(out, out, out), grid=(r // tr,),
        in_specs=[blk, blk, blk, blk], out_specs=(blk, blk, blk),
        compiler_params=_cparams(1),
    )(w, g, m, v)


def _adamw_small(items, *, name):
    n = len(items)

    def body(*refs):
        ins, outs = refs[:4 * n], refs[4 * n:]
        for k in range(n):
            dl, mn, vn = _adamw_math(*(r[...] for r in ins[4 * k:4 * k + 4]))
            outs[3 * k][...] = dl
            outs[3 * k + 1][...] = mn
            outs[3 * k + 2][...] = vn

    flat = [a for it in items for a in it]
    out_shape = tuple(jax.ShapeDtypeStruct(it[0].shape, F32) for it in items for _ in range(3))
    vm = pl.BlockSpec(memory_space=pltpu.VMEM)
    res = pl.pallas_call(
        body, name=name, out_shape=out_shape, in_specs=[vm] * (4 * n), out_specs=tuple([vm] * (3 * n)),
    )(*flat)
    return [tuple(res[3 * k:3 * k + 3]) for k in range(n)]


def _mlp_fwd(x, g, scale, shift, gate, w1g, w2g, tag, side_w1=None):
    h = _norm_mod(x, g, scale, shift, name=f"{tag}_norm")
    act = _matmul(h, w1g, b_kind="colblk", epi="relu2", out_dtype=BF16, name=f"{tag}_w1", side=side_w1)
    side_res = ()
    if side_w1 is not None:
        act, side_res = act
    z, x_out = _matmul(act, w2g, b_kind="rowblk", epi="resgate", extras=(x, gate), name=f"{tag}_w2")
    return x_out, (h, act, z), side_res


def _mlp_bwd(dx_out, x, saved, g, scale, gate, w1g, w2g, tag, side_dact=None, make_side_dh=None):
    h, act, z = saved
    dz, gate_acc = _gate_bwd(dx_out, z, gate, name=f"{tag}_gate_bwd")
    du = _matmul(dz, w2g, tb=True, b_kind="rowblk", epi="dact", extras=(act,), out_dtype=BF16,
                 name=f"{tag}_dact", side=side_dact)
    res_dact = ()
    if side_dact is not None:
        du, res_dact = du
    dw2 = _matmul(act, dz, ta=True, name=f"{tag}_dw2")
    dw1_t = _matmul(du, h, ta=True, name=f"{tag}_dw1")
    side_dh = None if make_side_dh is None else make_side_dh(dw1_t, dw2)
    dh = _matmul(du, w1g, tb=True, b_kind="colblk", name=f"{tag}_dh", side=side_dh)
    res_dh = ()
    if side_dh is not None:
        dh, res_dh = dh
    dx, nacc = _norm_mod_bwd(x, dh, dx_out, g, scale, name=f"{tag}_norm_bwd")
    return dx, dw1_t, dw2, (nacc[0:1], nacc[1:2], gate_acc[0:1]), nacc[2:3], (res_dact, res_dh)


def kernel(x, c, ada_w, ada_b, norm_mix, norm_mlp, w_in_ab, conv_w, hg_norm, lb_logits, w_out_ab, w_qkv, q_norm, k_norm, w_out_c, mlp_w1, mlp_w2, loss_target, m_ada_w, m_ada_b, m_norm_mix, m_norm_mlp, m_w_in_ab, m_conv_w, m_hg_norm, m_lb_logits, m_w_out_ab, m_w_qkv, m_q_norm, m_k_norm, m_w_out_c, m_mlp_w1, m_mlp_w2, v_ada_w, v_ada_b, v_norm_mix, v_norm_mlp, v_w_in_ab, v_conv_w, v_hg_norm, v_lb_logits, v_w_out_ab, v_w_qkv, v_q_norm, v_k_norm, v_w_out_c, v_mlp_w1, v_mlp_w2):
    d = D_MODEL
    my_x, my_y, my_c = lax.axis_index("x"), lax.axis_index("y"), lax.axis_index("c")
    me = 4 * my_x + 2 * my_y + my_c
    xs = x[0]
    tgt = loss_target[0]

    def bf(w):
        return w.astype(BF16)

    (wing,) = _run_side(_gather_side([bf(w_in_ab)]), name="gather_w_in")
    win = wing[:, 0].transpose(1, 0, 2).reshape(d, AB_IN)

    ncv = CONV_DIM // N_DEV
    c_and_conv = jnp.concatenate([c, jnp.pad(conv_w[0], ((0, 0), (0, d - ncv))), jnp.zeros((4, d), F32)], axis=0)
    c_and_conv = _all_gather(c_and_conv, name="gather_c", in_vmem=True).reshape(N_DEV, 8, d)
    c_all = c_and_conv[:, 0]
    conv_full = c_and_conv[:, 1:4, :ncv].transpose(1, 0, 2).reshape(3, CONV_DIM)
    ncol = ada_w.shape[2]
    ada_b_my = lax.dynamic_slice(ada_b, (0, me * ncol), (2, ncol)).reshape(2, 1, ncol)
    part, c_act = _mod_part(c_all, ada_w, ada_b_my, name="mod_part")
    parts = _all_gather(part.reshape(2 * N_DEV, ncol), name="gather_mod", in_vmem=True)
    parts = parts.reshape(N_DEV, 2, N_DEV, ncol)
    mod = lax.dynamic_index_in_dim(parts, me, axis=2, keepdims=False)
    mod = mod.transpose(1, 0, 2).reshape(2, 6, 1, d)

    qn_t = jnp.tile(q_norm, (1, d // SB_HEAD_DIM))
    kn_t = jnp.tile(k_norm, (1, d // SB_HEAD_DIM))

    sh1, sc1, gt1, sh2, sc2, gt2 = [mod[0, k] for k in range(6)]
    h0 = _norm_mod(xs, norm_mix[0:1], sc1, sh1, name="l0_mix_norm")
    u = _matmul(h0, win, name="l0_in_proj")
    y_a = _conv_fwd(u, conv_full, name="l0_conv")
    (y_b, o_hg, sall), (woutg_ab, w1g0, w2g0) = _hgrn_fwd(
        u, lb_logits, hg_norm, name="l0_hgrn",
        side=_gather_side([bf(w_out_ab), bf(mlp_w1[0:1]), bf(mlp_w2[0:1])]))
    wout_ab = woutg_ab.reshape(d, d)
    y_ab = jnp.concatenate([y_a, y_b], axis=1)
    z0, x_mid0 = _matmul(y_ab, wout_ab, epi="resgate", extras=(xs, gt1), name="l0_out_proj")
    x1, mlp0, (wqkvg, woutg_c) = _mlp_fwd(x_mid0, norm_mlp[0:1], sc2, sh2, gt2, w1g0, w2g0, "l0_mlp",
                                          side_w1=_gather_side([bf(w_qkv), bf(w_out_c)]))
    wout_c = woutg_c.reshape(d, d)

    sh1b, sc1b, gt1b, sh2b, sc2b, gt2b = [mod[1, k] for k in range(6)]
    h1 = _norm_mod(x1, norm_mix[1:2], sc1b, sh1b, name="l1_mix_norm")
    qkv = _matmul(h1, wqkvg, b_kind="colblk", name="l1_qkv_proj")
    qn_a, kn_a, v_a = _qk_norm_fwd(qkv, qn_t, kn_t, name="l1_qk_norm")
    o_sb, (w1g1, w2g1) = _sb_fwd(qn_a, kn_a, v_a, name="l1_sb",
                                 side=_gather_side([bf(mlp_w1[1:2]), bf(mlp_w2[1:2])]))
    z1, x_mid1 = _matmul(o_sb, wout_c, epi="resgate", extras=(x1, gt1b), name="l1_out_proj")
    x2, mlp1, _ = _mlp_fwd(x_mid1, norm_mlp[1:2], sc2b, sh2b, gt2b, w1g1, w2g1, "l1_mlp")

    dx, loss_part = _loss_grad(x2, tgt, name="loss")
    loss = lax.psum(loss_part[0, 0], MESH_AXES)

    my_q = 2 * my_x + my_y
    far_q = [my_q ^ 2, my_q ^ 1, my_q ^ 3]
    blk_ids = jnp.stack([2 * q + my_c for q in far_q] + far_q).astype(jnp.int32)
    my_ids = jnp.stack([me, my_q]).astype(jnp.int32)

    def blocks(g):
        return g.reshape(N_DEV, g.shape[0] // N_DEV, d)

    def by_rows(fn, tag, *lists):
        out = [None] * len(lists[0])
        heights = {}
        for t, g in enumerate(lists[0]):
            heights.setdefault(g.shape[1], []).append(t)
        for r, ts in heights.items():
            for t, v in zip(ts, fn(*[[lst[t] for t in ts] for lst in lists], name=f"{tag}_{r}")):
                out[t] = v
        return out

    def pair_sums(gs, sibs, tag):
        return by_rows(lambda a, b, name: _rs_pair_sum(a, b, blk_ids, name=name), f"rs_pair_sum_{tag}", gs, sibs)

    def final_sums(gs, sibs, fars, tag):
        return by_rows(lambda a, b, c_, name: _rs_final_sum(a, b, c_, my_ids, name=name),
                       f"rs_final_sum_{tag}", gs, sibs, fars)

    dx, dw1t_1, dw2_1, (dsh2b, dsc2b, dgt2b), dnmlp1, (_, sib1) = _mlp_bwd(
        dx, x_mid1, mlp1, norm_mlp[1:2], sc2b, gt2b, w1g1, w2g1, "l1_mlp",
        make_side_dh=lambda a, b: _sibling_exchange_side([blocks(a), blocks(b)]))
    g1 = [blocks(dw1t_1), blocks(dw2_1)]
    pair1 = pair_sums(g1, sib1, "g1")
    dyp, gacc = _gate_bwd(dx, z1, gt1b, name="l1_mix_gate_bwd")
    dwout_c = _matmul(o_sb, dyp, ta=True, name="l1_dwout")
    do_sb = _matmul(dyp, wout_c, tb=True, name="l1_do")
    (dqn_a, dkn_a, dv_a), far1 = _sb_bwd(qn_a, kn_a, v_a, o_sb, do_sb, name="l1_sb_bwd",
                                         side=_chip_exchange_side(pair1))
    gsh1 = final_sums(g1, sib1, far1, "g1")
    dqkv, qkacc = _qk_norm_bwd(qkv, dqn_a, dkn_a, dv_a, qn_t, kn_t, name="l1_qk_norm_bwd")
    dwqkv_t = _matmul(dqkv, h1, ta=True, name="l1_dwqkv")
    g2 = [blocks(dwqkv_t), blocks(dwout_c)]
    dh1, sib2 = _matmul(dqkv, wqkvg, tb=True, b_kind="colblk", name="l1_dh", side=_sibling_exchange_side(g2))
    pair2 = pair_sums(g2, sib2, "g2")
    dx, nacc = _norm_mod_bwd(x1, dh1, dx, norm_mix[1:2], sc1b, name="l1_mix_norm_bwd")
    dmod1 = [nacc[0:1], nacc[1:2], gacc[0:1], dsh2b, dsc2b, dgt2b]
    dnmix1 = nacc[2:3]

    dx, dw1t_0, dw2_0, (dsh2, dsc2, dgt2), dnmlp0, (far2, sib3) = _mlp_bwd(
        dx, x_mid0, mlp0, norm_mlp[0:1], sc2, gt2, w1g0, w2g0, "l0_mlp",
        side_dact=_chip_exchange_side(pair2),
        make_side_dh=lambda a, b: _sibling_exchange_side([blocks(a), blocks(b)]))
    gsh2 = final_sums(g2, sib2, far2, "g2")
    g3 = [blocks(dw1t_0), blocks(dw2_0)]
    pair3 = pair_sums(g3, sib3, "g3")
    dyp, gacc = _gate_bwd(dx, z0, gt1, name="l0_mix_gate_bwd")
    dwout_ab = _matmul(y_ab, dyp, ta=True, name="l0_dwout")
    dy_ab = _matmul(dyp, wout_ab, tb=True, name="l0_dy")
    du_a, dconv = _conv_bwd(u, dy_ab, conv_full, name="l0_conv_bwd")
    (du_b, hgacc), far3 = _hgrn_bwd(u, o_hg, sall, dy_ab, lb_logits, hg_norm, name="l0_hgrn_bwd",
                                    side=_chip_exchange_side(pair3))
    gsh3 = final_sums(g3, sib3, far3, "g3")
    du = jnp.concatenate([du_a, du_b], axis=1)
    dwin_t = _matmul(du, h0, ta=True, name="l0_dwin")
    g4 = [blocks(dwin_t), blocks(dwout_ab)]
    dh0, sib4 = _matmul(du, win, tb=True, name="l0_dh", side=_sibling_exchange_side(g4))
    pair4 = pair_sums(g4, sib4, "g4")
    far4 = _run_side(_chip_exchange_side(pair4), name="rs_chips_g4")
    gsh4 = final_sums(g4, sib4, far4, "g4")
    grad_x, nacc = _norm_mod_bwd(xs, dh0, dx, norm_mix[0:1], sc1, name="l0_mix_norm_bwd")
    dmod0 = [nacc[0:1], nacc[1:2], gacc[0:1], dsh2, dsc2, dgt2]
    dnmix0 = nacc[2:3]

    g_big = [gsh4[0].T[None], gsh4[1][None], gsh2[0].T[None], gsh2[1][None],
             jnp.stack([gsh3[0].T, gsh1[0].T]), jnp.stack([gsh3[1], gsh1[1]])]

    packed_small = jnp.concatenate(
        [jnp.concatenate(dmod0, axis=1).reshape(-1, LANES), jnp.concatenate(dmod1, axis=1).reshape(-1, LANES),
         dnmix0.reshape(-1, LANES), dnmix1.reshape(-1, LANES), dnmlp0.reshape(-1, LANES), dnmlp1.reshape(-1, LANES),
         hgacc[0:1].reshape(-1, LANES), hgacc[1:2].reshape(-1, LANES),
         qkacc[0:1].reshape(-1, LANES), qkacc[1:2].reshape(-1, LANES),
         dconv[0:3].reshape(-1, LANES), jnp.zeros((PK_ROWS - PK_CONV - 12, LANES), F32)], axis=0)
    gath = _all_gather(packed_small, name="gather_small_grads", in_vmem=True).reshape(N_DEV, PK_ROWS, LANES)
    dmod_all = gath[:, PK_MOD:PK_NMIX].reshape(N_DEV, 2, 6 * d)
    dmod_my = lax.dynamic_slice(dmod_all, (0, 0, me * ncol), (N_DEV, 2, ncol))
    g_ada_w, gsum, g_lb, g_qk = _small_grads(gath, c_act[:, :, None], dmod_my, lb_logits.reshape(3, 4, LANES),
                                             name="small_grads")
    g_ada_b = gsum[PK_MOD:PK_NMIX].reshape(2, 6 * d)
    g_norm_mix = gsum[PK_NMIX:PK_NMLP].reshape(2, d)
    g_norm_mlp = gsum[PK_NMLP:PK_HGN].reshape(2, d)
    g_hg_norm = gsum[PK_HGN:PK_LB].reshape(1, HG_WIDTH)
    g_lb_logits = g_lb.reshape(3, HG_WIDTH)
    g_q_norm = g_qk[0:1, :SB_HEAD_DIM]
    g_k_norm = g_qk[1:2, :SB_HEAD_DIM]
    g_conv_w = lax.dynamic_slice(gsum[PK_CONV:PK_CONV + 12].reshape(3, CONV_DIM), (0, me * ncv), (3, ncv))[None]

    def flat2(a):
        return a.reshape(-1, a.shape[-1])

    grads = dict(ada_w=g_ada_w, ada_b=g_ada_b, norm_mix=g_norm_mix, norm_mlp=g_norm_mlp, w_in_ab=g_big[0],
                 conv_w=g_conv_w, hg_norm=g_hg_norm, lb_logits=g_lb_logits, w_out_ab=g_big[1], w_qkv=g_big[2],
                 q_norm=g_q_norm, k_norm=g_k_norm, w_out_c=g_big[3], mlp_w1=g_big[4], mlp_w2=g_big[5])
    weights = dict(ada_w=(ada_w, m_ada_w, v_ada_w), ada_b=(ada_b, m_ada_b, v_ada_b),
                   norm_mix=(norm_mix, m_norm_mix, v_norm_mix), norm_mlp=(norm_mlp, m_norm_mlp, v_norm_mlp),
                   w_in_ab=(w_in_ab, m_w_in_ab, v_w_in_ab), conv_w=(conv_w, m_conv_w, v_conv_w),
                   hg_norm=(hg_norm, m_hg_norm, v_hg_norm), lb_logits=(lb_logits, m_lb_logits, v_lb_logits),
                   w_out_ab=(w_out_ab, m_w_out_ab, v_w_out_ab), w_qkv=(w_qkv, m_w_qkv, v_w_qkv),
                   q_norm=(q_norm, m_q_norm, v_q_norm), k_norm=(k_norm, m_k_norm, v_k_norm),
                   w_out_c=(w_out_c, m_w_out_c, v_w_out_c), mlp_w1=(mlp_w1, m_mlp_w1, v_mlp_w1),
                   mlp_w2=(mlp_w2, m_mlp_w2, v_mlp_w2))
    names = list(weights)
    small_names = ["ada_b", "norm_mix", "norm_mlp", "conv_w", "hg_norm", "lb_logits", "q_norm", "k_norm"]
    upd = {}
    small_items = []
    for n in small_names:
        wv, mv, vv = weights[n]
        small_items.append((flat2(wv), flat2(grads[n]), flat2(mv), flat2(vv)))
    for n, res in zip(small_names, _adamw_small(small_items, name="adamw_small")):
        upd[n] = tuple(r.reshape(weights[n][0].shape) for r in res)
    for n in names:
        if n in small_names:
            continue
        wv, mv, vv = weights[n]
        res = _adamw(flat2(wv), flat2(grads[n]), flat2(mv), flat2(vv), name=f"adamw_{n}")
        upd[n] = tuple(r.reshape(wv.shape) for r in res)

    return (loss, grad_x[None], *[grads[n].reshape(weights[n][0].shape) for n in names],
            *[upd[n][0] for n in names], *[upd[n][1] for n in names], *[upd[n][2] for n in names])
```

```python
import functools

import jax
import jax.numpy as jnp
from jax import lax
from jax.experimental import pallas as pl
from jax.experimental.pallas import tpu as pltpu

F32 = jnp.float32
BF16 = jnp.bfloat16
EPS = 1e-6
N_DEV = 8
MESH_AXES = ("x", "y", "c")

D_MODEL = 1024
CONV_DIM = 512
HG_HEADS = 4
HG_DK = 128
HG_WIDTH = 512
CHUNK = 64
HG_TILE = 128
HG_SUB = 16
HG_EXP_CLAMP = 60.0
SB_HEAD_DIM = 64
SB_SCALE = SB_HEAD_DIM ** -0.5
LOG2E = 1.4426950408889634
LN2 = 0.6931471805599453
SB_TQ = 512
SB_TK = 256
SB_DEAD = 150.0
D_FF = 4096
AB_IN = 3584

ADAM_LR = 0.001
ADAM_B1 = 0.9
ADAM_B2 = 0.999
ADAM_EPS = 1e-08
ADAM_WD = 0.01
ADAM_STEP = 10

VMEM_LIMIT = 48 * 1024 * 1024
LANES = 128


def _cparams(n_grid):
    return pltpu.CompilerParams(dimension_semantics=("arbitrary",) * n_grid, vmem_limit_bytes=VMEM_LIMIT)


def _nt(a, b):
    return lax.dot_general(a, b, (((1,), (1,)), ((), ())), preferred_element_type=F32)


def _tn(a, b):
    return lax.dot_general(a, b, (((0,), (0,)), ((), ())), preferred_element_type=F32)


def _nn(a, b):
    return jnp.dot(a, b, preferred_element_type=F32)


def _split3(x):
    hi = x.astype(BF16)
    r1 = x - hi.astype(F32)
    mid = r1.astype(BF16)
    lo = (r1 - mid.astype(F32)).astype(BF16)
    return hi, mid, lo


def _exact_left(m01, x):
    hi, mid, lo = _split3(x)
    return _nn(m01, hi) + _nn(m01, mid) + _nn(m01, lo)


def _exact_right(x, m01):
    hi, mid, lo = _split3(x)
    return _nn(hi, m01) + _nn(mid, m01) + _nn(lo, m01)


def _exact_right2(x, m01):
    hi = x.astype(BF16)
    lo = (x - hi.astype(F32)).astype(BF16)
    return _nn(hi, m01) + _nn(lo, m01)


def _sp(x):
    hi = x.astype(BF16)
    return hi, (x - hi.astype(F32)).astype(BF16)


def _dot3(fn, a, b):
    return fn(a[0], b[0]) + fn(a[0], b[1]) + fn(a[1], b[0])


def _tile(pref, n):
    t = min(pref, n)
    assert n % t == 0, (pref, n)
    return t


def _tile_rows(pref, n):
    for t in range(min(pref, n) - min(pref, n) % 16, 0, -16):
        if n % t == 0:
            return t
    raise ValueError((pref, n))


def _tile_lanes(pref, n):
    if n <= pref:
        return n
    for t in range(pref - pref % LANES, 0, -LANES):
        if n % t == 0:
            return t
    raise ValueError((pref, n))


def _all_gather(x, *, name, in_vmem):
    m_per, n = x.shape

    def body(x_ref, out_ref, send_sems, recv_sems, local_sem):
        mx, my, mc = lax.axis_index("x"), lax.axis_index("y"), lax.axis_index("c")
        me, sibling = (mx, my, mc), (mx, my, 1 - mc)
        chips = [(1 - mx, my), (mx, 1 - my), (1 - mx, 1 - my)]

        def rows(px, py, pc):
            return out_ref.at[pl.ds((4 * px + 2 * py + pc) * m_per, m_per), :]

        def copy(k, block, to, src=None):
            return pltpu.make_async_remote_copy(
                src_ref=rows(*block) if src is None else src, dst_ref=rows(*block),
                send_sem=send_sems.at[k], recv_sem=recv_sems.at[k],
                device_id=to, device_id_type=pl.DeviceIdType.MESH)

        mine = pltpu.make_async_copy(x_ref, rows(*me), local_sem)
        mine.start()
        first = [copy(0, me, sibling, src=x_ref)]
        first += [copy(1 + j, me, (*chip, mc), src=x_ref) for j, chip in enumerate(chips)]
        for cp in first:
            cp.start()
        passed = [copy(4 + j, (*chip, mc), sibling) for j, chip in enumerate(chips)]
        for j, chip in enumerate(chips):
            copy(1 + j, (*chip, mc), me).wait_recv()
            passed[j].start()
        copy(0, sibling, me).wait_recv()
        for j, chip in enumerate(chips):
            copy(4 + j, (*chip, 1 - mc), me).wait_recv()
        for cp in first + passed:
            cp.wait_send()
        mine.wait()

    space = pltpu.VMEM if in_vmem else pl.ANY
    return pl.pallas_call(
        body, name=name,
        out_shape=jax.ShapeDtypeStruct((N_DEV * m_per, n), x.dtype),
        in_specs=[pl.BlockSpec(memory_space=space)],
        out_specs=pl.BlockSpec(memory_space=space),
        scratch_shapes=[pltpu.SemaphoreType.DMA((7,)), pltpu.SemaphoreType.DMA((7,)), pltpu.SemaphoreType.DMA],
    )(x)


class _Side:
    def __init__(self, inputs, out_shape, scratch, start, finish):
        self.inputs, self.out_shape, self.scratch = list(inputs), tuple(out_shape), list(scratch)
        self.start, self.finish = start, finish


def _run_side(side, *, name):
    n_in, n_out = len(side.inputs), len(side.out_shape)

    def body(*refs):
        parts = (refs[:n_in], refs[n_in:n_in + n_out], refs[n_in + n_out:])
        side.start(*parts)
        side.finish(*parts)

    hbm = pl.BlockSpec(memory_space=pl.ANY)
    return pl.pallas_call(body, name=name, out_shape=side.out_shape, in_specs=[hbm] * n_in,
                          out_specs=tuple([hbm] * n_out), scratch_shapes=side.scratch)(*side.inputs)


def _carry(body, side, *, name, grid, in_specs, out_specs, out_shape, scratch_shapes, args):
    in_specs, out_specs, out_shape = list(in_specs), tuple(out_specs), tuple(out_shape)
    scratch_shapes = list(scratch_shapes)
    if side is None:
        res = pl.pallas_call(body, name=name, grid=grid, in_specs=in_specs, out_specs=out_specs,
                             out_shape=out_shape, scratch_shapes=scratch_shapes,
                             compiler_params=_cparams(len(grid)))(*args)
        return tuple(res), ()
    n_in, n_out, n_scr = len(in_specs), len(out_specs), len(scratch_shapes)
    s_in, s_out = len(side.inputs), len(side.out_shape)

    def wrapped(*refs):
        ins, rest = refs[:n_in], refs[n_in:]
        s_ins, rest = rest[:s_in], rest[s_in:]
        outs, rest = rest[:n_out], rest[n_out:]
        s_outs, rest = rest[:s_out], rest[s_out:]
        scr, s_scr = rest[:n_scr], rest[n_scr:]
        ids = [pl.program_id(ax) for ax in range(len(grid))]
        first = functools.reduce(jnp.logical_and, [i == 0 for i in ids])
        last = functools.reduce(jnp.logical_and, [i == g - 1 for i, g in zip(ids, grid)])

        @pl.when(first)
        def _():
            side.start(s_ins, s_outs, s_scr)

        body(*ins, *outs, *scr)

        @pl.when(last)
        def _():
            side.finish(s_ins, s_outs, s_scr)

    hbm = pl.BlockSpec(memory_space=pl.ANY)
    res = pl.pallas_call(
        wrapped, name=name, grid=grid, in_specs=in_specs + [hbm] * s_in,
        out_specs=out_specs + tuple([hbm] * s_out), out_shape=out_shape + side.out_shape,
        scratch_shapes=scratch_shapes + side.scratch, compiler_params=_cparams(len(grid)),
    )(*args, *side.inputs)
    return tuple(res[:n_out]), tuple(res[n_out:])


def _gather_side(xs):
    n = len(xs)

    def tools(x_refs, out_refs, sems):
        send_sems, recv_sems, local_sems = sems
        mx, my, mc = lax.axis_index("x"), lax.axis_index("y"), lax.axis_index("c")
        me, sibling = (mx, my, mc), (mx, my, 1 - mc)
        chips = [(1 - mx, my), (mx, 1 - my), (1 - mx, 1 - my)]

        def slot(t, px, py, pc):
            return out_refs[t].at[4 * px + 2 * py + pc]

        def copy(t, k, block, to, src=None):
            return pltpu.make_async_remote_copy(
                src_ref=slot(t, *block) if src is None else src, dst_ref=slot(t, *block),
                send_sem=send_sems.at[7 * t + k], recv_sem=recv_sems.at[7 * t + k],
                device_id=to, device_id_type=pl.DeviceIdType.MESH)

        mine = [pltpu.make_async_copy(x_refs[t], slot(t, *me), local_sems.at[t]) for t in range(n)]
        first = []
        for t in range(n):
            first.append(copy(t, 0, me, sibling, src=x_refs[t]))
            first += [copy(t, 1 + j, me, (*chip, mc), src=x_refs[t]) for j, chip in enumerate(chips)]
        return me, sibling, chips, mc, copy, mine, first

    def start(x_refs, out_refs, sems):
        *_, mine, first = tools(x_refs, out_refs, sems)
        for cp in mine + first:
            cp.start()

    def finish(x_refs, out_refs, sems):
        me, sibling, chips, mc, copy, mine, first = tools(x_refs, out_refs, sems)
        passed = []
        for j, chip in enumerate(chips):
            for t in range(n):
                copy(t, 1 + j, (*chip, mc), me).wait_recv()
                passed.append(copy(t, 4 + j, (*chip, mc), sibling))
                passed[-1].start()
        for t in range(n):
            copy(t, 0, sibling, me).wait_recv()
            for j, chip in enumerate(chips):
                copy(t, 4 + j, (*chip, 1 - mc), me).wait_recv()
        for cp in first + passed:
            cp.wait_send()
        for cp in mine:
            cp.wait()

    return _Side(xs, [jax.ShapeDtypeStruct((N_DEV,) + x.shape, x.dtype) for x in xs],
                 [pltpu.SemaphoreType.DMA((7 * n,)), pltpu.SemaphoreType.DMA((7 * n,)),
                  pltpu.SemaphoreType.DMA((n,))], start, finish)


def _sibling_exchange_side(gs):
    n = len(gs)

    def copies(g_refs, out_refs, sems):
        send_sems, recv_sems = sems
        mx, my, mc = lax.axis_index("x"), lax.axis_index("y"), lax.axis_index("c")
        return [pltpu.make_async_remote_copy(
            src_ref=g_refs[t].at[2 * q + (1 - mc)], dst_ref=out_refs[t].at[q],
            send_sem=send_sems.at[4 * t + q], recv_sem=recv_sems.at[4 * t + q],
            device_id=(mx, my, 1 - mc), device_id_type=pl.DeviceIdType.MESH)
            for t in range(n) for q in range(4)]

    def start(g_refs, out_refs, sems):
        for cp in copies(g_refs, out_refs, sems):
            cp.start()

    def finish(g_refs, out_refs, sems):
        cps = copies(g_refs, out_refs, sems)
        for cp in cps:
            cp.wait_recv()
        for cp in cps:
            cp.wait_send()

    return _Side(gs, [jax.ShapeDtypeStruct((4,) + g.shape[1:], g.dtype) for g in gs],
                 [pltpu.SemaphoreType.DMA((4 * n,)), pltpu.SemaphoreType.DMA((4 * n,))], start, finish)


def _chip_exchange_side(ts):
    n = len(ts)

    def copies(t_refs, out_refs, sems):
        send_sems, recv_sems = sems
        mx, my, mc = lax.axis_index("x"), lax.axis_index("y"), lax.axis_index("c")
        chips = [(1 - mx, my), (mx, 1 - my), (1 - mx, 1 - my)]
        return [pltpu.make_async_remote_copy(
            src_ref=t_refs[t].at[k], dst_ref=out_refs[t].at[k],
            send_sem=send_sems.at[3 * t + k], recv_sem=recv_sems.at[3 * t + k],
            device_id=(px, py, mc), device_id_type=pl.DeviceIdType.MESH)
            for t in range(n) for k, (px, py) in enumerate(chips)]

    def start(t_refs, out_refs, sems):
        for cp in copies(t_refs, out_refs, sems):
            cp.start()

    def finish(t_refs, out_refs, sems):
        cps = copies(t_refs, out_refs, sems)
        for cp in cps:
            cp.wait_recv()
        for cp in cps:
            cp.wait_send()

    return _Side(ts, [jax.ShapeDtypeStruct(t.shape, t.dtype) for t in ts],
                 [pltpu.SemaphoreType.DMA((3 * n,)), pltpu.SemaphoreType.DMA((3 * n,))], start, finish)


def _rs_pair_sum(gs, p1s, blk_ids, *, name, tr=256):
    n = len(gs)
    _, r, ncol = gs[0].shape
    tr = _tile_rows(tr, r)

    def body(id_ref, *refs):
        for t in range(n):
            refs[2 * n + t][...] = (refs[t][...] + refs[n + t][...]).astype(BF16)

    blk = lambda off: pl.BlockSpec((None, tr, ncol), lambda k, i, ids: (ids[off + k], i, 0))
    out = pl.BlockSpec((None, tr, ncol), lambda k, i, ids: (k, i, 0))
    return pl.pallas_call(
        body, name=name,
        out_shape=tuple(jax.ShapeDtypeStruct((3, r, ncol), BF16) for _ in gs),
        grid_spec=pltpu.PrefetchScalarGridSpec(
            num_scalar_prefetch=1, grid=(3, r // tr),
            in_specs=[blk(0)] * n + [blk(3)] * n, out_specs=tuple([out] * n)),
        compiler_params=_cparams(2),
    )(blk_ids, *gs, *p1s)


def _rs_final_sum(gs, p1s, p3s, my_ids, *, name, tr=256):
    n = len(gs)
    _, r, ncol = gs[0].shape
    tr = _tile_rows(tr, r)

    def body(id_ref, *refs):
        for t in range(n):
            g_ref, s_ref = refs[t], refs[n + t]
            a_ref, b_ref, c_ref = refs[2 * n + 3 * t:2 * n + 3 * t + 3]
            own = g_ref[...] + s_ref[...]
            refs[5 * n + t][...] = (((own + a_ref[...].astype(F32)) + b_ref[...].astype(F32))
                                    + c_ref[...].astype(F32))

    sel = lambda which: pl.BlockSpec((None, tr, ncol), lambda i, ids: (ids[which], i, 0))
    fix = lambda k: pl.BlockSpec((None, tr, ncol), lambda i, ids: (k, i, 0))
    p3_specs, p3_args = [], []
    for p3 in p3s:
        p3_specs += [fix(0), fix(1), fix(2)]
        p3_args += [p3, p3, p3]
    return pl.pallas_call(
        body, name=name,
        out_shape=tuple(jax.ShapeDtypeStruct((r, ncol), F32) for _ in gs),
        grid_spec=pltpu.PrefetchScalarGridSpec(
            num_scalar_prefetch=1, grid=(r // tr,),
            in_specs=[sel(0)] * n + [sel(1)] * n + p3_specs,
            out_specs=tuple([pl.BlockSpec((tr, ncol), lambda i, ids: (i, 0))] * n)),
        compiler_params=_cparams(1),
    )(my_ids, *gs, *p1s, *p3_args)


def _matmul(a, b, *, name, ta=False, tb=False, epi="plain", extras=(), out_dtype=F32, tm=None, tn=1024, tk=1024,
            b_kind=None, layer=0, side=None):
    if ta:
        kdim, m = a.shape
    else:
        m, kdim = a.shape
    if epi == "resgate":
        tn = min(tn, 512)
    pair = 1
    if b_kind is None:
        if tb:
            n, kb = b.shape
        else:
            kb, n = b.shape
        tn, tk = _tile_lanes(tn, n), _tile_lanes(tk, kb)
        b_spec = (pl.BlockSpec((tn, tk), lambda i, j, k: (j, k)) if tb
                  else pl.BlockSpec((tk, tn), lambda i, j, k: (k, j)))
    elif b_kind == "colblk":
        assert not ta
        _, _, kw, nsh = b.shape
        if tb:
            kb, n, pair = N_DEV * nsh, kw, 2
            tn, tk = _tile_lanes(tn, n), pair * nsh
            b_spec = pl.BlockSpec((pair, None, tn, nsh), lambda i, j, k: (k, layer, j, 0))
        else:
            kb, n, pair = kw, N_DEV * nsh, 2
            tn, tk = pair * nsh, _tile_lanes(tk, kb)
            b_spec = pl.BlockSpec((pair, None, tk, nsh), lambda i, j, k: (j, layer, k, 0))
    elif b_kind == "rowblk":
        assert not ta
        _, _, r, ncol = b.shape
        pair = 2
        if tb:
            kb, n = ncol, N_DEV * r
            tn, tk = pair * r, _tile_lanes(tk, kb)
            b_spec = pl.BlockSpec((pair, None, r, tk), lambda i, j, k: (j, layer, 0, k))
        else:
            kb, n = N_DEV * r, ncol
            tn, tk = _tile_lanes(tn, n), pair * r
            b_spec = pl.BlockSpec((pair, None, r, tn), lambda i, j, k: (k, layer, 0, j))
    else:
        raise ValueError(b_kind)
    assert kdim == kb, (a.shape, b.shape)
    if tm is None:
        tm = 1024 if (ta or epi == "resgate") else 2048
    tm = _tile_lanes(tm, m)
    nk = kdim // tk
    a_spec = pl.BlockSpec((tk, tm), lambda i, j, k: (k, i)) if ta else pl.BlockSpec((tm, tk), lambda i, j, k: (i, k))
    dims = (((0 if ta else 1,), (1 if tb else 0,)), ((), ()))
    mn_spec = pl.BlockSpec((tm, tn), lambda i, j, k: (i, j))
    row_spec = pl.BlockSpec((1, tn), lambda i, j, k: (0, j))
    if epi == "resgate":
        extra_specs = [mn_spec, row_spec]
        out_shape = (jax.ShapeDtypeStruct((m, n), F32), jax.ShapeDtypeStruct((m, n), F32))
        out_specs = (mn_spec, mn_spec)
    elif epi == "dact":
        extra_specs = [mn_spec]
        out_shape = jax.ShapeDtypeStruct((m, n), out_dtype)
        out_specs = mn_spec
    else:
        extra_specs = []
        out_shape = jax.ShapeDtypeStruct((m, n), out_dtype)
        out_specs = mn_spec
    n_extra = len(extra_specs)

    def body(a_ref, b_ref, *rest):
        ex = rest[:n_extra]
        outs = rest[n_extra:n_extra + n_out]
        k = pl.program_id(2)

        def prod():
            av = a_ref[...].astype(BF16)
            if b_kind == "rowblk":
                bv = b_ref[...].astype(BF16)
                return lax.dot_general(av, bv.reshape(bv.shape[0] * bv.shape[1], bv.shape[2]), dims,
                                       preferred_element_type=F32)
            if b_kind == "colblk" and tb:
                nsh = b_ref.shape[-1]
                return sum(lax.dot_general(av[:, p * nsh:(p + 1) * nsh], b_ref[p].astype(BF16), dims,
                                           preferred_element_type=F32) for p in range(pair))
            if b_kind == "colblk":
                return jnp.concatenate([lax.dot_general(av, b_ref[p].astype(BF16), dims, preferred_element_type=F32)
                                        for p in range(pair)], axis=1)
            return lax.dot_general(av, b_ref[...].astype(BF16), dims, preferred_element_type=F32)

        def finish(r):
            if epi == "plain":
                outs[0][...] = r.astype(outs[0].dtype)
            elif epi == "resgate":
                outs[0][...] = r
                outs[1][...] = ex[0][...] + ex[1][...] * r
            elif epi == "relu2":
                p = jnp.maximum(r, 0.0)
                outs[0][...] = (p * p).astype(outs[0].dtype)
            elif epi == "dact":
                outs[0][...] = (r * (2.0 * jnp.sqrt(ex[0][...].astype(F32)))).astype(outs[0].dtype)

        if nk == 1:
            finish(prod())
        else:
            acc = rest[-1]

            @pl.when(k == 0)
            def _():
                acc[...] = prod()

            if nk > 2:
                @pl.when(jnp.logical_and(k > 0, k < nk - 1))
                def _():
                    acc[...] += prod()

            @pl.when(k == nk - 1)
            def _():
                finish(acc[...] + prod())

    n_out = 2 if epi == "resgate" else 1
    if n_out == 1:
        out_shape, out_specs = (out_shape,), (out_specs,)
    res, side_res = _carry(
        body, side, name=name, grid=(m // tm, n // tn, nk), in_specs=[a_spec, b_spec] + extra_specs,
        out_specs=out_specs, out_shape=out_shape,
        scratch_shapes=[pltpu.VMEM((tm, tn), F32)] if nk > 1 else [], args=(a, b, *extras))
    res = res if n_out == 2 else res[0]
    return res if side is None else (res, side_res)


def _norm_mod(x, g, scale, shift, *, name, tm=512):
    s, d = x.shape
    tm = _tile(tm, s)

    def body(x_ref, g_ref, sc_ref, sh_ref, h_ref):
        xv = x_ref[...]
        r = lax.rsqrt(jnp.mean(xv * xv, axis=-1, keepdims=True) + EPS)
        h_ref[...] = (((xv * r) * g_ref[...]) * (1.0 + sc_ref[...]) + sh_ref[...]).astype(BF16)

    row = pl.BlockSpec((1, d), lambda i: (0, 0))
    return pl.pallas_call(
        body, name=name, out_shape=jax.ShapeDtypeStruct((s, d), BF16), grid=(s // tm,),
        in_specs=[pl.BlockSpec((tm, d), lambda i: (i, 0)), row, row, row],
        out_specs=pl.BlockSpec((tm, d), lambda i: (i, 0)),
        compiler_params=_cparams(1),
    )(x, g, scale, shift)


def _norm_mod_bwd(x, dh, dres, g, scale, *, name, tm=512, gated=None, side=None):
    s, d = x.shape
    tm = _tile(tm, s)
    n_in = 5 if gated is None else 7

    def body(*refs):
        x_ref, dh_ref, dr_ref, g_ref, sc_ref = refs[:5]
        dx_ref, acc_ref = refs[n_in:n_in + 2]
        i = pl.program_id(0)

        @pl.when(i == 0)
        def _():
            acc_ref[...] = jnp.zeros_like(acc_ref)

        xv = x_ref[...]
        dhv = dh_ref[...]
        gv = g_ref[...]
        one_sc = 1.0 + sc_ref[...]
        r = lax.rsqrt(jnp.mean(xv * xv, axis=-1, keepdims=True) + EPS)
        xn = xv * r
        dxn = dhv * (gv * one_sc)
        dxv = dr_ref[...] + r * (dxn - xn * jnp.mean(dxn * xn, axis=-1, keepdims=True))
        dx_ref[...] = dxv
        dhxn = dhv * xn
        acc_ref[0:1, :] += jnp.sum(dhv, axis=0, keepdims=True)
        acc_ref[1:2, :] += jnp.sum(dhxn * gv, axis=0, keepdims=True)
        acc_ref[2:3, :] += jnp.sum(dhxn * one_sc, axis=0, keepdims=True)
        if gated is not None:
            z_ref, gate_ref, dz_ref = refs[5], refs[6], refs[n_in + 2]
            dz_ref[...] = (dxv * gate_ref[...]).astype(BF16)
            acc_ref[3:4, :] += jnp.sum(dxv * z_ref[...], axis=0, keepdims=True)

    row = pl.BlockSpec((1, d), lambda i: (0, 0))
    blk = pl.BlockSpec((tm, d), lambda i: (i, 0))
    in_specs, args = [blk, blk, blk, row, row], [x, dh, dres, g, scale]
    out_shape = [jax.ShapeDtypeStruct((s, d), F32), jax.ShapeDtypeStruct((8, d), F32)]
    out_specs = [blk, pl.BlockSpec((8, d), lambda i: (0, 0))]
    if gated is not None:
        in_specs += [blk, row]
        args += list(gated)
        out_shape.append(jax.ShapeDtypeStruct((s, d), BF16))
        out_specs.append(blk)
    res, side_res = _carry(body, side, name=name, grid=(s // tm,), in_specs=in_specs, out_specs=out_specs,
                           out_shape=out_shape, scratch_shapes=[], args=args)
    return res if side is None else res + (side_res,)


def _loss_grad(xf, target, z, gate, *, name, tm=512):
    s, d = xf.shape
    tm = _tile(tm, s)
    nt = s // tm

    def body(x_ref, t_ref, z_ref, gate_ref, dx_ref, loss_ref, dz_ref, dgate_ref, acc_ref):
        i = pl.program_id(0)

        @pl.when(i == 0)
        def _():
            acc_ref[...] = jnp.zeros_like(acc_ref)
            dgate_ref[...] = jnp.zeros_like(dgate_ref)

        e = x_ref[...] - t_ref[...]
        dxv = e * (1.0 / d)
        dx_ref[...] = dxv
        dz_ref[...] = (dxv * gate_ref[...]).astype(BF16)
        dgate_ref[...] += jnp.sum(dxv * z_ref[...], axis=0, keepdims=True)
        acc_ref[...] += jnp.sum(e * e, axis=0, keepdims=True)

        @pl.when(i == nt - 1)
        def _():
            loss_ref[...] = (0.5 / d) * jnp.sum(acc_ref[...], axis=1, keepdims=True)

    blk = pl.BlockSpec((tm, d), lambda i: (i, 0))
    row = pl.BlockSpec((1, d), lambda i: (0, 0))
    return pl.pallas_call(
        body, name=name,
        out_shape=(jax.ShapeDtypeStruct((s, d), F32), jax.ShapeDtypeStruct((1, 1), F32),
                   jax.ShapeDtypeStruct((s, d), BF16), jax.ShapeDtypeStruct((1, d), F32)),
        grid=(nt,), in_specs=[blk, blk, blk, row],
        out_specs=(blk, pl.BlockSpec((1, 1), lambda i: (0, 0)), blk, row),
        scratch_shapes=[pltpu.VMEM((1, d), F32)],
        compiler_params=_cparams(1),
    )(xf, target, z, gate)


def _shift_down(p, prev, k):
    tm = p.shape[0]
    row = lax.broadcasted_iota(jnp.int32, p.shape, 0)
    out = pltpu.roll(p, k, 0)
    for j in range(k):
        out = jnp.where(row == j, prev[8 - k + j:8 - k + j + 1, :], out)
    return out


def _shift_up(p, nxt, k):
    tm = p.shape[0]
    row = lax.broadcasted_iota(jnp.int32, p.shape, 0)
    out = pltpu.roll(p, tm - k, 0)
    for j in range(k):
        out = jnp.where(row == tm - k + j, nxt[j:j + 1, :], out)
    return out


def _conv_fwd(u, w, *, name, tm=512):
    s = u.shape[0]
    tm = _tile(tm, s)
    c = CONV_DIM

    def body(ab_ref, ac_ref, ah_ref, w_ref, y_ref, carry_ref):
        i = pl.program_id(0)

        @pl.when(i == 0)
        def _():
            carry_ref[...] = jnp.zeros_like(carry_ref)

        p = ac_ref[...] * ah_ref[...]
        prev = carry_ref[...]
        wv = w_ref[...]
        conv = wv[2:3, :] * p + wv[1:2, :] * _shift_down(p, prev, 1) + wv[0:1, :] * _shift_down(p, prev, 2)
        y_ref[...] = (ab_ref[...] * conv).astype(BF16)
        carry_ref[...] = p[tm - 8:tm, :]

    return pl.pallas_call(
        body, name=name, out_shape=jax.ShapeDtypeStruct((s, c), BF16), grid=(s // tm,),
        in_specs=[pl.BlockSpec((tm, c), lambda i: (i, 0)), pl.BlockSpec((tm, c), lambda i: (i, 1)),
                  pl.BlockSpec((tm, c), lambda i: (i, 2)), pl.BlockSpec((3, c), lambda i: (0, 0))],
        out_specs=pl.BlockSpec((tm, c), lambda i: (i, 0)),
        scratch_shapes=[pltpu.VMEM((8, c), F32)],
        compiler_params=_cparams(1),
    )(u, u, u, w)


def _conv_bwd(u, dy, w, *, name, tm=512):
    s = u.shape[0]
    tm = _tile(tm, s)
    nt = s // tm
    c = CONV_DIM
    hb = tm // 8

    def body(ab_ref, ac_ref, ah_ref, hc_ref, hh_ref, dy_ref, w_ref, du_ref, dw_ref, carry_ref):
        i = pl.program_id(0)

        @pl.when(i == 0)
        def _():
            carry_ref[...] = jnp.zeros_like(carry_ref)
            dw_ref[...] = jnp.zeros_like(dw_ref)

        first_tile = (nt - 1 - i) == 0
        ab, ac, ah = ab_ref[...], ac_ref[...], ah_ref[...]
        p = ac * ah
        prev = jnp.where(first_tile, 0.0, hc_ref[...] * hh_ref[...])
        wv = w_ref[...]
        p1 = _shift_down(p, prev, 1)
        p2 = _shift_down(p, prev, 2)
        conv = wv[2:3, :] * p + wv[1:2, :] * p1 + wv[0:1, :] * p2
        dyv = dy_ref[...]
        dconv = dyv * ab
        nxt = carry_ref[...]
        dp = wv[2:3, :] * dconv + wv[1:2, :] * _shift_up(dconv, nxt, 1) + wv[0:1, :] * _shift_up(dconv, nxt, 2)
        du_ref[:, 0:c] = (dyv * conv).astype(BF16)
        du_ref[:, c:2 * c] = (dp * ah).astype(BF16)
        du_ref[:, 2 * c:3 * c] = (dp * ac).astype(BF16)
        dw_ref[0:1, :] += jnp.sum(dconv * p2, axis=0, keepdims=True)
        dw_ref[1:2, :] += jnp.sum(dconv * p1, axis=0, keepdims=True)
        dw_ref[2:3, :] += jnp.sum(dconv * p, axis=0, keepdims=True)
        carry_ref[...] = dconv[0:8, :]

    rev = lambda i: nt - 1 - i
    halo = lambda i: jnp.maximum(rev(i) * hb - 1, 0)
    return pl.pallas_call(
        body, name=name,
        out_shape=(jax.ShapeDtypeStruct((s, 3 * c), BF16), jax.ShapeDtypeStruct((8, c), F32)),
        grid=(nt,),
        in_specs=[pl.BlockSpec((tm, c), lambda i: (rev(i), 0)), pl.BlockSpec((tm, c), lambda i: (rev(i), 1)),
                  pl.BlockSpec((tm, c), lambda i: (rev(i), 2)),
                  pl.BlockSpec((8, c), lambda i: (halo(i), 1)), pl.BlockSpec((8, c), lambda i: (halo(i), 2)),
                  pl.BlockSpec((tm, c), lambda i: (rev(i), 0)), pl.BlockSpec((3, c), lambda i: (0, 0))],
        out_specs=(pl.BlockSpec((tm, 3 * c), lambda i: (rev(i), 0)), pl.BlockSpec((8, c), lambda i: (0, 0))),
        scratch_shapes=[pltpu.VMEM((8, c), F32)],
        compiler_params=_cparams(1),
    )(u, u, u, u, u, dy, w)


def _lower_bound(lbl):
    m = jnp.max(lbl, axis=0, keepdims=True)
    e = jnp.exp(lbl - m)
    return e[0:1, :] / jnp.sum(e, axis=0, keepdims=True)


def _hg_masks():
    t = HG_TILE
    row = lax.broadcasted_iota(jnp.int32, (t, t), 0)
    col = lax.broadcasted_iota(jnp.int32, (t, t), 1)
    same = (row >= CHUNK) == (col >= CHUNK)
    lower = same & (col <= row)
    upper = same & (row <= col)
    return row, col, lower, upper


def _hg_gates(hf, lb):
    sig = jax.nn.sigmoid(hf)
    f = lb + (1.0 - lb) * sig
    return sig, f, jnp.log(f), 1.0 - f


def _hg_refs(b_ref, hs):
    refs = []
    for i in range(HG_TILE // HG_SUB):
        if (i * HG_SUB) % CHUNK == 0:
            refs.append(jnp.zeros((1, HG_DK), F32))
        else:
            refs.append(b_ref[i * HG_SUB - 1:i * HG_SUB, hs])
    return refs


def _hgrn_fwd(u, lbl, gn, *, name, side=None):
    s = u.shape[0]
    t = HG_TILE
    nt = s // t
    nsub = t // HG_SUB
    w = HG_WIDTH

    def body(hq_ref, hf_ref, hi_ref, hg_ref, lbl_ref, gn_ref, y_ref, o_ref, sall_ref, st_ref, b_ref):
        i = pl.program_id(0)

        @pl.when(i == 0)
        def _():
            st_ref[...] = jnp.zeros_like(st_ref)

        lb = _lower_bound(lbl_ref[...])
        _, _, g, kin = _hg_gates(hf_ref[...], lb)
        _, _, lower, _ = _hg_masks()
        b_ref[...] = _exact_left(lower.astype(BF16), g)

        for h in range(HG_HEADS):
            hs = slice(h * HG_DK, (h + 1) * HG_DK)
            bh = b_ref[:, hs]
            qh = hq_ref[:, hs]
            kh = kin[:, hs]
            vh = hi_ref[:, hs]
            vsp = _sp(vh)
            refs = _hg_refs(b_ref, hs)
            rmat = jnp.concatenate([jnp.broadcast_to(r, (HG_SUB, HG_DK)) for r in refs], axis=0)
            qt = qh * jnp.exp(bh - rmat)
            prow = []
            for j in range(nsub):
                kj = kh * jnp.exp(jnp.minimum(refs[j] - bh, HG_EXP_CLAMP))
                prow.append(_dot3(_nt, _sp(qt[j * HG_SUB:(j + 1) * HG_SUB]), _sp(kj)))
            p = jnp.where(lower, jnp.concatenate(prow, axis=0), 0.0)
            intra = _dot3(_nn, _sp(p), vsp)
            o_parts = []
            for c in range(t // CHUNK):
                rs = slice(c * CHUNK, (c + 1) * CHUNK)
                st0 = st_ref[hs, :]
                sall_ref[c * w + h * HG_DK:c * w + (h + 1) * HG_DK, :] = st0
                bl = b_ref[c * CHUNK + CHUNK - 1:c * CHUNK + CHUNK, hs]
                qf = qh[rs] * jnp.exp(bh[rs])
                o_parts.append(_dot3(_nt, _sp(qf), _sp(st0)) + intra[rs])
                khat = kh[rs] * jnp.exp(bl - bh[rs])
                st_ref[hs, :] = st0 * jnp.exp(bl) + _dot3(_tn, _sp(vh[rs]), _sp(khat))
            o = jnp.concatenate(o_parts, axis=0)
            o_ref[:, hs] = o
            r = lax.rsqrt(jnp.mean(o * o, axis=-1, keepdims=True) + EPS)
            hg = hg_ref[:, hs]
            y_ref[:, hs] = (((o * r) * gn_ref[:, hs]) * (hg * jax.nn.sigmoid(hg))).astype(BF16)

    blk = lambda j: pl.BlockSpec((t, w), lambda i, j=j: (i, j))
    srows = (t // CHUNK) * w
    res, side_res = _carry(
        body, side, name=name,
        out_shape=(jax.ShapeDtypeStruct((s, w), BF16), jax.ShapeDtypeStruct((s, w), F32),
                   jax.ShapeDtypeStruct((nt * srows, HG_DK), F32)),
        grid=(nt,),
        in_specs=[blk(3), blk(4), blk(5), blk(6), pl.BlockSpec((3, w), lambda i: (0, 0)),
                  pl.BlockSpec((1, w), lambda i: (0, 0))],
        out_specs=(pl.BlockSpec((t, w), lambda i: (i, 0)), pl.BlockSpec((t, w), lambda i: (i, 0)),
                   pl.BlockSpec((srows, HG_DK), lambda i: (i, 0))),
        scratch_shapes=[pltpu.VMEM((w, HG_DK), F32), pltpu.VMEM((t, w), F32)],
        args=(u, u, u, u, lbl, gn))
    return res if side is None else (res, side_res)


def _hgrn_bwd(u, o_all, sall, dy, lbl, gn, *, name, side=None):
    s = u.shape[0]
    t = HG_TILE
    nt = s // t
    nsub = t // HG_SUB
    w = HG_WIDTH
    nch = t // CHUNK

    def body(hq_ref, hf_ref, hi_ref, hg_ref, o_ref, sall_ref, dy_ref, lbl_ref, gn_ref,
             du_ref, acc_ref, dst_ref, b_ref):
        i = pl.program_id(0)

        @pl.when(i == 0)
        def _():
            dst_ref[...] = jnp.zeros_like(dst_ref)
            acc_ref[...] = jnp.zeros_like(acc_ref)

        lb = _lower_bound(lbl_ref[...])
        sig, f, g, kin = _hg_gates(hf_ref[...], lb)
        row, col, lower, upper = _hg_masks()
        b_ref[...] = _exact_left(lower.astype(BF16), g)
        upper_bf = upper.astype(BF16)
        rowblk = [((row >= j * HG_SUB) & (row < (j + 1) * HG_SUB)) for j in range(nsub)]
        colblk = [((col >= j * HG_SUB) & (col < (j + 1) * HG_SUB)) for j in range(nsub)]
        row1 = lax.broadcasted_iota(jnp.int32, (t, HG_DK), 0)

        for h in range(HG_HEADS):
            hs = slice(h * HG_DK, (h + 1) * HG_DK)
            bh = b_ref[:, hs]
            qh = hq_ref[:, hs]
            kh = kin[:, hs]
            vh = hi_ref[:, hs]
            vsp = _sp(vh)
            hg = hg_ref[:, hs]
            gnh = gn_ref[:, hs]
            o = o_ref[:, hs]
            dyv = dy_ref[:, hs]
            sg = jax.nn.sigmoid(hg)
            r = lax.rsqrt(jnp.mean(o * o, axis=-1, keepdims=True) + EPS)
            ohat = o * r
            du_ref[:, 3 * w + h * HG_DK:3 * w + (h + 1) * HG_DK] = (
                dyv * (ohat * gnh) * (sg * (1.0 + hg * (1.0 - sg)))).astype(BF16)
            don = dyv * (hg * sg)
            acc_ref[0:1, hs] += jnp.sum(don * ohat, axis=0, keepdims=True)
            dohat = don * gnh
            do = r * (dohat - ohat * jnp.mean(dohat * ohat, axis=-1, keepdims=True))
            dosp = _sp(do)
            refs = _hg_refs(b_ref, hs)
            rmat = jnp.concatenate([jnp.broadcast_to(rr, (HG_SUB, HG_DK)) for rr in refs], axis=0)
            eq = jnp.exp(bh - rmat)
            qt = qh * eq
            qtsp = _sp(qt)
            dp = jnp.where(lower, _dot3(_nt, dosp, vsp), 0.0)
            dpt = jnp.where(upper, _dot3(_nt, vsp, dosp), 0.0)
            pt = jnp.zeros((t, t), F32)
            dk = jnp.zeros((t, HG_DK), F32)
            dq_rows = []
            for j in range(nsub):
                ek = jnp.exp(jnp.minimum(refs[j] - bh, HG_EXP_CLAMP))
                kjsp = _sp(kh * ek)
                pt = pt + _dot3(_nt, kjsp, _sp(jnp.where(rowblk[j], qt, 0.0)))
                dq_rows.append(_dot3(_nn, _sp(dp[j * HG_SUB:(j + 1) * HG_SUB]), kjsp))
                dk = dk + ek * _dot3(_nn, _sp(jnp.where(colblk[j], dpt, 0.0)), qtsp)
            pt = jnp.where(upper, pt, 0.0)
            dv = _dot3(_nn, _sp(pt), dosp)
            dq = jnp.concatenate(dq_rows, axis=0) * eq
            dq_c, dk_c, dv_c, ex_c = [None] * nch, [None] * nch, [None] * nch, [None] * nch
            for c in reversed(range(nch)):
                rs = slice(c * CHUNK, (c + 1) * CHUNK)
                st0 = sall_ref[c * w + h * HG_DK:c * w + (h + 1) * HG_DK, :]
                dst1 = dst_ref[hs, :]
                dst1sp = _sp(dst1)
                dosp_c = _sp(do[rs])
                bl = b_ref[c * CHUNK + CHUNK - 1:c * CHUNK + CHUNK, hs]
                e = jnp.exp(bh[rs])
                el = jnp.exp(bl)
                ekl = jnp.exp(bl - bh[rs])
                dq_c[c] = _dot3(_nn, dosp_c, _sp(st0)) * e
                khat = kh[rs] * ekl
                dv_c[c] = _dot3(_nt, _sp(khat), dst1sp)
                dkhat = _dot3(_nn, _sp(vh[rs]), dst1sp)
                dk_c[c] = dkhat * ekl
                ex_c[c] = (jnp.sum(dkhat * khat, axis=0, keepdims=True)
                           + el * jnp.sum(dst1 * st0, axis=0, keepdims=True))
                dst_ref[hs, :] = _dot3(_tn, dosp_c, _sp(qh[rs] * e)) + dst1 * el
            dq = dq + jnp.concatenate(dq_c, axis=0)
            dk = dk + jnp.concatenate(dk_c, axis=0)
            dv = dv + jnp.concatenate(dv_c, axis=0)
            db = qh * dq - kh * dk
            for c in range(nch):
                db = db + jnp.where(row1 == c * CHUNK + CHUNK - 1, ex_c[c], 0.0)
            dg = _exact_left(upper_bf, db)
            fh = f[:, hs]
            sgf = sig[:, hs]
            lbh = lb[:, hs]
            df = dg / fh - dk
            du_ref[:, hs] = dq.astype(BF16)
            du_ref[:, w + h * HG_DK:w + (h + 1) * HG_DK] = (df * (1.0 - lbh) * sgf * (1.0 - sgf)).astype(BF16)
            du_ref[:, 2 * w + h * HG_DK:2 * w + (h + 1) * HG_DK] = dv.astype(BF16)
            acc_ref[1:2, hs] += jnp.sum(df * (1.0 - sgf), axis=0, keepdims=True)

    rev = lambda i: nt - 1 - i
    blk = lambda j: pl.BlockSpec((t, w), lambda i, j=j: (rev(i), j))
    srows = nch * w
    res, side_res = _carry(
        body, side, name=name,
        out_shape=(jax.ShapeDtypeStruct((s, 4 * w), BF16), jax.ShapeDtypeStruct((8, w), F32)),
        grid=(nt,),
        in_specs=[blk(3), blk(4), blk(5), blk(6), pl.BlockSpec((t, w), lambda i: (rev(i), 0)),
                  pl.BlockSpec((srows, HG_DK), lambda i: (rev(i), 0)),
                  pl.BlockSpec((t, w), lambda i: (rev(i), 1)),
                  pl.BlockSpec((3, w), lambda i: (0, 0)), pl.BlockSpec((1, w), lambda i: (0, 0))],
        out_specs=(pl.BlockSpec((t, 4 * w), lambda i: (rev(i), 0)), pl.BlockSpec((8, w), lambda i: (0, 0))),
        scratch_shapes=[pltpu.VMEM((w, HG_DK), F32), pltpu.VMEM((t, w), F32)],
        args=(u, u, u, u, o_all, sall, dy, lbl, gn))
    return res if side is None else (res, side_res)


def _pair_matrix():
    row = lax.broadcasted_iota(jnp.int32, (LANES, LANES), 0)
    col = lax.broadcasted_iota(jnp.int32, (LANES, LANES), 1)
    return ((row >= SB_HEAD_DIM) == (col >= SB_HEAD_DIM)).astype(BF16)


def _qk_norm_fwd(qkv, qn, kn, *, name, tm=256):
    s = qkv.shape[0]
    d = D_MODEL
    tm = _tile(tm, s)

    def body(q_ref, k_ref, v_ref, qn_ref, kn_ref, qo_ref, ko_ref, vo_ref):
        bd = _pair_matrix()
        for src, gain, dst, fac in ((q_ref, qn_ref, qo_ref, SB_SCALE * LOG2E), (k_ref, kn_ref, ko_ref, None)):
            for grp in range(d // LANES):
                ls = slice(grp * LANES, (grp + 1) * LANES)
                xv = src[:, ls]
                ms = _exact_right(xv * xv, bd) * (1.0 / SB_HEAD_DIM)
                y = (xv * lax.rsqrt(ms + EPS)) * gain[:, ls]
                dst[:, ls] = (y if fac is None else y * fac).astype(BF16)
        vo_ref[...] = v_ref[...].astype(BF16)

    blk = lambda j: pl.BlockSpec((tm, d), lambda i, j=j: (i, j))
    row = pl.BlockSpec((1, d), lambda i: (0, 0))
    out = jax.ShapeDtypeStruct((s, d), BF16)
    return pl.pallas_call(
        body, name=name, out_shape=(out, out, out), grid=(s // tm,),
        in_specs=[blk(0), blk(1), blk(2), row, row],
        out_specs=(blk(0), blk(0), blk(0)),
        compiler_params=_cparams(1),
    )(qkv, qkv, qkv, qn, kn)


def _qk_norm_bwd(qkv, dqn, dkn, dv, qn, kn, *, name, tm=256):
    s = qkv.shape[0]
    d = D_MODEL
    tm = _tile(tm, s)

    def body(q_ref, k_ref, dq_ref, dk_ref, dv_ref, qn_ref, kn_ref, o_ref, acc_ref):
        i = pl.program_id(0)

        @pl.when(i == 0)
        def _():
            acc_ref[...] = jnp.zeros_like(acc_ref)

        bd = _pair_matrix()
        for idx, (src, dsrc, gain) in enumerate(((q_ref, dq_ref, qn_ref), (k_ref, dk_ref, kn_ref))):
            for grp in range(d // LANES):
                ls = slice(grp * LANES, (grp + 1) * LANES)
                xv = src[:, ls]
                dyv = dsrc[:, ls]
                r = lax.rsqrt(_exact_right(xv * xv, bd) * (1.0 / SB_HEAD_DIM) + EPS)
                xh = xv * r
                acc_ref[idx:idx + 1, ls] += jnp.sum(dyv * xh, axis=0, keepdims=True)
                dxh = dyv * gain[:, ls]
                mean = _exact_right(dxh * xh, bd) * (1.0 / SB_HEAD_DIM)
                o_ref[:, idx * d + grp * LANES:idx * d + (grp + 1) * LANES] = (r * (dxh - xh * mean)).astype(BF16)
        o_ref[:, 2 * d:3 * d] = dv_ref[...].astype(BF16)

    blk = lambda j: pl.BlockSpec((tm, d), lambda i, j=j: (i, j))
    row = pl.BlockSpec((1, d), lambda i: (0, 0))
    return pl.pallas_call(
        body, name=name,
        out_shape=(jax.ShapeDtypeStruct((s, 3 * d), BF16), jax.ShapeDtypeStruct((8, d), F32)),
        grid=(s // tm,),
        in_specs=[blk(0), blk(1), blk(0), blk(0), blk(0), row, row],
        out_specs=(pl.BlockSpec((tm, 3 * d), lambda i: (i, 0)), pl.BlockSpec((8, d), lambda i: (0, 0))),
        compiler_params=_cparams(1),
    )(qkv, qkv, dqn, dkn, dv, qn, kn)


def _sb_tile(qh, kb, suffix_ones, run, mask):
    z = _nt(qh, kb)
    neg_abs = lax.bitcast_convert_type(lax.bitcast_convert_type(z, jnp.uint32) | jnp.uint32(0x80000000), F32)
    l1m = -(jnp.maximum(z, 0.0) + jnp.log2(1.0 + jnp.exp2(neg_abs)))
    logb = z + l1m
    if mask is not None:
        l1m = jnp.where(mask, l1m, 0.0)
    later = _nn(l1m.astype(BF16), suffix_ones) + run
    wgt = jnp.exp2(logb + later)
    if mask is not None:
        wgt = jnp.where(mask, wgt, 0.0)
    return logb, l1m, wgt


def _suffix_ones(tk):
    row = lax.broadcasted_iota(jnp.int32, (tk, tk), 0)
    col = lax.broadcasted_iota(jnp.int32, (tk, tk), 1)
    return (row > col).astype(BF16)


def _sb_alive(runs):
    return jnp.max(jnp.maximum(runs[0], runs[1])) > -SB_DEAD


def _sb_mask(qi, j, tq, tk):
    qpos = qi * tq + lax.broadcasted_iota(jnp.int32, (tq, tk), 0)
    kpos = j * tk + lax.broadcasted_iota(jnp.int32, (tq, tk), 1)
    return kpos < qpos


def _sb_fwd(qn, kn, v, *, name, side=None):
    s, d = qn.shape
    tq, tk = _tile(SB_TQ, s), _tile(SB_TK, s)
    assert tk % tq == 0 or tq % tk == 0
    nq = s // tq

    def body(q_ref, k_ref, v_ref, o_ref, acc_ref):
        qi = pl.program_id(1)
        lane = lax.broadcasted_iota(jnp.int32, (tq, LANES), 1)
        first = lane < SB_HEAD_DIM
        q = q_ref[...]
        qh = [jnp.where(first, q, 0).astype(BF16), jnp.where(first, 0, q).astype(BF16)]
        ones = _suffix_ones(tk)
        acc_ref[...] = jnp.zeros_like(acc_ref)

        def tile(j, runs, masked):
            ks = pl.ds(pl.multiple_of(j * tk, tk), tk)
            kb = k_ref[ks, :]
            vb = v_ref[ks, :]
            mask = _sb_mask(qi, j, tq, tk) if masked else None
            new_runs = []
            for hh in range(2):
                _, l1m, wgt = _sb_tile(qh[hh], kb, ones, runs[hh], mask)
                acc_ref[hh] += _nn(wgt.astype(BF16), vb)
                new_runs.append(runs[hh] + jnp.sum(l1m, axis=1, keepdims=True))
            return tuple(new_runs)

        nfull = (qi * tq) // tk
        zero = jnp.zeros((tq, 1), F32)
        runs = (zero, zero)
        for m in reversed(range(max(tq // tk, 1))):
            runs = tile(nfull + m, runs, True)

        def step(c):
            it, _, r = c
            r = tile(nfull - 1 - it, r, False)
            return it + 1, _sb_alive(r), r

        lax.while_loop(lambda c: jnp.logical_and(c[0] < nfull, c[1]), step, (0, _sb_alive(runs), runs))
        o_ref[...] = jnp.where(first, acc_ref[0], acc_ref[1])

    res, side_res = _carry(
        body, side, name=name, out_shape=(jax.ShapeDtypeStruct((s, d), F32),), grid=(d // LANES, nq),
        in_specs=[pl.BlockSpec((tq, LANES), lambda p, i: (i, p)), pl.BlockSpec((s, LANES), lambda p, i: (0, p)),
                  pl.BlockSpec((s, LANES), lambda p, i: (0, p))],
        out_specs=(pl.BlockSpec((tq, LANES), lambda p, i: (i, p)),),
        scratch_shapes=[pltpu.VMEM((2, tq, LANES), F32)], args=(qn, kn, v))
    return res[0] if side is None else (res[0], side_res)


def _sb_bwd(qn, kn, v, o, do, *, name, side=None):
    s, d = qn.shape
    tq, tk = _tile(SB_TQ, s), _tile(SB_TK, s)
    assert tk % tq == 0 or tq % tk == 0
    nq = s // tq

    def body(q_ref, k_ref, v_ref, o_ref, do_ref, dq_ref, dk_ref, dv_ref, acc_ref):
        qi = pl.program_id(1)

        @pl.when(qi == 0)
        def _():
            dk_ref[...] = jnp.zeros_like(dk_ref)
            dv_ref[...] = jnp.zeros_like(dv_ref)

        first = lax.broadcasted_iota(jnp.int32, (tq, LANES), 1) < SB_HEAD_DIM
        sel = [first, jnp.logical_not(first)]
        kfirst = lax.broadcasted_iota(jnp.int32, (tk, LANES), 1) < SB_HEAD_DIM
        ksel = [kfirst, jnp.logical_not(kfirst)]
        q = q_ref[...]
        dob = do_ref[...].astype(BF16)
        qh = [jnp.where(sel[hh], q, 0).astype(BF16) for hh in range(2)]
        doh = [jnp.where(sel[hh], dob, 0).astype(BF16) for hh in range(2)]
        prod = dob.astype(F32) * o_ref[...]
        gtot = [jnp.sum(jnp.where(sel[hh], prod, 0.0), axis=1, keepdims=True) for hh in range(2)]
        ones = _suffix_ones(tk)
        acc_ref[...] = jnp.zeros_like(acc_ref)

        def tile(j, carry, masked):
            runs, gruns = carry
            ks = pl.ds(pl.multiple_of(j * tk, tk), tk)
            kb = k_ref[ks, :]
            vb = v_ref[ks, :]
            mask = _sb_mask(qi, j, tq, tk) if masked else None
            new_runs, new_gruns = [], []
            dk_add = jnp.zeros((tk, LANES), F32)
            dv_add = jnp.zeros((tk, LANES), F32)
            for hh in range(2):
                logb, l1m, wgt = _sb_tile(qh[hh], kb, ones, runs[hh], mask)
                wb = wgt.astype(BF16)
                g = _nt(doh[hh], vb) * wb.astype(F32)
                gsuf = _exact_right2(g, ones) + g + gruns[hh]
                dz = g - jnp.exp2(logb) * (g + (gtot[hh] - gsuf))
                if masked:
                    dz = jnp.where(mask, dz, 0.0)
                dzb = dz.astype(BF16)
                acc_ref[hh] += _nn(dzb, kb)
                dk_add = dk_add + jnp.where(ksel[hh], _tn(dzb, qh[hh]), 0.0)
                dv_add = dv_add + jnp.where(ksel[hh], _tn(wb, doh[hh]), 0.0)
                new_runs.append(runs[hh] + jnp.sum(l1m, axis=1, keepdims=True))
                new_gruns.append(gruns[hh] + jnp.sum(g, axis=1, keepdims=True))
            dk_ref[ks, :] += dk_add * LN2
            dv_ref[ks, :] += dv_add
            return tuple(new_runs), tuple(new_gruns)

        nfull = (qi * tq) // tk
        zero = jnp.zeros((tq, 1), F32)
        carry = ((zero, zero), (zero, zero))
        for m in reversed(range(max(tq // tk, 1))):
            carry = tile(nfull + m, carry, True)

        def step(c):
            it, _, cr = c
            cr = tile(nfull - 1 - it, cr, False)
            return it + 1, _sb_alive(cr[0]), cr

        lax.while_loop(lambda c: jnp.logical_and(c[0] < nfull, c[1]), step, (0, _sb_alive(carry[0]), carry))
        dq_ref[...] = jnp.where(first, acc_ref[0], acc_ref[1]) * SB_SCALE

    blk = pl.BlockSpec((tq, LANES), lambda p, i: (i, p))
    full = pl.BlockSpec((s, LANES), lambda p, i: (0, p))
    out = jax.ShapeDtypeStruct((s, d), F32)
    res, side_res = _carry(
        body, side, name=name, out_shape=(out, out, out), grid=(d // LANES, nq),
        in_specs=[blk, full, full, blk, blk], out_specs=(blk, full, full),
        scratch_shapes=[pltpu.VMEM((2, tq, LANES), F32)], args=(qn, kn, v, o, do))
    return res if side is None else (res, side_res)


def _mod_part(c_all, ada_w, ada_b_my, *, name):
    nl, d, ncol = ada_w.shape

    def body(c_ref, w_ref, b_ref, part_ref, ca_ref):
        cv = c_ref[...]
        ca = cv * jax.nn.sigmoid(cv)
        ca_ref[...] = ca
        part_ref[...] = _nn(ca.astype(BF16), w_ref[...].astype(BF16)) + b_ref[...]

    return pl.pallas_call(
        body, name=name,
        out_shape=(jax.ShapeDtypeStruct((nl, N_DEV, ncol), F32), jax.ShapeDtypeStruct((N_DEV, d), F32)),
        grid=(nl,),
        in_specs=[pl.BlockSpec((N_DEV, d), lambda l: (0, 0)), pl.BlockSpec((None, d, ncol), lambda l: (l, 0, 0)),
                  pl.BlockSpec((None, 1, ncol), lambda l: (l, 0, 0))],
        out_specs=(pl.BlockSpec((None, N_DEV, ncol), lambda l: (l, 0, 0)), pl.BlockSpec((N_DEV, d), lambda l: (0, 0))),
        compiler_params=_cparams(1),
    )(c_all, ada_w, ada_b_my)


PK_MOD, PK_NMIX, PK_NMLP, PK_HGN, PK_LB, PK_QN, PK_KN, PK_CONV, PK_ROWS = 0, 96, 112, 128, 132, 136, 144, 152, 168


def _small_grads(gath, ca_col, dmod_my, lbl4, *, name):
    def body(g_ref, ca_ref, dm_ref, lbl_ref, gw_ref, gsum_ref, glb_ref, gqk_ref):
        tot = g_ref[0]
        for dev in range(1, N_DEV):
            tot = tot + g_ref[dev]
        gsum_ref[...] = tot
        lv = lbl_ref[...]
        m = jnp.maximum(jnp.maximum(lv[0], lv[1]), lv[2])
        e = [jnp.exp(lv[k] - m) for k in range(3)]
        den = e[0] + e[1] + e[2]
        p = [ek / den for ek in e]
        dlb = tot[PK_LB:PK_LB + 4, :]
        glb_ref[0] = dlb * p[0] * (1.0 - p[0])
        glb_ref[1] = -dlb * p[0] * p[1]
        glb_ref[2] = -dlb * p[0] * p[2]
        for idx, base in enumerate((PK_QN, PK_KN)):
            rowsum = jnp.sum(tot[base:base + 8, :], axis=0, keepdims=True)
            gqk_ref[idx:idx + 1, :] = rowsum + pltpu.roll(rowsum, SB_HEAD_DIM, 1)
        for l in range(2):
            acc = ca_ref[0] * dm_ref[0, l:l + 1, :]
            for smp in range(1, N_DEV):
                acc = acc + ca_ref[smp] * dm_ref[smp, l:l + 1, :]
            gw_ref[l] = acc

    d, ncol = ca_col.shape[1], dmod_my.shape[2]
    vm = pl.BlockSpec(memory_space=pltpu.VMEM)
    return pl.pallas_call(
        body, name=name,
        out_shape=(jax.ShapeDtypeStruct((2, d, ncol), F32), jax.ShapeDtypeStruct((PK_ROWS, LANES), F32),
                   jax.ShapeDtypeStruct((3, 4, LANES), F32), jax.ShapeDtypeStruct((8, LANES), F32)),
        in_specs=[vm, vm, vm, vm], out_specs=(vm, vm, vm, vm),
        compiler_params=pltpu.CompilerParams(vmem_limit_bytes=VMEM_LIMIT),
    )(gath, ca_col, dmod_my, lbl4)


def _adamw_math(w, g, m, v):
    m = ADAM_B1 * m + (1.0 - ADAM_B1) * g
    v = ADAM_B2 * v + (1.0 - ADAM_B2) * (g * g)
    m_hat = m / (1.0 - ADAM_B1 ** ADAM_STEP)
    v_hat = v / (1.0 - ADAM_B2 ** ADAM_STEP)
    delta = -ADAM_LR * (m_hat / (jnp.sqrt(v_hat) + ADAM_EPS) + ADAM_WD * w)
    return delta, m, v


def _adamw(w, g, m, v, *, name, tr=256):
    r, n = w.shape
    tr = _tile(tr, r)

    def body(w_ref, g_ref, m_ref, v_ref, d_ref, mo_ref, vo_ref):
        dl, mn, vn = _adamw_math(w_ref[...], g_ref[...], m_ref[...], v_ref[...])
        d_ref[...] = dl
        mo_ref[...] = mn
        vo_ref[...] = vn

    blk = pl.BlockSpec((tr, n), lambda i: (i, 0))
    out = jax.ShapeDtypeStruct((r, n), F32)
    return pl.pallas_call(
        body, name=name, out_shape=(out, out, out), grid=(r // tr,),
        in_specs=[blk, blk, blk, blk], out_specs=(blk, blk, blk),
        compiler_params=_cparams(1),
    )(w, g, m, v)


def _adamw_small(items, *, name):
    n = len(items)

    def body(*refs):
        ins, outs = refs[:4 * n], refs[4 * n:]
        for k in range(n):
            dl, mn, vn = _adamw_math(*(r[...] for r in ins[4 * k:4 * k + 4]))
            outs[3 * k][...] = dl
            outs[3 * k + 1][...] = mn
            outs[3 * k + 2][...] = vn

    flat = [a for it in items for a in it]
    out_shape = tuple(jax.ShapeDtypeStruct(it[0].shape, F32) for it in items for _ in range(3))
    vm = pl.BlockSpec(memory_space=pltpu.VMEM)
    res = pl.pallas_call(
        body, name=name, out_shape=out_shape, in_specs=[vm] * (4 * n), out_specs=tuple([vm] * (3 * n)),
    )(*flat)
    return [tuple(res[3 * k:3 * k + 3]) for k in range(n)]


def _mlp_fwd(x, g, scale, shift, gate, w1g, w2g, tag, side_w1=None):
    h = _norm_mod(x, g, scale, shift, name=f"{tag}_norm")
    act = _matmul(h, w1g, b_kind="colblk", epi="relu2", out_dtype=BF16, name=f"{tag}_w1", side=side_w1)
    side_res = ()
    if side_w1 is not None:
        act, side_res = act
    z, x_out = _matmul(act, w2g, b_kind="rowblk", epi="resgate", extras=(x, gate), name=f"{tag}_w2")
    return x_out, (h, act, z), side_res


def _mlp_bwd(dz, dx_out, x, saved, g, scale, w1g, w2g, tag, gated, side_dact=None, make_side_dh=None):
    h, act, _ = saved
    du = _matmul(dz, w2g, tb=True, b_kind="rowblk", epi="dact", extras=(act,), out_dtype=BF16,
                 name=f"{tag}_dact", side=side_dact)
    res_dact = ()
    if side_dact is not None:
        du, res_dact = du
    dw2 = _matmul(act, dz, ta=True, name=f"{tag}_dw2")
    dw1_t = _matmul(du, h, ta=True, name=f"{tag}_dw1")
    side_dh = None if make_side_dh is None else make_side_dh(dw1_t, dw2)
    dh = _matmul(du, w1g, tb=True, b_kind="colblk", name=f"{tag}_dh", side=side_dh)
    res_dh = ()
    if side_dh is not None:
        dh, res_dh = dh
    dx, nacc, dz_mix = _norm_mod_bwd(x, dh, dx_out, g, scale, name=f"{tag}_norm_bwd", gated=gated)
    return dx, dw1_t, dw2, nacc, dz_mix, (res_dact, res_dh)


def kernel(x, c, ada_w, ada_b, norm_mix, norm_mlp, w_in_ab, conv_w, hg_norm, lb_logits, w_out_ab, w_qkv, q_norm, k_norm, w_out_c, mlp_w1, mlp_w2, loss_target, m_ada_w, m_ada_b, m_norm_mix, m_norm_mlp, m_w_in_ab, m_conv_w, m_hg_norm, m_lb_logits, m_w_out_ab, m_w_qkv, m_q_norm, m_k_norm, m_w_out_c, m_mlp_w1, m_mlp_w2, v_ada_w, v_ada_b, v_norm_mix, v_norm_mlp, v_w_in_ab, v_conv_w, v_hg_norm, v_lb_logits, v_w_out_ab, v_w_qkv, v_q_norm, v_k_norm, v_w_out_c, v_mlp_w1, v_mlp_w2):
    d = D_MODEL
    my_x, my_y, my_c = lax.axis_index("x"), lax.axis_index("y"), lax.axis_index("c")
    me = 4 * my_x + 2 * my_y + my_c
    xs = x[0]
    tgt = loss_target[0]

    def bf(w):
        return w.astype(BF16)

    (wing,) = _run_side(_gather_side([bf(w_in_ab)]), name="gather_w_in")
    win = wing[:, 0].transpose(1, 0, 2).reshape(d, AB_IN)

    ncv = CONV_DIM // N_DEV
    c_and_conv = jnp.concatenate([c, jnp.pad(conv_w[0], ((0, 0), (0, d - ncv))), jnp.zeros((4, d), F32)], axis=0)
    c_and_conv = _all_gather(c_and_conv, name="gather_c", in_vmem=True).reshape(N_DEV, 8, d)
    c_all = c_and_conv[:, 0]
    conv_full = c_and_conv[:, 1:4, :ncv].transpose(1, 0, 2).reshape(3, CONV_DIM)
    ncol = ada_w.shape[2]
    ada_b_my = lax.dynamic_slice(ada_b, (0, me * ncol), (2, ncol)).reshape(2, 1, ncol)
    part, c_act = _mod_part(c_all, ada_w, ada_b_my, name="mod_part")
    parts = _all_gather(part.reshape(2 * N_DEV, ncol), name="gather_mod", in_vmem=True)
    parts = parts.reshape(N_DEV, 2, N_DEV, ncol)
    mod = lax.dynamic_index_in_dim(parts, me, axis=2, keepdims=False)
    mod = mod.transpose(1, 0, 2).reshape(2, 6, 1, d)

    qn_t = jnp.tile(q_norm, (1, d // SB_HEAD_DIM))
    kn_t = jnp.tile(k_norm, (1, d // SB_HEAD_DIM))

    sh1, sc1, gt1, sh2, sc2, gt2 = [mod[0, k] for k in range(6)]
    h0 = _norm_mod(xs, norm_mix[0:1], sc1, sh1, name="l0_mix_norm")
    u = _matmul(h0, win, name="l0_in_proj")
    y_a = _conv_fwd(u, conv_full, name="l0_conv")
    (y_b, o_hg, sall), (woutg_ab, w1g0, w2g0) = _hgrn_fwd(
        u, lb_logits, hg_norm, name="l0_hgrn",
        side=_gather_side([bf(w_out_ab), bf(mlp_w1[0:1]), bf(mlp_w2[0:1])]))
    wout_ab = woutg_ab.reshape(d, d)
    y_ab = jnp.concatenate([y_a, y_b], axis=1)
    z0, x_mid0 = _matmul(y_ab, wout_ab, epi="resgate", extras=(xs, gt1), name="l0_out_proj")
    x1, mlp0, (wqkvg, woutg_c) = _mlp_fwd(x_mid0, norm_mlp[0:1], sc2, sh2, gt2, w1g0, w2g0, "l0_mlp",
                                          side_w1=_gather_side([bf(w_qkv), bf(w_out_c)]))
    wout_c = woutg_c.reshape(d, d)

    sh1b, sc1b, gt1b, sh2b, sc2b, gt2b = [mod[1, k] for k in range(6)]
    h1 = _norm_mod(x1, norm_mix[1:2], sc1b, sh1b, name="l1_mix_norm")
    qkv = _matmul(h1, wqkvg, b_kind="colblk", name="l1_qkv_proj")
    qn_a, kn_a, v_a = _qk_norm_fwd(qkv, qn_t, kn_t, name="l1_qk_norm")
    o_sb, (w1g1, w2g1) = _sb_fwd(qn_a, kn_a, v_a, name="l1_sb",
                                 side=_gather_side([bf(mlp_w1[1:2]), bf(mlp_w2[1:2])]))
    z1, x_mid1 = _matmul(o_sb, wout_c, epi="resgate", extras=(x1, gt1b), name="l1_out_proj")
    x2, mlp1, _ = _mlp_fwd(x_mid1, norm_mlp[1:2], sc2b, sh2b, gt2b, w1g1, w2g1, "l1_mlp")

    dx, loss_part, dz_mlp, dgt2b = _loss_grad(x2, tgt, mlp1[2], gt2b, name="loss")
    loss = lax.psum(loss_part[0, 0], MESH_AXES)

    my_q = 2 * my_x + my_y
    far_q = [my_q ^ 2, my_q ^ 1, my_q ^ 3]
    blk_ids = jnp.stack([2 * q + my_c for q in far_q] + far_q).astype(jnp.int32)
    my_ids = jnp.stack([me, my_q]).astype(jnp.int32)

    def blocks(g):
        return g.reshape(N_DEV, g.shape[0] // N_DEV, d)

    def by_rows(fn, tag, *lists):
        out = [None] * len(lists[0])
        heights = {}
        for t, g in enumerate(lists[0]):
            heights.setdefault(g.shape[1], []).append(t)
        for r, ts in heights.items():
            for t, v in zip(ts, fn(*[[lst[t] for t in ts] for lst in lists], name=f"{tag}_{r}")):
                out[t] = v
        return out

    def pair_sums(gs, sibs, tag):
        return by_rows(lambda a, b, name: _rs_pair_sum(a, b, blk_ids, name=name), f"rs_pair_sum_{tag}", gs, sibs)

    def final_sums(gs, sibs, fars, tag):
        return by_rows(lambda a, b, c_, name: _rs_final_sum(a, b, c_, my_ids, name=name),
                       f"rs_final_sum_{tag}", gs, sibs, fars)

    dx, dw1t_1, dw2_1, nacc, dyp, (_, sib1) = _mlp_bwd(
        dz_mlp, dx, x_mid1, mlp1, norm_mlp[1:2], sc2b, w1g1, w2g1, "l1_mlp", (z1, gt1b),
        make_side_dh=lambda a, b: _sibling_exchange_side([blocks(a), blocks(b)]))
    dsh2b, dsc2b, dnmlp1, dgt1b = nacc[0:1], nacc[1:2], nacc[2:3], nacc[3:4]
    g1 = [blocks(dw1t_1), blocks(dw2_1)]
    pair1 = pair_sums(g1, sib1, "g1")
    dwout_c = _matmul(o_sb, dyp, ta=True, name="l1_dwout")
    do_sb = _matmul(dyp, wout_c, tb=True, name="l1_do")
    (dqn_a, dkn_a, dv_a), far1 = _sb_bwd(qn_a, kn_a, v_a, o_sb, do_sb, name="l1_sb_bwd",
                                         side=_chip_exchange_side(pair1))
    gsh1 = final_sums(g1, sib1, far1, "g1")
    dqkv, qkacc = _qk_norm_bwd(qkv, dqn_a, dkn_a, dv_a, qn_t, kn_t, name="l1_qk_norm_bwd")
    dwqkv_t = _matmul(dqkv, h1, ta=True, name="l1_dwqkv")
    g2 = [blocks(dwqkv_t), blocks(dwout_c)]
    dh1, sib2 = _matmul(dqkv, wqkvg, tb=True, b_kind="colblk", name="l1_dh", side=_sibling_exchange_side(g2))
    pair2 = pair_sums(g2, sib2, "g2")
    dx, nacc, dz_mlp = _norm_mod_bwd(x1, dh1, dx, norm_mix[1:2], sc1b, name="l1_mix_norm_bwd",
                                     gated=(mlp0[2], gt2))
    dmod1 = [nacc[0:1], nacc[1:2], dgt1b, dsh2b, dsc2b, dgt2b]
    dnmix1, dgt2 = nacc[2:3], nacc[3:4]

    dx, dw1t_0, dw2_0, nacc, dyp, (far2, sib3) = _mlp_bwd(
        dz_mlp, dx, x_mid0, mlp0, norm_mlp[0:1], sc2, w1g0, w2g0, "l0_mlp", (z0, gt1),
        side_dact=_chip_exchange_side(pair2),
        make_side_dh=lambda a, b: _sibling_exchange_side([blocks(a), blocks(b)]))
    dsh2, dsc2, dnmlp0, dgt1 = nacc[0:1], nacc[1:2], nacc[2:3], nacc[3:4]
    gsh2 = final_sums(g2, sib2, far2, "g2")
    g3 = [blocks(dw1t_0), blocks(dw2_0)]
    pair3 = pair_sums(g3, sib3, "g3")
    dwout_ab = _matmul(y_ab, dyp, ta=True, name="l0_dwout")
    dy_ab = _matmul(dyp, wout_ab, tb=True, name="l0_dy")
    du_a, dconv = _conv_bwd(u, dy_ab, conv_full, name="l0_conv_bwd")
    (du_b, hgacc), far3 = _hgrn_bwd(u, o_hg, sall, dy_ab, lb_logits, hg_norm, name="l0_hgrn_bwd",
                                    side=_chip_exchange_side(pair3))
    gsh3 = final_sums(g3, sib3, far3, "g3")
    du = jnp.concatenate([du_a, du_b], axis=1)
    dwin_t = _matmul(du, h0, ta=True, name="l0_dwin")
    g4 = [blocks(dwin_t), blocks(dwout_ab)]
    dh0, sib4 = _matmul(du, win, tb=True, name="l0_dh", side=_sibling_exchange_side(g4))
    pair4 = pair_sums(g4, sib4, "g4")
    grad_x, nacc, far4 = _norm_mod_bwd(xs, dh0, dx, norm_mix[0:1], sc1, name="l0_mix_norm_bwd",
                                       side=_chip_exchange_side(pair4))
    gsh4 = final_sums(g4, sib4, far4, "g4")
    dmod0 = [nacc[0:1], nacc[1:2], dgt1, dsh2, dsc2, dgt2]
    dnmix0 = nacc[2:3]

    g_big = [gsh4[0].T[None], gsh4[1][None], gsh2[0].T[None], gsh2[1][None],
             jnp.stack([gsh3[0].T, gsh1[0].T]), jnp.stack([gsh3[1], gsh1[1]])]

    packed_small = jnp.concatenate(
        [jnp.concatenate(dmod0, axis=1).reshape(-1, LANES), jnp.concatenate(dmod1, axis=1).reshape(-1, LANES),
         dnmix0.reshape(-1, LANES), dnmix1.reshape(-1, LANES), dnmlp0.reshape(-1, LANES), dnmlp1.reshape(-1, LANES),
         hgacc[0:1].reshape(-1, LANES), hgacc[1:2].reshape(-1, LANES),
         qkacc[0:1].reshape(-1, LANES), qkacc[1:2].reshape(-1, LANES),
         dconv[0:3].reshape(-1, LANES), jnp.zeros((PK_ROWS - PK_CONV - 12, LANES), F32)], axis=0)
    gath = _all_gather(packed_small, name="gather_small_grads", in_vmem=True).reshape(N_DEV, PK_ROWS, LANES)
    dmod_all = gath[:, PK_MOD:PK_NMIX].reshape(N_DEV, 2, 6 * d)
    dmod_my = lax.dynamic_slice(dmod_all, (0, 0, me * ncol), (N_DEV, 2, ncol))
    g_ada_w, gsum, g_lb, g_qk = _small_grads(gath, c_act[:, :, None], dmod_my, lb_logits.reshape(3, 4, LANES),
                                             name="small_grads")
    g_ada_b = gsum[PK_MOD:PK_NMIX].reshape(2, 6 * d)
    g_norm_mix = gsum[PK_NMIX:PK_NMLP].reshape(2, d)
    g_norm_mlp = gsum[PK_NMLP:PK_HGN].reshape(2, d)
    g_hg_norm = gsum[PK_HGN:PK_LB].reshape(1, HG_WIDTH)
    g_lb_logits = g_lb.reshape(3, HG_WIDTH)
    g_q_norm = g_qk[0:1, :SB_HEAD_DIM]
    g_k_norm = g_qk[1:2, :SB_HEAD_DIM]
    g_conv_w = lax.dynamic_slice(gsum[PK_CONV:PK_CONV + 12].reshape(3, CONV_DIM), (0, me * ncv), (3, ncv))[None]

    def flat2(a):
        return a.reshape(-1, a.shape[-1])

    grads = dict(ada_w=g_ada_w, ada_b=g_ada_b, norm_mix=g_norm_mix, norm_mlp=g_norm_mlp, w_in_ab=g_big[0],
                 conv_w=g_conv_w, hg_norm=g_hg_norm, lb_logits=g_lb_logits, w_out_ab=g_big[1], w_qkv=g_big[2],
                 q_norm=g_q_norm, k_norm=g_k_norm, w_out_c=g_big[3], mlp_w1=g_big[4], mlp_w2=g_big[5])
    weights = dict(ada_w=(ada_w, m_ada_w, v_ada_w), ada_b=(ada_b, m_ada_b, v_ada_b),
                   norm_mix=(norm_mix, m_norm_mix, v_norm_mix), norm_mlp=(norm_mlp, m_norm_mlp, v_norm_mlp),
                   w_in_ab=(w_in_ab, m_w_in_ab, v_w_in_ab), conv_w=(conv_w, m_conv_w, v_conv_w),
                   hg_norm=(hg_norm, m_hg_norm, v_hg_norm), lb_logits=(lb_logits, m_lb_logits, v_lb_logits),
                   w_out_ab=(w_out_ab, m_w_out_ab, v_w_out_ab), w_qkv=(w_qkv, m_w_qkv, v_w_qkv),
                   q_norm=(q_norm, m_q_norm, v_q_norm), k_norm=(k_norm, m_k_norm, v_k_norm),
                   w_out_c=(w_out_c, m_w_out_c, v_w_out_c), mlp_w1=(mlp_w1, m_mlp_w1, v_mlp_w1),
                   mlp_w2=(mlp_w2, m_mlp_w2, v_mlp_w2))
    names = list(weights)
    small_names = ["ada_b", "norm_mix", "norm_mlp", "conv_w", "hg_norm", "lb_logits", "q_norm", "k_norm"]
    upd = {}
    small_items = []
    for n in small_names:
        wv, mv, vv = weights[n]
        small_items.append((flat2(wv), flat2(grads[n]), flat2(mv), flat2(vv)))
    for n, res in zip(small_names, _adamw_small(small_items, name="adamw_small")):
        upd[n] = tuple(r.reshape(weights[n][0].shape) for r in res)
    for n in names:
        if n in small_names:
            continue
        wv, mv, vv = weights[n]
        res = _adamw(flat2(wv), flat2(grads[n]), flat2(mv), flat2(vv), name=f"adamw_{n}")
        upd[n] = tuple(r.reshape(wv.shape) for r in res)

    return (loss, grad_x[None], *[grads[n].reshape(weights[n][0].shape) for n in names],
            *[upd[n][0] for n in names], *[upd[n][1] for n in names], *[upd[n][2] for n in names])
```

```python
import functools

import jax
import jax.numpy as jnp
from jax import lax
from jax.experimental import pallas as pl
from jax.experimental.pallas import tpu as pltpu

F32 = jnp.float32
BF16 = jnp.bfloat16
EPS = 1e-6
N_DEV = 8
MESH_AXES = ("x", "y", "c")

D_MODEL = 1024
CONV_DIM = 512
HG_HEADS = 4
HG_DK = 128
HG_WIDTH = 512
CHUNK = 64
HG_TILE = 128
HG_SUB = 16
HG_EXP_CLAMP = 60.0
SB_HEAD_DIM = 64
SB_SCALE = SB_HEAD_DIM ** -0.5
LOG2E = 1.4426950408889634
LN2 = 0.6931471805599453
SB_TQ = 512
SB_TK = 256
SB_DEAD = 150.0
D_FF = 4096
AB_IN = 3584

ADAM_LR = 0.001
ADAM_B1 = 0.9
ADAM_B2 = 0.999
ADAM_EPS = 1e-08
ADAM_WD = 0.01
ADAM_STEP = 10

VMEM_LIMIT = 48 * 1024 * 1024
LANES = 128


def _cparams(n_grid):
    return pltpu.CompilerParams(dimension_semantics=("arbitrary",) * n_grid, vmem_limit_bytes=VMEM_LIMIT)


def _nt(a, b):
    return lax.dot_general(a, b, (((1,), (1,)), ((), ())), preferred_element_type=F32)


def _tn(a, b):
    return lax.dot_general(a, b, (((0,), (0,)), ((), ())), preferred_element_type=F32)


def _nn(a, b):
    return jnp.dot(a, b, preferred_element_type=F32)


def _split3(x):
    hi = x.astype(BF16)
    r1 = x - hi.astype(F32)
    mid = r1.astype(BF16)
    lo = (r1 - mid.astype(F32)).astype(BF16)
    return hi, mid, lo


def _exact_left(m01, x):
    hi, mid, lo = _split3(x)
    return _nn(m01, hi) + _nn(m01, mid) + _nn(m01, lo)


def _exact_right(x, m01):
    hi, mid, lo = _split3(x)
    return _nn(hi, m01) + _nn(mid, m01) + _nn(lo, m01)


def _exact_right2(x, m01):
    hi = x.astype(BF16)
    lo = (x - hi.astype(F32)).astype(BF16)
    return _nn(hi, m01) + _nn(lo, m01)


def _sp(x):
    hi = x.astype(BF16)
    return hi, (x - hi.astype(F32)).astype(BF16)


def _dot3(fn, a, b):
    return fn(a[0], b[0]) + fn(a[0], b[1]) + fn(a[1], b[0])


def _tile(pref, n):
    t = min(pref, n)
    assert n % t == 0, (pref, n)
    return t


def _tile_rows(pref, n):
    for t in range(min(pref, n) - min(pref, n) % 16, 0, -16):
        if n % t == 0:
            return t
    raise ValueError((pref, n))


def _tile_lanes(pref, n):
    if n <= pref:
        return n
    for t in range(pref - pref % LANES, 0, -LANES):
        if n % t == 0:
            return t
    raise ValueError((pref, n))


def _all_gather(x, *, name, in_vmem):
    m_per, n = x.shape

    def body(x_ref, out_ref, send_sems, recv_sems, local_sem):
        mx, my, mc = lax.axis_index("x"), lax.axis_index("y"), lax.axis_index("c")
        me, sibling = (mx, my, mc), (mx, my, 1 - mc)
        chips = [(1 - mx, my), (mx, 1 - my), (1 - mx, 1 - my)]

        def rows(px, py, pc):
            return out_ref.at[pl.ds((4 * px + 2 * py + pc) * m_per, m_per), :]

        def copy(k, block, to, src=None):
            return pltpu.make_async_remote_copy(
                src_ref=rows(*block) if src is None else src, dst_ref=rows(*block),
                send_sem=send_sems.at[k], recv_sem=recv_sems.at[k],
                device_id=to, device_id_type=pl.DeviceIdType.MESH)

        mine = pltpu.make_async_copy(x_ref, rows(*me), local_sem)
        mine.start()
        first = [copy(0, me, sibling, src=x_ref)]
        first += [copy(1 + j, me, (*chip, mc), src=x_ref) for j, chip in enumerate(chips)]
        for cp in first:
            cp.start()
        passed = [copy(4 + j, (*chip, mc), sibling) for j, chip in enumerate(chips)]
        for j, chip in enumerate(chips):
            copy(1 + j, (*chip, mc), me).wait_recv()
            passed[j].start()
        copy(0, sibling, me).wait_recv()
        for j, chip in enumerate(chips):
            copy(4 + j, (*chip, 1 - mc), me).wait_recv()
        for cp in first + passed:
            cp.wait_send()
        mine.wait()

    space = pltpu.VMEM if in_vmem else pl.ANY
    return pl.pallas_call(
        body, name=name,
        out_shape=jax.ShapeDtypeStruct((N_DEV * m_per, n), x.dtype),
        in_specs=[pl.BlockSpec(memory_space=space)],
        out_specs=pl.BlockSpec(memory_space=space),
        scratch_shapes=[pltpu.SemaphoreType.DMA((7,)), pltpu.SemaphoreType.DMA((7,)), pltpu.SemaphoreType.DMA],
    )(x)


class _Side:
    def __init__(self, inputs, out_shape, scratch, start, finish):
        self.inputs, self.out_shape, self.scratch = list(inputs), tuple(out_shape), list(scratch)
        self.start, self.finish = start, finish


def _run_side(side, *, name):
    n_in, n_out = len(side.inputs), len(side.out_shape)

    def body(*refs):
        parts = (refs[:n_in], refs[n_in:n_in + n_out], refs[n_in + n_out:])
        side.start(*parts)
        side.finish(*parts)

    hbm = pl.BlockSpec(memory_space=pl.ANY)
    return pl.pallas_call(body, name=name, out_shape=side.out_shape, in_specs=[hbm] * n_in,
                          out_specs=tuple([hbm] * n_out), scratch_shapes=side.scratch)(*side.inputs)


def _carry(body, side, *, name, grid, in_specs, out_specs, out_shape, scratch_shapes, args):
    in_specs, out_specs, out_shape = list(in_specs), tuple(out_specs), tuple(out_shape)
    scratch_shapes = list(scratch_shapes)
    if side is None:
        res = pl.pallas_call(body, name=name, grid=grid, in_specs=in_specs, out_specs=out_specs,
                             out_shape=out_shape, scratch_shapes=scratch_shapes,
                             compiler_params=_cparams(len(grid)))(*args)
        return tuple(res), ()
    n_in, n_out, n_scr = len(in_specs), len(out_specs), len(scratch_shapes)
    s_in, s_out = len(side.inputs), len(side.out_shape)

    def wrapped(*refs):
        ins, rest = refs[:n_in], refs[n_in:]
        s_ins, rest = rest[:s_in], rest[s_in:]
        outs, rest = rest[:n_out], rest[n_out:]
        s_outs, rest = rest[:s_out], rest[s_out:]
        scr, s_scr = rest[:n_scr], rest[n_scr:]
        ids = [pl.program_id(ax) for ax in range(len(grid))]
        first = functools.reduce(jnp.logical_and, [i == 0 for i in ids])
        last = functools.reduce(jnp.logical_and, [i == g - 1 for i, g in zip(ids, grid)])

        @pl.when(first)
        def _():
            side.start(s_ins, s_outs, s_scr)

        body(*ins, *outs, *scr)

        @pl.when(last)
        def _():
            side.finish(s_ins, s_outs, s_scr)

    hbm = pl.BlockSpec(memory_space=pl.ANY)
    res = pl.pallas_call(
        wrapped, name=name, grid=grid, in_specs=in_specs + [hbm] * s_in,
        out_specs=out_specs + tuple([hbm] * s_out), out_shape=out_shape + side.out_shape,
        scratch_shapes=scratch_shapes + side.scratch, compiler_params=_cparams(len(grid)),
    )(*args, *side.inputs)
    return tuple(res[:n_out]), tuple(res[n_out:])


def _gather_side(xs):
    n = len(xs)

    def tools(x_refs, out_refs, sems):
        send_sems, recv_sems, local_sems = sems
        mx, my, mc = lax.axis_index("x"), lax.axis_index("y"), lax.axis_index("c")
        me, sibling = (mx, my, mc), (mx, my, 1 - mc)
        chips = [(1 - mx, my), (mx, 1 - my), (1 - mx, 1 - my)]

        def slot(t, px, py, pc):
            return out_refs[t].at[4 * px + 2 * py + pc]

        def copy(t, k, block, to, src=None):
            return pltpu.make_async_remote_copy(
                src_ref=slot(t, *block) if src is None else src, dst_ref=slot(t, *block),
                send_sem=send_sems.at[7 * t + k], recv_sem=recv_sems.at[7 * t + k],
                device_id=to, device_id_type=pl.DeviceIdType.MESH)

        mine = [pltpu.make_async_copy(x_refs[t], slot(t, *me), local_sems.at[t]) for t in range(n)]
        first = []
        for t in range(n):
            first.append(copy(t, 0, me, sibling, src=x_refs[t]))
            first += [copy(t, 1 + j, me, (*chip, mc), src=x_refs[t]) for j, chip in enumerate(chips)]
        return me, sibling, chips, mc, copy, mine, first

    def start(x_refs, out_refs, sems):
        *_, mine, first = tools(x_refs, out_refs, sems)
        for cp in mine + first:
            cp.start()

    def finish(x_refs, out_refs, sems):
        me, sibling, chips, mc, copy, mine, first = tools(x_refs, out_refs, sems)
        passed = []
        for j, chip in enumerate(chips):
            for t in range(n):
                copy(t, 1 + j, (*chip, mc), me).wait_recv()
                passed.append(copy(t, 4 + j, (*chip, mc), sibling))
                passed[-1].start()
        for t in range(n):
            copy(t, 0, sibling, me).wait_recv()
            for j, chip in enumerate(chips):
                copy(t, 4 + j, (*chip, 1 - mc), me).wait_recv()
        for cp in first + passed:
            cp.wait_send()
        for cp in mine:
            cp.wait()

    return _Side(xs, [jax.ShapeDtypeStruct((N_DEV,) + x.shape, x.dtype) for x in xs],
                 [pltpu.SemaphoreType.DMA((7 * n,)), pltpu.SemaphoreType.DMA((7 * n,)),
                  pltpu.SemaphoreType.DMA((n,))], start, finish)


def _sibling_exchange_side(gs):
    n = len(gs)

    def copies(g_refs, out_refs, sems):
        send_sems, recv_sems = sems
        mx, my, mc = lax.axis_index("x"), lax.axis_index("y"), lax.axis_index("c")
        return [pltpu.make_async_remote_copy(
            src_ref=g_refs[t].at[2 * q + (1 - mc)], dst_ref=out_refs[t].at[q],
            send_sem=send_sems.at[4 * t + q], recv_sem=recv_sems.at[4 * t + q],
            device_id=(mx, my, 1 - mc), device_id_type=pl.DeviceIdType.MESH)
            for t in range(n) for q in range(4)]

    def start(g_refs, out_refs, sems):
        for cp in copies(g_refs, out_refs, sems):
            cp.start()

    def finish(g_refs, out_refs, sems):
        cps = copies(g_refs, out_refs, sems)
        for cp in cps:
            cp.wait_recv()
        for cp in cps:
            cp.wait_send()

    return _Side(gs, [jax.ShapeDtypeStruct((4,) + g.shape[1:], g.dtype) for g in gs],
                 [pltpu.SemaphoreType.DMA((4 * n,)), pltpu.SemaphoreType.DMA((4 * n,))], start, finish)


def _chip_exchange_side(ts):
    n = len(ts)

    def copies(t_refs, out_refs, sems):
        send_sems, recv_sems = sems
        mx, my, mc = lax.axis_index("x"), lax.axis_index("y"), lax.axis_index("c")
        chips = [(1 - mx, my), (mx, 1 - my), (1 - mx, 1 - my)]
        return [pltpu.make_async_remote_copy(
            src_ref=t_refs[t].at[k], dst_ref=out_refs[t].at[k],
            send_sem=send_sems.at[3 * t + k], recv_sem=recv_sems.at[3 * t + k],
            device_id=(px, py, mc), device_id_type=pl.DeviceIdType.MESH)
            for t in range(n) for k, (px, py) in enumerate(chips)]

    def start(t_refs, out_refs, sems):
        for cp in copies(t_refs, out_refs, sems):
            cp.start()

    def finish(t_refs, out_refs, sems):
        cps = copies(t_refs, out_refs, sems)
        for cp in cps:
            cp.wait_recv()
        for cp in cps:
            cp.wait_send()

    return _Side(ts, [jax.ShapeDtypeStruct(t.shape, t.dtype) for t in ts],
                 [pltpu.SemaphoreType.DMA((3 * n,)), pltpu.SemaphoreType.DMA((3 * n,))], start, finish)


def _rs_pair_sum(gs, p1s, blk_ids, *, name, tr=256):
    n = len(gs)
    _, r, ncol = gs[0].shape
    tr = _tile_rows(tr, r)

    def body(id_ref, *refs):
        for t in range(n):
            refs[2 * n + t][...] = (refs[t][...] + refs[n + t][...]).astype(BF16)

    blk = lambda off: pl.BlockSpec((None, tr, ncol), lambda k, i, ids: (ids[off + k], i, 0))
    out = pl.BlockSpec((None, tr, ncol), lambda k, i, ids: (k, i, 0))
    return pl.pallas_call(
        body, name=name,
        out_shape=tuple(jax.ShapeDtypeStruct((3, r, ncol), BF16) for _ in gs),
        grid_spec=pltpu.PrefetchScalarGridSpec(
            num_scalar_prefetch=1, grid=(3, r // tr),
            in_specs=[blk(0)] * n + [blk(3)] * n, out_specs=tuple([out] * n)),
        compiler_params=_cparams(2),
    )(blk_ids, *gs, *p1s)


def _rs_final_sum(gs, p1s, p3s, my_ids, *, name, tr=256):
    n = len(gs)
    _, r, ncol = gs[0].shape
    tr = _tile_rows(tr, r)

    def body(id_ref, *refs):
        for t in range(n):
            g_ref, s_ref = refs[t], refs[n + t]
            a_ref, b_ref, c_ref = refs[2 * n + 3 * t:2 * n + 3 * t + 3]
            own = g_ref[...] + s_ref[...]
            refs[5 * n + t][...] = (((own + a_ref[...].astype(F32)) + b_ref[...].astype(F32))
                                    + c_ref[...].astype(F32))

    sel = lambda which: pl.BlockSpec((None, tr, ncol), lambda i, ids: (ids[which], i, 0))
    fix = lambda k: pl.BlockSpec((None, tr, ncol), lambda i, ids: (k, i, 0))
    p3_specs, p3_args = [], []
    for p3 in p3s:
        p3_specs += [fix(0), fix(1), fix(2)]
        p3_args += [p3, p3, p3]
    return pl.pallas_call(
        body, name=name,
        out_shape=tuple(jax.ShapeDtypeStruct((r, ncol), F32) for _ in gs),
        grid_spec=pltpu.PrefetchScalarGridSpec(
            num_scalar_prefetch=1, grid=(r // tr,),
            in_specs=[sel(0)] * n + [sel(1)] * n + p3_specs,
            out_specs=tuple([pl.BlockSpec((tr, ncol), lambda i, ids: (i, 0))] * n)),
        compiler_params=_cparams(1),
    )(my_ids, *gs, *p1s, *p3_args)


def _matmul(a, b, *, name, ta=False, tb=False, epi="plain", extras=(), out_dtype=F32, tm=None, tn=1024, tk=1024,
            b_kind=None, layer=0, side=None):
    if ta:
        kdim, m = a.shape
    else:
        m, kdim = a.shape
    pair = 1
    if b_kind is None:
        if tb:
            n, kb = b.shape
        else:
            kb, n = b.shape
        tn, tk = _tile_lanes(tn, n), _tile_lanes(tk, kb)
        b_spec = (pl.BlockSpec((tn, tk), lambda i, j, k: (j, k)) if tb
                  else pl.BlockSpec((tk, tn), lambda i, j, k: (k, j)))
    elif b_kind == "colblk":
        assert not ta
        _, _, kw, nsh = b.shape
        if tb:
            kb, n, pair = N_DEV * nsh, kw, 2
            tn, tk = _tile_lanes(tn, n), pair * nsh
            b_spec = pl.BlockSpec((pair, None, tn, nsh), lambda i, j, k: (k, layer, j, 0))
        else:
            kb, n, pair = kw, N_DEV * nsh, 2
            tn, tk = pair * nsh, _tile_lanes(tk, kb)
            b_spec = pl.BlockSpec((pair, None, tk, nsh), lambda i, j, k: (j, layer, k, 0))
    elif b_kind == "rowblk":
        assert not ta
        _, _, r, ncol = b.shape
        pair = 2
        if tb:
            kb, n = ncol, N_DEV * r
            tn, tk = pair * r, _tile_lanes(tk, kb)
            b_spec = pl.BlockSpec((pair, None, r, tk), lambda i, j, k: (j, layer, 0, k))
        else:
            kb, n = N_DEV * r, ncol
            tn, tk = _tile_lanes(tn, n), pair * r
            b_spec = pl.BlockSpec((pair, None, r, tn), lambda i, j, k: (k, layer, 0, j))
    else:
        raise ValueError(b_kind)
    assert kdim == kb, (a.shape, b.shape)
    if tm is None:
        tm = 1024 if (ta or epi == "resgate") else 2048
    tm = _tile_lanes(tm, m)
    nk = kdim // tk
    a_spec = pl.BlockSpec((tk, tm), lambda i, j, k: (k, i)) if ta else pl.BlockSpec((tm, tk), lambda i, j, k: (i, k))
    dims = (((0 if ta else 1,), (1 if tb else 0,)), ((), ()))
    mn_spec = pl.BlockSpec((tm, tn), lambda i, j, k: (i, j))
    row_spec = pl.BlockSpec((1, tn), lambda i, j, k: (0, j))
    if epi == "resgate":
        extra_specs = [mn_spec, row_spec]
        out_shape = (jax.ShapeDtypeStruct((m, n), BF16), jax.ShapeDtypeStruct((m, n), F32))
        out_specs = (mn_spec, mn_spec)
    elif epi == "dact":
        extra_specs = [mn_spec]
        out_shape = jax.ShapeDtypeStruct((m, n), out_dtype)
        out_specs = mn_spec
    else:
        extra_specs = []
        out_shape = jax.ShapeDtypeStruct((m, n), out_dtype)
        out_specs = mn_spec
    n_extra = len(extra_specs)

    def body(a_ref, b_ref, *rest):
        ex = rest[:n_extra]
        outs = rest[n_extra:n_extra + n_out]
        k = pl.program_id(2)

        def prod():
            av = a_ref[...].astype(BF16)
            if b_kind == "rowblk":
                bv = b_ref[...].astype(BF16)
                return lax.dot_general(av, bv.reshape(bv.shape[0] * bv.shape[1], bv.shape[2]), dims,
                                       preferred_element_type=F32)
            if b_kind == "colblk" and tb:
                nsh = b_ref.shape[-1]
                return sum(lax.dot_general(av[:, p * nsh:(p + 1) * nsh], b_ref[p].astype(BF16), dims,
                                           preferred_element_type=F32) for p in range(pair))
            if b_kind == "colblk":
                return jnp.concatenate([lax.dot_general(av, b_ref[p].astype(BF16), dims, preferred_element_type=F32)
                                        for p in range(pair)], axis=1)
            return lax.dot_general(av, b_ref[...].astype(BF16), dims, preferred_element_type=F32)

        def finish(r):
            if epi == "plain":
                outs[0][...] = r.astype(outs[0].dtype)
            elif epi == "resgate":
                outs[0][...] = r.astype(BF16)
                outs[1][...] = ex[0][...] + ex[1][...] * r
            elif epi == "relu2":
                p = jnp.maximum(r, 0.0)
                outs[0][...] = (p * p).astype(outs[0].dtype)
            elif epi == "dact":
                outs[0][...] = (r * (2.0 * jnp.sqrt(ex[0][...].astype(F32)))).astype(outs[0].dtype)

        if nk == 1:
            finish(prod())
        else:
            acc = rest[-1]

            @pl.when(k == 0)
            def _():
                acc[...] = prod()

            if nk > 2:
                @pl.when(jnp.logical_and(k > 0, k < nk - 1))
                def _():
                    acc[...] += prod()

            @pl.when(k == nk - 1)
            def _():
                finish(acc[...] + prod())

    n_out = 2 if epi == "resgate" else 1
    if n_out == 1:
        out_shape, out_specs = (out_shape,), (out_specs,)
    res, side_res = _carry(
        body, side, name=name, grid=(m // tm, n // tn, nk), in_specs=[a_spec, b_spec] + extra_specs,
        out_specs=out_specs, out_shape=out_shape,
        scratch_shapes=[pltpu.VMEM((tm, tn), F32)] if nk > 1 else [], args=(a, b, *extras))
    res = res if n_out == 2 else res[0]
    return res if side is None else (res, side_res)


def _norm_mod(x, g, scale, shift, *, name, tm=512):
    s, d = x.shape
    tm = _tile(tm, s)

    def body(x_ref, g_ref, sc_ref, sh_ref, h_ref):
        xv = x_ref[...]
        r = lax.rsqrt(jnp.mean(xv * xv, axis=-1, keepdims=True) + EPS)
        h_ref[...] = (((xv * r) * g_ref[...]) * (1.0 + sc_ref[...]) + sh_ref[...]).astype(BF16)

    row = pl.BlockSpec((1, d), lambda i: (0, 0))
    return pl.pallas_call(
        body, name=name, out_shape=jax.ShapeDtypeStruct((s, d), BF16), grid=(s // tm,),
        in_specs=[pl.BlockSpec((tm, d), lambda i: (i, 0)), row, row, row],
        out_specs=pl.BlockSpec((tm, d), lambda i: (i, 0)),
        compiler_params=_cparams(1),
    )(x, g, scale, shift)


def _norm_mod_bwd(x, dh, dres, g, scale, *, name, tm=512, gated=None, side=None):
    s, d = x.shape
    tm = _tile(tm, s)
    n_in = 5 if gated is None else 7

    def body(*refs):
        x_ref, dh_ref, dr_ref, g_ref, sc_ref = refs[:5]
        dx_ref, acc_ref = refs[n_in:n_in + 2]
        i = pl.program_id(0)

        @pl.when(i == 0)
        def _():
            acc_ref[...] = jnp.zeros_like(acc_ref)

        xv = x_ref[...]
        dhv = dh_ref[...]
        gv = g_ref[...]
        one_sc = 1.0 + sc_ref[...]
        r = lax.rsqrt(jnp.mean(xv * xv, axis=-1, keepdims=True) + EPS)
        xn = xv * r
        dxn = dhv * (gv * one_sc)
        dxv = dr_ref[...] + r * (dxn - xn * jnp.mean(dxn * xn, axis=-1, keepdims=True))
        dx_ref[...] = dxv
        dhxn = dhv * xn
        acc_ref[0:1, :] += jnp.sum(dhv, axis=0, keepdims=True)
        acc_ref[1:2, :] += jnp.sum(dhxn * gv, axis=0, keepdims=True)
        acc_ref[2:3, :] += jnp.sum(dhxn * one_sc, axis=0, keepdims=True)
        if gated is not None:
            z_ref, gate_ref, dz_ref = refs[5], refs[6], refs[n_in + 2]
            dz_ref[...] = (dxv * gate_ref[...]).astype(BF16)
            acc_ref[3:4, :] += jnp.sum(dxv * z_ref[...], axis=0, keepdims=True)

    row = pl.BlockSpec((1, d), lambda i: (0, 0))
    blk = pl.BlockSpec((tm, d), lambda i: (i, 0))
    in_specs, args = [blk, blk, blk, row, row], [x, dh, dres, g, scale]
    out_shape = [jax.ShapeDtypeStruct((s, d), F32), jax.ShapeDtypeStruct((8, d), F32)]
    out_specs = [blk, pl.BlockSpec((8, d), lambda i: (0, 0))]
    if gated is not None:
        in_specs += [blk, row]
        args += list(gated)
        out_shape.append(jax.ShapeDtypeStruct((s, d), BF16))
        out_specs.append(blk)
    res, side_res = _carry(body, side, name=name, grid=(s // tm,), in_specs=in_specs, out_specs=out_specs,
                           out_shape=out_shape, scratch_shapes=[], args=args)
    return res if side is None else res + (side_res,)


def _loss_grad(xf, target, z, gate, *, name, tm=512):
    s, d = xf.shape
    tm = _tile(tm, s)
    nt = s // tm

    def body(x_ref, t_ref, z_ref, gate_ref, dx_ref, loss_ref, dz_ref, dgate_ref, acc_ref):
        i = pl.program_id(0)

        @pl.when(i == 0)
        def _():
            acc_ref[...] = jnp.zeros_like(acc_ref)
            dgate_ref[...] = jnp.zeros_like(dgate_ref)

        e = x_ref[...] - t_ref[...]
        dxv = e * (1.0 / d)
        dx_ref[...] = dxv
        dz_ref[...] = (dxv * gate_ref[...]).astype(BF16)
        dgate_ref[...] += jnp.sum(dxv * z_ref[...], axis=0, keepdims=True)
        acc_ref[...] += jnp.sum(e * e, axis=0, keepdims=True)

        @pl.when(i == nt - 1)
        def _():
            loss_ref[...] = (0.5 / d) * jnp.sum(acc_ref[...], axis=1, keepdims=True)

    blk = pl.BlockSpec((tm, d), lambda i: (i, 0))
    row = pl.BlockSpec((1, d), lambda i: (0, 0))
    return pl.pallas_call(
        body, name=name,
        out_shape=(jax.ShapeDtypeStruct((s, d), F32), jax.ShapeDtypeStruct((1, 1), F32),
                   jax.ShapeDtypeStruct((s, d), BF16), jax.ShapeDtypeStruct((1, d), F32)),
        grid=(nt,), in_specs=[blk, blk, blk, row],
        out_specs=(blk, pl.BlockSpec((1, 1), lambda i: (0, 0)), blk, row),
        scratch_shapes=[pltpu.VMEM((1, d), F32)],
        compiler_params=_cparams(1),
    )(xf, target, z, gate)


def _shift_down(p, prev, k):
    tm = p.shape[0]
    row = lax.broadcasted_iota(jnp.int32, p.shape, 0)
    out = pltpu.roll(p, k, 0)
    for j in range(k):
        out = jnp.where(row == j, prev[8 - k + j:8 - k + j + 1, :], out)
    return out


def _shift_up(p, nxt, k):
    tm = p.shape[0]
    row = lax.broadcasted_iota(jnp.int32, p.shape, 0)
    out = pltpu.roll(p, tm - k, 0)
    for j in range(k):
        out = jnp.where(row == tm - k + j, nxt[j:j + 1, :], out)
    return out


def _conv_fwd(u, w, *, name, tm=512):
    s = u.shape[0]
    tm = _tile(tm, s)
    c = CONV_DIM

    def body(ab_ref, ac_ref, ah_ref, w_ref, y_ref, carry_ref):
        i = pl.program_id(0)

        @pl.when(i == 0)
        def _():
            carry_ref[...] = jnp.zeros_like(carry_ref)

        p = ac_ref[...] * ah_ref[...]
        prev = carry_ref[...]
        wv = w_ref[...]
        conv = wv[2:3, :] * p + wv[1:2, :] * _shift_down(p, prev, 1) + wv[0:1, :] * _shift_down(p, prev, 2)
        y_ref[...] = (ab_ref[...] * conv).astype(BF16)
        carry_ref[...] = p[tm - 8:tm, :]

    return pl.pallas_call(
        body, name=name, out_shape=jax.ShapeDtypeStruct((s, c), BF16), grid=(s // tm,),
        in_specs=[pl.BlockSpec((tm, c), lambda i: (i, 0)), pl.BlockSpec((tm, c), lambda i: (i, 1)),
                  pl.BlockSpec((tm, c), lambda i: (i, 2)), pl.BlockSpec((3, c), lambda i: (0, 0))],
        out_specs=pl.BlockSpec((tm, c), lambda i: (i, 0)),
        scratch_shapes=[pltpu.VMEM((8, c), F32)],
        compiler_params=_cparams(1),
    )(u, u, u, w)


def _conv_bwd(u, dy, w, *, name, tm=512):
    s = u.shape[0]
    tm = _tile(tm, s)
    nt = s // tm
    c = CONV_DIM
    hb = tm // 8

    def body(ab_ref, ac_ref, ah_ref, hc_ref, hh_ref, dy_ref, w_ref, du_ref, dw_ref, carry_ref):
        i = pl.program_id(0)

        @pl.when(i == 0)
        def _():
            carry_ref[...] = jnp.zeros_like(carry_ref)
            dw_ref[...] = jnp.zeros_like(dw_ref)

        first_tile = (nt - 1 - i) == 0
        ab, ac, ah = ab_ref[...], ac_ref[...], ah_ref[...]
        p = ac * ah
        prev = jnp.where(first_tile, 0.0, hc_ref[...] * hh_ref[...])
        wv = w_ref[...]
        p1 = _shift_down(p, prev, 1)
        p2 = _shift_down(p, prev, 2)
        conv = wv[2:3, :] * p + wv[1:2, :] * p1 + wv[0:1, :] * p2
        dyv = dy_ref[...]
        dconv = dyv * ab
        nxt = carry_ref[...]
        dp = wv[2:3, :] * dconv + wv[1:2, :] * _shift_up(dconv, nxt, 1) + wv[0:1, :] * _shift_up(dconv, nxt, 2)
        du_ref[:, 0:c] = (dyv * conv).astype(BF16)
        du_ref[:, c:2 * c] = (dp * ah).astype(BF16)
        du_ref[:, 2 * c:3 * c] = (dp * ac).astype(BF16)
        dw_ref[0:1, :] += jnp.sum(dconv * p2, axis=0, keepdims=True)
        dw_ref[1:2, :] += jnp.sum(dconv * p1, axis=0, keepdims=True)
        dw_ref[2:3, :] += jnp.sum(dconv * p, axis=0, keepdims=True)
        carry_ref[...] = dconv[0:8, :]

    rev = lambda i: nt - 1 - i
    halo = lambda i: jnp.maximum(rev(i) * hb - 1, 0)
    return pl.pallas_call(
        body, name=name,
        out_shape=(jax.ShapeDtypeStruct((s, 3 * c), BF16), jax.ShapeDtypeStruct((8, c), F32)),
        grid=(nt,),
        in_specs=[pl.BlockSpec((tm, c), lambda i: (rev(i), 0)), pl.BlockSpec((tm, c), lambda i: (rev(i), 1)),
                  pl.BlockSpec((tm, c), lambda i: (rev(i), 2)),
                  pl.BlockSpec((8, c), lambda i: (halo(i), 1)), pl.BlockSpec((8, c), lambda i: (halo(i), 2)),
                  pl.BlockSpec((tm, c), lambda i: (rev(i), 0)), pl.BlockSpec((3, c), lambda i: (0, 0))],
        out_specs=(pl.BlockSpec((tm, 3 * c), lambda i: (rev(i), 0)), pl.BlockSpec((8, c), lambda i: (0, 0))),
        scratch_shapes=[pltpu.VMEM((8, c), F32)],
        compiler_params=_cparams(1),
    )(u, u, u, u, u, dy, w)


def _lower_bound(lbl):
    m = jnp.max(lbl, axis=0, keepdims=True)
    e = jnp.exp(lbl - m)
    return e[0:1, :] / jnp.sum(e, axis=0, keepdims=True)


def _hg_masks():
    t = HG_TILE
    row = lax.broadcasted_iota(jnp.int32, (t, t), 0)
    col = lax.broadcasted_iota(jnp.int32, (t, t), 1)
    same = (row >= CHUNK) == (col >= CHUNK)
    lower = same & (col <= row)
    upper = same & (row <= col)
    return row, col, lower, upper


def _hg_gates(hf, lb):
    sig = jax.nn.sigmoid(hf)
    f = lb + (1.0 - lb) * sig
    return sig, f, jnp.log(f), 1.0 - f


def _hg_refs(b_ref, hs):
    refs = []
    for i in range(HG_TILE // HG_SUB):
        if (i * HG_SUB) % CHUNK == 0:
            refs.append(jnp.zeros((1, HG_DK), F32))
        else:
            refs.append(b_ref[i * HG_SUB - 1:i * HG_SUB, hs])
    return refs


def _hgrn_fwd(u, lbl, gn, *, name, side=None):
    s = u.shape[0]
    t = HG_TILE
    nt = s // t
    nsub = t // HG_SUB
    w = HG_WIDTH

    def body(hq_ref, hf_ref, hi_ref, hg_ref, lbl_ref, gn_ref, y_ref, o_ref, sall_ref, st_ref, b_ref):
        i = pl.program_id(0)

        @pl.when(i == 0)
        def _():
            st_ref[...] = jnp.zeros_like(st_ref)

        lb = _lower_bound(lbl_ref[...])
        _, _, g, kin = _hg_gates(hf_ref[...], lb)
        _, _, lower, _ = _hg_masks()
        b_ref[...] = _exact_left(lower.astype(BF16), g)

        for h in range(HG_HEADS):
            hs = slice(h * HG_DK, (h + 1) * HG_DK)
            bh = b_ref[:, hs]
            qh = hq_ref[:, hs]
            kh = kin[:, hs]
            vb = hi_ref[:, hs].astype(BF16)
            refs = _hg_refs(b_ref, hs)
            rmat = jnp.concatenate([jnp.broadcast_to(r, (HG_SUB, HG_DK)) for r in refs], axis=0)
            qt = (qh * jnp.exp(bh - rmat)).astype(BF16)
            prow = []
            for j in range(nsub):
                kj = kh * jnp.exp(jnp.minimum(refs[j] - bh, HG_EXP_CLAMP))
                prow.append(_nt(qt[j * HG_SUB:(j + 1) * HG_SUB], kj.astype(BF16)))
            p = jnp.where(lower, jnp.concatenate(prow, axis=0), 0.0)
            intra = _nn(p.astype(BF16), vb)
            o_parts = []
            for c in range(t // CHUNK):
                rs = slice(c * CHUNK, (c + 1) * CHUNK)
                st0 = st_ref[hs, :]
                sall_ref[c * w + h * HG_DK:c * w + (h + 1) * HG_DK, :] = st0
                bl = b_ref[c * CHUNK + CHUNK - 1:c * CHUNK + CHUNK, hs]
                qf = qh[rs] * jnp.exp(bh[rs])
                o_parts.append(_nt(qf.astype(BF16), st0.astype(BF16)) + intra[rs])
                khat = kh[rs] * jnp.exp(bl - bh[rs])
                st_ref[hs, :] = st0 * jnp.exp(bl) + _tn(vb[rs], khat.astype(BF16))
            o = jnp.concatenate(o_parts, axis=0)
            o_ref[:, hs] = o
            r = lax.rsqrt(jnp.mean(o * o, axis=-1, keepdims=True) + EPS)
            hg = hg_ref[:, hs]
            y_ref[:, hs] = (((o * r) * gn_ref[:, hs]) * (hg * jax.nn.sigmoid(hg))).astype(BF16)

    blk = lambda j: pl.BlockSpec((t, w), lambda i, j=j: (i, j))
    srows = (t // CHUNK) * w
    res, side_res = _carry(
        body, side, name=name,
        out_shape=(jax.ShapeDtypeStruct((s, w), BF16), jax.ShapeDtypeStruct((s, w), F32),
                   jax.ShapeDtypeStruct((nt * srows, HG_DK), F32)),
        grid=(nt,),
        in_specs=[blk(3), blk(4), blk(5), blk(6), pl.BlockSpec((3, w), lambda i: (0, 0)),
                  pl.BlockSpec((1, w), lambda i: (0, 0))],
        out_specs=(pl.BlockSpec((t, w), lambda i: (i, 0)), pl.BlockSpec((t, w), lambda i: (i, 0)),
                   pl.BlockSpec((srows, HG_DK), lambda i: (i, 0))),
        scratch_shapes=[pltpu.VMEM((w, HG_DK), F32), pltpu.VMEM((t, w), F32)],
        args=(u, u, u, u, lbl, gn))
    return res if side is None else (res, side_res)


def _hgrn_bwd(u, o_all, sall, dy, lbl, gn, *, name, side=None):
    s = u.shape[0]
    t = HG_TILE
    nt = s // t
    nsub = t // HG_SUB
    w = HG_WIDTH
    nch = t // CHUNK

    def body(hq_ref, hf_ref, hi_ref, hg_ref, o_ref, sall_ref, dy_ref, lbl_ref, gn_ref,
             du_ref, acc_ref, dst_ref, b_ref):
        i = pl.program_id(0)

        @pl.when(i == 0)
        def _():
            dst_ref[...] = jnp.zeros_like(dst_ref)
            acc_ref[...] = jnp.zeros_like(acc_ref)

        lb = _lower_bound(lbl_ref[...])
        sig, f, g, kin = _hg_gates(hf_ref[...], lb)
        row, col, lower, upper = _hg_masks()
        b_ref[...] = _exact_left(lower.astype(BF16), g)
        upper_bf = upper.astype(BF16)
        rowblk = [((row >= j * HG_SUB) & (row < (j + 1) * HG_SUB)) for j in range(nsub)]
        colblk = [((col >= j * HG_SUB) & (col < (j + 1) * HG_SUB)) for j in range(nsub)]
        row1 = lax.broadcasted_iota(jnp.int32, (t, HG_DK), 0)

        for h in range(HG_HEADS):
            hs = slice(h * HG_DK, (h + 1) * HG_DK)
            bh = b_ref[:, hs]
            qh = hq_ref[:, hs]
            kh = kin[:, hs]
            vh = hi_ref[:, hs]
            vsp = _sp(vh)
            hg = hg_ref[:, hs]
            gnh = gn_ref[:, hs]
            o = o_ref[:, hs]
            dyv = dy_ref[:, hs]
            sg = jax.nn.sigmoid(hg)
            r = lax.rsqrt(jnp.mean(o * o, axis=-1, keepdims=True) + EPS)
            ohat = o * r
            du_ref[:, 3 * w + h * HG_DK:3 * w + (h + 1) * HG_DK] = (
                dyv * (ohat * gnh) * (sg * (1.0 + hg * (1.0 - sg)))).astype(BF16)
            don = dyv * (hg * sg)
            acc_ref[0:1, hs] += jnp.sum(don * ohat, axis=0, keepdims=True)
            dohat = don * gnh
            do = r * (dohat - ohat * jnp.mean(dohat * ohat, axis=-1, keepdims=True))
            dosp = _sp(do)
            refs = _hg_refs(b_ref, hs)
            rmat = jnp.concatenate([jnp.broadcast_to(rr, (HG_SUB, HG_DK)) for rr in refs], axis=0)
            eq = jnp.exp(bh - rmat)
            qt = qh * eq
            qtsp = _sp(qt)
            dp = jnp.where(lower, _dot3(_nt, dosp, vsp), 0.0)
            dpt = jnp.where(upper, _dot3(_nt, vsp, dosp), 0.0)
            pt = jnp.zeros((t, t), F32)
            dk = jnp.zeros((t, HG_DK), F32)
            dq_rows = []
            for j in range(nsub):
                ek = jnp.exp(jnp.minimum(refs[j] - bh, HG_EXP_CLAMP))
                kjsp = _sp(kh * ek)
                pt = pt + _nt(kjsp[0], jnp.where(rowblk[j], qtsp[0], 0))
                dq_rows.append(_dot3(_nn, _sp(dp[j * HG_SUB:(j + 1) * HG_SUB]), kjsp))
                dk = dk + ek * _dot3(_nn, _sp(jnp.where(colblk[j], dpt, 0.0)), qtsp)
            pt = jnp.where(upper, pt, 0.0)
            dv = _nn(pt.astype(BF16), dosp[0])
            dq = jnp.concatenate(dq_rows, axis=0) * eq
            dq_c, dk_c, dv_c, ex_c = [None] * nch, [None] * nch, [None] * nch, [None] * nch
            for c in reversed(range(nch)):
                rs = slice(c * CHUNK, (c + 1) * CHUNK)
                st0 = sall_ref[c * w + h * HG_DK:c * w + (h + 1) * HG_DK, :]
                dst1 = dst_ref[hs, :]
                dst1sp = _sp(dst1)
                dosp_c = _sp(do[rs])
                bl = b_ref[c * CHUNK + CHUNK - 1:c * CHUNK + CHUNK, hs]
                e = jnp.exp(bh[rs])
                el = jnp.exp(bl)
                ekl = jnp.exp(bl - bh[rs])
                dq_c[c] = _dot3(_nn, dosp_c, _sp(st0)) * e
                khat = kh[rs] * ekl
                dv_c[c] = _nt(khat.astype(BF16), dst1sp[0])
                dkhat = _dot3(_nn, _sp(vh[rs]), dst1sp)
                dk_c[c] = dkhat * ekl
                ex_c[c] = (jnp.sum(dkhat * khat, axis=0, keepdims=True)
                           + el * jnp.sum(dst1 * st0, axis=0, keepdims=True))
                dst_ref[hs, :] = _tn(dosp_c[0], (qh[rs] * e).astype(BF16)) + dst1 * el
            dq = dq + jnp.concatenate(dq_c, axis=0)
            dk = dk + jnp.concatenate(dk_c, axis=0)
            dv = dv + jnp.concatenate(dv_c, axis=0)
            db = qh * dq - kh * dk
            for c in range(nch):
                db = db + jnp.where(row1 == c * CHUNK + CHUNK - 1, ex_c[c], 0.0)
            dg = _exact_left(upper_bf, db)
            fh = f[:, hs]
            sgf = sig[:, hs]
            lbh = lb[:, hs]
            df = dg / fh - dk
            du_ref[:, hs] = dq.astype(BF16)
            du_ref[:, w + h * HG_DK:w + (h + 1) * HG_DK] = (df * (1.0 - lbh) * sgf * (1.0 - sgf)).astype(BF16)
            du_ref[:, 2 * w + h * HG_DK:2 * w + (h + 1) * HG_DK] = dv.astype(BF16)
            acc_ref[1:2, hs] += jnp.sum(df * (1.0 - sgf), axis=0, keepdims=True)

    rev = lambda i: nt - 1 - i
    blk = lambda j: pl.BlockSpec((t, w), lambda i, j=j: (rev(i), j))
    srows = nch * w
    res, side_res = _carry(
        body, side, name=name,
        out_shape=(jax.ShapeDtypeStruct((s, 4 * w), BF16), jax.ShapeDtypeStruct((8, w), F32)),
        grid=(nt,),
        in_specs=[blk(3), blk(4), blk(5), blk(6), pl.BlockSpec((t, w), lambda i: (rev(i), 0)),
                  pl.BlockSpec((srows, HG_DK), lambda i: (rev(i), 0)),
                  pl.BlockSpec((t, w), lambda i: (rev(i), 1)),
                  pl.BlockSpec((3, w), lambda i: (0, 0)), pl.BlockSpec((1, w), lambda i: (0, 0))],
        out_specs=(pl.BlockSpec((t, 4 * w), lambda i: (rev(i), 0)), pl.BlockSpec((8, w), lambda i: (0, 0))),
        scratch_shapes=[pltpu.VMEM((w, HG_DK), F32), pltpu.VMEM((t, w), F32)],
        args=(u, u, u, u, o_all, sall, dy, lbl, gn))
    return res if side is None else (res, side_res)


def _pair_matrix():
    row = lax.broadcasted_iota(jnp.int32, (LANES, LANES), 0)
    col = lax.broadcasted_iota(jnp.int32, (LANES, LANES), 1)
    return ((row >= SB_HEAD_DIM) == (col >= SB_HEAD_DIM)).astype(BF16)


def _qk_norm_fwd(qkv, qn, kn, *, name, tm=256):
    s = qkv.shape[0]
    d = D_MODEL
    tm = _tile(tm, s)

    def body(q_ref, k_ref, v_ref, qn_ref, kn_ref, qo_ref, ko_ref, vo_ref):
        bd = _pair_matrix()
        for src, gain, dst, fac in ((q_ref, qn_ref, qo_ref, SB_SCALE * LOG2E), (k_ref, kn_ref, ko_ref, None)):
            for grp in range(d // LANES):
                ls = slice(grp * LANES, (grp + 1) * LANES)
                xv = src[:, ls]
                ms = _exact_right(xv * xv, bd) * (1.0 / SB_HEAD_DIM)
                y = (xv * lax.rsqrt(ms + EPS)) * gain[:, ls]
                dst[:, ls] = (y if fac is None else y * fac).astype(BF16)
        vo_ref[...] = v_ref[...].astype(BF16)

    blk = lambda j: pl.BlockSpec((tm, d), lambda i, j=j: (i, j))
    row = pl.BlockSpec((1, d), lambda i: (0, 0))
    out = jax.ShapeDtypeStruct((s, d), BF16)
    return pl.pallas_call(
        body, name=name, out_shape=(out, out, out), grid=(s // tm,),
        in_specs=[blk(0), blk(1), blk(2), row, row],
        out_specs=(blk(0), blk(0), blk(0)),
        compiler_params=_cparams(1),
    )(qkv, qkv, qkv, qn, kn)


def _qk_norm_bwd(qkv, dqn, dkn, dv, qn, kn, *, name, tm=256):
    s = qkv.shape[0]
    d = D_MODEL
    tm = _tile(tm, s)

    def body(q_ref, k_ref, dq_ref, dk_ref, dv_ref, qn_ref, kn_ref, o_ref, acc_ref):
        i = pl.program_id(0)

        @pl.when(i == 0)
        def _():
            acc_ref[...] = jnp.zeros_like(acc_ref)

        bd = _pair_matrix()
        for idx, (src, dsrc, gain) in enumerate(((q_ref, dq_ref, qn_ref), (k_ref, dk_ref, kn_ref))):
            for grp in range(d // LANES):
                ls = slice(grp * LANES, (grp + 1) * LANES)
                xv = src[:, ls]
                dyv = dsrc[:, ls]
                r = lax.rsqrt(_exact_right(xv * xv, bd) * (1.0 / SB_HEAD_DIM) + EPS)
                xh = xv * r
                acc_ref[idx:idx + 1, ls] += jnp.sum(dyv * xh, axis=0, keepdims=True)
                dxh = dyv * gain[:, ls]
                mean = _exact_right(dxh * xh, bd) * (1.0 / SB_HEAD_DIM)
                o_ref[:, idx * d + grp * LANES:idx * d + (grp + 1) * LANES] = (r * (dxh - xh * mean)).astype(BF16)
        o_ref[:, 2 * d:3 * d] = dv_ref[...].astype(BF16)

    blk = lambda j: pl.BlockSpec((tm, d), lambda i, j=j: (i, j))
    row = pl.BlockSpec((1, d), lambda i: (0, 0))
    return pl.pallas_call(
        body, name=name,
        out_shape=(jax.ShapeDtypeStruct((s, 3 * d), BF16), jax.ShapeDtypeStruct((8, d), F32)),
        grid=(s // tm,),
        in_specs=[blk(0), blk(1), blk(0), blk(0), blk(0), row, row],
        out_specs=(pl.BlockSpec((tm, 3 * d), lambda i: (i, 0)), pl.BlockSpec((8, d), lambda i: (0, 0))),
        compiler_params=_cparams(1),
    )(qkv, qkv, dqn, dkn, dv, qn, kn)


def _sb_tile(qh, kb, suffix_ones, run, mask):
    z = _nt(qh, kb)
    neg_abs = lax.bitcast_convert_type(lax.bitcast_convert_type(z, jnp.uint32) | jnp.uint32(0x80000000), F32)
    l1m = -(jnp.maximum(z, 0.0) + jnp.log2(1.0 + jnp.exp2(neg_abs)))
    logb = z + l1m
    if mask is not None:
        l1m = jnp.where(mask, l1m, 0.0)
    later = _nn(l1m.astype(BF16), suffix_ones) + run
    wgt = jnp.exp2(logb + later)
    if mask is not None:
        wgt = jnp.where(mask, wgt, 0.0)
    return logb, l1m, wgt


def _suffix_ones(tk):
    row = lax.broadcasted_iota(jnp.int32, (tk, tk), 0)
    col = lax.broadcasted_iota(jnp.int32, (tk, tk), 1)
    return (row > col).astype(BF16)


def _sb_alive(runs):
    return jnp.max(jnp.maximum(runs[0], runs[1])) > -SB_DEAD


def _sb_mask(q0, j, nr, tk):
    qpos = q0 + lax.broadcasted_iota(jnp.int32, (nr, tk), 0)
    kpos = j * tk + lax.broadcasted_iota(jnp.int32, (nr, tk), 1)
    return kpos < qpos


def _sb_fwd(qn, kn, v, *, name, side=None):
    s, d = qn.shape
    tq, tk = _tile(SB_TQ, s), _tile(SB_TK, s)
    assert tk % tq == 0 or tq % tk == 0
    nq = s // tq

    def body(q_ref, k_ref, v_ref, o_ref, acc_ref):
        qi = pl.program_id(1)
        lane = lax.broadcasted_iota(jnp.int32, (tq, LANES), 1)
        first = lane < SB_HEAD_DIM
        q = q_ref[...]
        qh = [jnp.where(first, q, 0).astype(BF16), jnp.where(first, 0, q).astype(BF16)]
        ones = _suffix_ones(tk)
        acc_ref[...] = jnp.zeros_like(acc_ref)

        def tile(j, runs, masked, rows=(0, tq)):
            r0, nr = rows
            ks = pl.ds(pl.multiple_of(j * tk, tk), tk)
            kb = k_ref[ks, :]
            vb = v_ref[ks, :]
            mask = _sb_mask(qi * tq + r0, j, nr, tk) if masked else None
            new_runs = []
            for hh in range(2):
                _, l1m, wgt = _sb_tile(qh[hh][r0:r0 + nr], kb, ones, runs[hh], mask)
                acc_ref[hh, r0:r0 + nr] += _nn(wgt.astype(BF16), vb)
                new_runs.append(runs[hh] + jnp.sum(l1m, axis=1, keepdims=True))
            return tuple(new_runs)

        nfull = (qi * tq) // tk
        zero = jnp.zeros((tq, 1), F32)
        runs = (zero, zero)
        for m in reversed(range(max(tq // tk, 1))):
            r0 = m * tk
            part = tile(nfull + m, tuple(r[r0:] for r in runs), True, (r0, tq - r0))
            runs = tuple(jnp.concatenate([r[:r0], p], axis=0) if r0 else p for r, p in zip(runs, part))

        def step(c):
            it, _, r = c
            r = tile(nfull - 1 - it, r, False)
            return it + 1, _sb_alive(r), r

        lax.while_loop(lambda c: jnp.logical_and(c[0] < nfull, c[1]), step, (0, _sb_alive(runs), runs))
        o_ref[...] = jnp.where(first, acc_ref[0], acc_ref[1])

    res, side_res = _carry(
        body, side, name=name, out_shape=(jax.ShapeDtypeStruct((s, d), F32),), grid=(d // LANES, nq),
        in_specs=[pl.BlockSpec((tq, LANES), lambda p, i: (i, p)), pl.BlockSpec((s, LANES), lambda p, i: (0, p)),
                  pl.BlockSpec((s, LANES), lambda p, i: (0, p))],
        out_specs=(pl.BlockSpec((tq, LANES), lambda p, i: (i, p)),),
        scratch_shapes=[pltpu.VMEM((2, tq, LANES), F32)], args=(qn, kn, v))
    return res[0] if side is None else (res[0], side_res)


def _sb_bwd(qn, kn, v, o, do, *, name, side=None):
    s, d = qn.shape
    tq, tk = _tile(SB_TQ, s), _tile(SB_TK, s)
    assert tk % tq == 0 or tq % tk == 0
    nq = s // tq

    def body(q_ref, k_ref, v_ref, o_ref, do_ref, dq_ref, dk_ref, dv_ref, acc_ref):
        qi = pl.program_id(1)

        @pl.when(qi == 0)
        def _():
            dk_ref[...] = jnp.zeros_like(dk_ref)
            dv_ref[...] = jnp.zeros_like(dv_ref)

        first = lax.broadcasted_iota(jnp.int32, (tq, LANES), 1) < SB_HEAD_DIM
        sel = [first, jnp.logical_not(first)]
        kfirst = lax.broadcasted_iota(jnp.int32, (tk, LANES), 1) < SB_HEAD_DIM
        ksel = [kfirst, jnp.logical_not(kfirst)]
        q = q_ref[...]
        dob = do_ref[...].astype(BF16)
        qh = [jnp.where(sel[hh], q, 0).astype(BF16) for hh in range(2)]
        doh = [jnp.where(sel[hh], dob, 0).astype(BF16) for hh in range(2)]
        prod = dob.astype(F32) * o_ref[...]
        gtot = [jnp.sum(jnp.where(sel[hh], prod, 0.0), axis=1, keepdims=True) for hh in range(2)]
        ones = _suffix_ones(tk)
        acc_ref[...] = jnp.zeros_like(acc_ref)

        def tile(j, carry, masked, rows=(0, tq)):
            r0, nr = rows
            runs, gruns = carry
            ks = pl.ds(pl.multiple_of(j * tk, tk), tk)
            kb = k_ref[ks, :]
            vb = v_ref[ks, :]
            mask = _sb_mask(qi * tq + r0, j, nr, tk) if masked else None
            new_runs, new_gruns = [], []
            dk_add = jnp.zeros((tk, LANES), F32)
            dv_add = jnp.zeros((tk, LANES), F32)
            for hh in range(2):
                qs, dos = qh[hh][r0:r0 + nr], doh[hh][r0:r0 + nr]
                logb, l1m, wgt = _sb_tile(qs, kb, ones, runs[hh], mask)
                wb = wgt.astype(BF16)
                g = _nt(dos, vb) * wb.astype(F32)
                gsuf = _exact_right2(g, ones) + g + gruns[hh]
                dz = g - jnp.exp2(logb) * (g + (gtot[hh][r0:r0 + nr] - gsuf))
                if masked:
                    dz = jnp.where(mask, dz, 0.0)
                dzb = dz.astype(BF16)
                acc_ref[hh, r0:r0 + nr] += _nn(dzb, kb)
                dk_add = dk_add + jnp.where(ksel[hh], _tn(dzb, qs), 0.0)
                dv_add = dv_add + jnp.where(ksel[hh], _tn(wb, dos), 0.0)
                new_runs.append(runs[hh] + jnp.sum(l1m, axis=1, keepdims=True))
                new_gruns.append(gruns[hh] + jnp.sum(g, axis=1, keepdims=True))
            dk_ref[ks, :] += dk_add * LN2
            dv_ref[ks, :] += dv_add
            return tuple(new_runs), tuple(new_gruns)

        nfull = (qi * tq) // tk
        zero = jnp.zeros((tq, 1), F32)
        carry = ((zero, zero), (zero, zero))
        for m in reversed(range(max(tq // tk, 1))):
            r0 = m * tk
            part = tile(nfull + m, tuple(tuple(r[r0:] for r in rs) for rs in carry), True, (r0, tq - r0))
            carry = tuple(tuple(jnp.concatenate([r[:r0], p], axis=0) if r0 else p for r, p in zip(rs, ps))
                          for rs, ps in zip(carry, part))

        def step(c):
            it, _, cr = c
            cr = tile(nfull - 1 - it, cr, False)
            return it + 1, _sb_alive(cr[0]), cr

        lax.while_loop(lambda c: jnp.logical_and(c[0] < nfull, c[1]), step, (0, _sb_alive(carry[0]), carry))
        dq_ref[...] = jnp.where(first, acc_ref[0], acc_ref[1]) * SB_SCALE

    blk = pl.BlockSpec((tq, LANES), lambda p, i: (i, p))
    full = pl.BlockSpec((s, LANES), lambda p, i: (0, p))
    out = jax.ShapeDtypeStruct((s, d), F32)
    res, side_res = _carry(
        body, side, name=name, out_shape=(out, out, out), grid=(d // LANES, nq),
        in_specs=[blk, full, full, blk, blk], out_specs=(blk, full, full),
        scratch_shapes=[pltpu.VMEM((2, tq, LANES), F32)], args=(qn, kn, v, o, do))
    return res if side is None else (res, side_res)


def _mod_part(c_all, ada_w, ada_b_my, *, name):
    nl, d, ncol = ada_w.shape

    def body(c_ref, w_ref, b_ref, part_ref, ca_ref):
        cv = c_ref[...]
        ca = cv * jax.nn.sigmoid(cv)
        ca_ref[...] = ca
        part_ref[...] = _nn(ca.astype(BF16), w_ref[...].astype(BF16)) + b_ref[...]

    return pl.pallas_call(
        body, name=name,
        out_shape=(jax.ShapeDtypeStruct((nl, N_DEV, ncol), F32), jax.ShapeDtypeStruct((N_DEV, d), F32)),
        grid=(nl,),
        in_specs=[pl.BlockSpec((N_DEV, d), lambda l: (0, 0)), pl.BlockSpec((None, d, ncol), lambda l: (l, 0, 0)),
                  pl.BlockSpec((None, 1, ncol), lambda l: (l, 0, 0))],
        out_specs=(pl.BlockSpec((None, N_DEV, ncol), lambda l: (l, 0, 0)), pl.BlockSpec((N_DEV, d), lambda l: (0, 0))),
        compiler_params=_cparams(1),
    )(c_all, ada_w, ada_b_my)


PK_MOD, PK_NMIX, PK_NMLP, PK_HGN, PK_LB, PK_QN, PK_KN, PK_CONV, PK_ROWS = 0, 96, 112, 128, 132, 136, 144, 152, 168


def _small_grads(gath, ca_col, dmod_my, lbl4, *, name):
    def body(g_ref, ca_ref, dm_ref, lbl_ref, gw_ref, gsum_ref, glb_ref, gqk_ref):
        tot = g_ref[0]
        for dev in range(1, N_DEV):
            tot = tot + g_ref[dev]
        gsum_ref[...] = tot
        lv = lbl_ref[...]
        m = jnp.maximum(jnp.maximum(lv[0], lv[1]), lv[2])
        e = [jnp.exp(lv[k] - m) for k in range(3)]
        den = e[0] + e[1] + e[2]
        p = [ek / den for ek in e]
        dlb = tot[PK_LB:PK_LB + 4, :]
        glb_ref[0] = dlb * p[0] * (1.0 - p[0])
        glb_ref[1] = -dlb * p[0] * p[1]
        glb_ref[2] = -dlb * p[0] * p[2]
        for idx, base in enumerate((PK_QN, PK_KN)):
            rowsum = jnp.sum(tot[base:base + 8, :], axis=0, keepdims=True)
            gqk_ref[idx:idx + 1, :] = rowsum + pltpu.roll(rowsum, SB_HEAD_DIM, 1)
        for l in range(2):
            acc = ca_ref[0] * dm_ref[0, l:l + 1, :]
            for smp in range(1, N_DEV):
                acc = acc + ca_ref[smp] * dm_ref[smp, l:l + 1, :]
            gw_ref[l] = acc

    d, ncol = ca_col.shape[1], dmod_my.shape[2]
    vm = pl.BlockSpec(memory_space=pltpu.VMEM)
    return pl.pallas_call(
        body, name=name,
        out_shape=(jax.ShapeDtypeStruct((2, d, ncol), F32), jax.ShapeDtypeStruct((PK_ROWS, LANES), F32),
                   jax.ShapeDtypeStruct((3, 4, LANES), F32), jax.ShapeDtypeStruct((8, LANES), F32)),
        in_specs=[vm, vm, vm, vm], out_specs=(vm, vm, vm, vm),
        compiler_params=pltpu.CompilerParams(vmem_limit_bytes=VMEM_LIMIT),
    )(gath, ca_col, dmod_my, lbl4)


def _adamw_math(w, g, m, v):
    m = ADAM_B1 * m + (1.0 - ADAM_B1) * g
    v = ADAM_B2 * v + (1.0 - ADAM_B2) * (g * g)
    m_hat = m / (1.0 - ADAM_B1 ** ADAM_STEP)
    v_hat = v / (1.0 - ADAM_B2 ** ADAM_STEP)
    delta = -ADAM_LR * (m_hat / (jnp.sqrt(v_hat) + ADAM_EPS) + ADAM_WD * w)
    return delta, m, v


def _adamw(w, g, m, v, *, name, tr=256):
    r, n = w.shape
    tr = _tile(tr, r)

    def body(w_ref, g_ref, m_ref, v_ref, d_ref, mo_ref, vo_ref):
        dl, mn, vn = _adamw_math(w_ref[...], g_ref[...], m_ref[...], v_ref[...])
        d_ref[...] = dl
        mo_ref[...] = mn
        vo_ref[...] = vn

    blk = pl.BlockSpec((tr, n), lambda i: (i, 0))
    out = jax.ShapeDtypeStruct((r, n), F32)
    return pl.pallas_call(
        body, name=name, out_shape=(out, out, out), grid=(r // tr,),
        in_specs=[blk, blk, blk, blk], out_specs=(blk, blk, blk),
        compiler_params=_cparams(1),
    )(w, g, m, v)


def _adamw_small(items, *, name):
    n = len(items)

    def body(*refs):
        ins, outs = refs[:4 * n], refs[4 * n:]
        for k in range(n):
            dl, mn, vn = _adamw_math(*(r[...] for r in ins[4 * k:4 * k + 4]))
            outs[3 * k][...] = dl
            outs[3 * k + 1][...] = mn
            outs[3 * k + 2][...] = vn

    flat = [a for it in items for a in it]
    out_shape = tuple(jax.ShapeDtypeStruct(it[0].shape, F32) for it in items for _ in range(3))
    vm = pl.BlockSpec(memory_space=pltpu.VMEM)
    res = pl.pallas_call(
        body, name=name, out_shape=out_shape, in_specs=[vm] * (4 * n), out_specs=tuple([vm] * (3 * n)),
    )(*flat)
    return [tuple(res[3 * k:3 * k + 3]) for k in range(n)]


def _mlp_fwd(x, g, scale, shift, gate, w1g, w2g, tag, side_w1=None):
    h = _norm_mod(x, g, scale, shift, name=f"{tag}_norm")
    act = _matmul(h, w1g, b_kind="colblk", epi="relu2", out_dtype=BF16, name=f"{tag}_w1", side=side_w1)
    side_res = ()
    if side_w1 is not None:
        act, side_res = act
    z, x_out = _matmul(act, w2g, b_kind="rowblk", epi="resgate", extras=(x, gate), name=f"{tag}_w2")
    return x_out, (h, act, z), side_res


def _mlp_bwd(dz, dx_out, x, saved, g, scale, w1g, w2g, tag, gated, side_dact=None, make_side_dh=None):
    h, act, _ = saved
    du = _matmul(dz, w2g, tb=True, b_kind="rowblk", epi="dact", extras=(act,), out_dtype=BF16,
                 name=f"{tag}_dact", side=side_dact)
    res_dact = ()
    if side_dact is not None:
        du, res_dact = du
    dw2 = _matmul(act, dz, ta=True, name=f"{tag}_dw2")
    dw1_t = _matmul(du, h, ta=True, name=f"{tag}_dw1")
    side_dh = None if make_side_dh is None else make_side_dh(dw1_t, dw2)
    dh = _matmul(du, w1g, tb=True, b_kind="colblk", name=f"{tag}_dh", side=side_dh)
    res_dh = ()
    if side_dh is not None:
        dh, res_dh = dh
    dx, nacc, dz_mix = _norm_mod_bwd(x, dh, dx_out, g, scale, name=f"{tag}_norm_bwd", gated=gated)
    return dx, dw1_t, dw2, nacc, dz_mix, (res_dact, res_dh)


def kernel(x, c, ada_w, ada_b, norm_mix, norm_mlp, w_in_ab, conv_w, hg_norm, lb_logits, w_out_ab, w_qkv, q_norm, k_norm, w_out_c, mlp_w1, mlp_w2, loss_target, m_ada_w, m_ada_b, m_norm_mix, m_norm_mlp, m_w_in_ab, m_conv_w, m_hg_norm, m_lb_logits, m_w_out_ab, m_w_qkv, m_q_norm, m_k_norm, m_w_out_c, m_mlp_w1, m_mlp_w2, v_ada_w, v_ada_b, v_norm_mix, v_norm_mlp, v_w_in_ab, v_conv_w, v_hg_norm, v_lb_logits, v_w_out_ab, v_w_qkv, v_q_norm, v_k_norm, v_w_out_c, v_mlp_w1, v_mlp_w2):
    d = D_MODEL
    my_x, my_y, my_c = lax.axis_index("x"), lax.axis_index("y"), lax.axis_index("c")
    me = 4 * my_x + 2 * my_y + my_c
    xs = x[0]
    tgt = loss_target[0]

    def bf(w):
        return w.astype(BF16)

    ncv = CONV_DIM // N_DEV
    c_and_conv = jnp.concatenate([c, jnp.pad(conv_w[0], ((0, 0), (0, d - ncv))), jnp.zeros((4, d), F32)], axis=0)
    wing, c_and_conv = _run_side(_gather_side([bf(w_in_ab), c_and_conv]), name="gather_w_in")
    win = wing[:, 0].transpose(1, 0, 2).reshape(d, AB_IN)

    c_all = c_and_conv[:, 0]
    conv_full = c_and_conv[:, 1:4, :ncv].transpose(1, 0, 2).reshape(3, CONV_DIM)
    ncol = ada_w.shape[2]
    ada_b_my = lax.dynamic_slice(ada_b, (0, me * ncol), (2, ncol)).reshape(2, 1, ncol)
    part, c_act = _mod_part(c_all, ada_w, ada_b_my, name="mod_part")
    parts = _all_gather(part.reshape(2 * N_DEV, ncol), name="gather_mod", in_vmem=True)
    parts = parts.reshape(N_DEV, 2, N_DEV, ncol)
    mod = lax.dynamic_index_in_dim(parts, me, axis=2, keepdims=False)
    mod = mod.transpose(1, 0, 2).reshape(2, 6, 1, d)

    qn_t = jnp.tile(q_norm, (1, d // SB_HEAD_DIM))
    kn_t = jnp.tile(k_norm, (1, d // SB_HEAD_DIM))

    sh1, sc1, gt1, sh2, sc2, gt2 = [mod[0, k] for k in range(6)]
    h0 = _norm_mod(xs, norm_mix[0:1], sc1, sh1, name="l0_mix_norm")
    u = _matmul(h0, win, name="l0_in_proj")
    y_a = _conv_fwd(u, conv_full, name="l0_conv")
    (y_b, o_hg, sall), (woutg_ab, w1g0, w2g0) = _hgrn_fwd(
        u, lb_logits, hg_norm, name="l0_hgrn",
        side=_gather_side([bf(w_out_ab), bf(mlp_w1[0:1]), bf(mlp_w2[0:1])]))
    wout_ab = woutg_ab.reshape(d, d)
    y_ab = jnp.concatenate([y_a, y_b], axis=1)
    z0, x_mid0 = _matmul(y_ab, wout_ab, epi="resgate", extras=(xs, gt1), name="l0_out_proj")
    x1, mlp0, (wqkvg, woutg_c) = _mlp_fwd(x_mid0, norm_mlp[0:1], sc2, sh2, gt2, w1g0, w2g0, "l0_mlp",
                                          side_w1=_gather_side([bf(w_qkv), bf(w_out_c)]))
    wout_c = woutg_c.reshape(d, d)

    sh1b, sc1b, gt1b, sh2b, sc2b, gt2b = [mod[1, k] for k in range(6)]
    h1 = _norm_mod(x1, norm_mix[1:2], sc1b, sh1b, name="l1_mix_norm")
    qkv = _matmul(h1, wqkvg, b_kind="colblk", name="l1_qkv_proj")
    qn_a, kn_a, v_a = _qk_norm_fwd(qkv, qn_t, kn_t, name="l1_qk_norm")
    o_sb, (w1g1, w2g1) = _sb_fwd(qn_a, kn_a, v_a, name="l1_sb",
                                 side=_gather_side([bf(mlp_w1[1:2]), bf(mlp_w2[1:2])]))
    z1, x_mid1 = _matmul(o_sb, wout_c, epi="resgate", extras=(x1, gt1b), name="l1_out_proj")
    x2, mlp1, _ = _mlp_fwd(x_mid1, norm_mlp[1:2], sc2b, sh2b, gt2b, w1g1, w2g1, "l1_mlp")

    dx, loss_part, dz_mlp, dgt2b = _loss_grad(x2, tgt, mlp1[2], gt2b, name="loss")
    loss = lax.psum(loss_part[0, 0], MESH_AXES)

    my_q = 2 * my_x + my_y
    far_q = [my_q ^ 2, my_q ^ 1, my_q ^ 3]
    blk_ids = jnp.stack([2 * q + my_c for q in far_q] + far_q).astype(jnp.int32)
    my_ids = jnp.stack([me, my_q]).astype(jnp.int32)

    def blocks(g):
        return g.reshape(N_DEV, g.shape[0] // N_DEV, d)

    def by_rows(fn, tag, *lists):
        out = [None] * len(lists[0])
        heights = {}
        for t, g in enumerate(lists[0]):
            heights.setdefault(g.shape[1], []).append(t)
        for r, ts in heights.items():
            for t, v in zip(ts, fn(*[[lst[t] for t in ts] for lst in lists], name=f"{tag}_{r}")):
                out[t] = v
        return out

    def pair_sums(gs, sibs, tag):
        return by_rows(lambda a, b, name: _rs_pair_sum(a, b, blk_ids, name=name), f"rs_pair_sum_{tag}", gs, sibs)

    def final_sums(gs, sibs, fars, tag):
        return by_rows(lambda a, b, c_, name: _rs_final_sum(a, b, c_, my_ids, name=name),
                       f"rs_final_sum_{tag}", gs, sibs, fars)

    dx, dw1t_1, dw2_1, nacc, dyp, (_, sib1) = _mlp_bwd(
        dz_mlp, dx, x_mid1, mlp1, norm_mlp[1:2], sc2b, w1g1, w2g1, "l1_mlp", (z1, gt1b),
        make_side_dh=lambda a, b: _sibling_exchange_side([blocks(a), blocks(b)]))
    dsh2b, dsc2b, dnmlp1, dgt1b = nacc[0:1], nacc[1:2], nacc[2:3], nacc[3:4]
    g1 = [blocks(dw1t_1), blocks(dw2_1)]
    pair1 = pair_sums(g1, sib1, "g1")
    dwout_c = _matmul(o_sb, dyp, ta=True, name="l1_dwout")
    do_sb = _matmul(dyp, wout_c, tb=True, name="l1_do")
    (dqn_a, dkn_a, dv_a), far1 = _sb_bwd(qn_a, kn_a, v_a, o_sb, do_sb, name="l1_sb_bwd",
                                         side=_chip_exchange_side(pair1))
    gsh1 = final_sums(g1, sib1, far1, "g1")
    dqkv, qkacc = _qk_norm_bwd(qkv, dqn_a, dkn_a, dv_a, qn_t, kn_t, name="l1_qk_norm_bwd")
    dwqkv_t = _matmul(dqkv, h1, ta=True, name="l1_dwqkv")
    g2 = [blocks(dwqkv_t), blocks(dwout_c)]
    dh1, sib2 = _matmul(dqkv, wqkvg, tb=True, b_kind="colblk", name="l1_dh", side=_sibling_exchange_side(g2))
    pair2 = pair_sums(g2, sib2, "g2")
    dx, nacc, dz_mlp = _norm_mod_bwd(x1, dh1, dx, norm_mix[1:2], sc1b, name="l1_mix_norm_bwd",
                                     gated=(mlp0[2], gt2))
    dmod1 = [nacc[0:1], nacc[1:2], dgt1b, dsh2b, dsc2b, dgt2b]
    dnmix1, dgt2 = nacc[2:3], nacc[3:4]

    dx, dw1t_0, dw2_0, nacc, dyp, (far2, sib3) = _mlp_bwd(
        dz_mlp, dx, x_mid0, mlp0, norm_mlp[0:1], sc2, w1g0, w2g0, "l0_mlp", (z0, gt1),
        side_dact=_chip_exchange_side(pair2),
        make_side_dh=lambda a, b: _sibling_exchange_side([blocks(a), blocks(b)]))
    dsh2, dsc2, dnmlp0, dgt1 = nacc[0:1], nacc[1:2], nacc[2:3], nacc[3:4]
    gsh2 = final_sums(g2, sib2, far2, "g2")
    g3 = [blocks(dw1t_0), blocks(dw2_0)]
    pair3 = pair_sums(g3, sib3, "g3")
    dwout_ab = _matmul(y_ab, dyp, ta=True, name="l0_dwout")
    dy_ab = _matmul(dyp, wout_ab, tb=True, name="l0_dy")
    du_a, dconv = _conv_bwd(u, dy_ab, conv_full, name="l0_conv_bwd")
    (du_b, hgacc), far3 = _hgrn_bwd(u, o_hg, sall, dy_ab, lb_logits, hg_norm, name="l0_hgrn_bwd",
                                    side=_chip_exchange_side(pair3))
    gsh3 = final_sums(g3, sib3, far3, "g3")
    du = jnp.concatenate([du_a, du_b], axis=1)
    dwin_t = _matmul(du, h0, ta=True, name="l0_dwin")
    g4 = [blocks(dwin_t), blocks(dwout_ab)]
    dh0, sib4 = _matmul(du, win, tb=True, name="l0_dh", side=_sibling_exchange_side(g4))
    pair4 = pair_sums(g4, sib4, "g4")
    grad_x, nacc, far4 = _norm_mod_bwd(xs, dh0, dx, norm_mix[0:1], sc1, name="l0_mix_norm_bwd",
                                       side=_chip_exchange_side(pair4))
    gsh4 = final_sums(g4, sib4, far4, "g4")
    dmod0 = [nacc[0:1], nacc[1:2], dgt1, dsh2, dsc2, dgt2]
    dnmix0 = nacc[2:3]

    g_big = [gsh4[0].T[None], gsh4[1][None], gsh2[0].T[None], gsh2[1][None],
             jnp.stack([gsh3[0].T, gsh1[0].T]), jnp.stack([gsh3[1], gsh1[1]])]

    packed_small = jnp.concatenate(
        [jnp.concatenate(dmod0, axis=1).reshape(-1, LANES), jnp.concatenate(dmod1, axis=1).reshape(-1, LANES),
         dnmix0.reshape(-1, LANES), dnmix1.reshape(-1, LANES), dnmlp0.reshape(-1, LANES), dnmlp1.reshape(-1, LANES),
         hgacc[0:1].reshape(-1, LANES), hgacc[1:2].reshape(-1, LANES),
         qkacc[0:1].reshape(-1, LANES), qkacc[1:2].reshape(-1, LANES),
         dconv[0:3].reshape(-1, LANES), jnp.zeros((PK_ROWS - PK_CONV - 12, LANES), F32)], axis=0)
    gath = _all_gather(packed_small, name="gather_small_grads", in_vmem=True).reshape(N_DEV, PK_ROWS, LANES)
    dmod_all = gath[:, PK_MOD:PK_NMIX].reshape(N_DEV, 2, 6 * d)
    dmod_my = lax.dynamic_slice(dmod_all, (0, 0, me * ncol), (N_DEV, 2, ncol))
    g_ada_w, gsum, g_lb, g_qk = _small_grads(gath, c_act[:, :, None], dmod_my, lb_logits.reshape(3, 4, LANES),
                                             name="small_grads")
    g_ada_b = gsum[PK_MOD:PK_NMIX].reshape(2, 6 * d)
    g_norm_mix = gsum[PK_NMIX:PK_NMLP].reshape(2, d)
    g_norm_mlp = gsum[PK_NMLP:PK_HGN].reshape(2, d)
    g_hg_norm = gsum[PK_HGN:PK_LB].reshape(1, HG_WIDTH)
    g_lb_logits = g_lb.reshape(3, HG_WIDTH)
    g_q_norm = g_qk[0:1, :SB_HEAD_DIM]
    g_k_norm = g_qk[1:2, :SB_HEAD_DIM]
    g_conv_w = lax.dynamic_slice(gsum[PK_CONV:PK_CONV + 12].reshape(3, CONV_DIM), (0, me * ncv), (3, ncv))[None]

    def flat2(a):
        return a.reshape(-1, a.shape[-1])

    grads = dict(ada_w=g_ada_w, ada_b=g_ada_b, norm_mix=g_norm_mix, norm_mlp=g_norm_mlp, w_in_ab=g_big[0],
                 conv_w=g_conv_w, hg_norm=g_hg_norm, lb_logits=g_lb_logits, w_out_ab=g_big[1], w_qkv=g_big[2],
                 q_norm=g_q_norm, k_norm=g_k_norm, w_out_c=g_big[3], mlp_w1=g_big[4], mlp_w2=g_big[5])
    weights = dict(ada_w=(ada_w, m_ada_w, v_ada_w), ada_b=(ada_b, m_ada_b, v_ada_b),
                   norm_mix=(norm_mix, m_norm_mix, v_norm_mix), norm_mlp=(norm_mlp, m_norm_mlp, v_norm_mlp),
                   w_in_ab=(w_in_ab, m_w_in_ab, v_w_in_ab), conv_w=(conv_w, m_conv_w, v_conv_w),
                   hg_norm=(hg_norm, m_hg_norm, v_hg_norm), lb_logits=(lb_logits, m_lb_logits, v_lb_logits),
                   w_out_ab=(w_out_ab, m_w_out_ab, v_w_out_ab), w_qkv=(w_qkv, m_w_qkv, v_w_qkv),
                   q_norm=(q_norm, m_q_norm, v_q_norm), k_norm=(k_norm, m_k_norm, v_k_norm),
                   w_out_c=(w_out_c, m_w_out_c, v_w_out_c), mlp_w1=(mlp_w1, m_mlp_w1, v_mlp_w1),
                   mlp_w2=(mlp_w2, m_mlp_w2, v_mlp_w2))
    names = list(weights)
    small_names = ["ada_b", "norm_mix", "norm_mlp", "conv_w", "hg_norm", "lb_logits", "q_norm", "k_norm"]
    upd = {}
    small_items = []
    for n in small_names:
        wv, mv, vv = weights[n]
        small_items.append((flat2(wv), flat2(grads[n]), flat2(mv), flat2(vv)))
    for n, res in zip(small_names, _adamw_small(small_items, name="adamw_small")):
        upd[n] = tuple(r.reshape(weights[n][0].shape) for r in res)
    for n in names:
        if n in small_names:
            continue
        wv, mv, vv = weights[n]
        res = _adamw(flat2(wv), flat2(grads[n]), flat2(mv), flat2(vv), name=f"adamw_{n}")
        upd[n] = tuple(r.reshape(wv.shape) for r in res)

    return (loss, grad_x[None], *[grads[n].reshape(weights[n][0].shape) for n in names],
            *[upd[n][0] for n in names], *[upd[n][1] for n in names], *[upd[n][2] for n in names])
```

```python
import functools

import jax
import jax.numpy as jnp
from jax import lax
from jax.experimental import pallas as pl
from jax.experimental.pallas import tpu as pltpu

F32 = jnp.float32
BF16 = jnp.bfloat16
EPS = 1e-6
N_DEV = 8
MESH_AXES = ("x", "y", "c")

D_MODEL = 1024
CONV_DIM = 512
HG_HEADS = 4
HG_DK = 128
HG_WIDTH = 512
CHUNK = 64
HG_TILE = 128
HG_SUB = 16
HG_EXP_CLAMP = 60.0
SB_HEAD_DIM = 64
SB_SCALE = SB_HEAD_DIM ** -0.5
LOG2E = 1.4426950408889634
LN2 = 0.6931471805599453
SB_TQ = 512
SB_TK = 256
SB_DEAD = 150.0
D_FF = 4096
AB_IN = 3584

ADAM_LR = 0.001
ADAM_B1 = 0.9
ADAM_B2 = 0.999
ADAM_EPS = 1e-08
ADAM_WD = 0.01
ADAM_STEP = 10

VMEM_LIMIT = 48 * 1024 * 1024
LANES = 128


def _cparams(n_grid):
    return pltpu.CompilerParams(dimension_semantics=("arbitrary",) * n_grid, vmem_limit_bytes=VMEM_LIMIT)


def _nt(a, b):
    return lax.dot_general(a, b, (((1,), (1,)), ((), ())), preferred_element_type=F32)


def _tn(a, b):
    return lax.dot_general(a, b, (((0,), (0,)), ((), ())), preferred_element_type=F32)


def _nn(a, b):
    return jnp.dot(a, b, preferred_element_type=F32)


def _split3(x):
    hi = x.astype(BF16)
    r1 = x - hi.astype(F32)
    mid = r1.astype(BF16)
    lo = (r1 - mid.astype(F32)).astype(BF16)
    return hi, mid, lo


def _exact_left(m01, x):
    hi, mid, lo = _split3(x)
    return _nn(m01, hi) + _nn(m01, mid) + _nn(m01, lo)


def _exact_right(x, m01):
    hi, mid, lo = _split3(x)
    return _nn(hi, m01) + _nn(mid, m01) + _nn(lo, m01)


def _exact_right2(x, m01):
    hi = x.astype(BF16)
    lo = (x - hi.astype(F32)).astype(BF16)
    return _nn(hi, m01) + _nn(lo, m01)


def _sp(x):
    hi = x.astype(BF16)
    return hi, (x - hi.astype(F32)).astype(BF16)


def _dot3(fn, a, b):
    return fn(a[0], b[0]) + fn(a[0], b[1]) + fn(a[1], b[0])


def _tile(pref, n):
    t = min(pref, n)
    assert n % t == 0, (pref, n)
    return t


def _tile_rows(pref, n):
    for t in range(min(pref, n) - min(pref, n) % 16, 0, -16):
        if n % t == 0:
            return t
    raise ValueError((pref, n))


def _tile_lanes(pref, n):
    if n <= pref:
        return n
    for t in range(pref - pref % LANES, 0, -LANES):
        if n % t == 0:
            return t
    raise ValueError((pref, n))


def _all_gather(x, *, name, in_vmem):
    m_per, n = x.shape

    def body(x_ref, out_ref, send_sems, recv_sems, local_sem):
        mx, my, mc = lax.axis_index("x"), lax.axis_index("y"), lax.axis_index("c")
        me, sibling = (mx, my, mc), (mx, my, 1 - mc)
        chips = [(1 - mx, my), (mx, 1 - my), (1 - mx, 1 - my)]

        def rows(px, py, pc):
            return out_ref.at[pl.ds((4 * px + 2 * py + pc) * m_per, m_per), :]

        def copy(k, block, to, src=None):
            return pltpu.make_async_remote_copy(
                src_ref=rows(*block) if src is None else src, dst_ref=rows(*block),
                send_sem=send_sems.at[k], recv_sem=recv_sems.at[k],
                device_id=to, device_id_type=pl.DeviceIdType.MESH)

        mine = pltpu.make_async_copy(x_ref, rows(*me), local_sem)
        mine.start()
        first = [copy(0, me, sibling, src=x_ref)]
        first += [copy(1 + j, me, (*chip, mc), src=x_ref) for j, chip in enumerate(chips)]
        for cp in first:
            cp.start()
        passed = [copy(4 + j, (*chip, mc), sibling) for j, chip in enumerate(chips)]
        for j, chip in enumerate(chips):
            copy(1 + j, (*chip, mc), me).wait_recv()
            passed[j].start()
        copy(0, sibling, me).wait_recv()
        for j, chip in enumerate(chips):
            copy(4 + j, (*chip, 1 - mc), me).wait_recv()
        for cp in first + passed:
            cp.wait_send()
        mine.wait()

    space = pltpu.VMEM if in_vmem else pl.ANY
    return pl.pallas_call(
        body, name=name,
        out_shape=jax.ShapeDtypeStruct((N_DEV * m_per, n), x.dtype),
        in_specs=[pl.BlockSpec(memory_space=space)],
        out_specs=pl.BlockSpec(memory_space=space),
        scratch_shapes=[pltpu.SemaphoreType.DMA((7,)), pltpu.SemaphoreType.DMA((7,)), pltpu.SemaphoreType.DMA],
    )(x)


class _Side:
    def __init__(self, inputs, out_shape, scratch, start, finish):
        self.inputs, self.out_shape, self.scratch = list(inputs), tuple(out_shape), list(scratch)
        self.start, self.finish = start, finish


def _run_side(side, *, name):
    n_in, n_out = len(side.inputs), len(side.out_shape)

    def body(*refs):
        parts = (refs[:n_in], refs[n_in:n_in + n_out], refs[n_in + n_out:])
        side.start(*parts)
        side.finish(*parts)

    hbm = pl.BlockSpec(memory_space=pl.ANY)
    return pl.pallas_call(body, name=name, out_shape=side.out_shape, in_specs=[hbm] * n_in,
                          out_specs=tuple([hbm] * n_out), scratch_shapes=side.scratch)(*side.inputs)


def _carry(body, side, *, name, grid, in_specs, out_specs, out_shape, scratch_shapes, args):
    in_specs, out_specs, out_shape = list(in_specs), tuple(out_specs), tuple(out_shape)
    scratch_shapes = list(scratch_shapes)
    if side is None:
        res = pl.pallas_call(body, name=name, grid=grid, in_specs=in_specs, out_specs=out_specs,
                             out_shape=out_shape, scratch_shapes=scratch_shapes,
                             compiler_params=_cparams(len(grid)))(*args)
        return tuple(res), ()
    n_in, n_out, n_scr = len(in_specs), len(out_specs), len(scratch_shapes)
    s_in, s_out = len(side.inputs), len(side.out_shape)

    def wrapped(*refs):
        ins, rest = refs[:n_in], refs[n_in:]
        s_ins, rest = rest[:s_in], rest[s_in:]
        outs, rest = rest[:n_out], rest[n_out:]
        s_outs, rest = rest[:s_out], rest[s_out:]
        scr, s_scr = rest[:n_scr], rest[n_scr:]
        ids = [pl.program_id(ax) for ax in range(len(grid))]
        first = functools.reduce(jnp.logical_and, [i == 0 for i in ids])
        last = functools.reduce(jnp.logical_and, [i == g - 1 for i, g in zip(ids, grid)])

        @pl.when(first)
        def _():
            side.start(s_ins, s_outs, s_scr)

        body(*ins, *outs, *scr)

        @pl.when(last)
        def _():
            side.finish(s_ins, s_outs, s_scr)

    hbm = pl.BlockSpec(memory_space=pl.ANY)
    res = pl.pallas_call(
        wrapped, name=name, grid=grid, in_specs=in_specs + [hbm] * s_in,
        out_specs=out_specs + tuple([hbm] * s_out), out_shape=out_shape + side.out_shape,
        scratch_shapes=scratch_shapes + side.scratch, compiler_params=_cparams(len(grid)),
    )(*args, *side.inputs)
    return tuple(res[:n_out]), tuple(res[n_out:])


def _gather_side(xs):
    n = len(xs)

    def tools(x_refs, out_refs, sems):
        send_sems, recv_sems, local_sems = sems
        mx, my, mc = lax.axis_index("x"), lax.axis_index("y"), lax.axis_index("c")
        me, sibling = (mx, my, mc), (mx, my, 1 - mc)
        chips = [(1 - mx, my), (mx, 1 - my), (1 - mx, 1 - my)]

        def slot(t, px, py, pc):
            return out_refs[t].at[4 * px + 2 * py + pc]

        def copy(t, k, block, to, src=None):
            return pltpu.make_async_remote_copy(
                src_ref=slot(t, *block) if src is None else src, dst_ref=slot(t, *block),
                send_sem=send_sems.at[7 * t + k], recv_sem=recv_sems.at[7 * t + k],
                device_id=to, device_id_type=pl.DeviceIdType.MESH)

        mine = [pltpu.make_async_copy(x_refs[t], slot(t, *me), local_sems.at[t]) for t in range(n)]
        first = []
        for t in range(n):
            first.append(copy(t, 0, me, sibling, src=x_refs[t]))
            first += [copy(t, 1 + j, me, (*chip, mc), src=x_refs[t]) for j, chip in enumerate(chips)]
        return me, sibling, chips, mc, copy, mine, first

    def start(x_refs, out_refs, sems):
        *_, mine, first = tools(x_refs, out_refs, sems)
        for cp in mine + first:
            cp.start()

    def finish(x_refs, out_refs, sems):
        me, sibling, chips, mc, copy, mine, first = tools(x_refs, out_refs, sems)
        passed = []
        for j, chip in enumerate(chips):
            for t in range(n):
                copy(t, 1 + j, (*chip, mc), me).wait_recv()
                passed.append(copy(t, 4 + j, (*chip, mc), sibling))
                passed[-1].start()
        for t in range(n):
            copy(t, 0, sibling, me).wait_recv()
            for j, chip in enumerate(chips):
                copy(t, 4 + j, (*chip, 1 - mc), me).wait_recv()
        for cp in first + passed:
            cp.wait_send()
        for cp in mine:
            cp.wait()

    return _Side(xs, [jax.ShapeDtypeStruct((N_DEV,) + x.shape, x.dtype) for x in xs],
                 [pltpu.SemaphoreType.DMA((7 * n,)), pltpu.SemaphoreType.DMA((7 * n,)),
                  pltpu.SemaphoreType.DMA((n,))], start, finish)


def _sibling_exchange_side(gs):
    n = len(gs)

    def copies(g_refs, out_refs, sems):
        send_sems, recv_sems = sems
        mx, my, mc = lax.axis_index("x"), lax.axis_index("y"), lax.axis_index("c")
        return [pltpu.make_async_remote_copy(
            src_ref=g_refs[t].at[2 * q + (1 - mc)], dst_ref=out_refs[t].at[q],
            send_sem=send_sems.at[4 * t + q], recv_sem=recv_sems.at[4 * t + q],
            device_id=(mx, my, 1 - mc), device_id_type=pl.DeviceIdType.MESH)
            for t in range(n) for q in range(4)]

    def start(g_refs, out_refs, sems):
        for cp in copies(g_refs, out_refs, sems):
            cp.start()

    def finish(g_refs, out_refs, sems):
        cps = copies(g_refs, out_refs, sems)
        for cp in cps:
            cp.wait_recv()
        for cp in cps:
            cp.wait_send()

    return _Side(gs, [jax.ShapeDtypeStruct((4,) + g.shape[1:], g.dtype) for g in gs],
                 [pltpu.SemaphoreType.DMA((4 * n,)), pltpu.SemaphoreType.DMA((4 * n,))], start, finish)


def _chip_exchange_side(ts):
    n = len(ts)

    def copies(t_refs, out_refs, sems):
        send_sems, recv_sems = sems
        mx, my, mc = lax.axis_index("x"), lax.axis_index("y"), lax.axis_index("c")
        chips = [(1 - mx, my), (mx, 1 - my), (1 - mx, 1 - my)]
        return [pltpu.make_async_remote_copy(
            src_ref=t_refs[t].at[k], dst_ref=out_refs[t].at[k],
            send_sem=send_sems.at[3 * t + k], recv_sem=recv_sems.at[3 * t + k],
            device_id=(px, py, mc), device_id_type=pl.DeviceIdType.MESH)
            for t in range(n) for k, (px, py) in enumerate(chips)]

    def start(t_refs, out_refs, sems):
        for cp in copies(t_refs, out_refs, sems):
            cp.start()

    def finish(t_refs, out_refs, sems):
        cps = copies(t_refs, out_refs, sems)
        for cp in cps:
            cp.wait_recv()
        for cp in cps:
            cp.wait_send()

    return _Side(ts, [jax.ShapeDtypeStruct(t.shape, t.dtype) for t in ts],
                 [pltpu.SemaphoreType.DMA((3 * n,)), pltpu.SemaphoreType.DMA((3 * n,))], start, finish)


def _rs_pair_sum(gs, p1s, blk_ids, *, name, tr=256):
    n = len(gs)
    _, r, ncol = gs[0].shape
    tr = _tile_rows(tr, r)

    def body(id_ref, *refs):
        for t in range(n):
            refs[2 * n + t][...] = (refs[t][...] + refs[n + t][...]).astype(BF16)

    blk = lambda off: pl.BlockSpec((None, tr, ncol), lambda k, i, ids: (ids[off + k], i, 0))
    out = pl.BlockSpec((None, tr, ncol), lambda k, i, ids: (k, i, 0))
    return pl.pallas_call(
        body, name=name,
        out_shape=tuple(jax.ShapeDtypeStruct((3, r, ncol), BF16) for _ in gs),
        grid_spec=pltpu.PrefetchScalarGridSpec(
            num_scalar_prefetch=1, grid=(3, r // tr),
            in_specs=[blk(0)] * n + [blk(3)] * n, out_specs=tuple([out] * n)),
        compiler_params=_cparams(2),
    )(blk_ids, *gs, *p1s)


def _rs_final_sum(gs, p1s, p3s, my_ids, *, name, tr=256):
    n = len(gs)
    _, r, ncol = gs[0].shape
    tr = _tile_rows(tr, r)

    def body(id_ref, *refs):
        for t in range(n):
            g_ref, s_ref = refs[t], refs[n + t]
            a_ref, b_ref, c_ref = refs[2 * n + 3 * t:2 * n + 3 * t + 3]
            own = g_ref[...] + s_ref[...]
            refs[5 * n + t][...] = (((own + a_ref[...].astype(F32)) + b_ref[...].astype(F32))
                                    + c_ref[...].astype(F32))

    sel = lambda which: pl.BlockSpec((None, tr, ncol), lambda i, ids: (ids[which], i, 0))
    fix = lambda k: pl.BlockSpec((None, tr, ncol), lambda i, ids: (k, i, 0))
    p3_specs, p3_args = [], []
    for p3 in p3s:
        p3_specs += [fix(0), fix(1), fix(2)]
        p3_args += [p3, p3, p3]
    return pl.pallas_call(
        body, name=name,
        out_shape=tuple(jax.ShapeDtypeStruct((r, ncol), F32) for _ in gs),
        grid_spec=pltpu.PrefetchScalarGridSpec(
            num_scalar_prefetch=1, grid=(r // tr,),
            in_specs=[sel(0)] * n + [sel(1)] * n + p3_specs,
            out_specs=tuple([pl.BlockSpec((tr, ncol), lambda i, ids: (i, 0))] * n)),
        compiler_params=_cparams(1),
    )(my_ids, *gs, *p1s, *p3_args)


def _matmul(a, b, *, name, ta=False, tb=False, epi="plain", extras=(), out_dtype=F32, tm=None, tn=1024, tk=1024,
            b_kind=None, layer=0, side=None):
    if ta:
        kdim, m = a.shape
    else:
        m, kdim = a.shape
    pair = 1
    if b_kind is None:
        if tb:
            n, kb = b.shape
        else:
            kb, n = b.shape
        tn, tk = _tile_lanes(tn, n), _tile_lanes(tk, kb)
        b_spec = (pl.BlockSpec((tn, tk), lambda i, j, k: (j, k)) if tb
                  else pl.BlockSpec((tk, tn), lambda i, j, k: (k, j)))
    elif b_kind == "colblk":
        assert not ta
        _, _, kw, nsh = b.shape
        if tb:
            kb, n, pair = N_DEV * nsh, kw, 2
            tn, tk = _tile_lanes(tn, n), pair * nsh
            b_spec = pl.BlockSpec((pair, None, tn, nsh), lambda i, j, k: (k, layer, j, 0))
        else:
            kb, n, pair = kw, N_DEV * nsh, 2
            tn, tk = pair * nsh, _tile_lanes(tk, kb)
            b_spec = pl.BlockSpec((pair, None, tk, nsh), lambda i, j, k: (j, layer, k, 0))
    elif b_kind == "rowblk":
        assert not ta
        _, _, r, ncol = b.shape
        pair = 2
        if tb:
            kb, n = ncol, N_DEV * r
            tn, tk = pair * r, _tile_lanes(tk, kb)
            b_spec = pl.BlockSpec((pair, None, r, tk), lambda i, j, k: (j, layer, 0, k))
        else:
            kb, n = N_DEV * r, ncol
            tn, tk = _tile_lanes(tn, n), pair * r
            b_spec = pl.BlockSpec((pair, None, r, tn), lambda i, j, k: (k, layer, 0, j))
    else:
        raise ValueError(b_kind)
    assert kdim == kb, (a.shape, b.shape)
    if tm is None:
        tm = 1024 if (ta or epi == "resgate") else 2048
    tm = _tile_lanes(tm, m)
    nk = kdim // tk
    a_spec = pl.BlockSpec((tk, tm), lambda i, j, k: (k, i)) if ta else pl.BlockSpec((tm, tk), lambda i, j, k: (i, k))
    dims = (((0 if ta else 1,), (1 if tb else 0,)), ((), ()))
    mn_spec = pl.BlockSpec((tm, tn), lambda i, j, k: (i, j))
    row_spec = pl.BlockSpec((1, tn), lambda i, j, k: (0, j))
    if epi == "resgate":
        extra_specs = [mn_spec, row_spec]
        out_shape = (jax.ShapeDtypeStruct((m, n), BF16), jax.ShapeDtypeStruct((m, n), F32))
        out_specs = (mn_spec, mn_spec)
    elif epi == "dact":
        extra_specs = [mn_spec]
        out_shape = jax.ShapeDtypeStruct((m, n), out_dtype)
        out_specs = mn_spec
    else:
        extra_specs = []
        out_shape = jax.ShapeDtypeStruct((m, n), out_dtype)
        out_specs = mn_spec
    n_extra = len(extra_specs)

    def body(a_ref, b_ref, *rest):
        ex = rest[:n_extra]
        outs = rest[n_extra:n_extra + n_out]
        k = pl.program_id(2)

        def prod():
            av = a_ref[...].astype(BF16)
            if b_kind == "rowblk":
                bv = b_ref[...].astype(BF16)
                return lax.dot_general(av, bv.reshape(bv.shape[0] * bv.shape[1], bv.shape[2]), dims,
                                       preferred_element_type=F32)
            if b_kind == "colblk" and tb:
                nsh = b_ref.shape[-1]
                return sum(lax.dot_general(av[:, p * nsh:(p + 1) * nsh], b_ref[p].astype(BF16), dims,
                                           preferred_element_type=F32) for p in range(pair))
            if b_kind == "colblk":
                return jnp.concatenate([lax.dot_general(av, b_ref[p].astype(BF16), dims, preferred_element_type=F32)
                                        for p in range(pair)], axis=1)
            return lax.dot_general(av, b_ref[...].astype(BF16), dims, preferred_element_type=F32)

        def finish(r):
            if epi == "plain":
                outs[0][...] = r.astype(outs[0].dtype)
            elif epi == "resgate":
                outs[0][...] = r.astype(BF16)
                outs[1][...] = ex[0][...] + ex[1][...] * r
            elif epi == "relu2":
                p = jnp.maximum(r, 0.0)
                outs[0][...] = (p * p).astype(outs[0].dtype)
            elif epi == "dact":
                outs[0][...] = (r * (2.0 * jnp.sqrt(ex[0][...].astype(F32)))).astype(outs[0].dtype)

        if nk == 1:
            finish(prod())
        else:
            acc = rest[-1]

            @pl.when(k == 0)
            def _():
                acc[...] = prod()

            if nk > 2:
                @pl.when(jnp.logical_and(k > 0, k < nk - 1))
                def _():
                    acc[...] += prod()

            @pl.when(k == nk - 1)
            def _():
                finish(acc[...] + prod())

    n_out = 2 if epi == "resgate" else 1
    if n_out == 1:
        out_shape, out_specs = (out_shape,), (out_specs,)
    res, side_res = _carry(
        body, side, name=name, grid=(m // tm, n // tn, nk), in_specs=[a_spec, b_spec] + extra_specs,
        out_specs=out_specs, out_shape=out_shape,
        scratch_shapes=[pltpu.VMEM((tm, tn), F32)] if nk > 1 else [], args=(a, b, *extras))
    res = res if n_out == 2 else res[0]
    return res if side is None else (res, side_res)


def _norm_mod(x, g, scale, shift, *, name, tm=512):
    s, d = x.shape
    tm = _tile(tm, s)

    def body(x_ref, g_ref, sc_ref, sh_ref, h_ref):
        xv = x_ref[...]
        r = lax.rsqrt(jnp.mean(xv * xv, axis=-1, keepdims=True) + EPS)
        h_ref[...] = (((xv * r) * g_ref[...]) * (1.0 + sc_ref[...]) + sh_ref[...]).astype(BF16)

    row = pl.BlockSpec((1, d), lambda i: (0, 0))
    return pl.pallas_call(
        body, name=name, out_shape=jax.ShapeDtypeStruct((s, d), BF16), grid=(s // tm,),
        in_specs=[pl.BlockSpec((tm, d), lambda i: (i, 0)), row, row, row],
        out_specs=pl.BlockSpec((tm, d), lambda i: (i, 0)),
        compiler_params=_cparams(1),
    )(x, g, scale, shift)


def _norm_mod_bwd(x, dh, dres, g, scale, *, name, tm=512, gated=None, side=None):
    s, d = x.shape
    tm = _tile(tm, s)
    n_in = 5 if gated is None else 7

    def body(*refs):
        x_ref, dh_ref, dr_ref, g_ref, sc_ref = refs[:5]
        dx_ref, acc_ref = refs[n_in:n_in + 2]
        i = pl.program_id(0)

        @pl.when(i == 0)
        def _():
            acc_ref[...] = jnp.zeros_like(acc_ref)

        xv = x_ref[...]
        dhv = dh_ref[...]
        gv = g_ref[...]
        one_sc = 1.0 + sc_ref[...]
        r = lax.rsqrt(jnp.mean(xv * xv, axis=-1, keepdims=True) + EPS)
        xn = xv * r
        dxn = dhv * (gv * one_sc)
        dxv = dr_ref[...] + r * (dxn - xn * jnp.mean(dxn * xn, axis=-1, keepdims=True))
        dx_ref[...] = dxv
        dhxn = dhv * xn
        acc_ref[0:1, :] += jnp.sum(dhv, axis=0, keepdims=True)
        acc_ref[1:2, :] += jnp.sum(dhxn * gv, axis=0, keepdims=True)
        acc_ref[2:3, :] += jnp.sum(dhxn * one_sc, axis=0, keepdims=True)
        if gated is not None:
            z_ref, gate_ref, dz_ref = refs[5], refs[6], refs[n_in + 2]
            dz_ref[...] = (dxv * gate_ref[...]).astype(BF16)
            acc_ref[3:4, :] += jnp.sum(dxv * z_ref[...], axis=0, keepdims=True)

    row = pl.BlockSpec((1, d), lambda i: (0, 0))
    blk = pl.BlockSpec((tm, d), lambda i: (i, 0))
    in_specs, args = [blk, blk, blk, row, row], [x, dh, dres, g, scale]
    out_shape = [jax.ShapeDtypeStruct((s, d), F32), jax.ShapeDtypeStruct((8, d), F32)]
    out_specs = [blk, pl.BlockSpec((8, d), lambda i: (0, 0))]
    if gated is not None:
        in_specs += [blk, row]
        args += list(gated)
        out_shape.append(jax.ShapeDtypeStruct((s, d), BF16))
        out_specs.append(blk)
    res, side_res = _carry(body, side, name=name, grid=(s // tm,), in_specs=in_specs, out_specs=out_specs,
                           out_shape=out_shape, scratch_shapes=[], args=args)
    return res if side is None else res + (side_res,)


def _loss_grad(xf, target, z, gate, *, name, tm=512):
    s, d = xf.shape
    tm = _tile(tm, s)
    nt = s // tm

    def body(x_ref, t_ref, z_ref, gate_ref, dx_ref, loss_ref, dz_ref, dgate_ref, acc_ref):
        i = pl.program_id(0)

        @pl.when(i == 0)
        def _():
            acc_ref[...] = jnp.zeros_like(acc_ref)
            dgate_ref[...] = jnp.zeros_like(dgate_ref)

        e = x_ref[...] - t_ref[...]
        dxv = e * (1.0 / d)
        dx_ref[...] = dxv
        dz_ref[...] = (dxv * gate_ref[...]).astype(BF16)
        dgate_ref[...] += jnp.sum(dxv * z_ref[...], axis=0, keepdims=True)
        acc_ref[...] += jnp.sum(e * e, axis=0, keepdims=True)

        @pl.when(i == nt - 1)
        def _():
            loss_ref[...] = (0.5 / d) * jnp.sum(acc_ref[...], axis=1, keepdims=True)

    blk = pl.BlockSpec((tm, d), lambda i: (i, 0))
    row = pl.BlockSpec((1, d), lambda i: (0, 0))
    return pl.pallas_call(
        body, name=name,
        out_shape=(jax.ShapeDtypeStruct((s, d), F32), jax.ShapeDtypeStruct((1, 1), F32),
                   jax.ShapeDtypeStruct((s, d), BF16), jax.ShapeDtypeStruct((1, d), F32)),
        grid=(nt,), in_specs=[blk, blk, blk, row],
        out_specs=(blk, pl.BlockSpec((1, 1), lambda i: (0, 0)), blk, row),
        scratch_shapes=[pltpu.VMEM((1, d), F32)],
        compiler_params=_cparams(1),
    )(xf, target, z, gate)


def _shift_down(p, prev, k):
    tm = p.shape[0]
    row = lax.broadcasted_iota(jnp.int32, p.shape, 0)
    out = pltpu.roll(p, k, 0)
    for j in range(k):
        out = jnp.where(row == j, prev[8 - k + j:8 - k + j + 1, :], out)
    return out


def _shift_up(p, nxt, k):
    tm = p.shape[0]
    row = lax.broadcasted_iota(jnp.int32, p.shape, 0)
    out = pltpu.roll(p, tm - k, 0)
    for j in range(k):
        out = jnp.where(row == tm - k + j, nxt[j:j + 1, :], out)
    return out


def _conv_fwd(u, w, *, name, tm=512):
    s = u.shape[0]
    tm = _tile(tm, s)
    c = CONV_DIM

    def body(ab_ref, ac_ref, ah_ref, w_ref, y_ref, carry_ref):
        i = pl.program_id(0)

        @pl.when(i == 0)
        def _():
            carry_ref[...] = jnp.zeros_like(carry_ref)

        p = ac_ref[...] * ah_ref[...]
        prev = carry_ref[...]
        wv = w_ref[...]
        conv = wv[2:3, :] * p + wv[1:2, :] * _shift_down(p, prev, 1) + wv[0:1, :] * _shift_down(p, prev, 2)
        y_ref[...] = (ab_ref[...] * conv).astype(BF16)
        carry_ref[...] = p[tm - 8:tm, :]

    return pl.pallas_call(
        body, name=name, out_shape=jax.ShapeDtypeStruct((s, c), BF16), grid=(s // tm,),
        in_specs=[pl.BlockSpec((tm, c), lambda i: (i, 0)), pl.BlockSpec((tm, c), lambda i: (i, 1)),
                  pl.BlockSpec((tm, c), lambda i: (i, 2)), pl.BlockSpec((3, c), lambda i: (0, 0))],
        out_specs=pl.BlockSpec((tm, c), lambda i: (i, 0)),
        scratch_shapes=[pltpu.VMEM((8, c), F32)],
        compiler_params=_cparams(1),
    )(u, u, u, w)


def _conv_bwd(u, dy, w, *, name, tm=512):
    s = u.shape[0]
    tm = _tile(tm, s)
    nt = s // tm
    c = CONV_DIM
    hb = tm // 8

    def body(ab_ref, ac_ref, ah_ref, hc_ref, hh_ref, dy_ref, w_ref, du_ref, dw_ref, carry_ref):
        i = pl.program_id(0)

        @pl.when(i == 0)
        def _():
            carry_ref[...] = jnp.zeros_like(carry_ref)
            dw_ref[...] = jnp.zeros_like(dw_ref)

        first_tile = (nt - 1 - i) == 0
        ab, ac, ah = ab_ref[...], ac_ref[...], ah_ref[...]
        p = ac * ah
        prev = jnp.where(first_tile, 0.0, hc_ref[...] * hh_ref[...])
        wv = w_ref[...]
        p1 = _shift_down(p, prev, 1)
        p2 = _shift_down(p, prev, 2)
        conv = wv[2:3, :] * p + wv[1:2, :] * p1 + wv[0:1, :] * p2
        dyv = dy_ref[...]
        dconv = dyv * ab
        nxt = carry_ref[...]
        dp = wv[2:3, :] * dconv + wv[1:2, :] * _shift_up(dconv, nxt, 1) + wv[0:1, :] * _shift_up(dconv, nxt, 2)
        du_ref[:, 0:c] = (dyv * conv).astype(BF16)
        du_ref[:, c:2 * c] = (dp * ah).astype(BF16)
        du_ref[:, 2 * c:3 * c] = (dp * ac).astype(BF16)
        dw_ref[0:1, :] += jnp.sum(dconv * p2, axis=0, keepdims=True)
        dw_ref[1:2, :] += jnp.sum(dconv * p1, axis=0, keepdims=True)
        dw_ref[2:3, :] += jnp.sum(dconv * p, axis=0, keepdims=True)
        carry_ref[...] = dconv[0:8, :]

    rev = lambda i: nt - 1 - i
    halo = lambda i: jnp.maximum(rev(i) * hb - 1, 0)
    return pl.pallas_call(
        body, name=name,
        out_shape=(jax.ShapeDtypeStruct((s, 3 * c), BF16), jax.ShapeDtypeStruct((8, c), F32)),
        grid=(nt,),
        in_specs=[pl.BlockSpec((tm, c), lambda i: (rev(i), 0)), pl.BlockSpec((tm, c), lambda i: (rev(i), 1)),
                  pl.BlockSpec((tm, c), lambda i: (rev(i), 2)),
                  pl.BlockSpec((8, c), lambda i: (halo(i), 1)), pl.BlockSpec((8, c), lambda i: (halo(i), 2)),
                  pl.BlockSpec((tm, c), lambda i: (rev(i), 0)), pl.BlockSpec((3, c), lambda i: (0, 0))],
        out_specs=(pl.BlockSpec((tm, 3 * c), lambda i: (rev(i), 0)), pl.BlockSpec((8, c), lambda i: (0, 0))),
        scratch_shapes=[pltpu.VMEM((8, c), F32)],
        compiler_params=_cparams(1),
    )(u, u, u, u, u, dy, w)


def _lower_bound(lbl):
    m = jnp.max(lbl, axis=0, keepdims=True)
    e = jnp.exp(lbl - m)
    return e[0:1, :] / jnp.sum(e, axis=0, keepdims=True)


def _hg_masks():
    t = HG_TILE
    row = lax.broadcasted_iota(jnp.int32, (t, t), 0)
    col = lax.broadcasted_iota(jnp.int32, (t, t), 1)
    same = (row >= CHUNK) == (col >= CHUNK)
    lower = same & (col <= row)
    upper = same & (row <= col)
    return row, col, lower, upper


def _hg_gates(hf, lb):
    sig = jax.nn.sigmoid(hf)
    f = lb + (1.0 - lb) * sig
    return sig, f, jnp.log(f), 1.0 - f


def _hg_refs(b_ref, hs):
    refs = []
    for i in range(HG_TILE // HG_SUB):
        if (i * HG_SUB) % CHUNK == 0:
            refs.append(jnp.zeros((1, HG_DK), F32))
        else:
            refs.append(b_ref[i * HG_SUB - 1:i * HG_SUB, hs])
    return refs


def _hgrn_fwd(u, lbl, gn, *, name, side=None):
    s = u.shape[0]
    t = HG_TILE
    nt = s // t
    nsub = t // HG_SUB
    w = HG_WIDTH

    def body(hq_ref, hf_ref, hi_ref, hg_ref, lbl_ref, gn_ref, y_ref, o_ref, sall_ref, st_ref, b_ref):
        i = pl.program_id(0)

        @pl.when(i == 0)
        def _():
            st_ref[...] = jnp.zeros_like(st_ref)

        lb = _lower_bound(lbl_ref[...])
        _, _, g, kin = _hg_gates(hf_ref[...], lb)
        _, _, lower, _ = _hg_masks()
        b_ref[...] = _exact_left(lower.astype(BF16), g)

        for h in range(HG_HEADS):
            hs = slice(h * HG_DK, (h + 1) * HG_DK)
            bh = b_ref[:, hs]
            qh = hq_ref[:, hs]
            kh = kin[:, hs]
            vh = hi_ref[:, hs]
            vsp = _sp(vh)
            refs = _hg_refs(b_ref, hs)
            rmat = jnp.concatenate([jnp.broadcast_to(r, (HG_SUB, HG_DK)) for r in refs], axis=0)
            qt = qh * jnp.exp(bh - rmat)
            prow = []
            for j in range(nsub):
                kj = kh * jnp.exp(jnp.minimum(refs[j] - bh, HG_EXP_CLAMP))
                prow.append(_dot3(_nt, _sp(qt[j * HG_SUB:(j + 1) * HG_SUB]), _sp(kj)))
            p = jnp.where(lower, jnp.concatenate(prow, axis=0), 0.0)
            intra = _dot3(_nn, _sp(p), vsp)
            o_parts = []
            for c in range(t // CHUNK):
                rs = slice(c * CHUNK, (c + 1) * CHUNK)
                st0 = st_ref[hs, :]
                sall_ref[c * w + h * HG_DK:c * w + (h + 1) * HG_DK, :] = st0
                bl = b_ref[c * CHUNK + CHUNK - 1:c * CHUNK + CHUNK, hs]
                qf = qh[rs] * jnp.exp(bh[rs])
                o_parts.append(_dot3(_nt, _sp(qf), _sp(st0)) + intra[rs])
                khat = kh[rs] * jnp.exp(bl - bh[rs])
                st_ref[hs, :] = st0 * jnp.exp(bl) + _dot3(_tn, _sp(vh[rs]), _sp(khat))
            o = jnp.concatenate(o_parts, axis=0)
            o_ref[:, hs] = o
            r = lax.rsqrt(jnp.mean(o * o, axis=-1, keepdims=True) + EPS)
            hg = hg_ref[:, hs]
            y_ref[:, hs] = (((o * r) * gn_ref[:, hs]) * (hg * jax.nn.sigmoid(hg))).astype(BF16)

    blk = lambda j: pl.BlockSpec((t, w), lambda i, j=j: (i, j))
    srows = (t // CHUNK) * w
    res, side_res = _carry(
        body, side, name=name,
        out_shape=(jax.ShapeDtypeStruct((s, w), BF16), jax.ShapeDtypeStruct((s, w), F32),
                   jax.ShapeDtypeStruct((nt * srows, HG_DK), F32)),
        grid=(nt,),
        in_specs=[blk(3), blk(4), blk(5), blk(6), pl.BlockSpec((3, w), lambda i: (0, 0)),
                  pl.BlockSpec((1, w), lambda i: (0, 0))],
        out_specs=(pl.BlockSpec((t, w), lambda i: (i, 0)), pl.BlockSpec((t, w), lambda i: (i, 0)),
                   pl.BlockSpec((srows, HG_DK), lambda i: (i, 0))),
        scratch_shapes=[pltpu.VMEM((w, HG_DK), F32), pltpu.VMEM((t, w), F32)],
        args=(u, u, u, u, lbl, gn))
    return res if side is None else (res, side_res)


def _hgrn_bwd(u, o_all, sall, dy, lbl, gn, *, name, side=None):
    s = u.shape[0]
    t = HG_TILE
    nt = s // t
    nsub = t // HG_SUB
    w = HG_WIDTH
    nch = t // CHUNK

    def body(hq_ref, hf_ref, hi_ref, hg_ref, o_ref, sall_ref, dy_ref, lbl_ref, gn_ref,
             du_ref, acc_ref, dst_ref, b_ref):
        i = pl.program_id(0)

        @pl.when(i == 0)
        def _():
            dst_ref[...] = jnp.zeros_like(dst_ref)
            acc_ref[...] = jnp.zeros_like(acc_ref)

        lb = _lower_bound(lbl_ref[...])
        sig, f, g, kin = _hg_gates(hf_ref[...], lb)
        row, col, lower, upper = _hg_masks()
        b_ref[...] = _exact_left(lower.astype(BF16), g)
        upper_bf = upper.astype(BF16)
        rowblk = [((row >= j * HG_SUB) & (row < (j + 1) * HG_SUB)) for j in range(nsub)]
        colblk = [((col >= j * HG_SUB) & (col < (j + 1) * HG_SUB)) for j in range(nsub)]
        row1 = lax.broadcasted_iota(jnp.int32, (t, HG_DK), 0)

        for h in range(HG_HEADS):
            hs = slice(h * HG_DK, (h + 1) * HG_DK)
            bh = b_ref[:, hs]
            qh = hq_ref[:, hs]
            kh = kin[:, hs]
            vh = hi_ref[:, hs]
            vsp = _sp(vh)
            hg = hg_ref[:, hs]
            gnh = gn_ref[:, hs]
            o = o_ref[:, hs]
            dyv = dy_ref[:, hs]
            sg = jax.nn.sigmoid(hg)
            r = lax.rsqrt(jnp.mean(o * o, axis=-1, keepdims=True) + EPS)
            ohat = o * r
            du_ref[:, 3 * w + h * HG_DK:3 * w + (h + 1) * HG_DK] = (
                dyv * (ohat * gnh) * (sg * (1.0 + hg * (1.0 - sg)))).astype(BF16)
            don = dyv * (hg * sg)
            acc_ref[0:1, hs] += jnp.sum(don * ohat, axis=0, keepdims=True)
            dohat = don * gnh
            do = r * (dohat - ohat * jnp.mean(dohat * ohat, axis=-1, keepdims=True))
            dosp = _sp(do)
            refs = _hg_refs(b_ref, hs)
            rmat = jnp.concatenate([jnp.broadcast_to(rr, (HG_SUB, HG_DK)) for rr in refs], axis=0)
            eq = jnp.exp(bh - rmat)
            qt = qh * eq
            qtsp = _sp(qt)
            dp = jnp.where(lower, _dot3(_nt, dosp, vsp), 0.0)
            dpt = jnp.where(upper, _dot3(_nt, vsp, dosp), 0.0)
            pt = jnp.zeros((t, t), F32)
            dk = jnp.zeros((t, HG_DK), F32)
            dq_rows = []
            for j in range(nsub):
                ek = jnp.exp(jnp.minimum(refs[j] - bh, HG_EXP_CLAMP))
                kjsp = _sp(kh * ek)
                pt = pt + _nt(kjsp[0], jnp.where(rowblk[j], qtsp[0], 0))
                dq_rows.append(_dot3(_nn, _sp(dp[j * HG_SUB:(j + 1) * HG_SUB]), kjsp))
                dk = dk + ek * _dot3(_nn, _sp(jnp.where(colblk[j], dpt, 0.0)), qtsp)
            pt = jnp.where(upper, pt, 0.0)
            dv = _nn(pt.astype(BF16), dosp[0])
            dq = jnp.concatenate(dq_rows, axis=0) * eq
            dq_c, dk_c, dv_c, ex_c = [None] * nch, [None] * nch, [None] * nch, [None] * nch
            for c in reversed(range(nch)):
                rs = slice(c * CHUNK, (c + 1) * CHUNK)
                st0 = sall_ref[c * w + h * HG_DK:c * w + (h + 1) * HG_DK, :]
                dst1 = dst_ref[hs, :]
                dst1sp = _sp(dst1)
                dosp_c = _sp(do[rs])
                bl = b_ref[c * CHUNK + CHUNK - 1:c * CHUNK + CHUNK, hs]
                e = jnp.exp(bh[rs])
                el = jnp.exp(bl)
                ekl = jnp.exp(bl - bh[rs])
                dq_c[c] = _dot3(_nn, dosp_c, _sp(st0)) * e
                khat = kh[rs] * ekl
                dv_c[c] = _nt(khat.astype(BF16), dst1sp[0])
                dkhat = _dot3(_nn, _sp(vh[rs]), dst1sp)
                dk_c[c] = dkhat * ekl
                ex_c[c] = (jnp.sum(dkhat * khat, axis=0, keepdims=True)
                           + el * jnp.sum(dst1 * st0, axis=0, keepdims=True))
                dst_ref[hs, :] = _dot3(_tn, dosp_c, _sp(qh[rs] * e)) + dst1 * el
            dq = dq + jnp.concatenate(dq_c, axis=0)
            dk = dk + jnp.concatenate(dk_c, axis=0)
            dv = dv + jnp.concatenate(dv_c, axis=0)
            db = qh * dq - kh * dk
            for c in range(nch):
                db = db + jnp.where(row1 == c * CHUNK + CHUNK - 1, ex_c[c], 0.0)
            dg = _exact_left(upper_bf, db)
            fh = f[:, hs]
            sgf = sig[:, hs]
            lbh = lb[:, hs]
            df = dg / fh - dk
            du_ref[:, hs] = dq.astype(BF16)
            du_ref[:, w + h * HG_DK:w + (h + 1) * HG_DK] = (df * (1.0 - lbh) * sgf * (1.0 - sgf)).astype(BF16)
            du_ref[:, 2 * w + h * HG_DK:2 * w + (h + 1) * HG_DK] = dv.astype(BF16)
            acc_ref[1:2, hs] += jnp.sum(df * (1.0 - sgf), axis=0, keepdims=True)

    rev = lambda i: nt - 1 - i
    blk = lambda j: pl.BlockSpec((t, w), lambda i, j=j: (rev(i), j))
    srows = nch * w
    res, side_res = _carry(
        body, side, name=name,
        out_shape=(jax.ShapeDtypeStruct((s, 4 * w), BF16), jax.ShapeDtypeStruct((8, w), F32)),
        grid=(nt,),
        in_specs=[blk(3), blk(4), blk(5), blk(6), pl.BlockSpec((t, w), lambda i: (rev(i), 0)),
                  pl.BlockSpec((srows, HG_DK), lambda i: (rev(i), 0)),
                  pl.BlockSpec((t, w), lambda i: (rev(i), 1)),
                  pl.BlockSpec((3, w), lambda i: (0, 0)), pl.BlockSpec((1, w), lambda i: (0, 0))],
        out_specs=(pl.BlockSpec((t, 4 * w), lambda i: (rev(i), 0)), pl.BlockSpec((8, w), lambda i: (0, 0))),
        scratch_shapes=[pltpu.VMEM((w, HG_DK), F32), pltpu.VMEM((t, w), F32)],
        args=(u, u, u, u, o_all, sall, dy, lbl, gn))
    return res if side is None else (res, side_res)


def _pair_matrix():
    row = lax.broadcasted_iota(jnp.int32, (LANES, LANES), 0)
    col = lax.broadcasted_iota(jnp.int32, (LANES, LANES), 1)
    return ((row >= SB_HEAD_DIM) == (col >= SB_HEAD_DIM)).astype(BF16)


def _qk_norm_fwd(qkv, qn, kn, *, name, tm=256):
    s = qkv.shape[0]
    d = D_MODEL
    tm = _tile(tm, s)

    def body(q_ref, k_ref, v_ref, qn_ref, kn_ref, qo_ref, ko_ref, vo_ref):
        bd = _pair_matrix()
        for src, gain, dst, fac in ((q_ref, qn_ref, qo_ref, SB_SCALE * LOG2E), (k_ref, kn_ref, ko_ref, None)):
            for grp in range(d // LANES):
                ls = slice(grp * LANES, (grp + 1) * LANES)
                xv = src[:, ls]
                ms = _exact_right(xv * xv, bd) * (1.0 / SB_HEAD_DIM)
                y = (xv * lax.rsqrt(ms + EPS)) * gain[:, ls]
                dst[:, ls] = (y if fac is None else y * fac).astype(BF16)
        vo_ref[...] = v_ref[...].astype(BF16)

    blk = lambda j: pl.BlockSpec((tm, d), lambda i, j=j: (i, j))
    row = pl.BlockSpec((1, d), lambda i: (0, 0))
    out = jax.ShapeDtypeStruct((s, d), BF16)
    return pl.pallas_call(
        body, name=name, out_shape=(out, out, out), grid=(s // tm,),
        in_specs=[blk(0), blk(1), blk(2), row, row],
        out_specs=(blk(0), blk(0), blk(0)),
        compiler_params=_cparams(1),
    )(qkv, qkv, qkv, qn, kn)


def _qk_norm_bwd(qkv, dqn, dkn, dv, qn, kn, *, name, tm=256):
    s = qkv.shape[0]
    d = D_MODEL
    tm = _tile(tm, s)

    def body(q_ref, k_ref, dq_ref, dk_ref, dv_ref, qn_ref, kn_ref, o_ref, acc_ref):
        i = pl.program_id(0)

        @pl.when(i == 0)
        def _():
            acc_ref[...] = jnp.zeros_like(acc_ref)

        bd = _pair_matrix()
        for idx, (src, dsrc, gain) in enumerate(((q_ref, dq_ref, qn_ref), (k_ref, dk_ref, kn_ref))):
            for grp in range(d // LANES):
                ls = slice(grp * LANES, (grp + 1) * LANES)
                xv = src[:, ls]
                dyv = dsrc[:, ls]
                r = lax.rsqrt(_exact_right(xv * xv, bd) * (1.0 / SB_HEAD_DIM) + EPS)
                xh = xv * r
                acc_ref[idx:idx + 1, ls] += jnp.sum(dyv * xh, axis=0, keepdims=True)
                dxh = dyv * gain[:, ls]
                mean = _exact_right(dxh * xh, bd) * (1.0 / SB_HEAD_DIM)
                o_ref[:, idx * d + grp * LANES:idx * d + (grp + 1) * LANES] = (r * (dxh - xh * mean)).astype(BF16)
        o_ref[:, 2 * d:3 * d] = dv_ref[...].astype(BF16)

    blk = lambda j: pl.BlockSpec((tm, d), lambda i, j=j: (i, j))
    row = pl.BlockSpec((1, d), lambda i: (0, 0))
    return pl.pallas_call(
        body, name=name,
        out_shape=(jax.ShapeDtypeStruct((s, 3 * d), BF16), jax.ShapeDtypeStruct((8, d), F32)),
        grid=(s // tm,),
        in_specs=[blk(0), blk(1), blk(0), blk(0), blk(0), row, row],
        out_specs=(pl.BlockSpec((tm, 3 * d), lambda i: (i, 0)), pl.BlockSpec((8, d), lambda i: (0, 0))),
        compiler_params=_cparams(1),
    )(qkv, qkv, dqn, dkn, dv, qn, kn)


def _sb_tile(qh, kb, suffix_ones, run, mask):
    z = _nt(qh, kb)
    neg_abs = lax.bitcast_convert_type(lax.bitcast_convert_type(z, jnp.uint32) | jnp.uint32(0x80000000), F32)
    l1m = -(jnp.maximum(z, 0.0) + jnp.log2(1.0 + jnp.exp2(neg_abs)))
    logb = z + l1m
    if mask is not None:
        l1m = jnp.where(mask, l1m, 0.0)
    later = _nn(l1m.astype(BF16), suffix_ones) + run
    wgt = jnp.exp2(logb + later)
    if mask is not None:
        wgt = jnp.where(mask, wgt, 0.0)
    return logb, l1m, wgt


def _suffix_ones(tk):
    row = lax.broadcasted_iota(jnp.int32, (tk, tk), 0)
    col = lax.broadcasted_iota(jnp.int32, (tk, tk), 1)
    return (row > col).astype(BF16)


def _sb_alive(runs):
    return jnp.max(jnp.maximum(runs[0], runs[1])) > -SB_DEAD


def _sb_mask(q0, j, nr, tk):
    qpos = q0 + lax.broadcasted_iota(jnp.int32, (nr, tk), 0)
    kpos = j * tk + lax.broadcasted_iota(jnp.int32, (nr, tk), 1)
    return kpos < qpos


def _sb_fwd(qn, kn, v, *, name, side=None):
    s, d = qn.shape
    tq, tk = _tile(SB_TQ, s), _tile(SB_TK, s)
    assert tk % tq == 0 or tq % tk == 0
    nq = s // tq

    def body(q_ref, k_ref, v_ref, o_ref, acc_ref):
        qi = pl.program_id(1)
        lane = lax.broadcasted_iota(jnp.int32, (tq, LANES), 1)
        first = lane < SB_HEAD_DIM
        q = q_ref[...]
        qh = [jnp.where(first, q, 0).astype(BF16), jnp.where(first, 0, q).astype(BF16)]
        ones = _suffix_ones(tk)
        acc_ref[...] = jnp.zeros_like(acc_ref)

        def tile(j, runs, masked, rows=(0, tq)):
            r0, nr = rows
            ks = pl.ds(pl.multiple_of(j * tk, tk), tk)
            kb = k_ref[ks, :]
            vb = v_ref[ks, :]
            mask = _sb_mask(qi * tq + r0, j, nr, tk) if masked else None
            new_runs = []
            for hh in range(2):
                _, l1m, wgt = _sb_tile(qh[hh][r0:r0 + nr], kb, ones, runs[hh], mask)
                acc_ref[hh, r0:r0 + nr] += _nn(wgt.astype(BF16), vb)
                new_runs.append(runs[hh] + jnp.sum(l1m, axis=1, keepdims=True))
            return tuple(new_runs)

        nfull = (qi * tq) // tk
        zero = jnp.zeros((tq, 1), F32)
        runs = (zero, zero)
        for m in reversed(range(max(tq // tk, 1))):
            r0 = m * tk
            part = tile(nfull + m, tuple(r[r0:] for r in runs), True, (r0, tq - r0))
            runs = tuple(jnp.concatenate([r[:r0], p], axis=0) if r0 else p for r, p in zip(runs, part))

        def step(c):
            it, _, r = c
            r = tile(nfull - 1 - it, r, False)
            return it + 1, _sb_alive(r), r

        lax.while_loop(lambda c: jnp.logical_and(c[0] < nfull, c[1]), step, (0, _sb_alive(runs), runs))
        o_ref[...] = jnp.where(first, acc_ref[0], acc_ref[1])

    res, side_res = _carry(
        body, side, name=name, out_shape=(jax.ShapeDtypeStruct((s, d), F32),), grid=(d // LANES, nq),
        in_specs=[pl.BlockSpec((tq, LANES), lambda p, i: (i, p)), pl.BlockSpec((s, LANES), lambda p, i: (0, p)),
                  pl.BlockSpec((s, LANES), lambda p, i: (0, p))],
        out_specs=(pl.BlockSpec((tq, LANES), lambda p, i: (i, p)),),
        scratch_shapes=[pltpu.VMEM((2, tq, LANES), F32)], args=(qn, kn, v))
    return res[0] if side is None else (res[0], side_res)


def _sb_bwd(qn, kn, v, o, do, *, name, side=None):
    s, d = qn.shape
    tq, tk = _tile(SB_TQ, s), _tile(SB_TK, s)
    assert tk % tq == 0 or tq % tk == 0
    nq = s // tq

    def body(q_ref, k_ref, v_ref, o_ref, do_ref, dq_ref, dk_ref, dv_ref, acc_ref):
        qi = pl.program_id(1)

        @pl.when(qi == 0)
        def _():
            dk_ref[...] = jnp.zeros_like(dk_ref)
            dv_ref[...] = jnp.zeros_like(dv_ref)

        first = lax.broadcasted_iota(jnp.int32, (tq, LANES), 1) < SB_HEAD_DIM
        sel = [first, jnp.logical_not(first)]
        kfirst = lax.broadcasted_iota(jnp.int32, (tk, LANES), 1) < SB_HEAD_DIM
        ksel = [kfirst, jnp.logical_not(kfirst)]
        q = q_ref[...]
        dob = do_ref[...].astype(BF16)
        qh = [jnp.where(sel[hh], q, 0).astype(BF16) for hh in range(2)]
        doh = [jnp.where(sel[hh], dob, 0).astype(BF16) for hh in range(2)]
        prod = dob.astype(F32) * o_ref[...]
        gtot = [jnp.sum(jnp.where(sel[hh], prod, 0.0), axis=1, keepdims=True) for hh in range(2)]
        ones = _suffix_ones(tk)
        acc_ref[...] = jnp.zeros_like(acc_ref)

        def tile(j, carry, masked, rows=(0, tq)):
            r0, nr = rows
            runs, gruns = carry
            ks = pl.ds(pl.multiple_of(j * tk, tk), tk)
            kb = k_ref[ks, :]
            vb = v_ref[ks, :]
            mask = _sb_mask(qi * tq + r0, j, nr, tk) if masked else None
            new_runs, new_gruns = [], []
            dk_add = jnp.zeros((tk, LANES), F32)
            dv_add = jnp.zeros((tk, LANES), F32)
            for hh in range(2):
                qs, dos = qh[hh][r0:r0 + nr], doh[hh][r0:r0 + nr]
                logb, l1m, wgt = _sb_tile(qs, kb, ones, runs[hh], mask)
                wb = wgt.astype(BF16)
                g = _nt(dos, vb) * wb.astype(F32)
                gsuf = _exact_right2(g, ones) + g + gruns[hh]
                dz = g - jnp.exp2(logb) * (g + (gtot[hh][r0:r0 + nr] - gsuf))
                if masked:
                    dz = jnp.where(mask, dz, 0.0)
                dzb = dz.astype(BF16)
                acc_ref[hh, r0:r0 + nr] += _nn(dzb, kb)
                dk_add = dk_add + jnp.where(ksel[hh], _tn(dzb, qs), 0.0)
                dv_add = dv_add + jnp.where(ksel[hh], _tn(wb, dos), 0.0)
                new_runs.append(runs[hh] + jnp.sum(l1m, axis=1, keepdims=True))
                new_gruns.append(gruns[hh] + jnp.sum(g, axis=1, keepdims=True))
            dk_ref[ks, :] += dk_add * LN2
            dv_ref[ks, :] += dv_add
            return tuple(new_runs), tuple(new_gruns)

        nfull = (qi * tq) // tk
        zero = jnp.zeros((tq, 1), F32)
        carry = ((zero, zero), (zero, zero))
        for m in reversed(range(max(tq // tk, 1))):
            r0 = m * tk
            part = tile(nfull + m, tuple(tuple(r[r0:] for r in rs) for rs in carry), True, (r0, tq - r0))
            carry = tuple(tuple(jnp.concatenate([r[:r0], p], axis=0) if r0 else p for r, p in zip(rs, ps))
                          for rs, ps in zip(carry, part))

        def step(c):
            it, _, cr = c
            cr = tile(nfull - 1 - it, cr, False)
            return it + 1, _sb_alive(cr[0]), cr

        lax.while_loop(lambda c: jnp.logical_and(c[0] < nfull, c[1]), step, (0, _sb_alive(carry[0]), carry))
        dq_ref[...] = jnp.where(first, acc_ref[0], acc_ref[1]) * SB_SCALE

    blk = pl.BlockSpec((tq, LANES), lambda p, i: (i, p))
    full = pl.BlockSpec((s, LANES), lambda p, i: (0, p))
    out = jax.ShapeDtypeStruct((s, d), F32)
    res, side_res = _carry(
        body, side, name=name, out_shape=(out, out, out), grid=(d // LANES, nq),
        in_specs=[blk, full, full, blk, blk], out_specs=(blk, full, full),
        scratch_shapes=[pltpu.VMEM((2, tq, LANES), F32)], args=(qn, kn, v, o, do))
    return res if side is None else (res, side_res)


def _mod_part(c_all, ada_w, ada_b_my, *, name):
    nl, d, ncol = ada_w.shape

    def body(c_ref, w_ref, b_ref, part_ref, ca_ref):
        cv = c_ref[...]
        ca = cv * jax.nn.sigmoid(cv)
        ca_ref[...] = ca
        part_ref[...] = _nn(ca.astype(BF16), w_ref[...].astype(BF16)) + b_ref[...]

    return pl.pallas_call(
        body, name=name,
        out_shape=(jax.ShapeDtypeStruct((nl, N_DEV, ncol), F32), jax.ShapeDtypeStruct((N_DEV, d), F32)),
        grid=(nl,),
        in_specs=[pl.BlockSpec((N_DEV, d), lambda l: (0, 0)), pl.BlockSpec((None, d, ncol), lambda l: (l, 0, 0)),
                  pl.BlockSpec((None, 1, ncol), lambda l: (l, 0, 0))],
        out_specs=(pl.BlockSpec((None, N_DEV, ncol), lambda l: (l, 0, 0)), pl.BlockSpec((N_DEV, d), lambda l: (0, 0))),
        compiler_params=_cparams(1),
    )(c_all, ada_w, ada_b_my)


PK_MOD, PK_NMIX, PK_NMLP, PK_HGN, PK_LB, PK_QN, PK_KN, PK_CONV, PK_ROWS = 0, 96, 112, 128, 132, 136, 144, 152, 168


def _small_grads(gath, ca_col, dmod_my, lbl4, *, name):
    def body(g_ref, ca_ref, dm_ref, lbl_ref, gw_ref, gsum_ref, glb_ref, gqk_ref):
        tot = g_ref[0]
        for dev in range(1, N_DEV):
            tot = tot + g_ref[dev]
        gsum_ref[...] = tot
        lv = lbl_ref[...]
        m = jnp.maximum(jnp.maximum(lv[0], lv[1]), lv[2])
        e = [jnp.exp(lv[k] - m) for k in range(3)]
        den = e[0] + e[1] + e[2]
        p = [ek / den for ek in e]
        dlb = tot[PK_LB:PK_LB + 4, :]
        glb_ref[0] = dlb * p[0] * (1.0 - p[0])
        glb_ref[1] = -dlb * p[0] * p[1]
        glb_ref[2] = -dlb * p[0] * p[2]
        for idx, base in enumerate((PK_QN, PK_KN)):
            rowsum = jnp.sum(tot[base:base + 8, :], axis=0, keepdims=True)
            gqk_ref[idx:idx + 1, :] = rowsum + pltpu.roll(rowsum, SB_HEAD_DIM, 1)
        for l in range(2):
            acc = ca_ref[0] * dm_ref[0, l:l + 1, :]
            for smp in range(1, N_DEV):
                acc = acc + ca_ref[smp] * dm_ref[smp, l:l + 1, :]
            gw_ref[l] = acc

    d, ncol = ca_col.shape[1], dmod_my.shape[2]
    vm = pl.BlockSpec(memory_space=pltpu.VMEM)
    return pl.pallas_call(
        body, name=name,
        out_shape=(jax.ShapeDtypeStruct((2, d, ncol), F32), jax.ShapeDtypeStruct((PK_ROWS, LANES), F32),
                   jax.ShapeDtypeStruct((3, 4, LANES), F32), jax.ShapeDtypeStruct((8, LANES), F32)),
        in_specs=[vm, vm, vm, vm], out_specs=(vm, vm, vm, vm),
        compiler_params=pltpu.CompilerParams(vmem_limit_bytes=VMEM_LIMIT),
    )(gath, ca_col, dmod_my, lbl4)


def _adamw_math(w, g, m, v):
    m = ADAM_B1 * m + (1.0 - ADAM_B1) * g
    v = ADAM_B2 * v + (1.0 - ADAM_B2) * (g * g)
    m_hat = m / (1.0 - ADAM_B1 ** ADAM_STEP)
    v_hat = v / (1.0 - ADAM_B2 ** ADAM_STEP)
    delta = -ADAM_LR * (m_hat / (jnp.sqrt(v_hat) + ADAM_EPS) + ADAM_WD * w)
    return delta, m, v


def _adamw(w, g, m, v, *, name, tr=256):
    r, n = w.shape
    tr = _tile(tr, r)

    def body(w_ref, g_ref, m_ref, v_ref, d_ref, mo_ref, vo_ref):
        dl, mn, vn = _adamw_math(w_ref[...], g_ref[...], m_ref[...], v_ref[...])
        d_ref[...] = dl
        mo_ref[...] = mn
        vo_ref[...] = vn

    blk = pl.BlockSpec((tr, n), lambda i: (i, 0))
    out = jax.ShapeDtypeStruct((r, n), F32)
    return pl.pallas_call(
        body, name=name, out_shape=(out, out, out), grid=(r // tr,),
        in_specs=[blk, blk, blk, blk], out_specs=(blk, blk, blk),
        compiler_params=_cparams(1),
    )(w, g, m, v)


def _adamw_small(items, *, name):
    n = len(items)

    def body(*refs):
        ins, outs = refs[:4 * n], refs[4 * n:]
        for k in range(n):
            dl, mn, vn = _adamw_math(*(r[...] for r in ins[4 * k:4 * k + 4]))
            outs[3 * k][...] = dl
            outs[3 * k + 1][...] = mn
            outs[3 * k + 2][...] = vn

    flat = [a for it in items for a in it]
    out_shape = tuple(jax.ShapeDtypeStruct(it[0].shape, F32) for it in items for _ in range(3))
    vm = pl.BlockSpec(memory_space=pltpu.VMEM)
    res = pl.pallas_call(
        body, name=name, out_shape=out_shape, in_specs=[vm] * (4 * n), out_specs=tuple([vm] * (3 * n)),
    )(*flat)
    return [tuple(res[3 * k:3 * k + 3]) for k in range(n)]


def _mlp_fwd(x, g, scale, shift, gate, w1g, w2g, tag, side_w1=None):
    h = _norm_mod(x, g, scale, shift, name=f"{tag}_norm")
    act = _matmul(h, w1g, b_kind="colblk", epi="relu2", out_dtype=BF16, name=f"{tag}_w1", side=side_w1)
    side_res = ()
    if side_w1 is not None:
        act, side_res = act
    z, x_out = _matmul(act, w2g, b_kind="rowblk", epi="resgate", extras=(x, gate), name=f"{tag}_w2")
    return x_out, (h, act, z), side_res


def _mlp_bwd(dz, dx_out, x, saved, g, scale, w1g, w2g, tag, gated, side_dact=None, make_side_dh=None):
    h, act, _ = saved
    du = _matmul(dz, w2g, tb=True, b_kind="rowblk", epi="dact", extras=(act,), out_dtype=BF16,
                 name=f"{tag}_dact", side=side_dact)
    res_dact = ()
    if side_dact is not None:
        du, res_dact = du
    dw2 = _matmul(act, dz, ta=True, name=f"{tag}_dw2")
    dw1_t = _matmul(du, h, ta=True, name=f"{tag}_dw1")
    side_dh = None if make_side_dh is None else make_side_dh(dw1_t, dw2)
    dh = _matmul(du, w1g, tb=True, b_kind="colblk", name=f"{tag}_dh", side=side_dh)
    res_dh = ()
    if side_dh is not None:
        dh, res_dh = dh
    dx, nacc, dz_mix = _norm_mod_bwd(x, dh, dx_out, g, scale, name=f"{tag}_norm_bwd", gated=gated)
    return dx, dw1_t, dw2, nacc, dz_mix, (res_dact, res_dh)


def kernel(x, c, ada_w, ada_b, norm_mix, norm_mlp, w_in_ab, conv_w, hg_norm, lb_logits, w_out_ab, w_qkv, q_norm, k_norm, w_out_c, mlp_w1, mlp_w2, loss_target, m_ada_w, m_ada_b, m_norm_mix, m_norm_mlp, m_w_in_ab, m_conv_w, m_hg_norm, m_lb_logits, m_w_out_ab, m_w_qkv, m_q_norm, m_k_norm, m_w_out_c, m_mlp_w1, m_mlp_w2, v_ada_w, v_ada_b, v_norm_mix, v_norm_mlp, v_w_in_ab, v_conv_w, v_hg_norm, v_lb_logits, v_w_out_ab, v_w_qkv, v_q_norm, v_k_norm, v_w_out_c, v_mlp_w1, v_mlp_w2):
    d = D_MODEL
    my_x, my_y, my_c = lax.axis_index("x"), lax.axis_index("y"), lax.axis_index("c")
    me = 4 * my_x + 2 * my_y + my_c
    xs = x[0]
    tgt = loss_target[0]

    def bf(w):
        return w.astype(BF16)

    ncv = CONV_DIM // N_DEV
    c_and_conv = jnp.concatenate([c, jnp.pad(conv_w[0], ((0, 0), (0, d - ncv))), jnp.zeros((4, d), F32)], axis=0)
    wing, c_and_conv = _run_side(_gather_side([bf(w_in_ab), c_and_conv]), name="gather_w_in")
    win = wing[:, 0].transpose(1, 0, 2).reshape(d, AB_IN)

    c_all = c_and_conv[:, 0]
    conv_full = c_and_conv[:, 1:4, :ncv].transpose(1, 0, 2).reshape(3, CONV_DIM)
    ncol = ada_w.shape[2]
    ada_b_my = lax.dynamic_slice(ada_b, (0, me * ncol), (2, ncol)).reshape(2, 1, ncol)
    part, c_act = _mod_part(c_all, ada_w, ada_b_my, name="mod_part")
    parts = _all_gather(part.reshape(2 * N_DEV, ncol), name="gather_mod", in_vmem=True)
    parts = parts.reshape(N_DEV, 2, N_DEV, ncol)
    mod = lax.dynamic_index_in_dim(parts, me, axis=2, keepdims=False)
    mod = mod.transpose(1, 0, 2).reshape(2, 6, 1, d)

    qn_t = jnp.tile(q_norm, (1, d // SB_HEAD_DIM))
    kn_t = jnp.tile(k_norm, (1, d // SB_HEAD_DIM))

    sh1, sc1, gt1, sh2, sc2, gt2 = [mod[0, k] for k in range(6)]
    h0 = _norm_mod(xs, norm_mix[0:1], sc1, sh1, name="l0_mix_norm")
    u = _matmul(h0, win, name="l0_in_proj")
    y_a = _conv_fwd(u, conv_full, name="l0_conv")
    (y_b, o_hg, sall), (woutg_ab, w1g0, w2g0) = _hgrn_fwd(
        u, lb_logits, hg_norm, name="l0_hgrn",
        side=_gather_side([bf(w_out_ab), bf(mlp_w1[0:1]), bf(mlp_w2[0:1])]))
    wout_ab = woutg_ab.reshape(d, d)
    y_ab = jnp.concatenate([y_a, y_b], axis=1)
    z0, x_mid0 = _matmul(y_ab, wout_ab, epi="resgate", extras=(xs, gt1), name="l0_out_proj")
    x1, mlp0, (wqkvg, woutg_c) = _mlp_fwd(x_mid0, norm_mlp[0:1], sc2, sh2, gt2, w1g0, w2g0, "l0_mlp",
                                          side_w1=_gather_side([bf(w_qkv), bf(w_out_c)]))
    wout_c = woutg_c.reshape(d, d)

    sh1b, sc1b, gt1b, sh2b, sc2b, gt2b = [mod[1, k] for k in range(6)]
    h1 = _norm_mod(x1, norm_mix[1:2], sc1b, sh1b, name="l1_mix_norm")
    qkv = _matmul(h1, wqkvg, b_kind="colblk", name="l1_qkv_proj")
    qn_a, kn_a, v_a = _qk_norm_fwd(qkv, qn_t, kn_t, name="l1_qk_norm")
    o_sb, (w1g1, w2g1) = _sb_fwd(qn_a, kn_a, v_a, name="l1_sb",
                                 side=_gather_side([bf(mlp_w1[1:2]), bf(mlp_w2[1:2])]))
    z1, x_mid1 = _matmul(o_sb, wout_c, epi="resgate", extras=(x1, gt1b), name="l1_out_proj")
    x2, mlp1, _ = _mlp_fwd(x_mid1, norm_mlp[1:2], sc2b, sh2b, gt2b, w1g1, w2g1, "l1_mlp")

    dx, loss_part, dz_mlp, dgt2b = _loss_grad(x2, tgt, mlp1[2], gt2b, name="loss")
    loss = lax.psum(loss_part[0, 0], MESH_AXES)

    my_q = 2 * my_x + my_y
    far_q = [my_q ^ 2, my_q ^ 1, my_q ^ 3]
    blk_ids = jnp.stack([2 * q + my_c for q in far_q] + far_q).astype(jnp.int32)
    my_ids = jnp.stack([me, my_q]).astype(jnp.int32)

    def blocks(g):
        return g.reshape(N_DEV, g.shape[0] // N_DEV, d)

    def by_rows(fn, tag, *lists):
        out = [None] * len(lists[0])
        heights = {}
        for t, g in enumerate(lists[0]):
            heights.setdefault(g.shape[1], []).append(t)
        for r, ts in heights.items():
            for t, v in zip(ts, fn(*[[lst[t] for t in ts] for lst in lists], name=f"{tag}_{r}")):
                out[t] = v
        return out

    def pair_sums(gs, sibs, tag):
        return by_rows(lambda a, b, name: _rs_pair_sum(a, b, blk_ids, name=name), f"rs_pair_sum_{tag}", gs, sibs)

    def final_sums(gs, sibs, fars, tag):
        return by_rows(lambda a, b, c_, name: _rs_final_sum(a, b, c_, my_ids, name=name),
                       f"rs_final_sum_{tag}", gs, sibs, fars)

    dx, dw1t_1, dw2_1, nacc, dyp, (_, sib1) = _mlp_bwd(
        dz_mlp, dx, x_mid1, mlp1, norm_mlp[1:2], sc2b, w1g1, w2g1, "l1_mlp", (z1, gt1b),
        make_side_dh=lambda a, b: _sibling_exchange_side([blocks(a), blocks(b)]))
    dsh2b, dsc2b, dnmlp1, dgt1b = nacc[0:1], nacc[1:2], nacc[2:3], nacc[3:4]
    g1 = [blocks(dw1t_1), blocks(dw2_1)]
    pair1 = pair_sums(g1, sib1, "g1")
    dwout_c = _matmul(o_sb, dyp, ta=True, name="l1_dwout")
    do_sb = _matmul(dyp, wout_c, tb=True, name="l1_do")
    (dqn_a, dkn_a, dv_a), far1 = _sb_bwd(qn_a, kn_a, v_a, o_sb, do_sb, name="l1_sb_bwd",
                                         side=_chip_exchange_side(pair1))
    gsh1 = final_sums(g1, sib1, far1, "g1")
    dqkv, qkacc = _qk_norm_bwd(qkv, dqn_a, dkn_a, dv_a, qn_t, kn_t, name="l1_qk_norm_bwd")
    dwqkv_t = _matmul(dqkv, h1, ta=True, name="l1_dwqkv")
    g2 = [blocks(dwqkv_t), blocks(dwout_c)]
    dh1, sib2 = _matmul(dqkv, wqkvg, tb=True, b_kind="colblk", name="l1_dh", side=_sibling_exchange_side(g2))
    pair2 = pair_sums(g2, sib2, "g2")
    dx, nacc, dz_mlp = _norm_mod_bwd(x1, dh1, dx, norm_mix[1:2], sc1b, name="l1_mix_norm_bwd",
                                     gated=(mlp0[2], gt2))
    dmod1 = [nacc[0:1], nacc[1:2], dgt1b, dsh2b, dsc2b, dgt2b]
    dnmix1, dgt2 = nacc[2:3], nacc[3:4]

    dx, dw1t_0, dw2_0, nacc, dyp, (far2, sib3) = _mlp_bwd(
        dz_mlp, dx, x_mid0, mlp0, norm_mlp[0:1], sc2, w1g0, w2g0, "l0_mlp", (z0, gt1),
        side_dact=_chip_exchange_side(pair2),
        make_side_dh=lambda a, b: _sibling_exchange_side([blocks(a), blocks(b)]))
    dsh2, dsc2, dnmlp0, dgt1 = nacc[0:1], nacc[1:2], nacc[2:3], nacc[3:4]
    gsh2 = final_sums(g2, sib2, far2, "g2")
    g3 = [blocks(dw1t_0), blocks(dw2_0)]
    pair3 = pair_sums(g3, sib3, "g3")
    dwout_ab = _matmul(y_ab, dyp, ta=True, name="l0_dwout")
    dy_ab = _matmul(dyp, wout_ab, tb=True, name="l0_dy")
    du_a, dconv = _conv_bwd(u, dy_ab, conv_full, name="l0_conv_bwd")
    (du_b, hgacc), far3 = _hgrn_bwd(u, o_hg, sall, dy_ab, lb_logits, hg_norm, name="l0_hgrn_bwd",
                                    side=_chip_exchange_side(pair3))
    gsh3 = final_sums(g3, sib3, far3, "g3")
    du = jnp.concatenate([du_a, du_b], axis=1)
    dwin_t = _matmul(du, h0, ta=True, name="l0_dwin")
    g4 = [blocks(dwin_t), blocks(dwout_ab)]
    dh0, sib4 = _matmul(du, win, tb=True, name="l0_dh", side=_sibling_exchange_side(g4))
    pair4 = pair_sums(g4, sib4, "g4")
    grad_x, nacc, far4 = _norm_mod_bwd(xs, dh0, dx, norm_mix[0:1], sc1, name="l0_mix_norm_bwd",
                                       side=_chip_exchange_side(pair4))
    gsh4 = final_sums(g4, sib4, far4, "g4")
    dmod0 = [nacc[0:1], nacc[1:2], dgt1, dsh2, dsc2, dgt2]
    dnmix0 = nacc[2:3]

    g_big = [gsh4[0].T[None], gsh4[1][None], gsh2[0].T[None], gsh2[1][None],
             jnp.stack([gsh3[0].T, gsh1[0].T]), jnp.stack([gsh3[1], gsh1[1]])]

    packed_small = jnp.concatenate(
        [jnp.concatenate(dmod0, axis=1).reshape(-1, LANES), jnp.concatenate(dmod1, axis=1).reshape(-1, LANES),
         dnmix0.reshape(-1, LANES), dnmix1.reshape(-1, LANES), dnmlp0.reshape(-1, LANES), dnmlp1.reshape(-1, LANES),
         hgacc[0:1].reshape(-1, LANES), hgacc[1:2].reshape(-1, LANES),
         qkacc[0:1].reshape(-1, LANES), qkacc[1:2].reshape(-1, LANES),
         dconv[0:3].reshape(-1, LANES), jnp.zeros((PK_ROWS - PK_CONV - 12, LANES), F32)], axis=0)
    gath = _all_gather(packed_small, name="gather_small_grads", in_vmem=True).reshape(N_DEV, PK_ROWS, LANES)
    dmod_all = gath[:, PK_MOD:PK_NMIX].reshape(N_DEV, 2, 6 * d)
    dmod_my = lax.dynamic_slice(dmod_all, (0, 0, me * ncol), (N_DEV, 2, ncol))
    g_ada_w, gsum, g_lb, g_qk = _small_grads(gath, c_act[:, :, None], dmod_my, lb_logits.reshape(3, 4, LANES),
                                             name="small_grads")
    g_ada_b = gsum[PK_MOD:PK_NMIX].reshape(2, 6 * d)
    g_norm_mix = gsum[PK_NMIX:PK_NMLP].reshape(2, d)
    g_norm_mlp = gsum[PK_NMLP:PK_HGN].reshape(2, d)
    g_hg_norm = gsum[PK_HGN:PK_LB].reshape(1, HG_WIDTH)
    g_lb_logits = g_lb.reshape(3, HG_WIDTH)
    g_q_norm = g_qk[0:1, :SB_HEAD_DIM]
    g_k_norm = g_qk[1:2, :SB_HEAD_DIM]
    g_conv_w = lax.dynamic_slice(gsum[PK_CONV:PK_CONV + 12].reshape(3, CONV_DIM), (0, me * ncv), (3, ncv))[None]

    def flat2(a):
        return a.reshape(-1, a.shape[-1])

    grads = dict(ada_w=g_ada_w, ada_b=g_ada_b, norm_mix=g_norm_mix, norm_mlp=g_norm_mlp, w_in_ab=g_big[0],
                 conv_w=g_conv_w, hg_norm=g_hg_norm, lb_logits=g_lb_logits, w_out_ab=g_big[1], w_qkv=g_big[2],
                 q_norm=g_q_norm, k_norm=g_k_norm, w_out_c=g_big[3], mlp_w1=g_big[4], mlp_w2=g_big[5])
    weights = dict(ada_w=(ada_w, m_ada_w, v_ada_w), ada_b=(ada_b, m_ada_b, v_ada_b),
                   norm_mix=(norm_mix, m_norm_mix, v_norm_mix), norm_mlp=(norm_mlp, m_norm_mlp, v_norm_mlp),
                   w_in_ab=(w_in_ab, m_w_in_ab, v_w_in_ab), conv_w=(conv_w, m_conv_w, v_conv_w),
                   hg_norm=(hg_norm, m_hg_norm, v_hg_norm), lb_logits=(lb_logits, m_lb_logits, v_lb_logits),
                   w_out_ab=(w_out_ab, m_w_out_ab, v_w_out_ab), w_qkv=(w_qkv, m_w_qkv, v_w_qkv),
                   q_norm=(q_norm, m_q_norm, v_q_norm), k_norm=(k_norm, m_k_norm, v_k_norm),
                   w_out_c=(w_out_c, m_w_out_c, v_w_out_c), mlp_w1=(mlp_w1, m_mlp_w1, v_mlp_w1),
                   mlp_w2=(mlp_w2, m_mlp_w2, v_mlp_w2))
    names = list(weights)
    small_names = ["ada_b", "norm_mix", "norm_mlp", "conv_w", "hg_norm", "lb_logits", "q_norm", "k_norm"]
    upd = {}
    small_items = []
    for n in small_names:
        wv, mv, vv = weights[n]
        small_items.append((flat2(wv), flat2(grads[n]), flat2(mv), flat2(vv)))
    for n, res in zip(small_names, _adamw_small(small_items, name="adamw_small")):
        upd[n] = tuple(r.reshape(weights[n][0].shape) for r in res)
    for n in names:
        if n in small_names:
            continue
        wv, mv, vv = weights[n]
        res = _adamw(flat2(wv), flat2(grads[n]), flat2(mv), flat2(vv), name=f"adamw_{n}")
        upd[n] = tuple(r.reshape(wv.shape) for r in res)

    return (loss, grad_x[None], *[grads[n].reshape(weights[n][0].shape) for n in names],
            *[upd[n][0] for n in names], *[upd[n][1] for n in names], *[upd[n][2] for n in names])
```

```python
import functools

import jax
import jax.numpy as jnp
from jax import lax
from jax.experimental import pallas as pl
from jax.experimental.pallas import tpu as pltpu

F32 = jnp.float32
BF16 = jnp.bfloat16
EPS = 1e-6
N_DEV = 8
MESH_AXES = ("x", "y", "c")

D_MODEL = 1024
CONV_DIM = 512
HG_HEADS = 4
HG_DK = 128
HG_WIDTH = 512
CHUNK = 64
HG_TILE = 128
HG_SUB = 16
HG_EXP_CLAMP = 60.0
SB_HEAD_DIM = 64
SB_SCALE = SB_HEAD_DIM ** -0.5
LOG2E = 1.4426950408889634
LN2 = 0.6931471805599453
SB_TQ = 512
SB_TK = 256
SB_DEAD = 150.0
D_FF = 4096
AB_IN = 3584

ADAM_LR = 0.001
ADAM_B1 = 0.9
ADAM_B2 = 0.999
ADAM_EPS = 1e-08
ADAM_WD = 0.01
ADAM_STEP = 10

VMEM_LIMIT = 48 * 1024 * 1024
LANES = 128


def _cparams(n_grid):
    return pltpu.CompilerParams(dimension_semantics=("arbitrary",) * n_grid, vmem_limit_bytes=VMEM_LIMIT)


def _nt(a, b):
    return lax.dot_general(a, b, (((1,), (1,)), ((), ())), preferred_element_type=F32)


def _tn(a, b):
    return lax.dot_general(a, b, (((0,), (0,)), ((), ())), preferred_element_type=F32)


def _nn(a, b):
    return jnp.dot(a, b, preferred_element_type=F32)


def _split3(x):
    hi = x.astype(BF16)
    r1 = x - hi.astype(F32)
    mid = r1.astype(BF16)
    lo = (r1 - mid.astype(F32)).astype(BF16)
    return hi, mid, lo


def _exact_left(m01, x):
    hi, mid, lo = _split3(x)
    return _nn(m01, hi) + _nn(m01, mid) + _nn(m01, lo)


def _exact_right(x, m01):
    hi, mid, lo = _split3(x)
    return _nn(hi, m01) + _nn(mid, m01) + _nn(lo, m01)


def _exact_right2(x, m01):
    hi = x.astype(BF16)
    lo = (x - hi.astype(F32)).astype(BF16)
    return _nn(hi, m01) + _nn(lo, m01)


def _sp(x):
    hi = x.astype(BF16)
    return hi, (x - hi.astype(F32)).astype(BF16)


def _dot3(fn, a, b):
    return fn(a[0], b[0]) + fn(a[0], b[1]) + fn(a[1], b[0])


def _tile(pref, n):
    t = min(pref, n)
    assert n % t == 0, (pref, n)
    return t


def _tile_rows(pref, n):
    for t in range(min(pref, n) - min(pref, n) % 16, 0, -16):
        if n % t == 0:
            return t
    raise ValueError((pref, n))


def _tile_lanes(pref, n):
    if n <= pref:
        return n
    for t in range(pref - pref % LANES, 0, -LANES):
        if n % t == 0:
            return t
    raise ValueError((pref, n))


def _all_gather(x, *, name, in_vmem):
    m_per, n = x.shape

    def body(x_ref, out_ref, send_sems, recv_sems, local_sem):
        mx, my, mc = lax.axis_index("x"), lax.axis_index("y"), lax.axis_index("c")
        me, sibling = (mx, my, mc), (mx, my, 1 - mc)
        chips = [(1 - mx, my), (mx, 1 - my), (1 - mx, 1 - my)]

        def rows(px, py, pc):
            return out_ref.at[pl.ds((4 * px + 2 * py + pc) * m_per, m_per), :]

        def copy(k, block, to, src=None):
            return pltpu.make_async_remote_copy(
                src_ref=rows(*block) if src is None else src, dst_ref=rows(*block),
                send_sem=send_sems.at[k], recv_sem=recv_sems.at[k],
                device_id=to, device_id_type=pl.DeviceIdType.MESH)

        mine = pltpu.make_async_copy(x_ref, rows(*me), local_sem)
        mine.start()
        first = [copy(0, me, sibling, src=x_ref)]
        first += [copy(1 + j, me, (*chip, mc), src=x_ref) for j, chip in enumerate(chips)]
        for cp in first:
            cp.start()
        passed = [copy(4 + j, (*chip, mc), sibling) for j, chip in enumerate(chips)]
        for j, chip in enumerate(chips):
            copy(1 + j, (*chip, mc), me).wait_recv()
            passed[j].start()
        copy(0, sibling, me).wait_recv()
        for j, chip in enumerate(chips):
            copy(4 + j, (*chip, 1 - mc), me).wait_recv()
        for cp in first + passed:
            cp.wait_send()
        mine.wait()

    space = pltpu.VMEM if in_vmem else pl.ANY
    return pl.pallas_call(
        body, name=name,
        out_shape=jax.ShapeDtypeStruct((N_DEV * m_per, n), x.dtype),
        in_specs=[pl.BlockSpec(memory_space=space)],
        out_specs=pl.BlockSpec(memory_space=space),
        scratch_shapes=[pltpu.SemaphoreType.DMA((7,)), pltpu.SemaphoreType.DMA((7,)), pltpu.SemaphoreType.DMA],
    )(x)


class _Side:
    def __init__(self, inputs, out_shape, scratch, start, finish):
        self.inputs, self.out_shape, self.scratch = list(inputs), tuple(out_shape), list(scratch)
        self.start, self.finish = start, finish


def _run_side(side, *, name):
    n_in, n_out = len(side.inputs), len(side.out_shape)

    def body(*refs):
        parts = (refs[:n_in], refs[n_in:n_in + n_out], refs[n_in + n_out:])
        side.start(*parts)
        side.finish(*parts)

    hbm = pl.BlockSpec(memory_space=pl.ANY)
    return pl.pallas_call(body, name=name, out_shape=side.out_shape, in_specs=[hbm] * n_in,
                          out_specs=tuple([hbm] * n_out), scratch_shapes=side.scratch)(*side.inputs)


def _carry(body, side, *, name, grid, in_specs, out_specs, out_shape, scratch_shapes, args):
    in_specs, out_specs, out_shape = list(in_specs), tuple(out_specs), tuple(out_shape)
    scratch_shapes = list(scratch_shapes)
    if side is None:
        res = pl.pallas_call(body, name=name, grid=grid, in_specs=in_specs, out_specs=out_specs,
                             out_shape=out_shape, scratch_shapes=scratch_shapes,
                             compiler_params=_cparams(len(grid)))(*args)
        return tuple(res), ()
    n_in, n_out, n_scr = len(in_specs), len(out_specs), len(scratch_shapes)
    s_in, s_out = len(side.inputs), len(side.out_shape)

    def wrapped(*refs):
        ins, rest = refs[:n_in], refs[n_in:]
        s_ins, rest = rest[:s_in], rest[s_in:]
        outs, rest = rest[:n_out], rest[n_out:]
        s_outs, rest = rest[:s_out], rest[s_out:]
        scr, s_scr = rest[:n_scr], rest[n_scr:]
        ids = [pl.program_id(ax) for ax in range(len(grid))]
        first = functools.reduce(jnp.logical_and, [i == 0 for i in ids])
        last = functools.reduce(jnp.logical_and, [i == g - 1 for i, g in zip(ids, grid)])

        @pl.when(first)
        def _():
            side.start(s_ins, s_outs, s_scr)

        body(*ins, *outs, *scr)

        @pl.when(last)
        def _():
            side.finish(s_ins, s_outs, s_scr)

    hbm = pl.BlockSpec(memory_space=pl.ANY)
    res = pl.pallas_call(
        wrapped, name=name, grid=grid, in_specs=in_specs + [hbm] * s_in,
        out_specs=out_specs + tuple([hbm] * s_out), out_shape=out_shape + side.out_shape,
        scratch_shapes=scratch_shapes + side.scratch, compiler_params=_cparams(len(grid)),
    )(*args, *side.inputs)
    return tuple(res[:n_out]), tuple(res[n_out:])


def _gather_side(xs):
    n = len(xs)

    def tools(x_refs, out_refs, sems):
        send_sems, recv_sems, local_sems = sems
        mx, my, mc = lax.axis_index("x"), lax.axis_index("y"), lax.axis_index("c")
        me, sibling = (mx, my, mc), (mx, my, 1 - mc)
        chips = [(1 - mx, my), (mx, 1 - my), (1 - mx, 1 - my)]

        def slot(t, px, py, pc):
            return out_refs[t].at[4 * px + 2 * py + pc]

        def copy(t, k, block, to, src=None):
            return pltpu.make_async_remote_copy(
                src_ref=slot(t, *block) if src is None else src, dst_ref=slot(t, *block),
                send_sem=send_sems.at[7 * t + k], recv_sem=recv_sems.at[7 * t + k],
                device_id=to, device_id_type=pl.DeviceIdType.MESH)

        mine = [pltpu.make_async_copy(x_refs[t], slot(t, *me), local_sems.at[t]) for t in range(n)]
        first = []
        for t in range(n):
            first.append(copy(t, 0, me, sibling, src=x_refs[t]))
            first += [copy(t, 1 + j, me, (*chip, mc), src=x_refs[t]) for j, chip in enumerate(chips)]
        return me, sibling, chips, mc, copy, mine, first

    def start(x_refs, out_refs, sems):
        *_, mine, first = tools(x_refs, out_refs, sems)
        for cp in mine + first:
            cp.start()

    def finish(x_refs, out_refs, sems):
        me, sibling, chips, mc, copy, mine, first = tools(x_refs, out_refs, sems)
        passed = []
        for j, chip in enumerate(chips):
            for t in range(n):
                copy(t, 1 + j, (*chip, mc), me).wait_recv()
                passed.append(copy(t, 4 + j, (*chip, mc), sibling))
                passed[-1].start()
        for t in range(n):
            copy(t, 0, sibling, me).wait_recv()
            for j, chip in enumerate(chips):
                copy(t, 4 + j, (*chip, 1 - mc), me).wait_recv()
        for cp in first + passed:
            cp.wait_send()
        for cp in mine:
            cp.wait()

    return _Side(xs, [jax.ShapeDtypeStruct((N_DEV,) + x.shape, x.dtype) for x in xs],
                 [pltpu.SemaphoreType.DMA((7 * n,)), pltpu.SemaphoreType.DMA((7 * n,)),
                  pltpu.SemaphoreType.DMA((n,))], start, finish)


def _sibling_exchange_side(gs):
    n = len(gs)

    def copies(g_refs, out_refs, sems):
        send_sems, recv_sems = sems
        mx, my, mc = lax.axis_index("x"), lax.axis_index("y"), lax.axis_index("c")
        return [pltpu.make_async_remote_copy(
            src_ref=g_refs[t].at[2 * q + (1 - mc)], dst_ref=out_refs[t].at[q],
            send_sem=send_sems.at[4 * t + q], recv_sem=recv_sems.at[4 * t + q],
            device_id=(mx, my, 1 - mc), device_id_type=pl.DeviceIdType.MESH)
            for t in range(n) for q in range(4)]

    def start(g_refs, out_refs, sems):
        for cp in copies(g_refs, out_refs, sems):
            cp.start()

    def finish(g_refs, out_refs, sems):
        cps = copies(g_refs, out_refs, sems)
        for cp in cps:
            cp.wait_recv()
        for cp in cps:
            cp.wait_send()

    return _Side(gs, [jax.ShapeDtypeStruct((4,) + g.shape[1:], g.dtype) for g in gs],
                 [pltpu.SemaphoreType.DMA((4 * n,)), pltpu.SemaphoreType.DMA((4 * n,))], start, finish)


def _chip_exchange_side(ts):
    n = len(ts)

    def copies(t_refs, out_refs, sems):
        send_sems, recv_sems = sems
        mx, my, mc = lax.axis_index("x"), lax.axis_index("y"), lax.axis_index("c")
        chips = [(1 - mx, my), (mx, 1 - my), (1 - mx, 1 - my)]
        return [pltpu.make_async_remote_copy(
            src_ref=t_refs[t].at[k], dst_ref=out_refs[t].at[k],
            send_sem=send_sems.at[3 * t + k], recv_sem=recv_sems.at[3 * t + k],
            device_id=(px, py, mc), device_id_type=pl.DeviceIdType.MESH)
            for t in range(n) for k, (px, py) in enumerate(chips)]

    def start(t_refs, out_refs, sems):
        for cp in copies(t_refs, out_refs, sems):
            cp.start()

    def finish(t_refs, out_refs, sems):
        cps = copies(t_refs, out_refs, sems)
        for cp in cps:
            cp.wait_recv()
        for cp in cps:
            cp.wait_send()

    return _Side(ts, [jax.ShapeDtypeStruct(t.shape, t.dtype) for t in ts],
                 [pltpu.SemaphoreType.DMA((3 * n,)), pltpu.SemaphoreType.DMA((3 * n,))], start, finish)


def _rs_pair_sum(gs, p1s, blk_ids, *, name, tr=256):
    n = len(gs)
    _, r, ncol = gs[0].shape
    tr = _tile_rows(tr, r)

    def body(id_ref, *refs):
        for t in range(n):
            refs[2 * n + t][...] = (refs[t][...] + refs[n + t][...]).astype(BF16)

    blk = lambda off: pl.BlockSpec((None, tr, ncol), lambda k, i, ids: (ids[off + k], i, 0))
    out = pl.BlockSpec((None, tr, ncol), lambda k, i, ids: (k, i, 0))
    return pl.pallas_call(
        body, name=name,
        out_shape=tuple(jax.ShapeDtypeStruct((3, r, ncol), BF16) for _ in gs),
        grid_spec=pltpu.PrefetchScalarGridSpec(
            num_scalar_prefetch=1, grid=(3, r // tr),
            in_specs=[blk(0)] * n + [blk(3)] * n, out_specs=tuple([out] * n)),
        compiler_params=_cparams(2),
    )(blk_ids, *gs, *p1s)


def _rs_final_sum(gs, p1s, p3s, my_ids, *, name, tr=256):
    n = len(gs)
    _, r, ncol = gs[0].shape
    tr = _tile_rows(tr, r)

    def body(id_ref, *refs):
        for t in range(n):
            g_ref, s_ref = refs[t], refs[n + t]
            a_ref, b_ref, c_ref = refs[2 * n + 3 * t:2 * n + 3 * t + 3]
            own = g_ref[...] + s_ref[...]
            refs[5 * n + t][...] = (((own + a_ref[...].astype(F32)) + b_ref[...].astype(F32))
                                    + c_ref[...].astype(F32))

    sel = lambda which: pl.BlockSpec((None, tr, ncol), lambda i, ids: (ids[which], i, 0))
    fix = lambda k: pl.BlockSpec((None, tr, ncol), lambda i, ids: (k, i, 0))
    p3_specs, p3_args = [], []
    for p3 in p3s:
        p3_specs += [fix(0), fix(1), fix(2)]
        p3_args += [p3, p3, p3]
    return pl.pallas_call(
        body, name=name,
        out_shape=tuple(jax.ShapeDtypeStruct((r, ncol), F32) for _ in gs),
        grid_spec=pltpu.PrefetchScalarGridSpec(
            num_scalar_prefetch=1, grid=(r // tr,),
            in_specs=[sel(0)] * n + [sel(1)] * n + p3_specs,
            out_specs=tuple([pl.BlockSpec((tr, ncol), lambda i, ids: (i, 0))] * n)),
        compiler_params=_cparams(1),
    )(my_ids, *gs, *p1s, *p3_args)


def _matmul(a, b, *, name, ta=False, tb=False, epi="plain", extras=(), out_dtype=F32, tm=None, tn=1024, tk=1024,
            b_kind=None, layer=0, side=None):
    if ta:
        kdim, m = a.shape
    else:
        m, kdim = a.shape
    pair = 1
    if b_kind is None:
        if tb:
            n, kb = b.shape
        else:
            kb, n = b.shape
        tn, tk = _tile_lanes(tn, n), _tile_lanes(tk, kb)
        b_spec = (pl.BlockSpec((tn, tk), lambda i, j, k: (j, k)) if tb
                  else pl.BlockSpec((tk, tn), lambda i, j, k: (k, j)))
    elif b_kind == "colblk":
        assert not ta
        _, _, kw, nsh = b.shape
        if tb:
            kb, n, pair = N_DEV * nsh, kw, 2
            tn, tk = _tile_lanes(tn, n), pair * nsh
            b_spec = pl.BlockSpec((pair, None, tn, nsh), lambda i, j, k: (k, layer, j, 0))
        else:
            kb, n, pair = kw, N_DEV * nsh, 2
            tn, tk = pair * nsh, _tile_lanes(tk, kb)
            b_spec = pl.BlockSpec((pair, None, tk, nsh), lambda i, j, k: (j, layer, k, 0))
    elif b_kind == "rowblk":
        assert not ta
        _, _, r, ncol = b.shape
        pair = 2
        if tb:
            kb, n = ncol, N_DEV * r
            tn, tk = pair * r, _tile_lanes(tk, kb)
            b_spec = pl.BlockSpec((pair, None, r, tk), lambda i, j, k: (j, layer, 0, k))
        else:
            kb, n = N_DEV * r, ncol
            tn, tk = _tile_lanes(tn, n), pair * r
            b_spec = pl.BlockSpec((pair, None, r, tn), lambda i, j, k: (k, layer, 0, j))
    else:
        raise ValueError(b_kind)
    assert kdim == kb, (a.shape, b.shape)
    if tm is None:
        tm = 1024 if epi == "resgate" else 2048
    tm = _tile_lanes(tm, m)
    nk = kdim // tk
    a_spec = pl.BlockSpec((tk, tm), lambda i, j, k: (k, i)) if ta else pl.BlockSpec((tm, tk), lambda i, j, k: (i, k))
    dims = (((0 if ta else 1,), (1 if tb else 0,)), ((), ()))
    mn_spec = pl.BlockSpec((tm, tn), lambda i, j, k: (i, j))
    row_spec = pl.BlockSpec((1, tn), lambda i, j, k: (0, j))
    if epi == "resgate":
        extra_specs = [mn_spec, row_spec]
        out_shape = (jax.ShapeDtypeStruct((m, n), BF16), jax.ShapeDtypeStruct((m, n), F32))
        out_specs = (mn_spec, mn_spec)
    elif epi == "dact":
        extra_specs = [mn_spec]
        out_shape = jax.ShapeDtypeStruct((m, n), out_dtype)
        out_specs = mn_spec
    else:
        extra_specs = []
        out_shape = jax.ShapeDtypeStruct((m, n), out_dtype)
        out_specs = mn_spec
    n_extra = len(extra_specs)

    def body(a_ref, b_ref, *rest):
        ex = rest[:n_extra]
        outs = rest[n_extra:n_extra + n_out]
        k = pl.program_id(2)

        def prod():
            av = a_ref[...].astype(BF16)
            if b_kind == "rowblk":
                bv = b_ref[...].astype(BF16)
                return lax.dot_general(av, bv.reshape(bv.shape[0] * bv.shape[1], bv.shape[2]), dims,
                                       preferred_element_type=F32)
            if b_kind == "colblk" and tb:
                nsh = b_ref.shape[-1]
                return sum(lax.dot_general(av[:, p * nsh:(p + 1) * nsh], b_ref[p].astype(BF16), dims,
                                           preferred_element_type=F32) for p in range(pair))
            if b_kind == "colblk":
                return jnp.concatenate([lax.dot_general(av, b_ref[p].astype(BF16), dims, preferred_element_type=F32)
                                        for p in range(pair)], axis=1)
            return lax.dot_general(av, b_ref[...].astype(BF16), dims, preferred_element_type=F32)

        def finish(r):
            if epi == "plain":
                outs[0][...] = r.astype(outs[0].dtype)
            elif epi == "resgate":
                outs[0][...] = r.astype(BF16)
                outs[1][...] = ex[0][...] + ex[1][...] * r
            elif epi == "relu2":
                p = jnp.maximum(r, 0.0)
                outs[0][...] = (p * p).astype(outs[0].dtype)
            elif epi == "dact":
                outs[0][...] = (r * (2.0 * jnp.sqrt(ex[0][...].astype(F32)))).astype(outs[0].dtype)

        if nk == 1:
            finish(prod())
        else:
            acc = rest[-1]

            @pl.when(k == 0)
            def _():
                acc[...] = prod()

            if nk > 2:
                @pl.when(jnp.logical_and(k > 0, k < nk - 1))
                def _():
                    acc[...] += prod()

            @pl.when(k == nk - 1)
            def _():
                finish(acc[...] + prod())

    n_out = 2 if epi == "resgate" else 1
    if n_out == 1:
        out_shape, out_specs = (out_shape,), (out_specs,)
    res, side_res = _carry(
        body, side, name=name, grid=(m // tm, n // tn, nk), in_specs=[a_spec, b_spec] + extra_specs,
        out_specs=out_specs, out_shape=out_shape,
        scratch_shapes=[pltpu.VMEM((tm, tn), F32)] if nk > 1 else [], args=(a, b, *extras))
    res = res if n_out == 2 else res[0]
    return res if side is None else (res, side_res)


def _norm_mod(x, g, scale, shift, *, name, tm=512):
    s, d = x.shape
    tm = _tile(tm, s)

    def body(x_ref, g_ref, sc_ref, sh_ref, h_ref):
        xv = x_ref[...]
        r = lax.rsqrt(jnp.mean(xv * xv, axis=-1, keepdims=True) + EPS)
        h_ref[...] = (((xv * r) * g_ref[...]) * (1.0 + sc_ref[...]) + sh_ref[...]).astype(BF16)

    row = pl.BlockSpec((1, d), lambda i: (0, 0))
    return pl.pallas_call(
        body, name=name, out_shape=jax.ShapeDtypeStruct((s, d), BF16), grid=(s // tm,),
        in_specs=[pl.BlockSpec((tm, d), lambda i: (i, 0)), row, row, row],
        out_specs=pl.BlockSpec((tm, d), lambda i: (i, 0)),
        compiler_params=_cparams(1),
    )(x, g, scale, shift)


def _norm_mod_bwd(x, dh, dres, g, scale, *, name, tm=512, gated=None, side=None):
    s, d = x.shape
    tm = _tile(tm, s)
    n_in = 5 if gated is None else 7

    def body(*refs):
        x_ref, dh_ref, dr_ref, g_ref, sc_ref = refs[:5]
        dx_ref, acc_ref = refs[n_in:n_in + 2]
        i = pl.program_id(0)

        @pl.when(i == 0)
        def _():
            acc_ref[...] = jnp.zeros_like(acc_ref)

        xv = x_ref[...]
        dhv = dh_ref[...]
        gv = g_ref[...]
        one_sc = 1.0 + sc_ref[...]
        r = lax.rsqrt(jnp.mean(xv * xv, axis=-1, keepdims=True) + EPS)
        xn = xv * r
        dxn = dhv * (gv * one_sc)
        dxv = dr_ref[...] + r * (dxn - xn * jnp.mean(dxn * xn, axis=-1, keepdims=True))
        dx_ref[...] = dxv
        dhxn = dhv * xn
        acc_ref[0:1, :] += jnp.sum(dhv, axis=0, keepdims=True)
        acc_ref[1:2, :] += jnp.sum(dhxn * gv, axis=0, keepdims=True)
        acc_ref[2:3, :] += jnp.sum(dhxn * one_sc, axis=0, keepdims=True)
        if gated is not None:
            z_ref, gate_ref, dz_ref = refs[5], refs[6], refs[n_in + 2]
            dz_ref[...] = (dxv * gate_ref[...]).astype(BF16)
            acc_ref[3:4, :] += jnp.sum(dxv * z_ref[...], axis=0, keepdims=True)

    row = pl.BlockSpec((1, d), lambda i: (0, 0))
    blk = pl.BlockSpec((tm, d), lambda i: (i, 0))
    in_specs, args = [blk, blk, blk, row, row], [x, dh, dres, g, scale]
    out_shape = [jax.ShapeDtypeStruct((s, d), F32), jax.ShapeDtypeStruct((8, d), F32)]
    out_specs = [blk, pl.BlockSpec((8, d), lambda i: (0, 0))]
    if gated is not None:
        in_specs += [blk, row]
        args += list(gated)
        out_shape.append(jax.ShapeDtypeStruct((s, d), BF16))
        out_specs.append(blk)
    res, side_res = _carry(body, side, name=name, grid=(s // tm,), in_specs=in_specs, out_specs=out_specs,
                           out_shape=out_shape, scratch_shapes=[], args=args)
    return res if side is None else res + (side_res,)


def _loss_grad(xf, target, z, gate, *, name, tm=512):
    s, d = xf.shape
    tm = _tile(tm, s)
    nt = s // tm

    def body(x_ref, t_ref, z_ref, gate_ref, dx_ref, loss_ref, dz_ref, dgate_ref, acc_ref):
        i = pl.program_id(0)

        @pl.when(i == 0)
        def _():
            acc_ref[...] = jnp.zeros_like(acc_ref)
            dgate_ref[...] = jnp.zeros_like(dgate_ref)

        e = x_ref[...] - t_ref[...]
        dxv = e * (1.0 / d)
        dx_ref[...] = dxv
        dz_ref[...] = (dxv * gate_ref[...]).astype(BF16)
        dgate_ref[...] += jnp.sum(dxv * z_ref[...], axis=0, keepdims=True)
        acc_ref[...] += jnp.sum(e * e, axis=0, keepdims=True)

        @pl.when(i == nt - 1)
        def _():
            loss_ref[...] = (0.5 / d) * jnp.sum(acc_ref[...], axis=1, keepdims=True)

    blk = pl.BlockSpec((tm, d), lambda i: (i, 0))
    row = pl.BlockSpec((1, d), lambda i: (0, 0))
    return pl.pallas_call(
        body, name=name,
        out_shape=(jax.ShapeDtypeStruct((s, d), F32), jax.ShapeDtypeStruct((1, 1), F32),
                   jax.ShapeDtypeStruct((s, d), BF16), jax.ShapeDtypeStruct((1, d), F32)),
        grid=(nt,), in_specs=[blk, blk, blk, row],
        out_specs=(blk, pl.BlockSpec((1, 1), lambda i: (0, 0)), blk, row),
        scratch_shapes=[pltpu.VMEM((1, d), F32)],
        compiler_params=_cparams(1),
    )(xf, target, z, gate)


def _shift_down(p, prev, k):
    tm = p.shape[0]
    row = lax.broadcasted_iota(jnp.int32, p.shape, 0)
    out = pltpu.roll(p, k, 0)
    for j in range(k):
        out = jnp.where(row == j, prev[8 - k + j:8 - k + j + 1, :], out)
    return out


def _shift_up(p, nxt, k):
    tm = p.shape[0]
    row = lax.broadcasted_iota(jnp.int32, p.shape, 0)
    out = pltpu.roll(p, tm - k, 0)
    for j in range(k):
        out = jnp.where(row == tm - k + j, nxt[j:j + 1, :], out)
    return out


def _conv_fwd(u, w, *, name, tm=512):
    s = u.shape[0]
    tm = _tile(tm, s)
    c = CONV_DIM

    def body(ab_ref, ac_ref, ah_ref, w_ref, y_ref, carry_ref):
        i = pl.program_id(0)

        @pl.when(i == 0)
        def _():
            carry_ref[...] = jnp.zeros_like(carry_ref)

        p = ac_ref[...] * ah_ref[...]
        prev = carry_ref[...]
        wv = w_ref[...]
        conv = wv[2:3, :] * p + wv[1:2, :] * _shift_down(p, prev, 1) + wv[0:1, :] * _shift_down(p, prev, 2)
        y_ref[...] = (ab_ref[...] * conv).astype(BF16)
        carry_ref[...] = p[tm - 8:tm, :]

    return pl.pallas_call(
        body, name=name, out_shape=jax.ShapeDtypeStruct((s, c), BF16), grid=(s // tm,),
        in_specs=[pl.BlockSpec((tm, c), lambda i: (i, 0)), pl.BlockSpec((tm, c), lambda i: (i, 1)),
                  pl.BlockSpec((tm, c), lambda i: (i, 2)), pl.BlockSpec((3, c), lambda i: (0, 0))],
        out_specs=pl.BlockSpec((tm, c), lambda i: (i, 0)),
        scratch_shapes=[pltpu.VMEM((8, c), F32)],
        compiler_params=_cparams(1),
    )(u, u, u, w)


def _conv_bwd(u, dy, w, *, name, tm=512):
    s = u.shape[0]
    tm = _tile(tm, s)
    nt = s // tm
    c = CONV_DIM
    hb = tm // 8

    def body(ab_ref, ac_ref, ah_ref, hc_ref, hh_ref, dy_ref, w_ref, du_ref, dw_ref, carry_ref):
        i = pl.program_id(0)

        @pl.when(i == 0)
        def _():
            carry_ref[...] = jnp.zeros_like(carry_ref)
            dw_ref[...] = jnp.zeros_like(dw_ref)

        first_tile = (nt - 1 - i) == 0
        ab, ac, ah = ab_ref[...], ac_ref[...], ah_ref[...]
        p = ac * ah
        prev = jnp.where(first_tile, 0.0, hc_ref[...] * hh_ref[...])
        wv = w_ref[...]
        p1 = _shift_down(p, prev, 1)
        p2 = _shift_down(p, prev, 2)
        conv = wv[2:3, :] * p + wv[1:2, :] * p1 + wv[0:1, :] * p2
        dyv = dy_ref[...]
        dconv = dyv * ab
        nxt = carry_ref[...]
        dp = wv[2:3, :] * dconv + wv[1:2, :] * _shift_up(dconv, nxt, 1) + wv[0:1, :] * _shift_up(dconv, nxt, 2)
        du_ref[:, 0:c] = (dyv * conv).astype(BF16)
        du_ref[:, c:2 * c] = (dp * ah).astype(BF16)
        du_ref[:, 2 * c:3 * c] = (dp * ac).astype(BF16)
        dw_ref[0:1, :] += jnp.sum(dconv * p2, axis=0, keepdims=True)
        dw_ref[1:2, :] += jnp.sum(dconv * p1, axis=0, keepdims=True)
        dw_ref[2:3, :] += jnp.sum(dconv * p, axis=0, keepdims=True)
        carry_ref[...] = dconv[0:8, :]

    rev = lambda i: nt - 1 - i
    halo = lambda i: jnp.maximum(rev(i) * hb - 1, 0)
    return pl.pallas_call(
        body, name=name,
        out_shape=(jax.ShapeDtypeStruct((s, 3 * c), BF16), jax.ShapeDtypeStruct((8, c), F32)),
        grid=(nt,),
        in_specs=[pl.BlockSpec((tm, c), lambda i: (rev(i), 0)), pl.BlockSpec((tm, c), lambda i: (rev(i), 1)),
                  pl.BlockSpec((tm, c), lambda i: (rev(i), 2)),
                  pl.BlockSpec((8, c), lambda i: (halo(i), 1)), pl.BlockSpec((8, c), lambda i: (halo(i), 2)),
                  pl.BlockSpec((tm, c), lambda i: (rev(i), 0)), pl.BlockSpec((3, c), lambda i: (0, 0))],
        out_specs=(pl.BlockSpec((tm, 3 * c), lambda i: (rev(i), 0)), pl.BlockSpec((8, c), lambda i: (0, 0))),
        scratch_shapes=[pltpu.VMEM((8, c), F32)],
        compiler_params=_cparams(1),
    )(u, u, u, u, u, dy, w)


def _lower_bound(lbl):
    m = jnp.max(lbl, axis=0, keepdims=True)
    e = jnp.exp(lbl - m)
    return e[0:1, :] / jnp.sum(e, axis=0, keepdims=True)


def _hg_masks():
    t = HG_TILE
    row = lax.broadcasted_iota(jnp.int32, (t, t), 0)
    col = lax.broadcasted_iota(jnp.int32, (t, t), 1)
    same = (row >= CHUNK) == (col >= CHUNK)
    lower = same & (col <= row)
    upper = same & (row <= col)
    return row, col, lower, upper


def _hg_gates(hf, lb):
    sig = jax.nn.sigmoid(hf)
    f = lb + (1.0 - lb) * sig
    return sig, f, jnp.log(f), 1.0 - f


def _hg_refs(b_ref, hs):
    refs = []
    for i in range(HG_TILE // HG_SUB):
        if (i * HG_SUB) % CHUNK == 0:
            refs.append(jnp.zeros((1, HG_DK), F32))
        else:
            refs.append(b_ref[i * HG_SUB - 1:i * HG_SUB, hs])
    return refs


def _hgrn_fwd(u, lbl, gn, *, name, side=None):
    s = u.shape[0]
    t = HG_TILE
    nt = s // t
    nsub = t // HG_SUB
    w = HG_WIDTH

    def body(hq_ref, hf_ref, hi_ref, hg_ref, lbl_ref, gn_ref, y_ref, o_ref, sall_ref, st_ref, b_ref):
        i = pl.program_id(0)

        @pl.when(i == 0)
        def _():
            st_ref[...] = jnp.zeros_like(st_ref)

        lb = _lower_bound(lbl_ref[...])
        _, _, g, kin = _hg_gates(hf_ref[...], lb)
        _, _, lower, _ = _hg_masks()
        b_ref[...] = _exact_left(lower.astype(BF16), g)

        for h in range(HG_HEADS):
            hs = slice(h * HG_DK, (h + 1) * HG_DK)
            bh = b_ref[:, hs]
            qh = hq_ref[:, hs]
            kh = kin[:, hs]
            vh = hi_ref[:, hs]
            vsp = _sp(vh)
            refs = _hg_refs(b_ref, hs)
            rmat = jnp.concatenate([jnp.broadcast_to(r, (HG_SUB, HG_DK)) for r in refs], axis=0)
            qt = qh * jnp.exp(bh - rmat)
            prow = []
            for j in range(nsub):
                kj = kh * jnp.exp(jnp.minimum(refs[j] - bh, HG_EXP_CLAMP))
                prow.append(_dot3(_nt, _sp(qt[j * HG_SUB:(j + 1) * HG_SUB]), _sp(kj)))
            p = jnp.where(lower, jnp.concatenate(prow, axis=0), 0.0)
            intra = _dot3(_nn, _sp(p), vsp)
            o_parts = []
            for c in range(t // CHUNK):
                rs = slice(c * CHUNK, (c + 1) * CHUNK)
                st0 = st_ref[hs, :]
                sall_ref[c * w + h * HG_DK:c * w + (h + 1) * HG_DK, :] = st0
                bl = b_ref[c * CHUNK + CHUNK - 1:c * CHUNK + CHUNK, hs]
                qf = qh[rs] * jnp.exp(bh[rs])
                o_parts.append(_dot3(_nt, _sp(qf), _sp(st0)) + intra[rs])
                khat = kh[rs] * jnp.exp(bl - bh[rs])
                st_ref[hs, :] = st0 * jnp.exp(bl) + _dot3(_tn, _sp(vh[rs]), _sp(khat))
            o = jnp.concatenate(o_parts, axis=0)
            o_ref[:, hs] = o
            r = lax.rsqrt(jnp.mean(o * o, axis=-1, keepdims=True) + EPS)
            hg = hg_ref[:, hs]
            y_ref[:, hs] = (((o * r) * gn_ref[:, hs]) * (hg * jax.nn.sigmoid(hg))).astype(BF16)

    blk = lambda j: pl.BlockSpec((t, w), lambda i, j=j: (i, j))
    srows = (t // CHUNK) * w
    res, side_res = _carry(
        body, side, name=name,
        out_shape=(jax.ShapeDtypeStruct((s, w), BF16), jax.ShapeDtypeStruct((s, w), F32),
                   jax.ShapeDtypeStruct((nt * srows, HG_DK), F32)),
        grid=(nt,),
        in_specs=[blk(3), blk(4), blk(5), blk(6), pl.BlockSpec((3, w), lambda i: (0, 0)),
                  pl.BlockSpec((1, w), lambda i: (0, 0))],
        out_specs=(pl.BlockSpec((t, w), lambda i: (i, 0)), pl.BlockSpec((t, w), lambda i: (i, 0)),
                   pl.BlockSpec((srows, HG_DK), lambda i: (i, 0))),
        scratch_shapes=[pltpu.VMEM((w, HG_DK), F32), pltpu.VMEM((t, w), F32)],
        args=(u, u, u, u, lbl, gn))
    return res if side is None else (res, side_res)


def _hgrn_bwd(u, o_all, sall, dy, lbl, gn, *, name, side=None):
    s = u.shape[0]
    t = HG_TILE
    nt = s // t
    nsub = t // HG_SUB
    w = HG_WIDTH
    nch = t // CHUNK

    def body(hq_ref, hf_ref, hi_ref, hg_ref, o_ref, sall_ref, dy_ref, lbl_ref, gn_ref,
             du_ref, acc_ref, dst_ref, b_ref):
        i = pl.program_id(0)

        @pl.when(i == 0)
        def _():
            dst_ref[...] = jnp.zeros_like(dst_ref)
            acc_ref[...] = jnp.zeros_like(acc_ref)

        lb = _lower_bound(lbl_ref[...])
        sig, f, g, kin = _hg_gates(hf_ref[...], lb)
        row, col, lower, upper = _hg_masks()
        b_ref[...] = _exact_left(lower.astype(BF16), g)
        upper_bf = upper.astype(BF16)
        rowblk = [((row >= j * HG_SUB) & (row < (j + 1) * HG_SUB)) for j in range(nsub)]
        colblk = [((col >= j * HG_SUB) & (col < (j + 1) * HG_SUB)) for j in range(nsub)]
        row1 = lax.broadcasted_iota(jnp.int32, (t, HG_DK), 0)

        for h in range(HG_HEADS):
            hs = slice(h * HG_DK, (h + 1) * HG_DK)
            bh = b_ref[:, hs]
            qh = hq_ref[:, hs]
            kh = kin[:, hs]
            vh = hi_ref[:, hs]
            vsp = _sp(vh)
            hg = hg_ref[:, hs]
            gnh = gn_ref[:, hs]
            o = o_ref[:, hs]
            dyv = dy_ref[:, hs]
            sg = jax.nn.sigmoid(hg)
            r = lax.rsqrt(jnp.mean(o * o, axis=-1, keepdims=True) + EPS)
            ohat = o * r
            du_ref[:, 3 * w + h * HG_DK:3 * w + (h + 1) * HG_DK] = (
                dyv * (ohat * gnh) * (sg * (1.0 + hg * (1.0 - sg)))).astype(BF16)
            don = dyv * (hg * sg)
            acc_ref[0:1, hs] += jnp.sum(don * ohat, axis=0, keepdims=True)
            dohat = don * gnh
            do = r * (dohat - ohat * jnp.mean(dohat * ohat, axis=-1, keepdims=True))
            dosp = _sp(do)
            refs = _hg_refs(b_ref, hs)
            rmat = jnp.concatenate([jnp.broadcast_to(rr, (HG_SUB, HG_DK)) for rr in refs], axis=0)
            eq = jnp.exp(bh - rmat)
            qt = qh * eq
            qtsp = _sp(qt)
            dp = jnp.where(lower, _dot3(_nt, dosp, vsp), 0.0)
            dpt = jnp.where(upper, _dot3(_nt, vsp, dosp), 0.0)
            pt = jnp.zeros((t, t), F32)
            dk = jnp.zeros((t, HG_DK), F32)
            dq_rows = []
            for j in range(nsub):
                ek = jnp.exp(jnp.minimum(refs[j] - bh, HG_EXP_CLAMP))
                kjsp = _sp(kh * ek)
                pt = pt + _nt(kjsp[0], jnp.where(rowblk[j], qtsp[0], 0))
                dq_rows.append(_dot3(_nn, _sp(dp[j * HG_SUB:(j + 1) * HG_SUB]), kjsp))
                dk = dk + ek * _dot3(_nn, _sp(jnp.where(colblk[j], dpt, 0.0)), qtsp)
            pt = jnp.where(upper, pt, 0.0)
            dv = _nn(pt.astype(BF16), dosp[0])
            dq = jnp.concatenate(dq_rows, axis=0) * eq
            dq_c, dk_c, dv_c, ex_c = [None] * nch, [None] * nch, [None] * nch, [None] * nch
            for c in reversed(range(nch)):
                rs = slice(c * CHUNK, (c + 1) * CHUNK)
                st0 = sall_ref[c * w + h * HG_DK:c * w + (h + 1) * HG_DK, :]
                dst1 = dst_ref[hs, :]
                dst1sp = _sp(dst1)
                dosp_c = _sp(do[rs])
                bl = b_ref[c * CHUNK + CHUNK - 1:c * CHUNK + CHUNK, hs]
                e = jnp.exp(bh[rs])
                el = jnp.exp(bl)
                ekl = jnp.exp(bl - bh[rs])
                dq_c[c] = _dot3(_nn, dosp_c, _sp(st0)) * e
                khat = kh[rs] * ekl
                dv_c[c] = _nt(khat.astype(BF16), dst1sp[0])
                dkhat = _dot3(_nn, _sp(vh[rs]), dst1sp)
                dk_c[c] = dkhat * ekl
                ex_c[c] = (jnp.sum(dkhat * khat, axis=0, keepdims=True)
                           + el * jnp.sum(dst1 * st0, axis=0, keepdims=True))
                dst_ref[hs, :] = _dot3(_tn, dosp_c, _sp(qh[rs] * e)) + dst1 * el
            dq = dq + jnp.concatenate(dq_c, axis=0)
            dk = dk + jnp.concatenate(dk_c, axis=0)
            dv = dv + jnp.concatenate(dv_c, axis=0)
            db = qh * dq - kh * dk
            for c in range(nch):
                db = db + jnp.where(row1 == c * CHUNK + CHUNK - 1, ex_c[c], 0.0)
            dg = _exact_left(upper_bf, db)
            fh = f[:, hs]
            sgf = sig[:, hs]
            lbh = lb[:, hs]
            df = dg / fh - dk
            du_ref[:, hs] = dq.astype(BF16)
            du_ref[:, w + h * HG_DK:w + (h + 1) * HG_DK] = (df * (1.0 - lbh) * sgf * (1.0 - sgf)).astype(BF16)
            du_ref[:, 2 * w + h * HG_DK:2 * w + (h + 1) * HG_DK] = dv.astype(BF16)
            acc_ref[1:2, hs] += jnp.sum(df * (1.0 - sgf), axis=0, keepdims=True)

    rev = lambda i: nt - 1 - i
    blk = lambda j: pl.BlockSpec((t, w), lambda i, j=j: (rev(i), j))
    srows = nch * w
    res, side_res = _carry(
        body, side, name=name,
        out_shape=(jax.ShapeDtypeStruct((s, 4 * w), BF16), jax.ShapeDtypeStruct((8, w), F32)),
        grid=(nt,),
        in_specs=[blk(3), blk(4), blk(5), blk(6), pl.BlockSpec((t, w), lambda i: (rev(i), 0)),
                  pl.BlockSpec((srows, HG_DK), lambda i: (rev(i), 0)),
                  pl.BlockSpec((t, w), lambda i: (rev(i), 1)),
                  pl.BlockSpec((3, w), lambda i: (0, 0)), pl.BlockSpec((1, w), lambda i: (0, 0))],
        out_specs=(pl.BlockSpec((t, 4 * w), lambda i: (rev(i), 0)), pl.BlockSpec((8, w), lambda i: (0, 0))),
        scratch_shapes=[pltpu.VMEM((w, HG_DK), F32), pltpu.VMEM((t, w), F32)],
        args=(u, u, u, u, o_all, sall, dy, lbl, gn))
    return res if side is None else (res, side_res)


def _pair_matrix():
    row = lax.broadcasted_iota(jnp.int32, (LANES, LANES), 0)
    col = lax.broadcasted_iota(jnp.int32, (LANES, LANES), 1)
    return ((row >= SB_HEAD_DIM) == (col >= SB_HEAD_DIM)).astype(BF16)


def _qk_norm_fwd(qkv, qn, kn, *, name, tm=256):
    s = qkv.shape[0]
    d = D_MODEL
    tm = _tile(tm, s)

    def body(q_ref, k_ref, v_ref, qn_ref, kn_ref, qo_ref, ko_ref, vo_ref):
        bd = _pair_matrix()
        for src, gain, dst, fac in ((q_ref, qn_ref, qo_ref, SB_SCALE * LOG2E), (k_ref, kn_ref, ko_ref, None)):
            for grp in range(d // LANES):
                ls = slice(grp * LANES, (grp + 1) * LANES)
                xv = src[:, ls]
                ms = _exact_right(xv * xv, bd) * (1.0 / SB_HEAD_DIM)
                y = (xv * lax.rsqrt(ms + EPS)) * gain[:, ls]
                dst[:, ls] = (y if fac is None else y * fac).astype(BF16)
        vo_ref[...] = v_ref[...].astype(BF16)

    blk = lambda j: pl.BlockSpec((tm, d), lambda i, j=j: (i, j))
    row = pl.BlockSpec((1, d), lambda i: (0, 0))
    out = jax.ShapeDtypeStruct((s, d), BF16)
    return pl.pallas_call(
        body, name=name, out_shape=(out, out, out), grid=(s // tm,),
        in_specs=[blk(0), blk(1), blk(2), row, row],
        out_specs=(blk(0), blk(0), blk(0)),
        compiler_params=_cparams(1),
    )(qkv, qkv, qkv, qn, kn)


def _qk_norm_bwd(qkv, dqn, dkn, dv, qn, kn, *, name, tm=256):
    s = qkv.shape[0]
    d = D_MODEL
    tm = _tile(tm, s)

    def body(q_ref, k_ref, dq_ref, dk_ref, dv_ref, qn_ref, kn_ref, o_ref, acc_ref):
        i = pl.program_id(0)

        @pl.when(i == 0)
        def _():
            acc_ref[...] = jnp.zeros_like(acc_ref)

        bd = _pair_matrix()
        for idx, (src, dsrc, gain) in enumerate(((q_ref, dq_ref, qn_ref), (k_ref, dk_ref, kn_ref))):
            for grp in range(d // LANES):
                ls = slice(grp * LANES, (grp + 1) * LANES)
                xv = src[:, ls]
                dyv = dsrc[:, ls]
                r = lax.rsqrt(_exact_right(xv * xv, bd) * (1.0 / SB_HEAD_DIM) + EPS)
                xh = xv * r
                acc_ref[idx:idx + 1, ls] += jnp.sum(dyv * xh, axis=0, keepdims=True)
                dxh = dyv * gain[:, ls]
                mean = _exact_right(dxh * xh, bd) * (1.0 / SB_HEAD_DIM)
                o_ref[:, idx * d + grp * LANES:idx * d + (grp + 1) * LANES] = (r * (dxh - xh * mean)).astype(BF16)
        o_ref[:, 2 * d:3 * d] = dv_ref[...].astype(BF16)

    blk = lambda j: pl.BlockSpec((tm, d), lambda i, j=j: (i, j))
    row = pl.BlockSpec((1, d), lambda i: (0, 0))
    return pl.pallas_call(
        body, name=name,
        out_shape=(jax.ShapeDtypeStruct((s, 3 * d), BF16), jax.ShapeDtypeStruct((8, d), F32)),
        grid=(s // tm,),
        in_specs=[blk(0), blk(1), blk(0), blk(0), blk(0), row, row],
        out_specs=(pl.BlockSpec((tm, 3 * d), lambda i: (i, 0)), pl.BlockSpec((8, d), lambda i: (0, 0))),
        compiler_params=_cparams(1),
    )(qkv, qkv, dqn, dkn, dv, qn, kn)


def _sb_tile(qh, kb, suffix_ones, run, mask):
    z = _nt(qh, kb)
    neg_abs = lax.bitcast_convert_type(lax.bitcast_convert_type(z, jnp.uint32) | jnp.uint32(0x80000000), F32)
    l1m = -(jnp.maximum(z, 0.0) + jnp.log2(1.0 + jnp.exp2(neg_abs)))
    logb = z + l1m
    if mask is not None:
        l1m = jnp.where(mask, l1m, 0.0)
    later = _nn(l1m.astype(BF16), suffix_ones) + run
    wgt = jnp.exp2(logb + later)
    if mask is not None:
        wgt = jnp.where(mask, wgt, 0.0)
    return logb, l1m, wgt


def _suffix_ones(tk):
    row = lax.broadcasted_iota(jnp.int32, (tk, tk), 0)
    col = lax.broadcasted_iota(jnp.int32, (tk, tk), 1)
    return (row > col).astype(BF16)


def _sb_alive(runs):
    return jnp.max(jnp.maximum(runs[0], runs[1])) > -SB_DEAD


def _sb_mask(q0, j, nr, tk):
    qpos = q0 + lax.broadcasted_iota(jnp.int32, (nr, tk), 0)
    kpos = j * tk + lax.broadcasted_iota(jnp.int32, (nr, tk), 1)
    return kpos < qpos


def _sb_fwd(qn, kn, v, *, name, side=None):
    s, d = qn.shape
    tq, tk = _tile(SB_TQ, s), _tile(SB_TK, s)
    assert tk % tq == 0 or tq % tk == 0
    nq = s // tq

    def body(q_ref, k_ref, v_ref, o_ref, acc_ref):
        qi = pl.program_id(1)
        lane = lax.broadcasted_iota(jnp.int32, (tq, LANES), 1)
        first = lane < SB_HEAD_DIM
        q = q_ref[...]
        qh = [jnp.where(first, q, 0).astype(BF16), jnp.where(first, 0, q).astype(BF16)]
        ones = _suffix_ones(tk)
        acc_ref[...] = jnp.zeros_like(acc_ref)

        def tile(j, runs, masked, rows=(0, tq)):
            r0, nr = rows
            ks = pl.ds(pl.multiple_of(j * tk, tk), tk)
            kb = k_ref[ks, :]
            vb = v_ref[ks, :]
            mask = _sb_mask(qi * tq + r0, j, nr, tk) if masked else None
            new_runs = []
            for hh in range(2):
                _, l1m, wgt = _sb_tile(qh[hh][r0:r0 + nr], kb, ones, runs[hh], mask)
                acc_ref[hh, r0:r0 + nr] += _nn(wgt.astype(BF16), vb)
                new_runs.append(runs[hh] + jnp.sum(l1m, axis=1, keepdims=True))
            return tuple(new_runs)

        nfull = (qi * tq) // tk
        zero = jnp.zeros((tq, 1), F32)
        runs = (zero, zero)
        for m in reversed(range(max(tq // tk, 1))):
            r0 = m * tk
            part = tile(nfull + m, tuple(r[r0:] for r in runs), True, (r0, tq - r0))
            runs = tuple(jnp.concatenate([r[:r0], p], axis=0) if r0 else p for r, p in zip(runs, part))

        def step(c):
            it, _, r = c
            r = tile(nfull - 1 - it, r, False)
            return it + 1, _sb_alive(r), r

        lax.while_loop(lambda c: jnp.logical_and(c[0] < nfull, c[1]), step, (0, _sb_alive(runs), runs))
        o_ref[...] = jnp.where(first, acc_ref[0], acc_ref[1])

    res, side_res = _carry(
        body, side, name=name, out_shape=(jax.ShapeDtypeStruct((s, d), F32),), grid=(d // LANES, nq),
        in_specs=[pl.BlockSpec((tq, LANES), lambda p, i: (i, p)), pl.BlockSpec((s, LANES), lambda p, i: (0, p)),
                  pl.BlockSpec((s, LANES), lambda p, i: (0, p))],
        out_specs=(pl.BlockSpec((tq, LANES), lambda p, i: (i, p)),),
        scratch_shapes=[pltpu.VMEM((2, tq, LANES), F32)], args=(qn, kn, v))
    return res[0] if side is None else (res[0], side_res)


def _sb_bwd(qn, kn, v, o, do, *, name, side=None):
    s, d = qn.shape
    tq, tk = _tile(SB_TQ, s), _tile(SB_TK, s)
    assert tk % tq == 0 or tq % tk == 0
    nq = s // tq

    def body(q_ref, k_ref, v_ref, o_ref, do_ref, dq_ref, dk_ref, dv_ref, acc_ref):
        qi = pl.program_id(1)

        @pl.when(qi == 0)
        def _():
            dk_ref[...] = jnp.zeros_like(dk_ref)
            dv_ref[...] = jnp.zeros_like(dv_ref)

        first = lax.broadcasted_iota(jnp.int32, (tq, LANES), 1) < SB_HEAD_DIM
        sel = [first, jnp.logical_not(first)]
        kfirst = lax.broadcasted_iota(jnp.int32, (tk, LANES), 1) < SB_HEAD_DIM
        ksel = [kfirst, jnp.logical_not(kfirst)]
        q = q_ref[...]
        dob = do_ref[...].astype(BF16)
        qh = [jnp.where(sel[hh], q, 0).astype(BF16) for hh in range(2)]
        doh = [jnp.where(sel[hh], dob, 0).astype(BF16) for hh in range(2)]
        prod = dob.astype(F32) * o_ref[...]
        gtot = [jnp.sum(jnp.where(sel[hh], prod, 0.0), axis=1, keepdims=True) for hh in range(2)]
        ones = _suffix_ones(tk)
        acc_ref[...] = jnp.zeros_like(acc_ref)

        def tile(j, carry, masked, rows=(0, tq)):
            r0, nr = rows
            runs, gruns = carry
            ks = pl.ds(pl.multiple_of(j * tk, tk), tk)
            kb = k_ref[ks, :]
            vb = v_ref[ks, :]
            mask = _sb_mask(qi * tq + r0, j, nr, tk) if masked else None
            new_runs, new_gruns = [], []
            dk_add = jnp.zeros((tk, LANES), F32)
            dv_add = jnp.zeros((tk, LANES), F32)
            for hh in range(2):
                qs, dos = qh[hh][r0:r0 + nr], doh[hh][r0:r0 + nr]
                logb, l1m, wgt = _sb_tile(qs, kb, ones, runs[hh], mask)
                wb = wgt.astype(BF16)
                g = _nt(dos, vb) * wb.astype(F32)
                gsuf = _exact_right2(g, ones) + g + gruns[hh]
                dz = g - jnp.exp2(logb) * (g + (gtot[hh][r0:r0 + nr] - gsuf))
                if masked:
                    dz = jnp.where(mask, dz, 0.0)
                dzb = dz.astype(BF16)
                acc_ref[hh, r0:r0 + nr] += _nn(dzb, kb)
                dk_add = dk_add + jnp.where(ksel[hh], _tn(dzb, qs), 0.0)
                dv_add = dv_add + jnp.where(ksel[hh], _tn(wb, dos), 0.0)
                new_runs.append(runs[hh] + jnp.sum(l1m, axis=1, keepdims=True))
                new_gruns.append(gruns[hh] + jnp.sum(g, axis=1, keepdims=True))
            dk_ref[ks, :] += dk_add * LN2
            dv_ref[ks, :] += dv_add
            return tuple(new_runs), tuple(new_gruns)

        nfull = (qi * tq) // tk
        zero = jnp.zeros((tq, 1), F32)
        carry = ((zero, zero), (zero, zero))
        for m in reversed(range(max(tq // tk, 1))):
            r0 = m * tk
            part = tile(nfull + m, tuple(tuple(r[r0:] for r in rs) for rs in carry), True, (r0, tq - r0))
            carry = tuple(tuple(jnp.concatenate([r[:r0], p], axis=0) if r0 else p for r, p in zip(rs, ps))
                          for rs, ps in zip(carry, part))

        def step(c):
            it, _, cr = c
            cr = tile(nfull - 1 - it, cr, False)
            return it + 1, _sb_alive(cr[0]), cr

        lax.while_loop(lambda c: jnp.logical_and(c[0] < nfull, c[1]), step, (0, _sb_alive(carry[0]), carry))
        dq_ref[...] = jnp.where(first, acc_ref[0], acc_ref[1]) * SB_SCALE

    blk = pl.BlockSpec((tq, LANES), lambda p, i: (i, p))
    full = pl.BlockSpec((s, LANES), lambda p, i: (0, p))
    out = jax.ShapeDtypeStruct((s, d), F32)
    res, side_res = _carry(
        body, side, name=name, out_shape=(out, out, out), grid=(d // LANES, nq),
        in_specs=[blk, full, full, blk, blk], out_specs=(blk, full, full),
        scratch_shapes=[pltpu.VMEM((2, tq, LANES), F32)], args=(qn, kn, v, o, do))
    return res if side is None else (res, side_res)


def _mod_part(c_all, ada_w, ada_b_my, *, name):
    nl, d, ncol = ada_w.shape

    def body(c_ref, w_ref, b_ref, part_ref, ca_ref):
        cv = c_ref[...]
        ca = cv * jax.nn.sigmoid(cv)
        ca_ref[...] = ca
        part_ref[...] = _nn(ca.astype(BF16), w_ref[...].astype(BF16)) + b_ref[...]

    return pl.pallas_call(
        body, name=name,
        out_shape=(jax.ShapeDtypeStruct((nl, N_DEV, ncol), F32), jax.ShapeDtypeStruct((N_DEV, d), F32)),
        grid=(nl,),
        in_specs=[pl.BlockSpec((N_DEV, d), lambda l: (0, 0)), pl.BlockSpec((None, d, ncol), lambda l: (l, 0, 0)),
                  pl.BlockSpec((None, 1, ncol), lambda l: (l, 0, 0))],
        out_specs=(pl.BlockSpec((None, N_DEV, ncol), lambda l: (l, 0, 0)), pl.BlockSpec((N_DEV, d), lambda l: (0, 0))),
        compiler_params=_cparams(1),
    )(c_all, ada_w, ada_b_my)


PK_MOD, PK_NMIX, PK_NMLP, PK_HGN, PK_LB, PK_QN, PK_KN, PK_CONV, PK_ROWS = 0, 96, 112, 128, 132, 136, 144, 152, 168


def _small_grads(gath, ca_col, dmod_my, lbl4, *, name):
    def body(g_ref, ca_ref, dm_ref, lbl_ref, gw_ref, gsum_ref, glb_ref, gqk_ref):
        tot = g_ref[0]
        for dev in range(1, N_DEV):
            tot = tot + g_ref[dev]
        gsum_ref[...] = tot
        lv = lbl_ref[...]
        m = jnp.maximum(jnp.maximum(lv[0], lv[1]), lv[2])
        e = [jnp.exp(lv[k] - m) for k in range(3)]
        den = e[0] + e[1] + e[2]
        p = [ek / den for ek in e]
        dlb = tot[PK_LB:PK_LB + 4, :]
        glb_ref[0] = dlb * p[0] * (1.0 - p[0])
        glb_ref[1] = -dlb * p[0] * p[1]
        glb_ref[2] = -dlb * p[0] * p[2]
        for idx, base in enumerate((PK_QN, PK_KN)):
            rowsum = jnp.sum(tot[base:base + 8, :], axis=0, keepdims=True)
            gqk_ref[idx:idx + 1, :] = rowsum + pltpu.roll(rowsum, SB_HEAD_DIM, 1)
        for l in range(2):
            acc = ca_ref[0] * dm_ref[0, l:l + 1, :]
            for smp in range(1, N_DEV):
                acc = acc + ca_ref[smp] * dm_ref[smp, l:l + 1, :]
            gw_ref[l] = acc

    d, ncol = ca_col.shape[1], dmod_my.shape[2]
    vm = pl.BlockSpec(memory_space=pltpu.VMEM)
    return pl.pallas_call(
        body, name=name,
        out_shape=(jax.ShapeDtypeStruct((2, d, ncol), F32), jax.ShapeDtypeStruct((PK_ROWS, LANES), F32),
                   jax.ShapeDtypeStruct((3, 4, LANES), F32), jax.ShapeDtypeStruct((8, LANES), F32)),
        in_specs=[vm, vm, vm, vm], out_specs=(vm, vm, vm, vm),
        compiler_params=pltpu.CompilerParams(vmem_limit_bytes=VMEM_LIMIT),
    )(gath, ca_col, dmod_my, lbl4)


def _adamw_math(w, g, m, v):
    m = ADAM_B1 * m + (1.0 - ADAM_B1) * g
    v = ADAM_B2 * v + (1.0 - ADAM_B2) * (g * g)
    m_hat = m / (1.0 - ADAM_B1 ** ADAM_STEP)
    v_hat = v / (1.0 - ADAM_B2 ** ADAM_STEP)
    delta = -ADAM_LR * (m_hat / (jnp.sqrt(v_hat) + ADAM_EPS) + ADAM_WD * w)
    return delta, m, v


def _adamw(w, g, m, v, *, name, tr=256):
    r, n = w.shape
    tr = _tile(tr, r)

    def body(w_ref, g_ref, m_ref, v_ref, d_ref, mo_ref, vo_ref):
        dl, mn, vn = _adamw_math(w_ref[...], g_ref[...], m_ref[...], v_ref[...])
        d_ref[...] = dl
        mo_ref[...] = mn
        vo_ref[...] = vn

    blk = pl.BlockSpec((tr, n), lambda i: (i, 0))
    out = jax.ShapeDtypeStruct((r, n), F32)
    return pl.pallas_call(
        body, name=name, out_shape=(out, out, out), grid=(r // tr,),
        in_specs=[blk, blk, blk, blk], out_specs=(blk, blk, blk),
        compiler_params=_cparams(1),
    )(w, g, m, v)


def _adamw_small(items, *, name):
    n = len(items)

    def body(*refs):
        ins, outs = refs[:4 * n], refs[4 * n:]
        for k in range(n):
            dl, mn, vn = _adamw_math(*(r[...] for r in ins[4 * k:4 * k + 4]))
            outs[3 * k][...] = dl
            outs[3 * k + 1][...] = mn
            outs[3 * k + 2][...] = vn

    flat = [a for it in items for a in it]
    out_shape = tuple(jax.ShapeDtypeStruct(it[0].shape, F32) for it in items for _ in range(3))
    vm = pl.BlockSpec(memory_space=pltpu.VMEM)
    res = pl.pallas_call(
        body, name=name, out_shape=out_shape, in_specs=[vm] * (4 * n), out_specs=tuple([vm] * (3 * n)),
    )(*flat)
    return [tuple(res[3 * k:3 * k + 3]) for k in range(n)]


def _mlp_fwd(x, g, scale, shift, gate, w1g, w2g, tag, side_w1=None):
    h = _norm_mod(x, g, scale, shift, name=f"{tag}_norm")
    act = _matmul(h, w1g, b_kind="colblk", epi="relu2", out_dtype=BF16, name=f"{tag}_w1", side=side_w1)
    side_res = ()
    if side_w1 is not None:
        act, side_res = act
    z, x_out = _matmul(act, w2g, b_kind="rowblk", epi="resgate", extras=(x, gate), name=f"{tag}_w2")
    return x_out, (h, act, z), side_res


def _mlp_bwd(dz, dx_out, x, saved, g, scale, w1g, w2g, tag, gated, side_dact=None, make_side_dh=None):
    h, act, _ = saved
    du = _matmul(dz, w2g, tb=True, b_kind="rowblk", epi="dact", extras=(act,), out_dtype=BF16,
                 name=f"{tag}_dact", side=side_dact)
    res_dact = ()
    if side_dact is not None:
        du, res_dact = du
    dw2 = _matmul(act, dz, ta=True, name=f"{tag}_dw2")
    dw1_t = _matmul(du, h, ta=True, name=f"{tag}_dw1")
    side_dh = None if make_side_dh is None else make_side_dh(dw1_t, dw2)
    dh = _matmul(du, w1g, tb=True, b_kind="colblk", name=f"{tag}_dh", side=side_dh)
    res_dh = ()
    if side_dh is not None:
        dh, res_dh = dh
    dx, nacc, dz_mix = _norm_mod_bwd(x, dh, dx_out, g, scale, name=f"{tag}_norm_bwd", gated=gated)
    return dx, dw1_t, dw2, nacc, dz_mix, (res_dact, res_dh)


def kernel(x, c, ada_w, ada_b, norm_mix, norm_mlp, w_in_ab, conv_w, hg_norm, lb_logits, w_out_ab, w_qkv, q_norm, k_norm, w_out_c, mlp_w1, mlp_w2, loss_target, m_ada_w, m_ada_b, m_norm_mix, m_norm_mlp, m_w_in_ab, m_conv_w, m_hg_norm, m_lb_logits, m_w_out_ab, m_w_qkv, m_q_norm, m_k_norm, m_w_out_c, m_mlp_w1, m_mlp_w2, v_ada_w, v_ada_b, v_norm_mix, v_norm_mlp, v_w_in_ab, v_conv_w, v_hg_norm, v_lb_logits, v_w_out_ab, v_w_qkv, v_q_norm, v_k_norm, v_w_out_c, v_mlp_w1, v_mlp_w2):
    d = D_MODEL
    my_x, my_y, my_c = lax.axis_index("x"), lax.axis_index("y"), lax.axis_index("c")
    me = 4 * my_x + 2 * my_y + my_c
    xs = x[0]
    tgt = loss_target[0]

    def bf(w):
        return w.astype(BF16)

    ncv = CONV_DIM // N_DEV
    c_and_conv = jnp.concatenate([c, jnp.pad(conv_w[0], ((0, 0), (0, d - ncv))), jnp.zeros((4, d), F32)], axis=0)
    wing, c_and_conv = _run_side(_gather_side([bf(w_in_ab), c_and_conv]), name="gather_w_in")
    win = wing[:, 0].transpose(1, 0, 2).reshape(d, AB_IN)

    c_all = c_and_conv[:, 0]
    conv_full = c_and_conv[:, 1:4, :ncv].transpose(1, 0, 2).reshape(3, CONV_DIM)
    ncol = ada_w.shape[2]
    ada_b_my = lax.dynamic_slice(ada_b, (0, me * ncol), (2, ncol)).reshape(2, 1, ncol)
    part, c_act = _mod_part(c_all, ada_w, ada_b_my, name="mod_part")
    parts = _all_gather(part.reshape(2 * N_DEV, ncol), name="gather_mod", in_vmem=True)
    parts = parts.reshape(N_DEV, 2, N_DEV, ncol)
    mod = lax.dynamic_index_in_dim(parts, me, axis=2, keepdims=False)
    mod = mod.transpose(1, 0, 2).reshape(2, 6, 1, d)

    qn_t = jnp.tile(q_norm, (1, d // SB_HEAD_DIM))
    kn_t = jnp.tile(k_norm, (1, d // SB_HEAD_DIM))

    sh1, sc1, gt1, sh2, sc2, gt2 = [mod[0, k] for k in range(6)]
    h0 = _norm_mod(xs, norm_mix[0:1], sc1, sh1, name="l0_mix_norm")
    u = _matmul(h0, win, name="l0_in_proj")
    y_a = _conv_fwd(u, conv_full, name="l0_conv")
    (y_b, o_hg, sall), (woutg_ab, w1g0, w2g0) = _hgrn_fwd(
        u, lb_logits, hg_norm, name="l0_hgrn",
        side=_gather_side([bf(w_out_ab), bf(mlp_w1[0:1]), bf(mlp_w2[0:1])]))
    wout_ab = woutg_ab.reshape(d, d)
    y_ab = jnp.concatenate([y_a, y_b], axis=1)
    z0, x_mid0 = _matmul(y_ab, wout_ab, epi="resgate", extras=(xs, gt1), name="l0_out_proj")
    x1, mlp0, (wqkvg, woutg_c) = _mlp_fwd(x_mid0, norm_mlp[0:1], sc2, sh2, gt2, w1g0, w2g0, "l0_mlp",
                                          side_w1=_gather_side([bf(w_qkv), bf(w_out_c)]))
    wout_c = woutg_c.reshape(d, d)

    sh1b, sc1b, gt1b, sh2b, sc2b, gt2b = [mod[1, k] for k in range(6)]
    h1 = _norm_mod(x1, norm_mix[1:2], sc1b, sh1b, name="l1_mix_norm")
    qkv = _matmul(h1, wqkvg, b_kind="colblk", name="l1_qkv_proj")
    qn_a, kn_a, v_a = _qk_norm_fwd(qkv, qn_t, kn_t, name="l1_qk_norm")
    o_sb, (w1g1, w2g1) = _sb_fwd(qn_a, kn_a, v_a, name="l1_sb",
                                 side=_gather_side([bf(mlp_w1[1:2]), bf(mlp_w2[1:2])]))
    z1, x_mid1 = _matmul(o_sb, wout_c, epi="resgate", extras=(x1, gt1b), name="l1_out_proj")
    x2, mlp1, _ = _mlp_fwd(x_mid1, norm_mlp[1:2], sc2b, sh2b, gt2b, w1g1, w2g1, "l1_mlp")

    dx, loss_part, dz_mlp, dgt2b = _loss_grad(x2, tgt, mlp1[2], gt2b, name="loss")
    loss = lax.psum(loss_part[0, 0], MESH_AXES)

    my_q = 2 * my_x + my_y
    far_q = [my_q ^ 2, my_q ^ 1, my_q ^ 3]
    blk_ids = jnp.stack([2 * q + my_c for q in far_q] + far_q).astype(jnp.int32)
    my_ids = jnp.stack([me, my_q]).astype(jnp.int32)

    def blocks(g):
        return g.reshape(N_DEV, g.shape[0] // N_DEV, d)

    def by_rows(fn, tag, *lists):
        out = [None] * len(lists[0])
        heights = {}
        for t, g in enumerate(lists[0]):
            heights.setdefault(g.shape[1], []).append(t)
        for r, ts in heights.items():
            for t, v in zip(ts, fn(*[[lst[t] for t in ts] for lst in lists], name=f"{tag}_{r}")):
                out[t] = v
        return out

    def pair_sums(gs, sibs, tag):
        return by_rows(lambda a, b, name: _rs_pair_sum(a, b, blk_ids, name=name), f"rs_pair_sum_{tag}", gs, sibs)

    def final_sums(gs, sibs, fars, tag):
        return by_rows(lambda a, b, c_, name: _rs_final_sum(a, b, c_, my_ids, name=name),
                       f"rs_final_sum_{tag}", gs, sibs, fars)

    dx, dw1t_1, dw2_1, nacc, dyp, (_, sib1) = _mlp_bwd(
        dz_mlp, dx, x_mid1, mlp1, norm_mlp[1:2], sc2b, w1g1, w2g1, "l1_mlp", (z1, gt1b),
        make_side_dh=lambda a, b: _sibling_exchange_side([blocks(a), blocks(b)]))
    dsh2b, dsc2b, dnmlp1, dgt1b = nacc[0:1], nacc[1:2], nacc[2:3], nacc[3:4]
    g1 = [blocks(dw1t_1), blocks(dw2_1)]
    pair1 = pair_sums(g1, sib1, "g1")
    dwout_c = _matmul(o_sb, dyp, ta=True, name="l1_dwout")
    do_sb = _matmul(dyp, wout_c, tb=True, name="l1_do")
    (dqn_a, dkn_a, dv_a), far1 = _sb_bwd(qn_a, kn_a, v_a, o_sb, do_sb, name="l1_sb_bwd",
                                         side=_chip_exchange_side(pair1))
    gsh1 = final_sums(g1, sib1, far1, "g1")
    dqkv, qkacc = _qk_norm_bwd(qkv, dqn_a, dkn_a, dv_a, qn_t, kn_t, name="l1_qk_norm_bwd")
    dwqkv_t = _matmul(dqkv, h1, ta=True, name="l1_dwqkv")
    g2 = [blocks(dwqkv_t), blocks(dwout_c)]
    dh1, sib2 = _matmul(dqkv, wqkvg, tb=True, b_kind="colblk", name="l1_dh", side=_sibling_exchange_side(g2))
    pair2 = pair_sums(g2, sib2, "g2")
    dx, nacc, dz_mlp = _norm_mod_bwd(x1, dh1, dx, norm_mix[1:2], sc1b, name="l1_mix_norm_bwd",
                                     gated=(mlp0[2], gt2))
    dmod1 = [nacc[0:1], nacc[1:2], dgt1b, dsh2b, dsc2b, dgt2b]
    dnmix1, dgt2 = nacc[2:3], nacc[3:4]

    dx, dw1t_0, dw2_0, nacc, dyp, (far2, sib3) = _mlp_bwd(
        dz_mlp, dx, x_mid0, mlp0, norm_mlp[0:1], sc2, w1g0, w2g0, "l0_mlp", (z0, gt1),
        side_dact=_chip_exchange_side(pair2),
        make_side_dh=lambda a, b: _sibling_exchange_side([blocks(a), blocks(b)]))
    dsh2, dsc2, dnmlp0, dgt1 = nacc[0:1], nacc[1:2], nacc[2:3], nacc[3:4]
    gsh2 = final_sums(g2, sib2, far2, "g2")
    g3 = [blocks(dw1t_0), blocks(dw2_0)]
    pair3 = pair_sums(g3, sib3, "g3")
    dwout_ab = _matmul(y_ab, dyp, ta=True, name="l0_dwout")
    g3b = [blocks(dwout_ab)]
    dy_ab, sib3b = _matmul(dyp, wout_ab, tb=True, name="l0_dy", side=_sibling_exchange_side(g3b))
    pair3b = pair_sums(g3b, sib3b, "g3b")
    du_a, dconv = _conv_bwd(u, dy_ab, conv_full, name="l0_conv_bwd")
    (du_b, hgacc), far3 = _hgrn_bwd(u, o_hg, sall, dy_ab, lb_logits, hg_norm, name="l0_hgrn_bwd",
                                    side=_chip_exchange_side(pair3 + pair3b))
    gsh3 = final_sums(g3, sib3, far3[:2], "g3")
    gsh3b = final_sums(g3b, sib3b, far3[2:], "g3b")
    du = jnp.concatenate([du_a, du_b], axis=1)
    dwin_t = _matmul(du, h0, ta=True, name="l0_dwin")
    g4 = [blocks(dwin_t)]
    dh0, sib4 = _matmul(du, win, tb=True, name="l0_dh", side=_sibling_exchange_side(g4))
    pair4 = pair_sums(g4, sib4, "g4")
    grad_x, nacc, far4 = _norm_mod_bwd(xs, dh0, dx, norm_mix[0:1], sc1, name="l0_mix_norm_bwd",
                                       side=_chip_exchange_side(pair4))
    gsh4 = final_sums(g4, sib4, far4, "g4")
    dmod0 = [nacc[0:1], nacc[1:2], dgt1, dsh2, dsc2, dgt2]
    dnmix0 = nacc[2:3]

    g_big = [gsh4[0].T[None], gsh3b[0][None], gsh2[0].T[None], gsh2[1][None],
             jnp.stack([gsh3[0].T, gsh1[0].T]), jnp.stack([gsh3[1], gsh1[1]])]

    packed_small = jnp.concatenate(
        [jnp.concatenate(dmod0, axis=1).reshape(-1, LANES), jnp.concatenate(dmod1, axis=1).reshape(-1, LANES),
         dnmix0.reshape(-1, LANES), dnmix1.reshape(-1, LANES), dnmlp0.reshape(-1, LANES), dnmlp1.reshape(-1, LANES),
         hgacc[0:1].reshape(-1, LANES), hgacc[1:2].reshape(-1, LANES),
         qkacc[0:1].reshape(-1, LANES), qkacc[1:2].reshape(-1, LANES),
         dconv[0:3].reshape(-1, LANES), jnp.zeros((PK_ROWS - PK_CONV - 12, LANES), F32)], axis=0)
    gath = _all_gather(packed_small, name="gather_small_grads", in_vmem=True).reshape(N_DEV, PK_ROWS, LANES)
    dmod_all = gath[:, PK_MOD:PK_NMIX].reshape(N_DEV, 2, 6 * d)
    dmod_my = lax.dynamic_slice(dmod_all, (0, 0, me * ncol), (N_DEV, 2, ncol))
    g_ada_w, gsum, g_lb, g_qk = _small_grads(gath, c_act[:, :, None], dmod_my, lb_logits.reshape(3, 4, LANES),
                                             name="small_grads")
    g_ada_b = gsum[PK_MOD:PK_NMIX].reshape(2, 6 * d)
    g_norm_mix = gsum[PK_NMIX:PK_NMLP].reshape(2, d)
    g_norm_mlp = gsum[PK_NMLP:PK_HGN].reshape(2, d)
    g_hg_norm = gsum[PK_HGN:PK_LB].reshape(1, HG_WIDTH)
    g_lb_logits = g_lb.reshape(3, HG_WIDTH)
    g_q_norm = g_qk[0:1, :SB_HEAD_DIM]
    g_k_norm = g_qk[1:2, :SB_HEAD_DIM]
    g_conv_w = lax.dynamic_slice(gsum[PK_CONV:PK_CONV + 12].reshape(3, CONV_DIM), (0, me * ncv), (3, ncv))[None]

    def flat2(a):
        return a.reshape(-1, a.shape[-1])

    grads = dict(ada_w=g_ada_w, ada_b=g_ada_b, norm_mix=g_norm_mix, norm_mlp=g_norm_mlp, w_in_ab=g_big[0],
                 conv_w=g_conv_w, hg_norm=g_hg_norm, lb_logits=g_lb_logits, w_out_ab=g_big[1], w_qkv=g_big[2],
                 q_norm=g_q_norm, k_norm=g_k_norm, w_out_c=g_big[3], mlp_w1=g_big[4], mlp_w2=g_big[5])
    weights = dict(ada_w=(ada_w, m_ada_w, v_ada_w), ada_b=(ada_b, m_ada_b, v_ada_b),
                   norm_mix=(norm_mix, m_norm_mix, v_norm_mix), norm_mlp=(norm_mlp, m_norm_mlp, v_norm_mlp),
                   w_in_ab=(w_in_ab, m_w_in_ab, v_w_in_ab), conv_w=(conv_w, m_conv_w, v_conv_w),
                   hg_norm=(hg_norm, m_hg_norm, v_hg_norm), lb_logits=(lb_logits, m_lb_logits, v_lb_logits),
                   w_out_ab=(w_out_ab, m_w_out_ab, v_w_out_ab), w_qkv=(w_qkv, m_w_qkv, v_w_qkv),
                   q_norm=(q_norm, m_q_norm, v_q_norm), k_norm=(k_norm, m_k_norm, v_k_norm),
                   w_out_c=(w_out_c, m_w_out_c, v_w_out_c), mlp_w1=(mlp_w1, m_mlp_w1, v_mlp_w1),
                   mlp_w2=(mlp_w2, m_mlp_w2, v_mlp_w2))
    names = list(weights)
    small_names = ["ada_b", "norm_mix", "norm_mlp", "conv_w", "hg_norm", "lb_logits", "q_norm", "k_norm"]
    upd = {}
    small_items = []
    for n in small_names:
        wv, mv, vv = weights[n]
        small_items.append((flat2(wv), flat2(grads[n]), flat2(mv), flat2(vv)))
    for n, res in zip(small_names, _adamw_small(small_items, name="adamw_small")):
        upd[n] = tuple(r.reshape(weights[n][0].shape) for r in res)
    for n in names:
        if n in small_names:
            continue
        wv, mv, vv = weights[n]
        res = _adamw(flat2(wv), flat2(grads[n]), flat2(mv), flat2(vv), name=f"adamw_{n}")
        upd[n] = tuple(r.reshape(wv.shape) for r in res)

    return (loss, grad_x[None], *[grads[n].reshape(weights[n][0].shape) for n in names],
            *[upd[n][0] for n in names], *[upd[n][1] for n in names], *[upd[n][2] for n in names])
```

```python
import functools

import jax
import jax.numpy as jnp
from jax import lax
from jax.experimental import pallas as pl
from jax.experimental.pallas import tpu as pltpu

F32 = jnp.float32
BF16 = jnp.bfloat16
EPS = 1e-6
N_DEV = 8
MESH_AXES = ("x", "y", "c")

D_MODEL = 1024
CONV_DIM = 512
HG_HEADS = 4
HG_DK = 128
HG_WIDTH = 512
CHUNK = 64
HG_TILE = 128
HG_SUB = 16
HG_EXP_CLAMP = 60.0
SB_HEAD_DIM = 64
SB_SCALE = SB_HEAD_DIM ** -0.5
LOG2E = 1.4426950408889634
LN2 = 0.6931471805599453
SB_TQ = 512
SB_TK = 256
SB_DEAD = 150.0
D_FF = 4096
AB_IN = 3584

ADAM_LR = 0.001
ADAM_B1 = 0.9
ADAM_B2 = 0.999
ADAM_EPS = 1e-08
ADAM_WD = 0.01
ADAM_STEP = 10

VMEM_LIMIT = 48 * 1024 * 1024
LANES = 128


def _cparams(n_grid):
    return pltpu.CompilerParams(dimension_semantics=("arbitrary",) * n_grid, vmem_limit_bytes=VMEM_LIMIT)


def _nt(a, b):
    return lax.dot_general(a, b, (((1,), (1,)), ((), ())), preferred_element_type=F32)


def _tn(a, b):
    return lax.dot_general(a, b, (((0,), (0,)), ((), ())), preferred_element_type=F32)


def _nn(a, b):
    return jnp.dot(a, b, preferred_element_type=F32)


def _split3(x):
    hi = x.astype(BF16)
    r1 = x - hi.astype(F32)
    mid = r1.astype(BF16)
    lo = (r1 - mid.astype(F32)).astype(BF16)
    return hi, mid, lo


def _exact_left(m01, x):
    hi, mid, lo = _split3(x)
    return _nn(m01, hi) + _nn(m01, mid) + _nn(m01, lo)


def _exact_right(x, m01):
    hi, mid, lo = _split3(x)
    return _nn(hi, m01) + _nn(mid, m01) + _nn(lo, m01)


def _exact_right2(x, m01):
    hi = x.astype(BF16)
    lo = (x - hi.astype(F32)).astype(BF16)
    return _nn(hi, m01) + _nn(lo, m01)


def _sp(x):
    hi = x.astype(BF16)
    return hi, (x - hi.astype(F32)).astype(BF16)


def _dot3(fn, a, b):
    return fn(a[0], b[0]) + fn(a[0], b[1]) + fn(a[1], b[0])


def _tile(pref, n):
    t = min(pref, n)
    assert n % t == 0, (pref, n)
    return t


def _tile_rows(pref, n):
    for t in range(min(pref, n) - min(pref, n) % 16, 0, -16):
        if n % t == 0:
            return t
    raise ValueError((pref, n))


def _tile_lanes(pref, n):
    if n <= pref:
        return n
    for t in range(pref - pref % LANES, 0, -LANES):
        if n % t == 0:
            return t
    raise ValueError((pref, n))


def _all_gather(x, *, name, in_vmem):
    m_per, n = x.shape

    def body(x_ref, out_ref, send_sems, recv_sems, local_sem):
        mx, my, mc = lax.axis_index("x"), lax.axis_index("y"), lax.axis_index("c")
        me, sibling = (mx, my, mc), (mx, my, 1 - mc)
        chips = [(1 - mx, my), (mx, 1 - my), (1 - mx, 1 - my)]

        def rows(px, py, pc):
            return out_ref.at[pl.ds((4 * px + 2 * py + pc) * m_per, m_per), :]

        def copy(k, block, to, src=None):
            return pltpu.make_async_remote_copy(
                src_ref=rows(*block) if src is None else src, dst_ref=rows(*block),
                send_sem=send_sems.at[k], recv_sem=recv_sems.at[k],
                device_id=to, device_id_type=pl.DeviceIdType.MESH)

        mine = pltpu.make_async_copy(x_ref, rows(*me), local_sem)
        mine.start()
        first = [copy(0, me, sibling, src=x_ref)]
        first += [copy(1 + j, me, (*chip, mc), src=x_ref) for j, chip in enumerate(chips)]
        for cp in first:
            cp.start()
        passed = [copy(4 + j, (*chip, mc), sibling) for j, chip in enumerate(chips)]
        for j, chip in enumerate(chips):
            copy(1 + j, (*chip, mc), me).wait_recv()
            passed[j].start()
        copy(0, sibling, me).wait_recv()
        for j, chip in enumerate(chips):
            copy(4 + j, (*chip, 1 - mc), me).wait_recv()
        for cp in first + passed:
            cp.wait_send()
        mine.wait()

    space = pltpu.VMEM if in_vmem else pl.ANY
    return pl.pallas_call(
        body, name=name,
        out_shape=jax.ShapeDtypeStruct((N_DEV * m_per, n), x.dtype),
        in_specs=[pl.BlockSpec(memory_space=space)],
        out_specs=pl.BlockSpec(memory_space=space),
        scratch_shapes=[pltpu.SemaphoreType.DMA((7,)), pltpu.SemaphoreType.DMA((7,)), pltpu.SemaphoreType.DMA],
    )(x)


class _Side:
    def __init__(self, inputs, out_shape, scratch, start, finish):
        self.inputs, self.out_shape, self.scratch = list(inputs), tuple(out_shape), list(scratch)
        self.start, self.finish = start, finish


def _run_side(side, *, name):
    n_in, n_out = len(side.inputs), len(side.out_shape)

    def body(*refs):
        parts = (refs[:n_in], refs[n_in:n_in + n_out], refs[n_in + n_out:])
        side.start(*parts)
        side.finish(*parts)

    hbm = pl.BlockSpec(memory_space=pl.ANY)
    return pl.pallas_call(body, name=name, out_shape=side.out_shape, in_specs=[hbm] * n_in,
                          out_specs=tuple([hbm] * n_out), scratch_shapes=side.scratch)(*side.inputs)


def _carry(body, side, *, name, grid, in_specs, out_specs, out_shape, scratch_shapes, args):
    in_specs, out_specs, out_shape = list(in_specs), tuple(out_specs), tuple(out_shape)
    scratch_shapes = list(scratch_shapes)
    if side is None:
        res = pl.pallas_call(body, name=name, grid=grid, in_specs=in_specs, out_specs=out_specs,
                             out_shape=out_shape, scratch_shapes=scratch_shapes,
                             compiler_params=_cparams(len(grid)))(*args)
        return tuple(res), ()
    n_in, n_out, n_scr = len(in_specs), len(out_specs), len(scratch_shapes)
    s_in, s_out = len(side.inputs), len(side.out_shape)

    def wrapped(*refs):
        ins, rest = refs[:n_in], refs[n_in:]
        s_ins, rest = rest[:s_in], rest[s_in:]
        outs, rest = rest[:n_out], rest[n_out:]
        s_outs, rest = rest[:s_out], rest[s_out:]
        scr, s_scr = rest[:n_scr], rest[n_scr:]
        ids = [pl.program_id(ax) for ax in range(len(grid))]
        first = functools.reduce(jnp.logical_and, [i == 0 for i in ids])
        last = functools.reduce(jnp.logical_and, [i == g - 1 for i, g in zip(ids, grid)])

        @pl.when(first)
        def _():
            side.start(s_ins, s_outs, s_scr)

        body(*ins, *outs, *scr)

        @pl.when(last)
        def _():
            side.finish(s_ins, s_outs, s_scr)

    hbm = pl.BlockSpec(memory_space=pl.ANY)
    res = pl.pallas_call(
        wrapped, name=name, grid=grid, in_specs=in_specs + [hbm] * s_in,
        out_specs=out_specs + tuple([hbm] * s_out), out_shape=out_shape + side.out_shape,
        scratch_shapes=scratch_shapes + side.scratch, compiler_params=_cparams(len(grid)),
    )(*args, *side.inputs)
    return tuple(res[:n_out]), tuple(res[n_out:])


def _gather_side(xs):
    n = len(xs)

    def tools(x_refs, out_refs, sems):
        send_sems, recv_sems, local_sems = sems
        mx, my, mc = lax.axis_index("x"), lax.axis_index("y"), lax.axis_index("c")
        me, sibling = (mx, my, mc), (mx, my, 1 - mc)
        chips = [(1 - mx, my), (mx, 1 - my), (1 - mx, 1 - my)]

        def slot(t, px, py, pc):
            return out_refs[t].at[4 * px + 2 * py + pc]

        def copy(t, k, block, to, src=None):
            return pltpu.make_async_remote_copy(
                src_ref=slot(t, *block) if src is None else src, dst_ref=slot(t, *block),
                send_sem=send_sems.at[7 * t + k], recv_sem=recv_sems.at[7 * t + k],
                device_id=to, device_id_type=pl.DeviceIdType.MESH)

        mine = [pltpu.make_async_copy(x_refs[t], slot(t, *me), local_sems.at[t]) for t in range(n)]
        first = []
        for t in range(n):
            first.append(copy(t, 0, me, sibling, src=x_refs[t]))
            first += [copy(t, 1 + j, me, (*chip, mc), src=x_refs[t]) for j, chip in enumerate(chips)]
        return me, sibling, chips, mc, copy, mine, first

    def start(x_refs, out_refs, sems):
        *_, mine, first = tools(x_refs, out_refs, sems)
        for cp in mine + first:
            cp.start()

    def finish(x_refs, out_refs, sems):
        me, sibling, chips, mc, copy, mine, first = tools(x_refs, out_refs, sems)
        passed = []
        for j, chip in enumerate(chips):
            for t in range(n):
                copy(t, 1 + j, (*chip, mc), me).wait_recv()
                passed.append(copy(t, 4 + j, (*chip, mc), sibling))
                passed[-1].start()
        for t in range(n):
            copy(t, 0, sibling, me).wait_recv()
            for j, chip in enumerate(chips):
                copy(t, 4 + j, (*chip, 1 - mc), me).wait_recv()
        for cp in first + passed:
            cp.wait_send()
        for cp in mine:
            cp.wait()

    return _Side(xs, [jax.ShapeDtypeStruct((N_DEV,) + x.shape, x.dtype) for x in xs],
                 [pltpu.SemaphoreType.DMA((7 * n,)), pltpu.SemaphoreType.DMA((7 * n,)),
                  pltpu.SemaphoreType.DMA((n,))], start, finish)


def _sibling_exchange_side(gs):
    n = len(gs)

    def copies(g_refs, out_refs, sems):
        send_sems, recv_sems = sems
        mx, my, mc = lax.axis_index("x"), lax.axis_index("y"), lax.axis_index("c")
        return [pltpu.make_async_remote_copy(
            src_ref=g_refs[t].at[2 * q + (1 - mc)], dst_ref=out_refs[t].at[q],
            send_sem=send_sems.at[4 * t + q], recv_sem=recv_sems.at[4 * t + q],
            device_id=(mx, my, 1 - mc), device_id_type=pl.DeviceIdType.MESH)
            for t in range(n) for q in range(4)]

    def start(g_refs, out_refs, sems):
        for cp in copies(g_refs, out_refs, sems):
            cp.start()

    def finish(g_refs, out_refs, sems):
        cps = copies(g_refs, out_refs, sems)
        for cp in cps:
            cp.wait_recv()
        for cp in cps:
            cp.wait_send()

    return _Side(gs, [jax.ShapeDtypeStruct((4,) + g.shape[1:], g.dtype) for g in gs],
                 [pltpu.SemaphoreType.DMA((4 * n,)), pltpu.SemaphoreType.DMA((4 * n,))], start, finish)


def _chip_exchange_side(ts):
    n = len(ts)

    def copies(t_refs, out_refs, sems):
        send_sems, recv_sems = sems
        mx, my, mc = lax.axis_index("x"), lax.axis_index("y"), lax.axis_index("c")
        chips = [(1 - mx, my), (mx, 1 - my), (1 - mx, 1 - my)]
        return [pltpu.make_async_remote_copy(
            src_ref=t_refs[t].at[k], dst_ref=out_refs[t].at[k],
            send_sem=send_sems.at[3 * t + k], recv_sem=recv_sems.at[3 * t + k],
            device_id=(px, py, mc), device_id_type=pl.DeviceIdType.MESH)
            for t in range(n) for k, (px, py) in enumerate(chips)]

    def start(t_refs, out_refs, sems):
        for cp in copies(t_refs, out_refs, sems):
            cp.start()

    def finish(t_refs, out_refs, sems):
        cps = copies(t_refs, out_refs, sems)
        for cp in cps:
            cp.wait_recv()
        for cp in cps:
            cp.wait_send()

    return _Side(ts, [jax.ShapeDtypeStruct(t.shape, t.dtype) for t in ts],
                 [pltpu.SemaphoreType.DMA((3 * n,)), pltpu.SemaphoreType.DMA((3 * n,))], start, finish)


def _rs_pair_sum(gs, p1s, blk_ids, *, name, tr=256):
    n = len(gs)
    _, r, ncol = gs[0].shape
    tr = _tile_rows(tr, r)

    def body(id_ref, *refs):
        for t in range(n):
            refs[2 * n + t][...] = (refs[t][...] + refs[n + t][...]).astype(BF16)

    blk = lambda off: pl.BlockSpec((None, tr, ncol), lambda k, i, ids: (ids[off + k], i, 0))
    out = pl.BlockSpec((None, tr, ncol), lambda k, i, ids: (k, i, 0))
    return pl.pallas_call(
        body, name=name,
        out_shape=tuple(jax.ShapeDtypeStruct((3, r, ncol), BF16) for _ in gs),
        grid_spec=pltpu.PrefetchScalarGridSpec(
            num_scalar_prefetch=1, grid=(3, r // tr),
            in_specs=[blk(0)] * n + [blk(3)] * n, out_specs=tuple([out] * n)),
        compiler_params=_cparams(2),
    )(blk_ids, *gs, *p1s)


def _rs_final_sum(gs, p1s, p3s, my_ids, *, name, tr=256):
    n = len(gs)
    _, r, ncol = gs[0].shape
    tr = _tile_rows(tr, r)

    def body(id_ref, *refs):
        for t in range(n):
            g_ref, s_ref = refs[t], refs[n + t]
            a_ref, b_ref, c_ref = refs[2 * n + 3 * t:2 * n + 3 * t + 3]
            own = g_ref[...] + s_ref[...]
            refs[5 * n + t][...] = (((own + a_ref[...].astype(F32)) + b_ref[...].astype(F32))
                                    + c_ref[...].astype(F32))

    sel = lambda which: pl.BlockSpec((None, tr, ncol), lambda i, ids: (ids[which], i, 0))
    fix = lambda k: pl.BlockSpec((None, tr, ncol), lambda i, ids: (k, i, 0))
    p3_specs, p3_args = [], []
    for p3 in p3s:
        p3_specs += [fix(0), fix(1), fix(2)]
        p3_args += [p3, p3, p3]
    return pl.pallas_call(
        body, name=name,
        out_shape=tuple(jax.ShapeDtypeStruct((r, ncol), F32) for _ in gs),
        grid_spec=pltpu.PrefetchScalarGridSpec(
            num_scalar_prefetch=1, grid=(r // tr,),
            in_specs=[sel(0)] * n + [sel(1)] * n + p3_specs,
            out_specs=tuple([pl.BlockSpec((tr, ncol), lambda i, ids: (i, 0))] * n)),
        compiler_params=_cparams(1),
    )(my_ids, *gs, *p1s, *p3_args)


def _matmul(a, b, *, name, ta=False, tb=False, epi="plain", extras=(), out_dtype=F32, tm=None, tn=1024, tk=1024,
            b_kind=None, layer=0, side=None):
    if ta:
        kdim, m = a.shape
    else:
        m, kdim = a.shape
    pair = 1
    if b_kind is None:
        if tb:
            n, kb = b.shape
        else:
            kb, n = b.shape
        tn, tk = _tile_lanes(tn, n), _tile_lanes(tk, kb)
        b_spec = (pl.BlockSpec((tn, tk), lambda i, j, k: (j, k)) if tb
                  else pl.BlockSpec((tk, tn), lambda i, j, k: (k, j)))
    elif b_kind == "colblk":
        assert not ta
        _, _, kw, nsh = b.shape
        if tb:
            kb, n, pair = N_DEV * nsh, kw, 2
            tn, tk = _tile_lanes(tn, n), pair * nsh
            b_spec = pl.BlockSpec((pair, None, tn, nsh), lambda i, j, k: (k, layer, j, 0))
        else:
            kb, n, pair = kw, N_DEV * nsh, 2
            tn, tk = pair * nsh, _tile_lanes(tk, kb)
            b_spec = pl.BlockSpec((pair, None, tk, nsh), lambda i, j, k: (j, layer, k, 0))
    elif b_kind == "rowblk":
        assert not ta
        _, _, r, ncol = b.shape
        pair = 2
        if tb:
            kb, n = ncol, N_DEV * r
            tn, tk = pair * r, _tile_lanes(tk, kb)
            b_spec = pl.BlockSpec((pair, None, r, tk), lambda i, j, k: (j, layer, 0, k))
        else:
            kb, n = N_DEV * r, ncol
            tn, tk = _tile_lanes(tn, n), pair * r
            b_spec = pl.BlockSpec((pair, None, r, tn), lambda i, j, k: (k, layer, 0, j))
    else:
        raise ValueError(b_kind)
    assert kdim == kb, (a.shape, b.shape)
    if tm is None:
        tm = 1024 if epi == "resgate" else 2048
    tm = _tile_lanes(tm, m)
    nk = kdim // tk
    a_spec = pl.BlockSpec((tk, tm), lambda i, j, k: (k, i)) if ta else pl.BlockSpec((tm, tk), lambda i, j, k: (i, k))
    dims = (((0 if ta else 1,), (1 if tb else 0,)), ((), ()))
    mn_spec = pl.BlockSpec((tm, tn), lambda i, j, k: (i, j))
    row_spec = pl.BlockSpec((1, tn), lambda i, j, k: (0, j))
    if epi == "resgate":
        extra_specs = [mn_spec, row_spec]
        out_shape = (jax.ShapeDtypeStruct((m, n), BF16), jax.ShapeDtypeStruct((m, n), F32))
        out_specs = (mn_spec, mn_spec)
    elif epi == "dact":
        extra_specs = [mn_spec]
        out_shape = jax.ShapeDtypeStruct((m, n), out_dtype)
        out_specs = mn_spec
    else:
        extra_specs = []
        out_shape = jax.ShapeDtypeStruct((m, n), out_dtype)
        out_specs = mn_spec
    n_extra = len(extra_specs)

    def body(a_ref, b_ref, *rest):
        ex = rest[:n_extra]
        outs = rest[n_extra:n_extra + n_out]
        k = pl.program_id(2)

        def prod():
            av = a_ref[...].astype(BF16)
            if b_kind == "rowblk":
                bv = b_ref[...].astype(BF16)
                return lax.dot_general(av, bv.reshape(bv.shape[0] * bv.shape[1], bv.shape[2]), dims,
                                       preferred_element_type=F32)
            if b_kind == "colblk" and tb:
                nsh = b_ref.shape[-1]
                return sum(lax.dot_general(av[:, p * nsh:(p + 1) * nsh], b_ref[p].astype(BF16), dims,
                                           preferred_element_type=F32) for p in range(pair))
            if b_kind == "colblk":
                return jnp.concatenate([lax.dot_general(av, b_ref[p].astype(BF16), dims, preferred_element_type=F32)
                                        for p in range(pair)], axis=1)
            return lax.dot_general(av, b_ref[...].astype(BF16), dims, preferred_element_type=F32)

        def finish(r):
            if epi == "plain":
                outs[0][...] = r.astype(outs[0].dtype)
            elif epi == "resgate":
                outs[0][...] = r.astype(BF16)
                outs[1][...] = ex[0][...] + ex[1][...] * r
            elif epi == "relu2":
                p = jnp.maximum(r, 0.0)
                outs[0][...] = (p * p).astype(outs[0].dtype)
            elif epi == "dact":
                outs[0][...] = (r * (2.0 * jnp.sqrt(ex[0][...].astype(F32)))).astype(outs[0].dtype)

        if nk == 1:
            finish(prod())
        else:
            acc = rest[-1]

            @pl.when(k == 0)
            def _():
                acc[...] = prod()

            if nk > 2:
                @pl.when(jnp.logical_and(k > 0, k < nk - 1))
                def _():
                    acc[...] += prod()

            @pl.when(k == nk - 1)
            def _():
                finish(acc[...] + prod())

    n_out = 2 if epi == "resgate" else 1
    if n_out == 1:
        out_shape, out_specs = (out_shape,), (out_specs,)
    res, side_res = _carry(
        body, side, name=name, grid=(m // tm, n // tn, nk), in_specs=[a_spec, b_spec] + extra_specs,
        out_specs=out_specs, out_shape=out_shape,
        scratch_shapes=[pltpu.VMEM((tm, tn), F32)] if nk > 1 else [], args=(a, b, *extras))
    res = res if n_out == 2 else res[0]
    return res if side is None else (res, side_res)


def _norm_mod(x, g, scale, shift, *, name, tm=512):
    s, d = x.shape
    tm = _tile(tm, s)

    def body(x_ref, g_ref, sc_ref, sh_ref, h_ref):
        xv = x_ref[...]
        r = lax.rsqrt(jnp.mean(xv * xv, axis=-1, keepdims=True) + EPS)
        h_ref[...] = (((xv * r) * g_ref[...]) * (1.0 + sc_ref[...]) + sh_ref[...]).astype(BF16)

    row = pl.BlockSpec((1, d), lambda i: (0, 0))
    return pl.pallas_call(
        body, name=name, out_shape=jax.ShapeDtypeStruct((s, d), BF16), grid=(s // tm,),
        in_specs=[pl.BlockSpec((tm, d), lambda i: (i, 0)), row, row, row],
        out_specs=pl.BlockSpec((tm, d), lambda i: (i, 0)),
        compiler_params=_cparams(1),
    )(x, g, scale, shift)


def _norm_mod_bwd(x, dh, dres, g, scale, *, name, tm=512, gated=None, side=None):
    s, d = x.shape
    tm = _tile(tm, s)
    n_in = 5 if gated is None else 7

    def body(*refs):
        x_ref, dh_ref, dr_ref, g_ref, sc_ref = refs[:5]
        dx_ref, acc_ref = refs[n_in:n_in + 2]
        i = pl.program_id(0)

        @pl.when(i == 0)
        def _():
            acc_ref[...] = jnp.zeros_like(acc_ref)

        xv = x_ref[...]
        dhv = dh_ref[...]
        gv = g_ref[...]
        one_sc = 1.0 + sc_ref[...]
        r = lax.rsqrt(jnp.mean(xv * xv, axis=-1, keepdims=True) + EPS)
        xn = xv * r
        dxn = dhv * (gv * one_sc)
        dxv = dr_ref[...] + r * (dxn - xn * jnp.mean(dxn * xn, axis=-1, keepdims=True))
        dx_ref[...] = dxv
        dhxn = dhv * xn
        acc_ref[0:1, :] += jnp.sum(dhv, axis=0, keepdims=True)
        acc_ref[1:2, :] += jnp.sum(dhxn * gv, axis=0, keepdims=True)
        acc_ref[2:3, :] += jnp.sum(dhxn * one_sc, axis=0, keepdims=True)
        if gated is not None:
            z_ref, gate_ref, dz_ref = refs[5], refs[6], refs[n_in + 2]
            dz_ref[...] = (dxv * gate_ref[...]).astype(BF16)
            acc_ref[3:4, :] += jnp.sum(dxv * z_ref[...], axis=0, keepdims=True)

    row = pl.BlockSpec((1, d), lambda i: (0, 0))
    blk = pl.BlockSpec((tm, d), lambda i: (i, 0))
    in_specs, args = [blk, blk, blk, row, row], [x, dh, dres, g, scale]
    out_shape = [jax.ShapeDtypeStruct((s, d), F32), jax.ShapeDtypeStruct((8, d), F32)]
    out_specs = [blk, pl.BlockSpec((8, d), lambda i: (0, 0))]
    if gated is not None:
        in_specs += [blk, row]
        args += list(gated)
        out_shape.append(jax.ShapeDtypeStruct((s, d), BF16))
        out_specs.append(blk)
    res, side_res = _carry(body, side, name=name, grid=(s // tm,), in_specs=in_specs, out_specs=out_specs,
                           out_shape=out_shape, scratch_shapes=[], args=args)
    return res if side is None else res + (side_res,)


def _loss_grad(xf, target, z, gate, *, name, tm=512):
    s, d = xf.shape
    tm = _tile(tm, s)
    nt = s // tm

    def body(x_ref, t_ref, z_ref, gate_ref, dx_ref, loss_ref, dz_ref, dgate_ref, acc_ref):
        i = pl.program_id(0)

        @pl.when(i == 0)
        def _():
            acc_ref[...] = jnp.zeros_like(acc_ref)
            dgate_ref[...] = jnp.zeros_like(dgate_ref)

        e = x_ref[...] - t_ref[...]
        dxv = e * (1.0 / d)
        dx_ref[...] = dxv
        dz_ref[...] = (dxv * gate_ref[...]).astype(BF16)
        dgate_ref[...] += jnp.sum(dxv * z_ref[...], axis=0, keepdims=True)
        acc_ref[...] += jnp.sum(e * e, axis=0, keepdims=True)

        @pl.when(i == nt - 1)
        def _():
            loss_ref[...] = (0.5 / d) * jnp.sum(acc_ref[...], axis=1, keepdims=True)

    blk = pl.BlockSpec((tm, d), lambda i: (i, 0))
    row = pl.BlockSpec((1, d), lambda i: (0, 0))
    return pl.pallas_call(
        body, name=name,
        out_shape=(jax.ShapeDtypeStruct((s, d), F32), jax.ShapeDtypeStruct((1, 1), F32),
                   jax.ShapeDtypeStruct((s, d), BF16), jax.ShapeDtypeStruct((1, d), F32)),
        grid=(nt,), in_specs=[blk, blk, blk, row],
        out_specs=(blk, pl.BlockSpec((1, 1), lambda i: (0, 0)), blk, row),
        scratch_shapes=[pltpu.VMEM((1, d), F32)],
        compiler_params=_cparams(1),
    )(xf, target, z, gate)


def _shift_down(p, prev, k):
    tm = p.shape[0]
    row = lax.broadcasted_iota(jnp.int32, p.shape, 0)
    out = pltpu.roll(p, k, 0)
    for j in range(k):
        out = jnp.where(row == j, prev[8 - k + j:8 - k + j + 1, :], out)
    return out


def _shift_up(p, nxt, k):
    tm = p.shape[0]
    row = lax.broadcasted_iota(jnp.int32, p.shape, 0)
    out = pltpu.roll(p, tm - k, 0)
    for j in range(k):
        out = jnp.where(row == tm - k + j, nxt[j:j + 1, :], out)
    return out


def _conv_fwd(u, w, *, name, tm=512):
    s = u.shape[0]
    tm = _tile(tm, s)
    c = CONV_DIM

    def body(ab_ref, ac_ref, ah_ref, w_ref, y_ref, carry_ref):
        i = pl.program_id(0)

        @pl.when(i == 0)
        def _():
            carry_ref[...] = jnp.zeros_like(carry_ref)

        p = ac_ref[...] * ah_ref[...]
        prev = carry_ref[...]
        wv = w_ref[...]
        conv = wv[2:3, :] * p + wv[1:2, :] * _shift_down(p, prev, 1) + wv[0:1, :] * _shift_down(p, prev, 2)
        y_ref[...] = (ab_ref[...] * conv).astype(BF16)
        carry_ref[...] = p[tm - 8:tm, :]

    return pl.pallas_call(
        body, name=name, out_shape=jax.ShapeDtypeStruct((s, c), BF16), grid=(s // tm,),
        in_specs=[pl.BlockSpec((tm, c), lambda i: (i, 0)), pl.BlockSpec((tm, c), lambda i: (i, 1)),
                  pl.BlockSpec((tm, c), lambda i: (i, 2)), pl.BlockSpec((3, c), lambda i: (0, 0))],
        out_specs=pl.BlockSpec((tm, c), lambda i: (i, 0)),
        scratch_shapes=[pltpu.VMEM((8, c), F32)],
        compiler_params=_cparams(1),
    )(u, u, u, w)


def _conv_bwd(u, dy, w, *, name, tm=512):
    s = u.shape[0]
    tm = _tile(tm, s)
    nt = s // tm
    c = CONV_DIM
    hb = tm // 8

    def body(ab_ref, ac_ref, ah_ref, hc_ref, hh_ref, dy_ref, w_ref, du_ref, dw_ref, carry_ref):
        i = pl.program_id(0)

        @pl.when(i == 0)
        def _():
            carry_ref[...] = jnp.zeros_like(carry_ref)
            dw_ref[...] = jnp.zeros_like(dw_ref)

        first_tile = (nt - 1 - i) == 0
        ab, ac, ah = ab_ref[...], ac_ref[...], ah_ref[...]
        p = ac * ah
        prev = jnp.where(first_tile, 0.0, hc_ref[...] * hh_ref[...])
        wv = w_ref[...]
        p1 = _shift_down(p, prev, 1)
        p2 = _shift_down(p, prev, 2)
        conv = wv[2:3, :] * p + wv[1:2, :] * p1 + wv[0:1, :] * p2
        dyv = dy_ref[...]
        dconv = dyv * ab
        nxt = carry_ref[...]
        dp = wv[2:3, :] * dconv + wv[1:2, :] * _shift_up(dconv, nxt, 1) + wv[0:1, :] * _shift_up(dconv, nxt, 2)
        du_ref[:, 0:c] = (dyv * conv).astype(BF16)
        du_ref[:, c:2 * c] = (dp * ah).astype(BF16)
        du_ref[:, 2 * c:3 * c] = (dp * ac).astype(BF16)
        dw_ref[0:1, :] += jnp.sum(dconv * p2, axis=0, keepdims=True)
        dw_ref[1:2, :] += jnp.sum(dconv * p1, axis=0, keepdims=True)
        dw_ref[2:3, :] += jnp.sum(dconv * p, axis=0, keepdims=True)
        carry_ref[...] = dconv[0:8, :]

    rev = lambda i: nt - 1 - i
    halo = lambda i: jnp.maximum(rev(i) * hb - 1, 0)
    return pl.pallas_call(
        body, name=name,
        out_shape=(jax.ShapeDtypeStruct((s, 3 * c), BF16), jax.ShapeDtypeStruct((8, c), F32)),
        grid=(nt,),
        in_specs=[pl.BlockSpec((tm, c), lambda i: (rev(i), 0)), pl.BlockSpec((tm, c), lambda i: (rev(i), 1)),
                  pl.BlockSpec((tm, c), lambda i: (rev(i), 2)),
                  pl.BlockSpec((8, c), lambda i: (halo(i), 1)), pl.BlockSpec((8, c), lambda i: (halo(i), 2)),
                  pl.BlockSpec((tm, c), lambda i: (rev(i), 0)), pl.BlockSpec((3, c), lambda i: (0, 0))],
        out_specs=(pl.BlockSpec((tm, 3 * c), lambda i: (rev(i), 0)), pl.BlockSpec((8, c), lambda i: (0, 0))),
        scratch_shapes=[pltpu.VMEM((8, c), F32)],
        compiler_params=_cparams(1),
    )(u, u, u, u, u, dy, w)


def _lower_bound(lbl):
    m = jnp.max(lbl, axis=0, keepdims=True)
    e = jnp.exp(lbl - m)
    return e[0:1, :] / jnp.sum(e, axis=0, keepdims=True)


def _hg_masks():
    t = HG_TILE
    row = lax.broadcasted_iota(jnp.int32, (t, t), 0)
    col = lax.broadcasted_iota(jnp.int32, (t, t), 1)
    same = (row >= CHUNK) == (col >= CHUNK)
    lower = same & (col <= row)
    upper = same & (row <= col)
    return row, col, lower, upper


def _hg_gates(hf, lb):
    sig = jax.nn.sigmoid(hf)
    f = lb + (1.0 - lb) * sig
    return sig, f, jnp.log(f), 1.0 - f


def _hg_refs(b_ref, hs):
    refs = []
    for i in range(HG_TILE // HG_SUB):
        if (i * HG_SUB) % CHUNK == 0:
            refs.append(jnp.zeros((1, HG_DK), F32))
        else:
            refs.append(b_ref[i * HG_SUB - 1:i * HG_SUB, hs])
    return refs


def _hgrn_fwd(u, lbl, gn, *, name, side=None):
    s = u.shape[0]
    t = HG_TILE
    nt = s // t
    nsub = t // HG_SUB
    w = HG_WIDTH

    def body(hq_ref, hf_ref, hi_ref, hg_ref, lbl_ref, gn_ref, y_ref, o_ref, sall_ref, st_ref, b_ref):
        i = pl.program_id(0)

        @pl.when(i == 0)
        def _():
            st_ref[...] = jnp.zeros_like(st_ref)

        lb = _lower_bound(lbl_ref[...])
        _, _, g, kin = _hg_gates(hf_ref[...], lb)
        _, _, lower, _ = _hg_masks()
        b_ref[...] = _exact_left(lower.astype(BF16), g)

        for h in range(HG_HEADS):
            hs = slice(h * HG_DK, (h + 1) * HG_DK)
            bh = b_ref[:, hs]
            qh = hq_ref[:, hs]
            kh = kin[:, hs]
            vh = hi_ref[:, hs]
            vsp = _sp(vh)
            refs = _hg_refs(b_ref, hs)
            rmat = jnp.concatenate([jnp.broadcast_to(r, (HG_SUB, HG_DK)) for r in refs], axis=0)
            qt = qh * jnp.exp(bh - rmat)
            prow = []
            for j in range(nsub):
                kj = kh * jnp.exp(jnp.minimum(refs[j] - bh, HG_EXP_CLAMP))
                prow.append(_dot3(_nt, _sp(qt[j * HG_SUB:(j + 1) * HG_SUB]), _sp(kj)))
            p = jnp.where(lower, jnp.concatenate(prow, axis=0), 0.0)
            intra = _dot3(_nn, _sp(p), vsp)
            o_parts = []
            for c in range(t // CHUNK):
                rs = slice(c * CHUNK, (c + 1) * CHUNK)
                st0 = st_ref[hs, :]
                sall_ref[c * w + h * HG_DK:c * w + (h + 1) * HG_DK, :] = st0
                bl = b_ref[c * CHUNK + CHUNK - 1:c * CHUNK + CHUNK, hs]
                qf = qh[rs] * jnp.exp(bh[rs])
                o_parts.append(_dot3(_nt, _sp(qf), _sp(st0)) + intra[rs])
                khat = kh[rs] * jnp.exp(bl - bh[rs])
                st_ref[hs, :] = st0 * jnp.exp(bl) + _dot3(_tn, _sp(vh[rs]), _sp(khat))
            o = jnp.concatenate(o_parts, axis=0)
            o_ref[:, hs] = o
            r = lax.rsqrt(jnp.mean(o * o, axis=-1, keepdims=True) + EPS)
            hg = hg_ref[:, hs]
            y_ref[:, hs] = (((o * r) * gn_ref[:, hs]) * (hg * jax.nn.sigmoid(hg))).astype(BF16)

    blk = lambda j: pl.BlockSpec((t, w), lambda i, j=j: (i, j))
    srows = (t // CHUNK) * w
    res, side_res = _carry(
        body, side, name=name,
        out_shape=(jax.ShapeDtypeStruct((s, w), BF16), jax.ShapeDtypeStruct((s, w), F32),
                   jax.ShapeDtypeStruct((nt * srows, HG_DK), F32)),
        grid=(nt,),
        in_specs=[blk(3), blk(4), blk(5), blk(6), pl.BlockSpec((3, w), lambda i: (0, 0)),
                  pl.BlockSpec((1, w), lambda i: (0, 0))],
        out_specs=(pl.BlockSpec((t, w), lambda i: (i, 0)), pl.BlockSpec((t, w), lambda i: (i, 0)),
                   pl.BlockSpec((srows, HG_DK), lambda i: (i, 0))),
        scratch_shapes=[pltpu.VMEM((w, HG_DK), F32), pltpu.VMEM((t, w), F32)],
        args=(u, u, u, u, lbl, gn))
    return res if side is None else (res, side_res)


def _hgrn_bwd(u, o_all, sall, dy, lbl, gn, *, name, side=None):
    s = u.shape[0]
    t = HG_TILE
    nt = s // t
    nsub = t // HG_SUB
    w = HG_WIDTH
    nch = t // CHUNK

    def body(hq_ref, hf_ref, hi_ref, hg_ref, o_ref, sall_ref, dy_ref, lbl_ref, gn_ref,
             du_ref, acc_ref, dst_ref, b_ref):
        i = pl.program_id(0)

        @pl.when(i == 0)
        def _():
            dst_ref[...] = jnp.zeros_like(dst_ref)
            acc_ref[...] = jnp.zeros_like(acc_ref)

        lb = _lower_bound(lbl_ref[...])
        sig, f, g, kin = _hg_gates(hf_ref[...], lb)
        row, col, lower, upper = _hg_masks()
        b_ref[...] = _exact_left(lower.astype(BF16), g)
        upper_bf = upper.astype(BF16)
        rowblk = [((row >= j * HG_SUB) & (row < (j + 1) * HG_SUB)) for j in range(nsub)]
        colblk = [((col >= j * HG_SUB) & (col < (j + 1) * HG_SUB)) for j in range(nsub)]
        row1 = lax.broadcasted_iota(jnp.int32, (t, HG_DK), 0)

        for h in range(HG_HEADS):
            hs = slice(h * HG_DK, (h + 1) * HG_DK)
            bh = b_ref[:, hs]
            qh = hq_ref[:, hs]
            kh = kin[:, hs]
            vh = hi_ref[:, hs]
            vsp = _sp(vh)
            hg = hg_ref[:, hs]
            gnh = gn_ref[:, hs]
            o = o_ref[:, hs]
            dyv = dy_ref[:, hs]
            sg = jax.nn.sigmoid(hg)
            r = lax.rsqrt(jnp.mean(o * o, axis=-1, keepdims=True) + EPS)
            ohat = o * r
            du_ref[:, 3 * w + h * HG_DK:3 * w + (h + 1) * HG_DK] = (
                dyv * (ohat * gnh) * (sg * (1.0 + hg * (1.0 - sg)))).astype(BF16)
            don = dyv * (hg * sg)
            acc_ref[0:1, hs] += jnp.sum(don * ohat, axis=0, keepdims=True)
            dohat = don * gnh
            do = r * (dohat - ohat * jnp.mean(dohat * ohat, axis=-1, keepdims=True))
            dosp = _sp(do)
            refs = _hg_refs(b_ref, hs)
            rmat = jnp.concatenate([jnp.broadcast_to(rr, (HG_SUB, HG_DK)) for rr in refs], axis=0)
            eq = jnp.exp(bh - rmat)
            qt = qh * eq
            qtsp = _sp(qt)
            dp = jnp.where(lower, _dot3(_nt, dosp, vsp), 0.0)
            dpt = jnp.where(upper, _dot3(_nt, vsp, dosp), 0.0)
            pt = jnp.zeros((t, t), F32)
            dk = jnp.zeros((t, HG_DK), F32)
            dq_rows = []
            for j in range(nsub):
                ek = jnp.exp(jnp.minimum(refs[j] - bh, HG_EXP_CLAMP))
                kjsp = _sp(kh * ek)
                pt = pt + _nt(kjsp[0], jnp.where(rowblk[j], qtsp[0], 0))
                dq_rows.append(_dot3(_nn, _sp(dp[j * HG_SUB:(j + 1) * HG_SUB]), kjsp))
                dk = dk + ek * _dot3(_nn, _sp(jnp.where(colblk[j], dpt, 0.0)), qtsp)
            pt = jnp.where(upper, pt, 0.0)
            dv = _nn(pt.astype(BF16), dosp[0])
            dq = jnp.concatenate(dq_rows, axis=0) * eq
            dq_c, dk_c, dv_c, ex_c = [None] * nch, [None] * nch, [None] * nch, [None] * nch
            for c in reversed(range(nch)):
                rs = slice(c * CHUNK, (c + 1) * CHUNK)
                st0 = sall_ref[c * w + h * HG_DK:c * w + (h + 1) * HG_DK, :]
                dst1 = dst_ref[hs, :]
                dst1sp = _sp(dst1)
                dosp_c = _sp(do[rs])
                bl = b_ref[c * CHUNK + CHUNK - 1:c * CHUNK + CHUNK, hs]
                e = jnp.exp(bh[rs])
                el = jnp.exp(bl)
                ekl = jnp.exp(bl - bh[rs])
                dq_c[c] = _dot3(_nn, dosp_c, _sp(st0)) * e
                khat = kh[rs] * ekl
                dv_c[c] = _nt(khat.astype(BF16), dst1sp[0])
                dkhat = _dot3(_nn, _sp(vh[rs]), dst1sp)
                dk_c[c] = dkhat * ekl
                ex_c[c] = (jnp.sum(dkhat * khat, axis=0, keepdims=True)
                           + el * jnp.sum(dst1 * st0, axis=0, keepdims=True))
                dst_ref[hs, :] = _dot3(_tn, dosp_c, _sp(qh[rs] * e)) + dst1 * el
            dq = dq + jnp.concatenate(dq_c, axis=0)
            dk = dk + jnp.concatenate(dk_c, axis=0)
            dv = dv + jnp.concatenate(dv_c, axis=0)
            db = qh * dq - kh * dk
            for c in range(nch):
                db = db + jnp.where(row1 == c * CHUNK + CHUNK - 1, ex_c[c], 0.0)
            dg = _exact_left(upper_bf, db)
            fh = f[:, hs]
            sgf = sig[:, hs]
            lbh = lb[:, hs]
            df = dg / fh - dk
            du_ref[:, hs] = dq.astype(BF16)
            du_ref[:, w + h * HG_DK:w + (h + 1) * HG_DK] = (df * (1.0 - lbh) * sgf * (1.0 - sgf)).astype(BF16)
            du_ref[:, 2 * w + h * HG_DK:2 * w + (h + 1) * HG_DK] = dv.astype(BF16)
            acc_ref[1:2, hs] += jnp.sum(df * (1.0 - sgf), axis=0, keepdims=True)

    rev = lambda i: nt - 1 - i
    blk = lambda j: pl.BlockSpec((t, w), lambda i, j=j: (rev(i), j))
    srows = nch * w
    res, side_res = _carry(
        body, side, name=name,
        out_shape=(jax.ShapeDtypeStruct((s, 4 * w), BF16), jax.ShapeDtypeStruct((8, w), F32)),
        grid=(nt,),
        in_specs=[blk(3), blk(4), blk(5), blk(6), pl.BlockSpec((t, w), lambda i: (rev(i), 0)),
                  pl.BlockSpec((srows, HG_DK), lambda i: (rev(i), 0)),
                  pl.BlockSpec((t, w), lambda i: (rev(i), 1)),
                  pl.BlockSpec((3, w), lambda i: (0, 0)), pl.BlockSpec((1, w), lambda i: (0, 0))],
        out_specs=(pl.BlockSpec((t, 4 * w), lambda i: (rev(i), 0)), pl.BlockSpec((8, w), lambda i: (0, 0))),
        scratch_shapes=[pltpu.VMEM((w, HG_DK), F32), pltpu.VMEM((t, w), F32)],
        args=(u, u, u, u, o_all, sall, dy, lbl, gn))
    return res if side is None else (res, side_res)


def _pair_matrix():
    row = lax.broadcasted_iota(jnp.int32, (LANES, LANES), 0)
    col = lax.broadcasted_iota(jnp.int32, (LANES, LANES), 1)
    return ((row >= SB_HEAD_DIM) == (col >= SB_HEAD_DIM)).astype(BF16)


def _qk_norm_fwd(qkv, qn, kn, *, name, tm=256):
    s = qkv.shape[0]
    d = D_MODEL
    tm = _tile(tm, s)

    def body(q_ref, k_ref, v_ref, qn_ref, kn_ref, qo_ref, ko_ref, vo_ref):
        bd = _pair_matrix()
        for src, gain, dst, fac in ((q_ref, qn_ref, qo_ref, SB_SCALE * LOG2E), (k_ref, kn_ref, ko_ref, None)):
            for grp in range(d // LANES):
                ls = slice(grp * LANES, (grp + 1) * LANES)
                xv = src[:, ls]
                ms = _exact_right2(xv * xv, bd) * (1.0 / SB_HEAD_DIM)
                y = (xv * lax.rsqrt(ms + EPS)) * gain[:, ls]
                dst[:, ls] = (y if fac is None else y * fac).astype(BF16)
        vo_ref[...] = v_ref[...].astype(BF16)

    blk = lambda j: pl.BlockSpec((tm, d), lambda i, j=j: (i, j))
    row = pl.BlockSpec((1, d), lambda i: (0, 0))
    out = jax.ShapeDtypeStruct((s, d), BF16)
    return pl.pallas_call(
        body, name=name, out_shape=(out, out, out), grid=(s // tm,),
        in_specs=[blk(0), blk(1), blk(2), row, row],
        out_specs=(blk(0), blk(0), blk(0)),
        compiler_params=_cparams(1),
    )(qkv, qkv, qkv, qn, kn)


def _qk_norm_bwd(qkv, dqn, dkn, dv, qn, kn, *, name, tm=256):
    s = qkv.shape[0]
    d = D_MODEL
    tm = _tile(tm, s)

    def body(q_ref, k_ref, dq_ref, dk_ref, dv_ref, qn_ref, kn_ref, o_ref, acc_ref):
        i = pl.program_id(0)

        @pl.when(i == 0)
        def _():
            acc_ref[...] = jnp.zeros_like(acc_ref)

        bd = _pair_matrix()
        for idx, (src, dsrc, gain) in enumerate(((q_ref, dq_ref, qn_ref), (k_ref, dk_ref, kn_ref))):
            for grp in range(d // LANES):
                ls = slice(grp * LANES, (grp + 1) * LANES)
                xv = src[:, ls]
                dyv = dsrc[:, ls]
                r = lax.rsqrt(_exact_right2(xv * xv, bd) * (1.0 / SB_HEAD_DIM) + EPS)
                xh = xv * r
                acc_ref[idx:idx + 1, ls] += jnp.sum(dyv * xh, axis=0, keepdims=True)
                dxh = dyv * gain[:, ls]
                mean = _exact_right2(dxh * xh, bd) * (1.0 / SB_HEAD_DIM)
                o_ref[:, idx * d + grp * LANES:idx * d + (grp + 1) * LANES] = (r * (dxh - xh * mean)).astype(BF16)
        o_ref[:, 2 * d:3 * d] = dv_ref[...].astype(BF16)

    blk = lambda j: pl.BlockSpec((tm, d), lambda i, j=j: (i, j))
    row = pl.BlockSpec((1, d), lambda i: (0, 0))
    return pl.pallas_call(
        body, name=name,
        out_shape=(jax.ShapeDtypeStruct((s, 3 * d), BF16), jax.ShapeDtypeStruct((8, d), F32)),
        grid=(s // tm,),
        in_specs=[blk(0), blk(1), blk(0), blk(0), blk(0), row, row],
        out_specs=(pl.BlockSpec((tm, 3 * d), lambda i: (i, 0)), pl.BlockSpec((8, d), lambda i: (0, 0))),
        compiler_params=_cparams(1),
    )(qkv, qkv, dqn, dkn, dv, qn, kn)


def _sb_tile(qh, kb, suffix_ones, run, mask):
    z = _nt(qh, kb)
    neg_abs = lax.bitcast_convert_type(lax.bitcast_convert_type(z, jnp.uint32) | jnp.uint32(0x80000000), F32)
    l1m = -(jnp.maximum(z, 0.0) + jnp.log2(1.0 + jnp.exp2(neg_abs)))
    logb = z + l1m
    if mask is not None:
        l1m = jnp.where(mask, l1m, 0.0)
    later = _nn(l1m.astype(BF16), suffix_ones) + run
    wgt = jnp.exp2(logb + later)
    if mask is not None:
        wgt = jnp.where(mask, wgt, 0.0)
    return logb, l1m, wgt


def _suffix_ones(tk):
    row = lax.broadcasted_iota(jnp.int32, (tk, tk), 0)
    col = lax.broadcasted_iota(jnp.int32, (tk, tk), 1)
    return (row > col).astype(BF16)


def _sb_alive(runs):
    return jnp.max(jnp.maximum(runs[0], runs[1])) > -SB_DEAD


def _sb_mask(q0, j, nr, tk):
    qpos = q0 + lax.broadcasted_iota(jnp.int32, (nr, tk), 0)
    kpos = j * tk + lax.broadcasted_iota(jnp.int32, (nr, tk), 1)
    return kpos < qpos


def _sb_fwd(qn, kn, v, *, name, side=None):
    s, d = qn.shape
    tq, tk = _tile(SB_TQ, s), _tile(SB_TK, s)
    assert tk % tq == 0 or tq % tk == 0
    nq = s // tq

    def body(q_ref, k_ref, v_ref, o_ref, acc_ref):
        qi = pl.program_id(1)
        lane = lax.broadcasted_iota(jnp.int32, (tq, LANES), 1)
        first = lane < SB_HEAD_DIM
        q = q_ref[...]
        qh = [jnp.where(first, q, 0).astype(BF16), jnp.where(first, 0, q).astype(BF16)]
        ones = _suffix_ones(tk)
        acc_ref[...] = jnp.zeros_like(acc_ref)

        def tile(j, runs, masked, rows=(0, tq)):
            r0, nr = rows
            ks = pl.ds(pl.multiple_of(j * tk, tk), tk)
            kb = k_ref[ks, :]
            vb = v_ref[ks, :]
            mask = _sb_mask(qi * tq + r0, j, nr, tk) if masked else None
            new_runs = []
            for hh in range(2):
                _, l1m, wgt = _sb_tile(qh[hh][r0:r0 + nr], kb, ones, runs[hh], mask)
                acc_ref[hh, r0:r0 + nr] += _nn(wgt.astype(BF16), vb)
                new_runs.append(runs[hh] + jnp.sum(l1m, axis=1, keepdims=True))
            return tuple(new_runs)

        nfull = (qi * tq) // tk
        zero = jnp.zeros((tq, 1), F32)
        runs = (zero, zero)
        for m in reversed(range(max(tq // tk, 1))):
            r0 = m * tk
            part = tile(nfull + m, tuple(r[r0:] for r in runs), True, (r0, tq - r0))
            runs = tuple(jnp.concatenate([r[:r0], p], axis=0) if r0 else p for r, p in zip(runs, part))

        def step(c):
            it, _, r = c
            r = tile(nfull - 1 - it, r, False)
            return it + 1, _sb_alive(r), r

        lax.while_loop(lambda c: jnp.logical_and(c[0] < nfull, c[1]), step, (0, _sb_alive(runs), runs))
        o_ref[...] = jnp.where(first, acc_ref[0], acc_ref[1])

    res, side_res = _carry(
        body, side, name=name, out_shape=(jax.ShapeDtypeStruct((s, d), F32),), grid=(d // LANES, nq),
        in_specs=[pl.BlockSpec((tq, LANES), lambda p, i: (i, p)), pl.BlockSpec((s, LANES), lambda p, i: (0, p)),
                  pl.BlockSpec((s, LANES), lambda p, i: (0, p))],
        out_specs=(pl.BlockSpec((tq, LANES), lambda p, i: (i, p)),),
        scratch_shapes=[pltpu.VMEM((2, tq, LANES), F32)], args=(qn, kn, v))
    return res[0] if side is None else (res[0], side_res)


def _sb_bwd(qn, kn, v, o, do, *, name, side=None):
    s, d = qn.shape
    tq, tk = _tile(SB_TQ, s), _tile(SB_TK, s)
    assert tk % tq == 0 or tq % tk == 0
    nq = s // tq

    def body(q_ref, k_ref, v_ref, o_ref, do_ref, dq_ref, dk_ref, dv_ref, acc_ref):
        qi = pl.program_id(1)

        @pl.when(qi == 0)
        def _():
            dk_ref[...] = jnp.zeros_like(dk_ref)
            dv_ref[...] = jnp.zeros_like(dv_ref)

        first = lax.broadcasted_iota(jnp.int32, (tq, LANES), 1) < SB_HEAD_DIM
        sel = [first, jnp.logical_not(first)]
        kfirst = lax.broadcasted_iota(jnp.int32, (tk, LANES), 1) < SB_HEAD_DIM
        ksel = [kfirst, jnp.logical_not(kfirst)]
        q = q_ref[...]
        dob = do_ref[...].astype(BF16)
        qh = [jnp.where(sel[hh], q, 0).astype(BF16) for hh in range(2)]
        doh = [jnp.where(sel[hh], dob, 0).astype(BF16) for hh in range(2)]
        prod = dob.astype(F32) * o_ref[...]
        gtot = [jnp.sum(jnp.where(sel[hh], prod, 0.0), axis=1, keepdims=True) for hh in range(2)]
        ones = _suffix_ones(tk)
        acc_ref[...] = jnp.zeros_like(acc_ref)

        def tile(j, carry, masked, rows=(0, tq)):
            r0, nr = rows
            runs, gruns = carry
            ks = pl.ds(pl.multiple_of(j * tk, tk), tk)
            kb = k_ref[ks, :]
            vb = v_ref[ks, :]
            mask = _sb_mask(qi * tq + r0, j, nr, tk) if masked else None
            new_runs, new_gruns = [], []
            dk_add = jnp.zeros((tk, LANES), F32)
            dv_add = jnp.zeros((tk, LANES), F32)
            for hh in range(2):
                qs, dos = qh[hh][r0:r0 + nr], doh[hh][r0:r0 + nr]
                logb, l1m, wgt = _sb_tile(qs, kb, ones, runs[hh], mask)
                wb = wgt.astype(BF16)
                g = _nt(dos, vb) * wb.astype(F32)
                gsuf = _exact_right2(g, ones) + g + gruns[hh]
                dz = g - jnp.exp2(logb) * (g + (gtot[hh][r0:r0 + nr] - gsuf))
                if masked:
                    dz = jnp.where(mask, dz, 0.0)
                dzb = dz.astype(BF16)
                acc_ref[hh, r0:r0 + nr] += _nn(dzb, kb)
                dk_add = dk_add + jnp.where(ksel[hh], _tn(dzb, qs), 0.0)
                dv_add = dv_add + jnp.where(ksel[hh], _tn(wb, dos), 0.0)
                new_runs.append(runs[hh] + jnp.sum(l1m, axis=1, keepdims=True))
                new_gruns.append(gruns[hh] + jnp.sum(g, axis=1, keepdims=True))
            dk_ref[ks, :] += dk_add * LN2
            dv_ref[ks, :] += dv_add
            return tuple(new_runs), tuple(new_gruns)

        nfull = (qi * tq) // tk
        zero = jnp.zeros((tq, 1), F32)
        carry = ((zero, zero), (zero, zero))
        for m in reversed(range(max(tq // tk, 1))):
            r0 = m * tk
            part = tile(nfull + m, tuple(tuple(r[r0:] for r in rs) for rs in carry), True, (r0, tq - r0))
            carry = tuple(tuple(jnp.concatenate([r[:r0], p], axis=0) if r0 else p for r, p in zip(rs, ps))
                          for rs, ps in zip(carry, part))

        def step(c):
            it, _, cr = c
            cr = tile(nfull - 1 - it, cr, False)
            return it + 1, _sb_alive(cr[0]), cr

        lax.while_loop(lambda c: jnp.logical_and(c[0] < nfull, c[1]), step, (0, _sb_alive(carry[0]), carry))
        dq_ref[...] = jnp.where(first, acc_ref[0], acc_ref[1]) * SB_SCALE

    blk = pl.BlockSpec((tq, LANES), lambda p, i: (i, p))
    full = pl.BlockSpec((s, LANES), lambda p, i: (0, p))
    out = jax.ShapeDtypeStruct((s, d), F32)
    res, side_res = _carry(
        body, side, name=name, out_shape=(out, out, out), grid=(d // LANES, nq),
        in_specs=[blk, full, full, blk, blk], out_specs=(blk, full, full),
        scratch_shapes=[pltpu.VMEM((2, tq, LANES), F32)], args=(qn, kn, v, o, do))
    return res if side is None else (res, side_res)


def _mod_part(c_all, ada_w, ada_b_my, *, name):
    nl, d, ncol = ada_w.shape

    def body(c_ref, w_ref, b_ref, part_ref, ca_ref):
        cv = c_ref[...]
        ca = cv * jax.nn.sigmoid(cv)
        ca_ref[...] = ca
        part_ref[...] = _nn(ca.astype(BF16), w_ref[...].astype(BF16)) + b_ref[...]

    return pl.pallas_call(
        body, name=name,
        out_shape=(jax.ShapeDtypeStruct((nl, N_DEV, ncol), F32), jax.ShapeDtypeStruct((N_DEV, d), F32)),
        grid=(nl,),
        in_specs=[pl.BlockSpec((N_DEV, d), lambda l: (0, 0)), pl.BlockSpec((None, d, ncol), lambda l: (l, 0, 0)),
                  pl.BlockSpec((None, 1, ncol), lambda l: (l, 0, 0))],
        out_specs=(pl.BlockSpec((None, N_DEV, ncol), lambda l: (l, 0, 0)), pl.BlockSpec((N_DEV, d), lambda l: (0, 0))),
        compiler_params=_cparams(1),
    )(c_all, ada_w, ada_b_my)


PK_MOD, PK_NMIX, PK_NMLP, PK_HGN, PK_LB, PK_QN, PK_KN, PK_CONV, PK_ROWS = 0, 96, 112, 128, 132, 136, 144, 152, 168


def _small_grads(gath, ca_col, dmod_my, lbl4, *, name):
    def body(g_ref, ca_ref, dm_ref, lbl_ref, gw_ref, gsum_ref, glb_ref, gqk_ref):
        tot = g_ref[0]
        for dev in range(1, N_DEV):
            tot = tot + g_ref[dev]
        gsum_ref[...] = tot
        lv = lbl_ref[...]
        m = jnp.maximum(jnp.maximum(lv[0], lv[1]), lv[2])
        e = [jnp.exp(lv[k] - m) for k in range(3)]
        den = e[0] + e[1] + e[2]
        p = [ek / den for ek in e]
        dlb = tot[PK_LB:PK_LB + 4, :]
        glb_ref[0] = dlb * p[0] * (1.0 - p[0])
        glb_ref[1] = -dlb * p[0] * p[1]
        glb_ref[2] = -dlb * p[0] * p[2]
        for idx, base in enumerate((PK_QN, PK_KN)):
            rowsum = jnp.sum(tot[base:base + 8, :], axis=0, keepdims=True)
            gqk_ref[idx:idx + 1, :] = rowsum + pltpu.roll(rowsum, SB_HEAD_DIM, 1)
        for l in range(2):
            acc = ca_ref[0] * dm_ref[0, l:l + 1, :]
            for smp in range(1, N_DEV):
                acc = acc + ca_ref[smp] * dm_ref[smp, l:l + 1, :]
            gw_ref[l] = acc

    d, ncol = ca_col.shape[1], dmod_my.shape[2]
    vm = pl.BlockSpec(memory_space=pltpu.VMEM)
    return pl.pallas_call(
        body, name=name,
        out_shape=(jax.ShapeDtypeStruct((2, d, ncol), F32), jax.ShapeDtypeStruct((PK_ROWS, LANES), F32),
                   jax.ShapeDtypeStruct((3, 4, LANES), F32), jax.ShapeDtypeStruct((8, LANES), F32)),
        in_specs=[vm, vm, vm, vm], out_specs=(vm, vm, vm, vm),
        compiler_params=pltpu.CompilerParams(vmem_limit_bytes=VMEM_LIMIT),
    )(gath, ca_col, dmod_my, lbl4)


def _adamw_math(w, g, m, v):
    m = ADAM_B1 * m + (1.0 - ADAM_B1) * g
    v = ADAM_B2 * v + (1.0 - ADAM_B2) * (g * g)
    m_hat = m / (1.0 - ADAM_B1 ** ADAM_STEP)
    v_hat = v / (1.0 - ADAM_B2 ** ADAM_STEP)
    delta = -ADAM_LR * (m_hat / (jnp.sqrt(v_hat) + ADAM_EPS) + ADAM_WD * w)
    return delta, m, v


def _adamw(w, g, m, v, *, name, tr=256):
    r, n = w.shape
    tr = _tile(tr, r)

    def body(w_ref, g_ref, m_ref, v_ref, d_ref, mo_ref, vo_ref):
        dl, mn, vn = _adamw_math(w_ref[...], g_ref[...], m_ref[...], v_ref[...])
        d_ref[...] = dl
        mo_ref[...] = mn
        vo_ref[...] = vn

    blk = pl.BlockSpec((tr, n), lambda i: (i, 0))
    out = jax.ShapeDtypeStruct((r, n), F32)
    return pl.pallas_call(
        body, name=name, out_shape=(out, out, out), grid=(r // tr,),
        in_specs=[blk, blk, blk, blk], out_specs=(blk, blk, blk),
        compiler_params=_cparams(1),
    )(w, g, m, v)


def _adamw_small(items, *, name):
    n = len(items)

    def body(*refs):
        ins, outs = refs[:4 * n], refs[4 * n:]
        for k in range(n):
            dl, mn, vn = _adamw_math(*(r[...] for r in ins[4 * k:4 * k + 4]))
            outs[3 * k][...] = dl
            outs[3 * k + 1][...] = mn
            outs[3 * k + 2][...] = vn

    flat = [a for it in items for a in it]
    out_shape = tuple(jax.ShapeDtypeStruct(it[0].shape, F32) for it in items for _ in range(3))
    vm = pl.BlockSpec(memory_space=pltpu.VMEM)
    res = pl.pallas_call(
        body, name=name, out_shape=out_shape, in_specs=[vm] * (4 * n), out_specs=tuple([vm] * (3 * n)),
    )(*flat)
    return [tuple(res[3 * k:3 * k + 3]) for k in range(n)]


def _mlp_fwd(x, g, scale, shift, gate, w1g, w2g, tag, side_w1=None):
    h = _norm_mod(x, g, scale, shift, name=f"{tag}_norm")
    act = _matmul(h, w1g, b_kind="colblk", epi="relu2", out_dtype=BF16, name=f"{tag}_w1", side=side_w1)
    side_res = ()
    if side_w1 is not None:
        act, side_res = act
    z, x_out = _matmul(act, w2g, b_kind="rowblk", epi="resgate", extras=(x, gate), name=f"{tag}_w2")
    return x_out, (h, act, z), side_res


def _mlp_bwd(dz, dx_out, x, saved, g, scale, w1g, w2g, tag, gated, side_dact=None, make_side_dh=None):
    h, act, _ = saved
    du = _matmul(dz, w2g, tb=True, b_kind="rowblk", epi="dact", extras=(act,), out_dtype=BF16,
                 name=f"{tag}_dact", side=side_dact)
    res_dact = ()
    if side_dact is not None:
        du, res_dact = du
    dw2 = _matmul(act, dz, ta=True, name=f"{tag}_dw2")
    dw1_t = _matmul(du, h, ta=True, name=f"{tag}_dw1")
    side_dh = None if make_side_dh is None else make_side_dh(dw1_t, dw2)
    dh = _matmul(du, w1g, tb=True, b_kind="colblk", name=f"{tag}_dh", side=side_dh)
    res_dh = ()
    if side_dh is not None:
        dh, res_dh = dh
    dx, nacc, dz_mix = _norm_mod_bwd(x, dh, dx_out, g, scale, name=f"{tag}_norm_bwd", gated=gated)
    return dx, dw1_t, dw2, nacc, dz_mix, (res_dact, res_dh)


def kernel(x, c, ada_w, ada_b, norm_mix, norm_mlp, w_in_ab, conv_w, hg_norm, lb_logits, w_out_ab, w_qkv, q_norm, k_norm, w_out_c, mlp_w1, mlp_w2, loss_target, m_ada_w, m_ada_b, m_norm_mix, m_norm_mlp, m_w_in_ab, m_conv_w, m_hg_norm, m_lb_logits, m_w_out_ab, m_w_qkv, m_q_norm, m_k_norm, m_w_out_c, m_mlp_w1, m_mlp_w2, v_ada_w, v_ada_b, v_norm_mix, v_norm_mlp, v_w_in_ab, v_conv_w, v_hg_norm, v_lb_logits, v_w_out_ab, v_w_qkv, v_q_norm, v_k_norm, v_w_out_c, v_mlp_w1, v_mlp_w2):
    d = D_MODEL
    my_x, my_y, my_c = lax.axis_index("x"), lax.axis_index("y"), lax.axis_index("c")
    me = 4 * my_x + 2 * my_y + my_c
    xs = x[0]
    tgt = loss_target[0]

    def bf(w):
        return w.astype(BF16)

    ncv = CONV_DIM // N_DEV
    c_and_conv = jnp.concatenate([c, jnp.pad(conv_w[0], ((0, 0), (0, d - ncv))), jnp.zeros((4, d), F32)], axis=0)
    wing, c_and_conv = _run_side(_gather_side([bf(w_in_ab), c_and_conv]), name="gather_w_in")
    win = wing[:, 0].transpose(1, 0, 2).reshape(d, AB_IN)

    c_all = c_and_conv[:, 0]
    conv_full = c_and_conv[:, 1:4, :ncv].transpose(1, 0, 2).reshape(3, CONV_DIM)
    ncol = ada_w.shape[2]
    ada_b_my = lax.dynamic_slice(ada_b, (0, me * ncol), (2, ncol)).reshape(2, 1, ncol)
    part, c_act = _mod_part(c_all, ada_w, ada_b_my, name="mod_part")
    parts = _all_gather(part.reshape(2 * N_DEV, ncol), name="gather_mod", in_vmem=True)
    parts = parts.reshape(N_DEV, 2, N_DEV, ncol)
    mod = lax.dynamic_index_in_dim(parts, me, axis=2, keepdims=False)
    mod = mod.transpose(1, 0, 2).reshape(2, 6, 1, d)

    qn_t = jnp.tile(q_norm, (1, d // SB_HEAD_DIM))
    kn_t = jnp.tile(k_norm, (1, d // SB_HEAD_DIM))

    sh1, sc1, gt1, sh2, sc2, gt2 = [mod[0, k] for k in range(6)]
    h0 = _norm_mod(xs, norm_mix[0:1], sc1, sh1, name="l0_mix_norm")
    u = _matmul(h0, win, name="l0_in_proj")
    y_a = _conv_fwd(u, conv_full, name="l0_conv")
    (y_b, o_hg, sall), (woutg_ab, w1g0, w2g0) = _hgrn_fwd(
        u, lb_logits, hg_norm, name="l0_hgrn",
        side=_gather_side([bf(w_out_ab), bf(mlp_w1[0:1]), bf(mlp_w2[0:1])]))
    wout_ab = woutg_ab.reshape(d, d)
    y_ab = jnp.concatenate([y_a, y_b], axis=1)
    z0, x_mid0 = _matmul(y_ab, wout_ab, epi="resgate", extras=(xs, gt1), name="l0_out_proj")
    x1, mlp0, (wqkvg, woutg_c) = _mlp_fwd(x_mid0, norm_mlp[0:1], sc2, sh2, gt2, w1g0, w2g0, "l0_mlp",
                                          side_w1=_gather_side([bf(w_qkv), bf(w_out_c)]))
    wout_c = woutg_c.reshape(d, d)

    sh1b, sc1b, gt1b, sh2b, sc2b, gt2b = [mod[1, k] for k in range(6)]
    h1 = _norm_mod(x1, norm_mix[1:2], sc1b, sh1b, name="l1_mix_norm")
    qkv = _matmul(h1, wqkvg, b_kind="colblk", name="l1_qkv_proj")
    qn_a, kn_a, v_a = _qk_norm_fwd(qkv, qn_t, kn_t, name="l1_qk_norm")
    o_sb, (w1g1, w2g1) = _sb_fwd(qn_a, kn_a, v_a, name="l1_sb",
                                 side=_gather_side([bf(mlp_w1[1:2]), bf(mlp_w2[1:2])]))
    z1, x_mid1 = _matmul(o_sb, wout_c, epi="resgate", extras=(x1, gt1b), name="l1_out_proj")
    x2, mlp1, _ = _mlp_fwd(x_mid1, norm_mlp[1:2], sc2b, sh2b, gt2b, w1g1, w2g1, "l1_mlp")

    dx, loss_part, dz_mlp, dgt2b = _loss_grad(x2, tgt, mlp1[2], gt2b, name="loss")
    loss = lax.psum(loss_part[0, 0], MESH_AXES)

    my_q = 2 * my_x + my_y
    far_q = [my_q ^ 2, my_q ^ 1, my_q ^ 3]
    blk_ids = jnp.stack([2 * q + my_c for q in far_q] + far_q).astype(jnp.int32)
    my_ids = jnp.stack([me, my_q]).astype(jnp.int32)

    def blocks(g):
        return g.reshape(N_DEV, g.shape[0] // N_DEV, d)

    def by_rows(fn, tag, *lists):
        out = [None] * len(lists[0])
        heights = {}
        for t, g in enumerate(lists[0]):
            heights.setdefault(g.shape[1], []).append(t)
        for r, ts in heights.items():
            for t, v in zip(ts, fn(*[[lst[t] for t in ts] for lst in lists], name=f"{tag}_{r}")):
                out[t] = v
        return out

    def pair_sums(gs, sibs, tag):
        return by_rows(lambda a, b, name: _rs_pair_sum(a, b, blk_ids, name=name), f"rs_pair_sum_{tag}", gs, sibs)

    def final_sums(gs, sibs, fars, tag):
        return by_rows(lambda a, b, c_, name: _rs_final_sum(a, b, c_, my_ids, name=name),
                       f"rs_final_sum_{tag}", gs, sibs, fars)

    dx, dw1t_1, dw2_1, nacc, dyp, (_, sib1) = _mlp_bwd(
        dz_mlp, dx, x_mid1, mlp1, norm_mlp[1:2], sc2b, w1g1, w2g1, "l1_mlp", (z1, gt1b),
        make_side_dh=lambda a, b: _sibling_exchange_side([blocks(a), blocks(b)]))
    dsh2b, dsc2b, dnmlp1, dgt1b = nacc[0:1], nacc[1:2], nacc[2:3], nacc[3:4]
    g1 = [blocks(dw1t_1), blocks(dw2_1)]
    pair1 = pair_sums(g1, sib1, "g1")
    dwout_c = _matmul(o_sb, dyp, ta=True, name="l1_dwout")
    do_sb = _matmul(dyp, wout_c, tb=True, name="l1_do")
    (dqn_a, dkn_a, dv_a), far1 = _sb_bwd(qn_a, kn_a, v_a, o_sb, do_sb, name="l1_sb_bwd",
                                         side=_chip_exchange_side(pair1))
    gsh1 = final_sums(g1, sib1, far1, "g1")
    dqkv, qkacc = _qk_norm_bwd(qkv, dqn_a, dkn_a, dv_a, qn_t, kn_t, name="l1_qk_norm_bwd")
    dwqkv_t = _matmul(dqkv, h1, ta=True, name="l1_dwqkv")
    g2 = [blocks(dwqkv_t), blocks(dwout_c)]
    dh1, sib2 = _matmul(dqkv, wqkvg, tb=True, b_kind="colblk", name="l1_dh", side=_sibling_exchange_side(g2))
    pair2 = pair_sums(g2, sib2, "g2")
    dx, nacc, dz_mlp = _norm_mod_bwd(x1, dh1, dx, norm_mix[1:2], sc1b, name="l1_mix_norm_bwd",
                                     gated=(mlp0[2], gt2))
    dmod1 = [nacc[0:1], nacc[1:2], dgt1b, dsh2b, dsc2b, dgt2b]
    dnmix1, dgt2 = nacc[2:3], nacc[3:4]

    dx, dw1t_0, dw2_0, nacc, dyp, (far2, sib3) = _mlp_bwd(
        dz_mlp, dx, x_mid0, mlp0, norm_mlp[0:1], sc2, w1g0, w2g0, "l0_mlp", (z0, gt1),
        side_dact=_chip_exchange_side(pair2),
        make_side_dh=lambda a, b: _sibling_exchange_side([blocks(a), blocks(b)]))
    dsh2, dsc2, dnmlp0, dgt1 = nacc[0:1], nacc[1:2], nacc[2:3], nacc[3:4]
    gsh2 = final_sums(g2, sib2, far2, "g2")
    g3 = [blocks(dw1t_0), blocks(dw2_0)]
    pair3 = pair_sums(g3, sib3, "g3")
    dwout_ab = _matmul(y_ab, dyp, ta=True, name="l0_dwout")
    g3b = [blocks(dwout_ab)]
    dy_ab, sib3b = _matmul(dyp, wout_ab, tb=True, name="l0_dy", side=_sibling_exchange_side(g3b))
    pair3b = pair_sums(g3b, sib3b, "g3b")
    du_a, dconv = _conv_bwd(u, dy_ab, conv_full, name="l0_conv_bwd")
    (du_b, hgacc), far3 = _hgrn_bwd(u, o_hg, sall, dy_ab, lb_logits, hg_norm, name="l0_hgrn_bwd",
                                    side=_chip_exchange_side(pair3 + pair3b))
    gsh3 = final_sums(g3, sib3, far3[:2], "g3")
    gsh3b = final_sums(g3b, sib3b, far3[2:], "g3b")
    du = jnp.concatenate([du_a, du_b], axis=1)
    dwin_t = _matmul(du, h0, ta=True, name="l0_dwin")
    g4 = [blocks(dwin_t)]
    dh0, sib4 = _matmul(du, win, tb=True, name="l0_dh", side=_sibling_exchange_side(g4))
    pair4 = pair_sums(g4, sib4, "g4")
    grad_x, nacc, far4 = _norm_mod_bwd(xs, dh0, dx, norm_mix[0:1], sc1, name="l0_mix_norm_bwd",
                                       side=_chip_exchange_side(pair4))
    gsh4 = final_sums(g4, sib4, far4, "g4")
    dmod0 = [nacc[0:1], nacc[1:2], dgt1, dsh2, dsc2, dgt2]
    dnmix0 = nacc[2:3]

    g_big = [gsh4[0].T[None], gsh3b[0][None], gsh2[0].T[None], gsh2[1][None],
             jnp.stack([gsh3[0].T, gsh1[0].T]), jnp.stack([gsh3[1], gsh1[1]])]

    packed_small = jnp.concatenate(
        [jnp.concatenate(dmod0, axis=1).reshape(-1, LANES), jnp.concatenate(dmod1, axis=1).reshape(-1, LANES),
         dnmix0.reshape(-1, LANES), dnmix1.reshape(-1, LANES), dnmlp0.reshape(-1, LANES), dnmlp1.reshape(-1, LANES),
         hgacc[0:1].reshape(-1, LANES), hgacc[1:2].reshape(-1, LANES),
         qkacc[0:1].reshape(-1, LANES), qkacc[1:2].reshape(-1, LANES),
         dconv[0:3].reshape(-1, LANES), jnp.zeros((PK_ROWS - PK_CONV - 12, LANES), F32)], axis=0)
    gath = _all_gather(packed_small, name="gather_small_grads", in_vmem=True).reshape(N_DEV, PK_ROWS, LANES)
    dmod_all = gath[:, PK_MOD:PK_NMIX].reshape(N_DEV, 2, 6 * d)
    dmod_my = lax.dynamic_slice(dmod_all, (0, 0, me * ncol), (N_DEV, 2, ncol))
    g_ada_w, gsum, g_lb, g_qk = _small_grads(gath, c_act[:, :, None], dmod_my, lb_logits.reshape(3, 4, LANES),
                                             name="small_grads")
    g_ada_b = gsum[PK_MOD:PK_NMIX].reshape(2, 6 * d)
    g_norm_mix = gsum[PK_NMIX:PK_NMLP].reshape(2, d)
    g_norm_mlp = gsum[PK_NMLP:PK_HGN].reshape(2, d)
    g_hg_norm = gsum[PK_HGN:PK_LB].reshape(1, HG_WIDTH)
    g_lb_logits = g_lb.reshape(3, HG_WIDTH)
    g_q_norm = g_qk[0:1, :SB_HEAD_DIM]
    g_k_norm = g_qk[1:2, :SB_HEAD_DIM]
    g_conv_w = lax.dynamic_slice(gsum[PK_CONV:PK_CONV + 12].reshape(3, CONV_DIM), (0, me * ncv), (3, ncv))[None]

    def flat2(a):
        return a.reshape(-1, a.shape[-1])

    grads = dict(ada_w=g_ada_w, ada_b=g_ada_b, norm_mix=g_norm_mix, norm_mlp=g_norm_mlp, w_in_ab=g_big[0],
                 conv_w=g_conv_w, hg_norm=g_hg_norm, lb_logits=g_lb_logits, w_out_ab=g_big[1], w_qkv=g_big[2],
                 q_norm=g_q_norm, k_norm=g_k_norm, w_out_c=g_big[3], mlp_w1=g_big[4], mlp_w2=g_big[5])
    weights = dict(ada_w=(ada_w, m_ada_w, v_ada_w), ada_b=(ada_b, m_ada_b, v_ada_b),
                   norm_mix=(norm_mix, m_norm_mix, v_norm_mix), norm_mlp=(norm_mlp, m_norm_mlp, v_norm_mlp),
                   w_in_ab=(w_in_ab, m_w_in_ab, v_w_in_ab), conv_w=(conv_w, m_conv_w, v_conv_w),
                   hg_norm=(hg_norm, m_hg_norm, v_hg_norm), lb_logits=(lb_logits, m_lb_logits, v_lb_logits),
                   w_out_ab=(w_out_ab, m_w_out_ab, v_w_out_ab), w_qkv=(w_qkv, m_w_qkv, v_w_qkv),
                   q_norm=(q_norm, m_q_norm, v_q_norm), k_norm=(k_norm, m_k_norm, v_k_norm),
                   w_out_c=(w_out_c, m_w_out_c, v_w_out_c), mlp_w1=(mlp_w1, m_mlp_w1, v_mlp_w1),
                   mlp_w2=(mlp_w2, m_mlp_w2, v_mlp_w2))
    names = list(weights)
    small_names = ["ada_b", "norm_mix", "norm_mlp", "conv_w", "hg_norm", "lb_logits", "q_norm", "k_norm"]
    upd = {}
    small_items = []
    for n in small_names:
        wv, mv, vv = weights[n]
        small_items.append((flat2(wv), flat2(grads[n]), flat2(mv), flat2(vv)))
    for n, res in zip(small_names, _adamw_small(small_items, name="adamw_small")):
        upd[n] = tuple(r.reshape(weights[n][0].shape) for r in res)
    for n in names:
        if n in small_names:
            continue
        wv, mv, vv = weights[n]
        res = _adamw(flat2(wv), flat2(grads[n]), flat2(mv), flat2(vv), name=f"adamw_{n}")
        upd[n] = tuple(r.reshape(wv.shape) for r in res)

    return (loss, grad_x[None], *[grads[n].reshape(weights[n][0].shape) for n in names],
            *[upd[n][0] for n in names], *[upd[n][1] for n in names], *[upd[n][2] for n in names])
```
